```python
import math
import jax, jax.numpy as jnp
from jax import lax
import numpy as np

D_MODEL = 1024
BATCH = 8
SEQ = 4096
DEPTH = 4

GRID_W = 64
CTX_LEN = 256
EPS = 1e-6
ROPE_THETA = 10000.0
NEG = -1e30

HEAD_DIM = 64
BLOCK = 128
WINDOW = 128
FOURIER_DIM = D_MODEL // 2
FOURIER_GROUPS = 4
FOURIER_GROUP_DIM = FOURIER_DIM // FOURIER_GROUPS
WIN_HEADS = (D_MODEL // 2) // HEAD_DIM
WIN_KV_HEADS = WIN_HEADS // 4
MIX_IN_DIM = FOURIER_DIM + (WIN_HEADS + 2 * WIN_KV_HEADS) * HEAD_DIM
DIFF_HEADS = D_MODEL // (2 * HEAD_DIM)
FFN_DIM = 2816
N_EXPERTS = 8
TOP_K = 2
EXPERT_DIM = 7 * D_MODEL // 2

kernel_name = "hybrid_fourier_window_diffattn_moe_dit"


def rmsnorm(x, g):
    xf = x.astype(jnp.float32)
    y = xf * lax.rsqrt(jnp.mean(xf * xf, axis=-1, keepdims=True) + EPS)
    return (y * g.astype(jnp.float32)).astype(x.dtype)


def modulate(x, g, shift, scale):
    return rmsnorm(x, g) * (1 + scale) + shift


def rope_tables(rows_count):
    rows = jnp.repeat(jnp.arange(rows_count), GRID_W).astype(jnp.float32)
    cols = jnp.tile(jnp.arange(GRID_W), rows_count).astype(jnp.float32)
    axis_dim = HEAD_DIM // 2
    inv = ROPE_THETA ** (-jnp.arange(0, axis_dim, 2, dtype=jnp.float32) / axis_dim)
    ar = rows[:, None] * inv
    ac = cols[:, None] * inv
    return (jnp.cos(ar)[:, None, :], jnp.sin(ar)[:, None, :],
            jnp.cos(ac)[:, None, :], jnp.sin(ac)[:, None, :])


def rope_axis(p, cos, sin):
    p1, p2 = jnp.split(p, 2, axis=-1)
    return jnp.concatenate([p1 * cos - p2 * sin, p2 * cos + p1 * sin], axis=-1)


def apply_rope2d(x, rope):
    cos_r, sin_r, cos_c, sin_c = rope
    half = x.shape[-1] // 2
    out = jnp.concatenate([rope_axis(x[..., :half], cos_r, sin_r),
                           rope_axis(x[..., half:], cos_c, sin_c)], axis=-1)
    return out.astype(x.dtype)


def fourier_mix(u):
    uf = u.astype(jnp.float32)
    return jnp.real(jnp.fft.fft2(uf, axes=(1, 3), norm="ortho")).astype(u.dtype)


def sink_column(sink, lead_shape, hkv, g):
    s = sink.astype(jnp.float32).reshape((1,) * (len(lead_shape) - 3) + (hkv, g, 1, 1))
    return jnp.broadcast_to(s, lead_shape + (1,))


def window_attention(q, k, v, kc, vc, sink):
    B, L, Hq, d = q.shape
    Hkv = k.shape[2]
    G = Hq // Hkv
    C = kc.shape[1]
    nb = L // BLOCK
    scale = d ** -0.5
    qb = q.reshape(B, nb, BLOCK, Hkv, G, d)
    pad = ((0, 0), (BLOCK, BLOCK), (0, 0), (0, 0))
    kp = jnp.pad(k, pad).reshape(B, nb + 2, BLOCK, Hkv, d)
    vp = jnp.pad(v, pad).reshape(B, nb + 2, BLOCK, Hkv, d)
    kw = jnp.concatenate([kp[:, :-2], kp[:, 1:-1], kp[:, 2:]], axis=2)
    vw = jnp.concatenate([vp[:, :-2], vp[:, 1:-1], vp[:, 2:]], axis=2)
    blk = jnp.arange(nb)[:, None, None]
    qabs = blk * BLOCK + jnp.arange(BLOCK)[None, :, None]
    kabs = (blk - 1) * BLOCK + jnp.arange(3 * BLOCK)[None, None, :]
    valid = (jnp.abs(qabs - kabs) <= WINDOW) & (kabs >= 0) & (kabs < L)
    s_loc = jnp.einsum('bnqhgd,bnkhd->bnhgqk', qb, kw).astype(jnp.float32) * scale
    s_loc = jnp.where(valid[None, :, None, None], s_loc, NEG)
    s_ctx = jnp.einsum('bnqhgd,bkhd->bnhgqk', qb, kc).astype(jnp.float32) * scale
    s_sink = sink_column(sink, s_loc.shape[:-1], Hkv, G)
    p = jax.nn.softmax(jnp.concatenate([s_ctx, s_loc, s_sink], axis=-1), axis=-1)
    p_ctx = p[..., :C].astype(v.dtype)
    p_loc = p[..., C:C + 3 * BLOCK].astype(v.dtype)
    o = (jnp.einsum('bnhgqk,bkhd->bnqhgd', p_ctx, vc)
         + jnp.einsum('bnhgqk,bnkhd->bnqhgd', p_loc, vw))
    return o.reshape(B, L, Hq * d)


def context_gqa_attention(q, k, v, sink):
    B, C, Hq, d = q.shape
    Hkv = k.shape[2]
    G = Hq // Hkv
    qg = q.reshape(B, C, Hkv, G, d)
    s = jnp.einsum('bqhgd,bkhd->bhgqk', qg, k).astype(jnp.float32) * (d ** -0.5)
    s_sink = sink_column(sink, s.shape[:-1], Hkv, G)
    p = jax.nn.softmax(jnp.concatenate([s, s_sink], axis=-1), axis=-1)[..., :C]
    o = jnp.einsum('bhgqk,bkhd->bqhgd', p.astype(v.dtype), v)
    return o.reshape(B, C, Hq * d)


def fourier_window_mixer(hx, hc, w_in, w_out, sink, rope, need_ctx):
    def split_proj(h):
        B, L, _ = h.shape
        z = h @ w_in
        o1 = FOURIER_DIM
        o2 = o1 + WIN_HEADS * HEAD_DIM
        o3 = o2 + WIN_KV_HEADS * HEAD_DIM
        f = z[..., :o1].reshape(B, L, FOURIER_GROUPS, FOURIER_GROUP_DIM)
        q = z[..., o1:o2].reshape(B, L, WIN_HEADS, HEAD_DIM)
        k = z[..., o2:o3].reshape(B, L, WIN_KV_HEADS, HEAD_DIM)
        v = z[..., o3:].reshape(B, L, WIN_KV_HEADS, HEAD_DIM)
        return f, q, k, v
    fx, qx, kx, vx = split_proj(hx)
    fc, qc, kc, vc = split_proj(hc)
    B, L, _ = hx.shape
    qx = apply_rope2d(qx, rope)
    kx = apply_rope2d(kx, rope)
    ox = jnp.concatenate([fourier_mix(fx).reshape(B, L, FOURIER_DIM),
                          window_attention(qx, kx, vx, kc, vc, sink)], axis=-1) @ w_out
    if not need_ctx:
        return ox, None
    C = hc.shape[1]
    oc = jnp.concatenate([fourier_mix(fc).reshape(B, C, FOURIER_DIM),
                          context_gqa_attention(qc, kc, vc, sink)], axis=-1) @ w_out
    return ox, oc


def diff_attend(q, k, v, lam):
    s = jnp.einsum('bqhcd,bkhcd->bhcqk', q, k).astype(jnp.float32) * (q.shape[-1] ** -0.5)
    p = jax.nn.softmax(s, axis=-1)
    w = p[:, :, 0] - lam * p[:, :, 1]
    return jnp.einsum('bhqk,bkhe->bqhe', w.astype(v.dtype), v)


def diff_mixer(hx, hc, w_qkv, w_out, lam_vec, subln_g, lam_init, rope, need_ctx):
    def proj(h, use_rope):
        B, L, _ = h.shape
        q, k, v = jnp.split(h @ w_qkv, 3, axis=-1)
        q = q.reshape(B, L, 2 * DIFF_HEADS, HEAD_DIM)
        k = k.reshape(B, L, 2 * DIFF_HEADS, HEAD_DIM)
        if use_rope:
            q = apply_rope2d(q, rope)
            k = apply_rope2d(k, rope)
        q = q.reshape(B, L, DIFF_HEADS, 2, HEAD_DIM)
        k = k.reshape(B, L, DIFF_HEADS, 2, HEAD_DIM)
        v = v.reshape(B, L, DIFF_HEADS, 2 * HEAD_DIM)
        return q, k, v
    lv = lam_vec.astype(jnp.float32)
    lam = jnp.exp(jnp.sum(lv[0] * lv[1])) - jnp.exp(jnp.sum(lv[2] * lv[3])) + lam_init
    qx, kx, vx = proj(hx, True)
    qc, kc, vc = proj(hc, False)
    B, L, _ = hx.shape
    nb = L // BLOCK
    k_all = jnp.concatenate([kc, kx], axis=1)
    v_all = jnp.concatenate([vc, vx], axis=1)
    qb = jnp.moveaxis(qx.reshape(B, nb, BLOCK, DIFF_HEADS, 2, HEAD_DIM), 1, 0)
    ob = lax.map(lambda qblk: diff_attend(qblk, k_all, v_all, lam), qb)
    ox = jnp.moveaxis(ob, 0, 1).reshape(B, L, DIFF_HEADS, 2 * HEAD_DIM)
    ox = (rmsnorm(ox, subln_g) * (1.0 - lam_init)).reshape(B, L, D_MODEL) @ w_out
    if not need_ctx:
        return ox, None
    C = hc.shape[1]
    oc = diff_attend(qc, kc, vc, lam)
    oc = (rmsnorm(oc, subln_g) * (1.0 - lam_init)).reshape(B, C, D_MODEL) @ w_out
    return ox, oc


def swiglu(h, w_in, w_out):
    g, u = jnp.split(h @ w_in, 2, axis=-1)
    return (jax.nn.silu(g) * u) @ w_out


def moe_swiglu(h, router_w, w_in, w_out):
    logits = jnp.einsum('btd,de->bte', h, router_w).astype(jnp.float32)
    top_vals, top_idx = lax.top_k(logits, TOP_K)
    gates = jax.nn.softmax(top_vals, axis=-1)
    combine = jnp.sum(jax.nn.one_hot(top_idx, N_EXPERTS, dtype=jnp.float32) * gates[..., None], axis=-2)
    out = jnp.zeros_like(h)
    for e in range(N_EXPERTS):
        out = out + combine[..., e:e + 1].astype(h.dtype) * swiglu(h, w_in[e], w_out[e])
    return out


def lambda_init_fn(layer):
    return 0.8 - 0.6 * math.exp(-0.3 * layer)


def setup_inputs(seed: int = 0) -> dict:
    key = jax.random.key(seed)
    ks = jax.random.split(key, 20)
    n_even = (DEPTH + 1) // 2
    n_odd = DEPTH // 2
    f32 = jnp.float32

    def w(k, shape, fan_in, gain=1.0):
        return jax.random.normal(k, shape, f32) * (gain * fan_in ** -0.5)

    return {
        "x": jax.random.normal(ks[0], (BATCH, SEQ, D_MODEL), f32),
        "c": jax.random.normal(ks[1], (BATCH, D_MODEL), f32),
        "ctx": jax.random.normal(ks[2], (BATCH, CTX_LEN, D_MODEL), f32),
        "c_ctx": jax.random.normal(ks[3], (D_MODEL,), f32),
        "ada_w": w(ks[4], (DEPTH, D_MODEL, 6 * D_MODEL), D_MODEL, 0.5),
        "ada_b": 0.01 * jax.random.normal(ks[5], (DEPTH, 6 * D_MODEL), f32),
        "norm_g": 1.0 + 0.01 * jax.random.normal(ks[6], (DEPTH, 4, D_MODEL), f32),
        "mix_in_w": w(ks[7], (n_even, D_MODEL, MIX_IN_DIM), D_MODEL),
        "mix_out_w": w(ks[8], (n_even, D_MODEL, D_MODEL), D_MODEL),
        "win_sink": 0.5 * jax.random.normal(ks[9], (n_even, WIN_HEADS), f32),
        "diff_qkv_w": w(ks[10], (n_odd, D_MODEL, 3 * D_MODEL), D_MODEL),
        "diff_out_w": w(ks[11], (n_odd, D_MODEL, D_MODEL), D_MODEL),
        "diff_lambda": 0.1 * jax.random.normal(ks[12], (n_odd, 4, HEAD_DIM), f32),
        "diff_subln_g": 1.0 + 0.01 * jax.random.normal(ks[13], (n_odd, 2 * HEAD_DIM), f32),
        "ffn_in_w": w(ks[14], (n_even, D_MODEL, 2 * FFN_DIM), D_MODEL),
        "ffn_out_w": w(ks[15], (n_even, FFN_DIM, D_MODEL), FFN_DIM),
        "router_w": w(ks[16], (n_odd, D_MODEL, N_EXPERTS), D_MODEL),
        "expert_in_w": w(ks[17], (n_odd, N_EXPERTS, D_MODEL, 2 * EXPERT_DIM), D_MODEL),
        "expert_out_w": w(ks[18], (n_odd, N_EXPERTS, EXPERT_DIM, D_MODEL), EXPERT_DIM),
    }


def reference(x, c, ctx, c_ctx, ada_w, ada_b, norm_g, mix_in_w, mix_out_w, win_sink,
              diff_qkv_w, diff_out_w, diff_lambda, diff_subln_g,
              ffn_in_w, ffn_out_w, router_w, expert_in_w, expert_out_w):
    n_tok = x.shape[1]
    ROWS = n_tok // GRID_W
    rope = rope_tables(ROWS)
    for layer in range(DEPTH):
        j = layer // 2
        even = layer % 2 == 0
        need_ctx = layer < DEPTH - 1
        mod_x = (jax.nn.silu(c) @ ada_w[layer] + ada_b[layer])[:, None, :]
        mod_c = jax.nn.silu(c_ctx) @ ada_w[layer] + ada_b[layer]
        sh1x, sc1x, g1x, sh2x, sc2x, g2x = jnp.split(mod_x, 6, axis=-1)
        sh1c, sc1c, g1c, sh2c, sc2c, g2c = jnp.split(mod_c, 6, axis=-1)

        def channel(h):
            if even:
                return swiglu(h, ffn_in_w[j], ffn_out_w[j])
            return moe_swiglu(h, router_w[j], expert_in_w[j], expert_out_w[j])

        hx = modulate(x, norm_g[layer, 0], sh1x, sc1x)
        hc = modulate(ctx, norm_g[layer, 0], sh1c, sc1c)
        if even:
            yx, yc = fourier_window_mixer(hx, hc, mix_in_w[j], mix_out_w[j], win_sink[j], rope, need_ctx)
        else:
            yx, yc = diff_mixer(hx, hc, diff_qkv_w[j], diff_out_w[j], diff_lambda[j], diff_subln_g[j],
                                lambda_init_fn(layer), rope, need_ctx)
        x = x + g1x * rmsnorm(yx, norm_g[layer, 1])
        hx = modulate(x, norm_g[layer, 2], sh2x, sc2x)
        x = x + g2x * rmsnorm(channel(hx), norm_g[layer, 3])
        if need_ctx:
            ctx = ctx + g1c * rmsnorm(yc, norm_g[layer, 1])
            hc = modulate(ctx, norm_g[layer, 2], sh2c, sc2c)
            ctx = ctx + g2c * rmsnorm(channel(hc), norm_g[layer, 3])
    return x
```

```python
import functools
import math

import numpy as np
import jax
import jax.numpy as jnp
from jax import lax
from jax.experimental import pallas as pl
from jax.experimental.pallas import tpu as pltpu

EPS = 1e-6
NEG = -1e30
HEAD_DIM = 64
LANES = 128
GRID_W = 64
BLOCK = 128
ROPE_THETA = 10000.0
N_EXPERTS = 8
F32 = jnp.float32
BF16 = jnp.bfloat16
HIGHEST = lax.Precision.HIGHEST
VMEM_LIMIT = 56 * 1024 * 1024


def _cparams(*sem):
    return pltpu.CompilerParams(dimension_semantics=sem, vmem_limit_bytes=VMEM_LIMIT)


def _dot(a, b):
    return jnp.dot(a, b, preferred_element_type=F32)


def _dot_nt(a, b):
    return lax.dot_general(a, b, (((1,), (1,)), ((), ())), preferred_element_type=F32)


def _dot_hi(a, b):
    return jnp.dot(a, b, precision=HIGHEST, preferred_element_type=F32)


def _rms(v):
    return v * lax.rsqrt(jnp.mean(v * v, axis=-1, keepdims=True) + EPS)


def _modulate(x, g, sh, sc):
    return _rms(x) * g * (1.0 + sc) + sh


def _pick_tile(pref, *dims):
    t = pref
    while any(d % t for d in dims):
        t //= 2
    return t


def _modvec_kernel(c_ref, w_ref, b_ref, o_ref):
    cv = c_ref[...]
    s = cv * (1.0 / (1.0 + jnp.exp(-cv)))
    o_ref[...] = _dot(s.astype(BF16), w_ref[...].astype(BF16)) + b_ref[...]


def _modvec(cv, ada_w, ada_b):
    depth, d, n = ada_w.shape
    r = cv.shape[0]
    tn = _pick_tile(1536, n)
    return pl.pallas_call(
        _modvec_kernel,
        grid=(depth, n // tn),
        in_specs=[
            pl.BlockSpec((r, d), lambda l, j: (0, 0)),
            pl.BlockSpec((None, d, tn), lambda l, j: (l, 0, j)),
            pl.BlockSpec((None, 1, tn), lambda l, j: (l, 0, j)),
        ],
        out_specs=pl.BlockSpec((None, r, tn), lambda l, j: (l, 0, j)),
        out_shape=jax.ShapeDtypeStruct((depth, r, n), F32),
        compiler_params=_cparams("parallel", "parallel"),
        name="modvec",
    )(cv, ada_w, ada_b.reshape(depth, 1, n))


def _proj_kernel(x_ref, g_ref, sh_ref, sc_ref, w_ref, cos_ref, sa_ref, sb_ref, *o_refs, plan, group):
    h = _modulate(x_ref[...], g_ref[...], sh_ref[...], sc_ref[...]).astype(BF16)
    n = w_ref.shape[1]
    for g0 in range(0, n, group):
        acc = _dot(h, w_ref[:, g0:g0 + group])
        for c0 in range(0, group, LANES):
            oi, oc, mode = plan[(g0 + c0) // LANES]
            v = acc[:, c0:c0 + LANES]
            if mode != "plain":
                v = (v * cos_ref[...] + pltpu.roll(v, LANES - 16, 1) * sa_ref[...]
                     + pltpu.roll(v, 16, 1) * sb_ref[...])
                if mode == "rope_q":
                    v = v * (HEAD_DIM ** -0.5)
            o_refs[oi][:, oc:oc + LANES] = v.astype(o_refs[oi].dtype)


def _proj(xt, normg, mod, k_sh, k_sc, w, rope, plan, outs, *, tm, n_lat, seq, n_batch):
    t, d = xt.shape
    n = w.shape[1]
    group = _pick_tile(512, n)
    nx = n_lat // tm
    per = seq // tm

    def mrow(i):
        return jnp.minimum(i * tm // seq, n_batch)

    def rrow(i):
        return jnp.where(i < nx, i % per, per)

    in_specs = [
        pl.BlockSpec((tm, d), lambda i: (i, 0)),
        pl.BlockSpec((None, 1, d), lambda i: (0, 0, 0)),
        pl.BlockSpec((None, None, 1, d), lambda i: (mrow(i), k_sh, 0, 0)),
        pl.BlockSpec((None, None, 1, d), lambda i: (mrow(i), k_sc, 0, 0)),
        pl.BlockSpec((d, n), lambda i: (0, 0)),
        pl.BlockSpec((tm, LANES), lambda i: (rrow(i), 0)),
        pl.BlockSpec((tm, LANES), lambda i: (rrow(i), 0)),
        pl.BlockSpec((tm, LANES), lambda i: (rrow(i), 0)),
    ]
    out_specs = [pl.BlockSpec((tm, wd), lambda i: (i, 0)) for wd, _ in outs]
    out_shape = [jax.ShapeDtypeStruct((t, wd), dt) for wd, dt in outs]
    return pl.pallas_call(
        functools.partial(_proj_kernel, plan=plan, group=group),
        grid=(t // tm,),
        in_specs=in_specs,
        out_specs=out_specs,
        out_shape=out_shape,
        compiler_params=_cparams("parallel"),
        name="proj",
    )(xt, normg, mod, mod, w, *rope)


def _rope_tables(seq, tm):
    rows_count = seq // GRID_W
    rows = jnp.repeat(jnp.arange(rows_count), GRID_W).astype(F32)
    cols = jnp.tile(jnp.arange(GRID_W), rows_count).astype(F32)
    axis_dim = HEAD_DIM // 2
    inv = ROPE_THETA ** (-jnp.arange(0, axis_dim, 2, dtype=F32) / axis_dim)
    ar = rows[:, None] * inv
    ac = cols[:, None] * inv
    cr, sr, cc, sc = jnp.cos(ar), jnp.sin(ar), jnp.cos(ac), jnp.sin(ac)
    z = jnp.zeros_like(sr)
    reps = LANES // HEAD_DIM
    cos = jnp.tile(jnp.concatenate([cr, cr, cc, cc], axis=1), (1, reps))
    sa = jnp.tile(jnp.concatenate([-sr, z, -sc, z], axis=1), (1, reps))
    sb = jnp.tile(jnp.concatenate([z, sr, z, sc], axis=1), (1, reps))
    ident = jnp.ones((tm, LANES), F32)
    zero = jnp.zeros((tm, LANES), F32)
    return (jnp.concatenate([cos, ident]), jnp.concatenate([sa, zero]), jnp.concatenate([sb, zero]))


def _fourier_tables(seq):
    n2 = GRID_W
    n1 = seq // n2
    norm = 1.0 / math.sqrt(seq * LANES)
    a = np.arange(n1)
    k1 = np.arange(n1)
    b = np.arange(n2)
    ang = (b[:, None, None] * k1[None, :, None] + (seq // n1) * k1[None, :, None] * a[None, None, :]) % seq
    th = 2.0 * np.pi * ang / seq
    m1 = np.concatenate([np.cos(th), -np.sin(th)], axis=1).astype(np.float32)
    ph = 2.0 * np.pi * ((b[:, None] * b[None, :]) % n2) / n2
    c2, s2 = np.cos(ph), np.sin(ph)
    g2 = np.block([[c2, s2], [-s2, c2]]).astype(np.float32)
    ch = np.arange(LANES)
    pc = 2.0 * np.pi * ((ch[:, None] * ch[None, :]) % LANES) / LANES
    cc = (np.cos(pc) * norm).astype(np.float32)
    sc = (np.sin(pc) * norm).astype(np.float32)
    return jnp.asarray(m1), jnp.asarray(g2), jnp.asarray(cc), jnp.asarray(sc)


def _fourier_kernel(u_ref, m1_ref, g2_ref, cc_ref, sc_ref, o_ref, b_scr, y_scr, *, n1, n2):
    for b in range(n2):
        xs = u_ref[pl.ds(b, n1, stride=n2), :]
        z = _dot_hi(m1_ref[b], xs)
        b_scr[pl.ds(b, n1, stride=2 * n2), :] = z[:n1]
        b_scr[pl.ds(n2 + b, n1, stride=2 * n2), :] = z[n1:]
    for k1 in range(n1):
        bk = b_scr[2 * n2 * k1:2 * n2 * (k1 + 1), :]
        xk = _dot_hi(g2_ref[...], bk)
        y = _dot_hi(xk[:n2], cc_ref[...]) + _dot_hi(xk[n2:], sc_ref[...])
        y_scr[pl.ds(k1, n2, stride=n1), :] = y
    o_ref[...] = y_scr[...].astype(o_ref.dtype)


def _fourier(f, tables, *, n_batch, seq):
    m1, g2, cc, sc = tables
    n2 = GRID_W
    n1 = seq // n2
    groups = f.shape[1] // LANES
    return pl.pallas_call(
        functools.partial(_fourier_kernel, n1=n1, n2=n2),
        grid=(n_batch, groups),
        in_specs=[
            pl.BlockSpec((seq, LANES), lambda b, g: (b, g)),
            pl.BlockSpec(m1.shape, lambda b, g: (0, 0, 0)),
            pl.BlockSpec(g2.shape, lambda b, g: (0, 0)),
            pl.BlockSpec(cc.shape, lambda b, g: (0, 0)),
            pl.BlockSpec(sc.shape, lambda b, g: (0, 0)),
        ],
        out_specs=pl.BlockSpec((seq, LANES), lambda b, g: (b, g)),
        out_shape=jax.ShapeDtypeStruct((n_batch * seq, f.shape[1]), BF16),
        scratch_shapes=[pltpu.VMEM((2 * seq, LANES), F32), pltpu.VMEM((seq, LANES), F32)],
        compiler_params=_cparams("parallel", "parallel"),
        name="fourier",
    )(f, m1, g2, cc, sc)


def _dense_fourier_tables(n):
    norm = 1.0 / math.sqrt(n * LANES)
    p = np.arange(n)
    ph = 2.0 * np.pi * ((p[:, None] * p[None, :]) % n) / n
    ch = np.arange(LANES)
    pc = 2.0 * np.pi * ((ch[:, None] * ch[None, :]) % LANES) / LANES
    return (jnp.asarray(np.cos(ph).astype(np.float32)), jnp.asarray(np.sin(ph).astype(np.float32)),
            jnp.asarray((np.cos(pc) * norm).astype(np.float32)), jnp.asarray((np.sin(pc) * norm).astype(np.float32)))


def _dense_fourier_kernel(u_ref, cl_ref, sl_ref, cc_ref, sc_ref, o_ref):
    u = u_ref[...]
    y = _dot_hi(cl_ref[...], _dot_hi(u, cc_ref[...])) - _dot_hi(sl_ref[...], _dot_hi(u, sc_ref[...]))
    o_ref[...] = y.astype(o_ref.dtype)


def _dense_fourier(f, tables, *, n_batch, n_pos, row_block0):
    cl, sl, cc, sc = tables
    groups = f.shape[1] // LANES
    return pl.pallas_call(
        _dense_fourier_kernel,
        grid=(n_batch, groups),
        in_specs=[
            pl.BlockSpec((n_pos, LANES), lambda b, g: (row_block0 + b, g)),
            pl.BlockSpec(cl.shape, lambda b, g: (0, 0)),
            pl.BlockSpec(sl.shape, lambda b, g: (0, 0)),
            pl.BlockSpec(cc.shape, lambda b, g: (0, 0)),
            pl.BlockSpec(sc.shape, lambda b, g: (0, 0)),
        ],
        out_specs=pl.BlockSpec((n_pos, LANES), lambda b, g: (b, g)),
        out_shape=jax.ShapeDtypeStruct((n_batch * n_pos, f.shape[1]), BF16),
        compiler_params=_cparams("parallel", "parallel"),
        name="fourier_ctx",
    )(f, cl, sl, cc, sc)


def _win_kernel(sink_ref, q_ref, *refs, n_blocks, has_local, n_heads, group_size):
    if has_local:
        kp_ref, kc_ref, kn_ref, vp_ref, vc_ref, vn_ref, kx_ref, vx_ref, o_ref = refs
    else:
        kx_ref, vx_ref, o_ref = refs
    tq = q_ref.shape[0]
    lane = lax.broadcasted_iota(jnp.int32, (1, LANES), 1)
    half_mask = [lane < HEAD_DIM, lane >= HEAD_DIM]
    kx = kx_ref[...]
    vx = vx_ref[...]
    if has_local:
        n = pl.program_id(1)
        qi = lax.broadcasted_iota(jnp.int32, (tq, BLOCK), 0)
        kj = lax.broadcasted_iota(jnp.int32, (tq, BLOCK), 1)
        valid_prev = (kj >= qi) & (n >= 1)
        valid_next = (kj <= qi) & (n <= n_blocks - 2)
        kp, kc, kn = kp_ref[...], kc_ref[...], kn_ref[...]
        vp, vc, vn = vp_ref[...], vc_ref[...], vn_ref[...]
    for pair in range(n_heads // 2):
        qp = q_ref[:, pair * LANES:(pair + 1) * LANES].astype(F32)
        qp_sw = pltpu.roll(qp, HEAD_DIM, 1)
        out_pair = jnp.zeros((tq, LANES), F32)
        for half in range(2):
            head = 2 * pair + half
            kv = head // group_size
            src = qp if half == kv else qp_sw
            qe = jnp.where(half_mask[kv], src, 0.0).astype(BF16)
            sink = sink_ref[head]
            s_x = _dot_nt(qe, kx)
            m = jnp.maximum(jnp.max(s_x, axis=-1, keepdims=True), sink)
            if has_local:
                s_p = jnp.where(valid_prev, _dot_nt(qe, kp), NEG)
                s_c = _dot_nt(qe, kc)
                s_n = jnp.where(valid_next, _dot_nt(qe, kn), NEG)
                m = jnp.maximum(m, jnp.max(s_p, axis=-1, keepdims=True))
                m = jnp.maximum(m, jnp.max(s_c, axis=-1, keepdims=True))
                m = jnp.maximum(m, jnp.max(s_n, axis=-1, keepdims=True))
            p_x = jnp.exp(s_x - m)
            den = jnp.sum(p_x, axis=-1, keepdims=True) + jnp.exp(sink - m)
            pv = _dot(p_x.astype(BF16), vx)
            if has_local:
                for s_l, v_l in ((s_p, vp), (s_c, vc), (s_n, vn)):
                    p_l = jnp.exp(s_l - m)
                    den = den + jnp.sum(p_l, axis=-1, keepdims=True)
                    pv = pv + _dot(p_l.astype(BF16), v_l)
            pv = pv / den
            if half != kv:
                pv = pltpu.roll(pv, HEAD_DIM, 1)
            out_pair = jnp.where(half_mask[half], pv, out_pair)
        o_ref[:, pair * LANES:(pair + 1) * LANES] = out_pair.astype(o_ref.dtype)


def _win_attention(z, sink, *, n_batch, seq, n_ctx, q_cols, k_col, v_col):
    nb = seq // BLOCK
    ctx_blk0 = n_batch * seq // n_ctx
    n_heads = q_cols // HEAD_DIM
    group_size = n_heads // (LANES // HEAD_DIM)

    def loc(col, off):
        return pl.BlockSpec(
            (BLOCK, LANES), lambda b, n: (b * nb + jnp.clip(n + off, 0, nb - 1), col))

    return pl.pallas_call(
        functools.partial(_win_kernel, n_blocks=nb, has_local=True, n_heads=n_heads, group_size=group_size),
        grid=(n_batch, nb),
        in_specs=[
            pl.BlockSpec(memory_space=pltpu.SMEM),
            pl.BlockSpec((BLOCK, q_cols), lambda b, n: (b * nb + n, 0)),
            loc(k_col, -1), loc(k_col, 0), loc(k_col, 1),
            loc(v_col, -1), loc(v_col, 0), loc(v_col, 1),
            pl.BlockSpec((n_ctx, LANES), lambda b, n: (ctx_blk0 + b, k_col)),
            pl.BlockSpec((n_ctx, LANES), lambda b, n: (ctx_blk0 + b, v_col)),
        ],
        out_specs=pl.BlockSpec((BLOCK, q_cols), lambda b, n: (b * nb + n, 0)),
        out_shape=jax.ShapeDtypeStruct((n_batch * seq, q_cols), BF16),
        compiler_params=_cparams("parallel", "parallel"),
        name="win_attn",
    )(sink, z, z, z, z, z, z, z, z, z)


def _ctx_gqa_attention(z, sink, *, n_batch, seq, n_ctx, q_cols, k_col, v_col):
    ctx_blk0 = n_batch * seq // n_ctx
    n_heads = q_cols // HEAD_DIM
    group_size = n_heads // (LANES // HEAD_DIM)
    return pl.pallas_call(
        functools.partial(_win_kernel, n_blocks=0, has_local=False, n_heads=n_heads, group_size=group_size),
        grid=(n_batch,),
        in_specs=[
            pl.BlockSpec(memory_space=pltpu.SMEM),
            pl.BlockSpec((n_ctx, q_cols), lambda b: (ctx_blk0 + b, 0)),
            pl.BlockSpec((n_ctx, LANES), lambda b: (ctx_blk0 + b, k_col)),
            pl.BlockSpec((n_ctx, LANES), lambda b: (ctx_blk0 + b, v_col)),
        ],
        out_specs=pl.BlockSpec((n_ctx, q_cols), lambda b: (b, 0)),
        out_shape=jax.ShapeDtypeStruct((n_batch * n_ctx, q_cols), BF16),
        compiler_params=_cparams("parallel"),
        name="ctx_gqa",
    )(sink, z, z, z)


def _diff_kernel(lam_ref, g_ref, q_ref, *refs, lam_init, has_x):
    if has_x:
        kx_ref, vx_ref, kc_ref, vc_ref, o_ref = refs
    else:
        kc_ref, vc_ref, o_ref = refs
    lv = lam_ref[...]
    lam = (jnp.exp(jnp.sum(lv[0:1] * lv[1:2], axis=-1, keepdims=True))
           - jnp.exp(jnp.sum(lv[2:3] * lv[3:4], axis=-1, keepdims=True)) + lam_init)
    q = q_ref[...]
    lane = lax.broadcasted_iota(jnp.int32, (1, LANES), 1)
    kc = kc_ref[...]
    vc = vc_ref[...]

    def component(qm):
        s_c = _dot_nt(qm, kc)
        m = jnp.max(s_c, axis=-1, keepdims=True)
        if has_x:
            s_x = _dot_nt(qm, kx_ref[...])
            m = jnp.maximum(m, jnp.max(s_x, axis=-1, keepdims=True))
        p_c = jnp.exp(s_c - m)
        den = jnp.sum(p_c, axis=-1, keepdims=True)
        acc = _dot(p_c.astype(BF16), vc)
        if has_x:
            p_x = jnp.exp(s_x - m)
            den = den + jnp.sum(p_x, axis=-1, keepdims=True)
            acc = acc + _dot(p_x.astype(BF16), vx_ref[...])
        return acc / den

    zero = jnp.zeros_like(q)
    o = component(jnp.where(lane < HEAD_DIM, q, zero)) - lam * component(jnp.where(lane >= HEAD_DIM, q, zero))
    o = _rms(o) * g_ref[...] * (1.0 - lam_init)
    o_ref[...] = o.astype(o_ref.dtype)


def _diff_attention(z, lam_vec, subln_g, lam_init, *, n_batch, seq, n_ctx, n_heads, tq):
    ctx_blk0 = n_batch * seq // n_ctx
    nq = seq // tq
    return pl.pallas_call(
        functools.partial(_diff_kernel, lam_init=lam_init, has_x=True),
        grid=(n_batch, n_heads, nq),
        in_specs=[
            pl.BlockSpec(lam_vec.shape, lambda b, h, i: (0, 0)),
            pl.BlockSpec((1, LANES), lambda b, h, i: (0, 0)),
            pl.BlockSpec((tq, LANES), lambda b, h, i: (b * nq + i, h)),
            pl.BlockSpec((seq, LANES), lambda b, h, i: (b, n_heads + h)),
            pl.BlockSpec((seq, LANES), lambda b, h, i: (b, 2 * n_heads + h)),
            pl.BlockSpec((n_ctx, LANES), lambda b, h, i: (ctx_blk0 + b, n_heads + h)),
            pl.BlockSpec((n_ctx, LANES), lambda b, h, i: (ctx_blk0 + b, 2 * n_heads + h)),
        ],
        out_specs=pl.BlockSpec((tq, LANES), lambda b, h, i: (b * nq + i, h)),
        out_shape=jax.ShapeDtypeStruct((n_batch * seq, n_heads * LANES), BF16),
        compiler_params=_cparams("parallel", "parallel", "parallel"),
        name="diff_attn",
    )(lam_vec, subln_g, z, z, z, z, z)


def _diff_attention_ctx(z, lam_vec, subln_g, lam_init, *, n_batch, seq, n_ctx, n_heads):
    ctx_blk0 = n_batch * seq // n_ctx
    return pl.pallas_call(
        functools.partial(_diff_kernel, lam_init=lam_init, has_x=False),
        grid=(n_batch, n_heads),
        in_specs=[
            pl.BlockSpec(lam_vec.shape, lambda b, h: (0, 0)),
            pl.BlockSpec((1, LANES), lambda b, h: (0, 0)),
            pl.BlockSpec((n_ctx, LANES), lambda b, h: (ctx_blk0 + b, h)),
            pl.BlockSpec((n_ctx, LANES), lambda b, h: (ctx_blk0 + b, n_heads + h)),
            pl.BlockSpec((n_ctx, LANES), lambda b, h: (ctx_blk0 + b, 2 * n_heads + h)),
        ],
        out_specs=pl.BlockSpec((n_ctx, LANES), lambda b, h: (b, h)),
        out_shape=jax.ShapeDtypeStruct((n_batch * n_ctx, n_heads * LANES), BF16),
        compiler_params=_cparams("parallel", "parallel"),
        name="diff_attn_ctx",
    )(lam_vec, subln_g, z, z, z)


def _outproj_kernel(a0_ref, a1_ref, w_ref, x_ref, g_ref, gate_ref, o_ref):
    half = a0_ref.shape[1]
    y = _dot(a0_ref[...], w_ref[:half, :]) + _dot(a1_ref[...], w_ref[half:, :])
    o_ref[...] = x_ref[...] + gate_ref[...] * (_rms(y) * g_ref[...])


def _outproj(a0, a0_col, a1, a1_col, w, xt, normg, mod, k_gate, *, tm, rows, seq, n_batch):
    d = xt.shape[1]
    half = d // 2

    def mrow(i):
        return jnp.minimum(i * tm // seq, n_batch)

    return pl.pallas_call(
        _outproj_kernel,
        grid=(rows // tm,),
        in_specs=[
            pl.BlockSpec((tm, half), lambda i: (i, a0_col)),
            pl.BlockSpec((tm, half), lambda i: (i, a1_col)),
            pl.BlockSpec((d, d), lambda i: (0, 0)),
            pl.BlockSpec((tm, d), lambda i: (i, 0)),
            pl.BlockSpec((None, 1, d), lambda i: (1, 0, 0)),
            pl.BlockSpec((None, None, 1, d), lambda i: (mrow(i), k_gate, 0, 0)),
        ],
        out_specs=pl.BlockSpec((tm, d), lambda i: (i, 0)),
        out_shape=jax.ShapeDtypeStruct((rows, d), F32),
        compiler_params=_cparams("parallel"),
        name="outproj",
    )(a0, a1, w, xt, normg, mod)


def _router_kernel(x_ref, g_ref, sh_ref, sc_ref, rw_ref, o_ref):
    h = _modulate(x_ref[...], g_ref[...], sh_ref[...], sc_ref[...])
    logits = _dot_hi(h, rw_ref[...])
    lane = lax.broadcasted_iota(jnp.int32, logits.shape, 1)
    ninf = -jnp.inf
    logits = jnp.where(lane < N_EXPERTS, logits, ninf)
    m1 = jnp.max(logits, axis=-1, keepdims=True)
    i1 = jnp.min(jnp.where(logits == m1, lane, LANES), axis=-1, keepdims=True)
    sel1 = lane == i1
    rest = jnp.where(sel1, ninf, logits)
    m2 = jnp.max(rest, axis=-1, keepdims=True)
    i2 = jnp.min(jnp.where(rest == m2, lane, LANES), axis=-1, keepdims=True)
    sel2 = lane == i2
    e2 = jnp.exp(m2 - m1)
    den = 1.0 + e2
    o_ref[...] = jnp.where(sel1, 1.0 / den, 0.0) + jnp.where(sel2, e2 / den, 0.0)


def _router(xt, normg, mod, router_w, *, tm, rows, seq, n_batch):
    d = xt.shape[1]
    rw = jnp.zeros((d, LANES), F32).at[:, :N_EXPERTS].set(router_w)

    def mrow(i):
        return jnp.minimum(i * tm // seq, n_batch)

    return pl.pallas_call(
        _router_kernel,
        grid=(rows // tm,),
        in_specs=[
            pl.BlockSpec((tm, d), lambda i: (i, 0)),
            pl.BlockSpec((None, 1, d), lambda i: (2, 0, 0)),
            pl.BlockSpec((None, None, 1, d), lambda i: (mrow(i), 3, 0, 0)),
            pl.BlockSpec((None, None, 1, d), lambda i: (mrow(i), 4, 0, 0)),
            pl.BlockSpec((d, LANES), lambda i: (0, 0)),
        ],
        out_specs=pl.BlockSpec((tm, LANES), lambda i: (i, 0)),
        out_shape=jax.ShapeDtypeStruct((rows, LANES), F32),
        compiler_params=_cparams("parallel"),
        name="router",
    )(xt, normg, mod, mod, rw)


def _ffn_kernel(*refs, n_exp, n_f, weighted):
    if weighted:
        x_ref, g2_ref, sh_ref, sc_ref, wg_ref, wu_ref, wo_ref, g3_ref, gate_ref, comb_ref, o_ref, h_scr, acc = refs
    else:
        x_ref, g2_ref, sh_ref, sc_ref, wg_ref, wu_ref, wo_ref, g3_ref, gate_ref, o_ref, h_scr, acc = refs
    e = pl.program_id(1)
    f = pl.program_id(2)

    @pl.when((e == 0) & (f == 0))
    def _():
        h_scr[...] = _modulate(x_ref[...], g2_ref[...], sh_ref[...], sc_ref[...]).astype(BF16)
        acc[...] = jnp.zeros_like(acc)

    h = h_scr[...]
    gp = _dot(h, wg_ref[...])
    up = _dot(h, wu_ref[...])
    a = gp * (1.0 / (1.0 + jnp.exp(-gp))) * up
    if weighted:
        lane = lax.broadcasted_iota(jnp.int32, comb_ref.shape, 1)
        cw = jnp.sum(jnp.where(lane == e, comb_ref[...], 0.0), axis=-1, keepdims=True)
        a = a * cw
    acc[...] += _dot(a.astype(BF16), wo_ref[...])

    @pl.when((e == n_exp - 1) & (f == n_f - 1))
    def _():
        o_ref[...] = x_ref[...] + gate_ref[...] * (_rms(acc[...]) * g3_ref[...])


def _ffn(xt, normg, mod, w_in, w_out, comb, *, tm, tf, rows, seq, n_batch):
    d = xt.shape[1]
    n_exp, fdim = w_out.shape[0], w_out.shape[1]
    n_f = fdim // tf
    weighted = comb is not None

    def mrow(i):
        return jnp.minimum(i * tm // seq, n_batch)

    in_specs = [
        pl.BlockSpec((tm, d), lambda i, e, f: (i, 0)),
        pl.BlockSpec((None, 1, d), lambda i, e, f: (2, 0, 0)),
        pl.BlockSpec((None, None, 1, d), lambda i, e, f: (mrow(i), 3, 0, 0)),
        pl.BlockSpec((None, None, 1, d), lambda i, e, f: (mrow(i), 4, 0, 0)),
        pl.BlockSpec((None, d, tf), lambda i, e, f: (e, 0, f)),
        pl.BlockSpec((None, d, tf), lambda i, e, f: (e, 0, n_f + f)),
        pl.BlockSpec((None, tf, d), lambda i, e, f: (e, f, 0)),
        pl.BlockSpec((None, 1, d), lambda i, e, f: (3, 0, 0)),
        pl.BlockSpec((None, None, 1, d), lambda i, e, f: (mrow(i), 5, 0, 0)),
    ]
    args = [xt, normg, mod, mod, w_in, w_in, w_out, normg, mod]
    if weighted:
        in_specs.append(pl.BlockSpec((tm, LANES), lambda i, e, f: (i, 0)))
        args.append(comb)
    return pl.pallas_call(
        functools.partial(_ffn_kernel, n_exp=n_exp, n_f=n_f, weighted=weighted),
        grid=(rows // tm, n_exp, n_f),
        in_specs=in_specs,
        out_specs=pl.BlockSpec((tm, d), lambda i, e, f: (i, 0)),
        out_shape=jax.ShapeDtypeStruct((rows, d), F32),
        scratch_shapes=[pltpu.VMEM((tm, d), BF16), pltpu.VMEM((tm, d), F32)],
        compiler_params=_cparams("parallel", "arbitrary", "arbitrary"),
        name="moe" if weighted else "ffn",
    )(*args)


def _lambda_init(layer):
    return 0.8 - 0.6 * math.exp(-0.3 * layer)


def kernel(x, c, ctx, c_ctx, ada_w, ada_b, norm_g, mix_in_w, mix_out_w, win_sink, diff_qkv_w, diff_out_w,
           diff_lambda, diff_subln_g, ffn_in_w, ffn_out_w, router_w, expert_in_w, expert_out_w):
    n_batch, seq, d = x.shape
    n_ctx = ctx.shape[1]
    depth = ada_w.shape[0]
    n_lat = n_batch * seq
    n_all = n_lat + n_batch * n_ctx
    fdim = mix_in_w.shape[2] - (d // 2 + 2 * LANES)
    q_cols = d // 2
    n_diff_heads = d // LANES

    tm_proj = _pick_tile(512, seq, n_batch * n_ctx)
    tm_out = _pick_tile(512, seq, n_batch * n_ctx)
    tm_ffn = _pick_tile(512, seq, n_batch * n_ctx)
    tm_moe = _pick_tile(1024, seq, n_batch * n_ctx)
    tq_diff = _pick_tile(256, seq)
    common = dict(seq=seq, n_batch=n_batch)

    xt = jnp.concatenate([x.reshape(n_lat, d), ctx.reshape(n_batch * n_ctx, d)], axis=0)
    n_mod = -(-(n_batch + 1) // 8) * 8
    cv = jnp.zeros((n_mod, d), F32).at[:n_batch].set(c).at[n_batch].set(c_ctx)
    mods = _modvec(cv, ada_w, ada_b).reshape(depth, n_mod, 6, 1, d)
    rope = _rope_tables(seq, tm_proj)
    f_tables = _fourier_tables(seq)
    fc_tables = _dense_fourier_tables(n_ctx)

    n_f = fdim // LANES
    plan_even = ([(0, i * LANES, "plain") for i in range(n_f)]
                 + [(1, i * LANES, "rope_q") for i in range(q_cols // LANES)]
                 + [(1, q_cols, "rope_k"), (1, q_cols + LANES, "plain")])
    plan_odd = ([(0, i * LANES, "rope_q") for i in range(n_diff_heads)]
                + [(0, d + i * LANES, "rope_k") for i in range(n_diff_heads)]
                + [(0, 2 * d + i * LANES, "plain") for i in range(n_diff_heads)])

    for layer in range(depth):
        j = layer // 2
        need_ctx = layer < depth - 1
        rows = n_all if need_ctx else n_lat
        mod = mods[layer]
        ng = norm_g[layer].reshape(4, 1, d)
        if layer % 2 == 0:
            f, z = _proj(xt, ng, mod, 0, 1, mix_in_w[j].astype(BF16), rope, plan_even,
                         [(fdim, F32), (q_cols + 2 * LANES, BF16)], tm=tm_proj, n_lat=n_lat, **common)
            k_col, v_col = q_cols // LANES, q_cols // LANES + 1
            att = dict(n_batch=n_batch, seq=seq, n_ctx=n_ctx, q_cols=q_cols, k_col=k_col, v_col=v_col)
            mix_f = _fourier(f, f_tables, n_batch=n_batch, seq=seq)
            mix_a = _win_attention(z, win_sink[j], **att)
            if need_ctx:
                mix_f = jnp.concatenate(
                    [mix_f, _dense_fourier(f, fc_tables, n_batch=n_batch, n_pos=n_ctx, row_block0=n_lat // n_ctx)])
                mix_a = jnp.concatenate([mix_a, _ctx_gqa_attention(z, win_sink[j], **att)])
            xt = _outproj(mix_f, 0, mix_a, 0, mix_out_w[j].astype(BF16), xt, ng, mod, 2, tm=tm_out, rows=rows,
                          **common)
            xt = _ffn(xt, ng, mod, ffn_in_w[j].astype(BF16)[None], ffn_out_w[j].astype(BF16)[None], None,
                      tm=tm_ffn, tf=ffn_out_w.shape[1] // 2, rows=rows, **common)
        else:
            lam_init = _lambda_init(layer)
            (z,) = _proj(xt, ng, mod, 0, 1, diff_qkv_w[j].astype(BF16), rope, plan_odd, [(3 * d, BF16)],
                         tm=tm_proj, n_lat=n_lat, **common)
            subg = diff_subln_g[j].reshape(1, LANES)
            att = dict(n_batch=n_batch, seq=seq, n_ctx=n_ctx, n_heads=n_diff_heads)
            mix = _diff_attention(z, diff_lambda[j], subg, lam_init, tq=tq_diff, **att)
            if need_ctx:
                mix = jnp.concatenate([mix, _diff_attention_ctx(z, diff_lambda[j], subg, lam_init, **att)])
            xt = _outproj(mix, 0, mix, 1, diff_out_w[j].astype(BF16), xt, ng, mod, 2, tm=tm_out, rows=rows,
                          **common)
            comb = _router(xt, ng, mod, router_w[j], tm=tm_out, rows=rows, **common)
            xt = _ffn(xt, ng, mod, expert_in_w[j].astype(BF16), expert_out_w[j].astype(BF16), comb,
                      tm=tm_moe, tf=512, rows=rows, **common)
    return xt[:n_lat].reshape(n_batch, seq, d)
```

```python
import functools
import math

import numpy as np
import jax
import jax.numpy as jnp
from jax import lax
from jax.experimental import pallas as pl
from jax.experimental.pallas import tpu as pltpu

EPS = 1e-6
NEG = -1e30
HEAD_DIM = 64
LANES = 128
GRID_W = 64
BLOCK = 128
ROPE_THETA = 10000.0
N_EXPERTS = 8
F32 = jnp.float32
BF16 = jnp.bfloat16
HIGHEST = lax.Precision.HIGHEST
VMEM_LIMIT = 56 * 1024 * 1024


def _cparams(*sem):
    return pltpu.CompilerParams(dimension_semantics=sem, vmem_limit_bytes=VMEM_LIMIT)


def _dot(a, b):
    return jnp.dot(a, b, preferred_element_type=F32)


def _dot_nt(a, b):
    return lax.dot_general(a, b, (((1,), (1,)), ((), ())), preferred_element_type=F32)


def _dot_hi(a, b):
    return jnp.dot(a, b, precision=HIGHEST, preferred_element_type=F32)


def _rms(v):
    return v * lax.rsqrt(jnp.mean(v * v, axis=-1, keepdims=True) + EPS)


def _modulate(x, g, sh, sc):
    return _rms(x) * g * (1.0 + sc) + sh


def _pick_tile(pref, *dims):
    t = pref
    while any(d % t for d in dims):
        t //= 2
    return t


def _modvec_kernel(c_ref, w_ref, b_ref, o_ref):
    cv = c_ref[...]
    s = cv * (1.0 / (1.0 + jnp.exp(-cv)))
    o_ref[...] = _dot(s.astype(BF16), w_ref[...].astype(BF16)) + b_ref[...]


def _modvec(cv, ada_w, ada_b):
    depth, d, n = ada_w.shape
    r = cv.shape[0]
    tn = _pick_tile(1536, n)
    return pl.pallas_call(
        _modvec_kernel,
        grid=(depth, n // tn),
        in_specs=[
            pl.BlockSpec((r, d), lambda l, j: (0, 0)),
            pl.BlockSpec((None, d, tn), lambda l, j: (l, 0, j)),
            pl.BlockSpec((None, 1, tn), lambda l, j: (l, 0, j)),
        ],
        out_specs=pl.BlockSpec((None, r, tn), lambda l, j: (l, 0, j)),
        out_shape=jax.ShapeDtypeStruct((depth, r, n), F32),
        compiler_params=_cparams("parallel", "parallel"),
        name="modvec",
    )(cv, ada_w, ada_b.reshape(depth, 1, n))


def _proj_kernel(x_ref, g_ref, sh_ref, sc_ref, w_ref, cos_ref, sa_ref, sb_ref, *o_refs, plan, group):
    h = _modulate(x_ref[...], g_ref[...], sh_ref[...], sc_ref[...]).astype(BF16)
    n = w_ref.shape[1]
    for g0 in range(0, n, group):
        acc = _dot(h, w_ref[:, g0:g0 + group])
        for c0 in range(0, group, LANES):
            oi, oc, mode = plan[(g0 + c0) // LANES]
            v = acc[:, c0:c0 + LANES]
            if mode != "plain":
                v = (v * cos_ref[...] + pltpu.roll(v, LANES - 16, 1) * sa_ref[...]
                     + pltpu.roll(v, 16, 1) * sb_ref[...])
                if mode == "rope_q":
                    v = v * (HEAD_DIM ** -0.5)
            o_refs[oi][:, oc:oc + LANES] = v.astype(o_refs[oi].dtype)


def _proj(xt, normg, mod, k_sh, k_sc, w, rope, plan, outs, *, tm, n_lat, seq, n_batch):
    t, d = xt.shape
    n = w.shape[1]
    group = _pick_tile(512, n)
    nx = n_lat // tm
    per = seq // tm

    def mrow(i):
        return jnp.minimum(i * tm // seq, n_batch)

    def rrow(i):
        return jnp.where(i < nx, i % per, per)

    in_specs = [
        pl.BlockSpec((tm, d), lambda i: (i, 0)),
        pl.BlockSpec((None, 1, d), lambda i: (0, 0, 0)),
        pl.BlockSpec((None, None, 1, d), lambda i: (mrow(i), k_sh, 0, 0)),
        pl.BlockSpec((None, None, 1, d), lambda i: (mrow(i), k_sc, 0, 0)),
        pl.BlockSpec((d, n), lambda i: (0, 0)),
        pl.BlockSpec((tm, LANES), lambda i: (rrow(i), 0)),
        pl.BlockSpec((tm, LANES), lambda i: (rrow(i), 0)),
        pl.BlockSpec((tm, LANES), lambda i: (rrow(i), 0)),
    ]
    out_specs = [pl.BlockSpec((tm, wd), lambda i: (i, 0)) for wd, _ in outs]
    out_shape = [jax.ShapeDtypeStruct((t, wd), dt) for wd, dt in outs]
    return pl.pallas_call(
        functools.partial(_proj_kernel, plan=plan, group=group),
        grid=(t // tm,),
        in_specs=in_specs,
        out_specs=out_specs,
        out_shape=out_shape,
        compiler_params=_cparams("parallel"),
        name="proj",
    )(xt, normg, mod, mod, w, *rope)


def _rope_tables(seq, tm):
    rows_count = seq // GRID_W
    rows = jnp.repeat(jnp.arange(rows_count), GRID_W).astype(F32)
    cols = jnp.tile(jnp.arange(GRID_W), rows_count).astype(F32)
    axis_dim = HEAD_DIM // 2
    inv = ROPE_THETA ** (-jnp.arange(0, axis_dim, 2, dtype=F32) / axis_dim)
    ar = rows[:, None] * inv
    ac = cols[:, None] * inv
    cr, sr, cc, sc = jnp.cos(ar), jnp.sin(ar), jnp.cos(ac), jnp.sin(ac)
    z = jnp.zeros_like(sr)
    reps = LANES // HEAD_DIM
    cos = jnp.tile(jnp.concatenate([cr, cr, cc, cc], axis=1), (1, reps))
    sa = jnp.tile(jnp.concatenate([-sr, z, -sc, z], axis=1), (1, reps))
    sb = jnp.tile(jnp.concatenate([z, sr, z, sc], axis=1), (1, reps))
    ident = jnp.ones((tm, LANES), F32)
    zero = jnp.zeros((tm, LANES), F32)
    return (jnp.concatenate([cos, ident]), jnp.concatenate([sa, zero]), jnp.concatenate([sb, zero]))


def _fourier_tables(seq):
    n2 = GRID_W
    n1 = seq // n2
    norm = 1.0 / math.sqrt(seq * LANES)
    a = np.arange(n1)
    k1 = np.arange(n1)
    b = np.arange(n2)
    ang = (b[:, None, None] * k1[None, :, None] + (seq // n1) * k1[None, :, None] * a[None, None, :]) % seq
    th = 2.0 * np.pi * ang / seq
    m1 = np.concatenate([np.cos(th), -np.sin(th)], axis=1).astype(np.float32)
    ph = 2.0 * np.pi * ((b[:, None] * b[None, :]) % n2) / n2
    c2, s2 = np.cos(ph), np.sin(ph)
    g2 = np.block([[c2, s2], [-s2, c2]]).astype(np.float32)
    ch = np.arange(LANES)
    pc = 2.0 * np.pi * ((ch[:, None] * ch[None, :]) % LANES) / LANES
    cc = (np.cos(pc) * norm).astype(np.float32)
    sc = (np.sin(pc) * norm).astype(np.float32)
    return jnp.asarray(m1), jnp.asarray(g2), jnp.asarray(cc), jnp.asarray(sc)


def _fourier_kernel(u_ref, m1_ref, g2_ref, cc_ref, sc_ref, o_ref, b_scr, y_scr, *, n1, n2):
    for b in range(n2):
        xs = u_ref[pl.ds(b, n1, stride=n2), :]
        z = _dot_hi(m1_ref[b], xs)
        b_scr[pl.ds(b, n1, stride=2 * n2), :] = z[:n1]
        b_scr[pl.ds(n2 + b, n1, stride=2 * n2), :] = z[n1:]
    for k1 in range(n1):
        bk = b_scr[2 * n2 * k1:2 * n2 * (k1 + 1), :]
        xk = _dot_hi(g2_ref[...], bk)
        y = _dot_hi(xk[:n2], cc_ref[...]) + _dot_hi(xk[n2:], sc_ref[...])
        y_scr[pl.ds(k1, n2, stride=n1), :] = y
    o_ref[...] = y_scr[...].astype(o_ref.dtype)


def _fourier(f, tables, *, n_batch, seq):
    m1, g2, cc, sc = tables
    n2 = GRID_W
    n1 = seq // n2
    groups = f.shape[1] // LANES
    return pl.pallas_call(
        functools.partial(_fourier_kernel, n1=n1, n2=n2),
        grid=(n_batch, groups),
        in_specs=[
            pl.BlockSpec((seq, LANES), lambda b, g: (b, g)),
            pl.BlockSpec(m1.shape, lambda b, g: (0, 0, 0)),
            pl.BlockSpec(g2.shape, lambda b, g: (0, 0)),
            pl.BlockSpec(cc.shape, lambda b, g: (0, 0)),
            pl.BlockSpec(sc.shape, lambda b, g: (0, 0)),
        ],
        out_specs=pl.BlockSpec((seq, LANES), lambda b, g: (b, g)),
        out_shape=jax.ShapeDtypeStruct((n_batch * seq, f.shape[1]), BF16),
        scratch_shapes=[pltpu.VMEM((2 * seq, LANES), F32), pltpu.VMEM((seq, LANES), F32)],
        compiler_params=_cparams("parallel", "parallel"),
        name="fourier",
    )(f, m1, g2, cc, sc)


def _dense_fourier_tables(n):
    norm = 1.0 / math.sqrt(n * LANES)
    p = np.arange(n)
    ph = 2.0 * np.pi * ((p[:, None] * p[None, :]) % n) / n
    ch = np.arange(LANES)
    pc = 2.0 * np.pi * ((ch[:, None] * ch[None, :]) % LANES) / LANES
    return (jnp.asarray(np.cos(ph).astype(np.float32)), jnp.asarray(np.sin(ph).astype(np.float32)),
            jnp.asarray((np.cos(pc) * norm).astype(np.float32)), jnp.asarray((np.sin(pc) * norm).astype(np.float32)))


def _dense_fourier_kernel(u_ref, cl_ref, sl_ref, cc_ref, sc_ref, o_ref):
    u = u_ref[...]
    y = _dot_hi(cl_ref[...], _dot_hi(u, cc_ref[...])) - _dot_hi(sl_ref[...], _dot_hi(u, sc_ref[...]))
    o_ref[...] = y.astype(o_ref.dtype)


def _dense_fourier(f, tables, *, n_batch, n_pos, row_block0):
    cl, sl, cc, sc = tables
    groups = f.shape[1] // LANES
    return pl.pallas_call(
        _dense_fourier_kernel,
        grid=(n_batch, groups),
        in_specs=[
            pl.BlockSpec((n_pos, LANES), lambda b, g: (row_block0 + b, g)),
            pl.BlockSpec(cl.shape, lambda b, g: (0, 0)),
            pl.BlockSpec(sl.shape, lambda b, g: (0, 0)),
            pl.BlockSpec(cc.shape, lambda b, g: (0, 0)),
            pl.BlockSpec(sc.shape, lambda b, g: (0, 0)),
        ],
        out_specs=pl.BlockSpec((n_pos, LANES), lambda b, g: (b, g)),
        out_shape=jax.ShapeDtypeStruct((n_batch * n_pos, f.shape[1]), BF16),
        compiler_params=_cparams("parallel", "parallel"),
        name="fourier_ctx",
    )(f, cl, sl, cc, sc)


def _win_kernel(sink_ref, q_ref, *refs, n_blocks, has_local, n_heads, group_size):
    if has_local:
        kp_ref, kc_ref, kn_ref, vp_ref, vc_ref, vn_ref, kx_ref, vx_ref, o_ref = refs
    else:
        kx_ref, vx_ref, o_ref = refs
    tq = q_ref.shape[0]
    lane = lax.broadcasted_iota(jnp.int32, (1, LANES), 1)
    half_mask = [lane < HEAD_DIM, lane >= HEAD_DIM]
    kx = kx_ref[...]
    vx = vx_ref[...]
    if has_local:
        n = pl.program_id(1)
        qi = lax.broadcasted_iota(jnp.int32, (tq, BLOCK), 0)
        kj = lax.broadcasted_iota(jnp.int32, (tq, BLOCK), 1)
        valid_prev = (kj >= qi) & (n >= 1)
        valid_next = (kj <= qi) & (n <= n_blocks - 2)
        kp, kc, kn = kp_ref[...], kc_ref[...], kn_ref[...]
        vp, vc, vn = vp_ref[...], vc_ref[...], vn_ref[...]
    for pair in range(n_heads // 2):
        qp = q_ref[:, pair * LANES:(pair + 1) * LANES].astype(F32)
        qp_sw = pltpu.roll(qp, HEAD_DIM, 1)
        out_pair = jnp.zeros((tq, LANES), F32)
        for half in range(2):
            head = 2 * pair + half
            kv = head // group_size
            src = qp if half == kv else qp_sw
            qe = jnp.where(half_mask[kv], src, 0.0).astype(BF16)
            sink = sink_ref[head]
            s_x = _dot_nt(qe, kx)
            m = jnp.maximum(jnp.max(s_x, axis=-1, keepdims=True), sink)
            if has_local:
                s_p = jnp.where(valid_prev, _dot_nt(qe, kp), NEG)
                s_c = _dot_nt(qe, kc)
                s_n = jnp.where(valid_next, _dot_nt(qe, kn), NEG)
                m = jnp.maximum(m, jnp.max(s_p, axis=-1, keepdims=True))
                m = jnp.maximum(m, jnp.max(s_c, axis=-1, keepdims=True))
                m = jnp.maximum(m, jnp.max(s_n, axis=-1, keepdims=True))
            p_x = jnp.exp(s_x - m)
            den = jnp.sum(p_x, axis=-1, keepdims=True) + jnp.exp(sink - m)
            pv = _dot(p_x.astype(BF16), vx)
            if has_local:
                for s_l, v_l in ((s_p, vp), (s_c, vc), (s_n, vn)):
                    p_l = jnp.exp(s_l - m)
                    den = den + jnp.sum(p_l, axis=-1, keepdims=True)
                    pv = pv + _dot(p_l.astype(BF16), v_l)
            pv = pv / den
            if half != kv:
                pv = pltpu.roll(pv, HEAD_DIM, 1)
            out_pair = jnp.where(half_mask[half], pv, out_pair)
        o_ref[:, pair * LANES:(pair + 1) * LANES] = out_pair.astype(o_ref.dtype)


def _win_attention(z, sink, *, n_batch, seq, n_ctx, q_cols, k_col, v_col):
    nb = seq // BLOCK
    ctx_blk0 = n_batch * seq // n_ctx
    n_heads = q_cols // HEAD_DIM
    group_size = n_heads // (LANES // HEAD_DIM)

    def loc(col, off):
        return pl.BlockSpec(
            (BLOCK, LANES), lambda b, n: (b * nb + jnp.clip(n + off, 0, nb - 1), col))

    return pl.pallas_call(
        functools.partial(_win_kernel, n_blocks=nb, has_local=True, n_heads=n_heads, group_size=group_size),
        grid=(n_batch, nb),
        in_specs=[
            pl.BlockSpec(memory_space=pltpu.SMEM),
            pl.BlockSpec((BLOCK, q_cols), lambda b, n: (b * nb + n, 0)),
            loc(k_col, -1), loc(k_col, 0), loc(k_col, 1),
            loc(v_col, -1), loc(v_col, 0), loc(v_col, 1),
            pl.BlockSpec((n_ctx, LANES), lambda b, n: (ctx_blk0 + b, k_col)),
            pl.BlockSpec((n_ctx, LANES), lambda b, n: (ctx_blk0 + b, v_col)),
        ],
        out_specs=pl.BlockSpec((BLOCK, q_cols), lambda b, n: (b * nb + n, 0)),
        out_shape=jax.ShapeDtypeStruct((n_batch * seq, q_cols), BF16),
        compiler_params=_cparams("parallel", "parallel"),
        name="win_attn",
    )(sink, z, z, z, z, z, z, z, z, z)


def _ctx_gqa_attention(z, sink, *, n_batch, seq, n_ctx, q_cols, k_col, v_col):
    ctx_blk0 = n_batch * seq // n_ctx
    n_heads = q_cols // HEAD_DIM
    group_size = n_heads // (LANES // HEAD_DIM)
    return pl.pallas_call(
        functools.partial(_win_kernel, n_blocks=0, has_local=False, n_heads=n_heads, group_size=group_size),
        grid=(n_batch,),
        in_specs=[
            pl.BlockSpec(memory_space=pltpu.SMEM),
            pl.BlockSpec((n_ctx, q_cols), lambda b: (ctx_blk0 + b, 0)),
            pl.BlockSpec((n_ctx, LANES), lambda b: (ctx_blk0 + b, k_col)),
            pl.BlockSpec((n_ctx, LANES), lambda b: (ctx_blk0 + b, v_col)),
        ],
        out_specs=pl.BlockSpec((n_ctx, q_cols), lambda b: (b, 0)),
        out_shape=jax.ShapeDtypeStruct((n_batch * n_ctx, q_cols), BF16),
        compiler_params=_cparams("parallel"),
        name="ctx_gqa",
    )(sink, z, z, z)


def _diff_kernel(lam_ref, g_ref, q_ref, *refs, lam_init, has_x):
    if has_x:
        kx_ref, vx_ref, kc_ref, vc_ref, o_ref = refs
    else:
        kc_ref, vc_ref, o_ref = refs
    lv = lam_ref[...]
    lam = (jnp.exp(jnp.sum(lv[0:1] * lv[1:2], axis=-1, keepdims=True))
           - jnp.exp(jnp.sum(lv[2:3] * lv[3:4], axis=-1, keepdims=True)) + lam_init)
    q = q_ref[...]
    lane = lax.broadcasted_iota(jnp.int32, (1, LANES), 1)
    kc = kc_ref[...]
    vc = vc_ref[...]

    def component(qm):
        s_c = _dot_nt(qm, kc)
        m = jnp.max(s_c, axis=-1, keepdims=True)
        if has_x:
            s_x = _dot_nt(qm, kx_ref[...])
            m = jnp.maximum(m, jnp.max(s_x, axis=-1, keepdims=True))
        p_c = jnp.exp(s_c - m)
        den = jnp.sum(p_c, axis=-1, keepdims=True)
        acc = _dot(p_c.astype(BF16), vc)
        if has_x:
            p_x = jnp.exp(s_x - m)
            den = den + jnp.sum(p_x, axis=-1, keepdims=True)
            acc = acc + _dot(p_x.astype(BF16), vx_ref[...])
        return acc / den

    zero = jnp.zeros_like(q)
    o = component(jnp.where(lane < HEAD_DIM, q, zero)) - lam * component(jnp.where(lane >= HEAD_DIM, q, zero))
    o = _rms(o) * g_ref[...] * (1.0 - lam_init)
    o_ref[...] = o.astype(o_ref.dtype)


def _diff_attention(z, lam_vec, subln_g, lam_init, *, n_batch, seq, n_ctx, n_heads, tq):
    ctx_blk0 = n_batch * seq // n_ctx
    nq = seq // tq
    return pl.pallas_call(
        functools.partial(_diff_kernel, lam_init=lam_init, has_x=True),
        grid=(n_batch, n_heads, nq),
        in_specs=[
            pl.BlockSpec(lam_vec.shape, lambda b, h, i: (0, 0)),
            pl.BlockSpec((1, LANES), lambda b, h, i: (0, 0)),
            pl.BlockSpec((tq, LANES), lambda b, h, i: (b * nq + i, h)),
            pl.BlockSpec((seq, LANES), lambda b, h, i: (b, n_heads + h)),
            pl.BlockSpec((seq, LANES), lambda b, h, i: (b, 2 * n_heads + h)),
            pl.BlockSpec((n_ctx, LANES), lambda b, h, i: (ctx_blk0 + b, n_heads + h)),
            pl.BlockSpec((n_ctx, LANES), lambda b, h, i: (ctx_blk0 + b, 2 * n_heads + h)),
        ],
        out_specs=pl.BlockSpec((tq, LANES), lambda b, h, i: (b * nq + i, h)),
        out_shape=jax.ShapeDtypeStruct((n_batch * seq, n_heads * LANES), BF16),
        compiler_params=_cparams("parallel", "parallel", "parallel"),
        name="diff_attn",
    )(lam_vec, subln_g, z, z, z, z, z)


def _diff_attention_ctx(z, lam_vec, subln_g, lam_init, *, n_batch, seq, n_ctx, n_heads):
    ctx_blk0 = n_batch * seq // n_ctx
    return pl.pallas_call(
        functools.partial(_diff_kernel, lam_init=lam_init, has_x=False),
        grid=(n_batch, n_heads),
        in_specs=[
            pl.BlockSpec(lam_vec.shape, lambda b, h: (0, 0)),
            pl.BlockSpec((1, LANES), lambda b, h: (0, 0)),
            pl.BlockSpec((n_ctx, LANES), lambda b, h: (ctx_blk0 + b, h)),
            pl.BlockSpec((n_ctx, LANES), lambda b, h: (ctx_blk0 + b, n_heads + h)),
            pl.BlockSpec((n_ctx, LANES), lambda b, h: (ctx_blk0 + b, 2 * n_heads + h)),
        ],
        out_specs=pl.BlockSpec((n_ctx, LANES), lambda b, h: (b, h)),
        out_shape=jax.ShapeDtypeStruct((n_batch * n_ctx, n_heads * LANES), BF16),
        compiler_params=_cparams("parallel", "parallel"),
        name="diff_attn_ctx",
    )(lam_vec, subln_g, z, z, z)


def _outproj_kernel(a0_ref, a1_ref, w_ref, x_ref, g_ref, gate_ref, o_ref):
    half = a0_ref.shape[1]
    y = _dot(a0_ref[...], w_ref[:half, :]) + _dot(a1_ref[...], w_ref[half:, :])
    o_ref[...] = x_ref[...] + gate_ref[...] * (_rms(y) * g_ref[...])


def _outproj(a0, a0_col, a1, a1_col, w, xt, normg, mod, k_gate, *, tm, rows, seq, n_batch):
    d = xt.shape[1]
    half = d // 2

    def mrow(i):
        return jnp.minimum(i * tm // seq, n_batch)

    return pl.pallas_call(
        _outproj_kernel,
        grid=(rows // tm,),
        in_specs=[
            pl.BlockSpec((tm, half), lambda i: (i, a0_col)),
            pl.BlockSpec((tm, half), lambda i: (i, a1_col)),
            pl.BlockSpec((d, d), lambda i: (0, 0)),
            pl.BlockSpec((tm, d), lambda i: (i, 0)),
            pl.BlockSpec((None, 1, d), lambda i: (1, 0, 0)),
            pl.BlockSpec((None, None, 1, d), lambda i: (mrow(i), k_gate, 0, 0)),
        ],
        out_specs=pl.BlockSpec((tm, d), lambda i: (i, 0)),
        out_shape=jax.ShapeDtypeStruct((rows, d), F32),
        compiler_params=_cparams("parallel"),
        name="outproj",
    )(a0, a1, w, xt, normg, mod)


def _router_kernel(x_ref, g_ref, sh_ref, sc_ref, rw_ref, o_ref, sel_ref):
    h = _modulate(x_ref[...], g_ref[...], sh_ref[...], sc_ref[...])
    logits = _dot_hi(h, rw_ref[...])
    lane = lax.broadcasted_iota(jnp.int32, logits.shape, 1)
    ninf = -jnp.inf
    logits = jnp.where(lane < N_EXPERTS, logits, ninf)
    m1 = jnp.max(logits, axis=-1, keepdims=True)
    i1 = jnp.min(jnp.where(logits == m1, lane, LANES), axis=-1, keepdims=True)
    sel1 = lane == i1
    rest = jnp.where(sel1, ninf, logits)
    m2 = jnp.max(rest, axis=-1, keepdims=True)
    i2 = jnp.min(jnp.where(rest == m2, lane, LANES), axis=-1, keepdims=True)
    sel2 = lane == i2
    e2 = jnp.exp(m2 - m1)
    den = 1.0 + e2
    o_ref[...] = jnp.where(sel1, 1.0 / den, 0.0) + jnp.where(sel2, e2 / den, 0.0)
    sel_ref[...] = jnp.where(sel1 | sel2, 1.0, 0.0)


def _router(xt, normg, mod, router_w, *, tm, rows, seq, n_batch):
    d = xt.shape[1]
    rw = jnp.zeros((d, LANES), F32).at[:, :N_EXPERTS].set(router_w)

    def mrow(i):
        return jnp.minimum(i * tm // seq, n_batch)

    return pl.pallas_call(
        _router_kernel,
        grid=(rows // tm,),
        in_specs=[
            pl.BlockSpec((tm, d), lambda i: (i, 0)),
            pl.BlockSpec((None, 1, d), lambda i: (2, 0, 0)),
            pl.BlockSpec((None, None, 1, d), lambda i: (mrow(i), 3, 0, 0)),
            pl.BlockSpec((None, None, 1, d), lambda i: (mrow(i), 4, 0, 0)),
            pl.BlockSpec((d, LANES), lambda i: (0, 0)),
        ],
        out_specs=[pl.BlockSpec((tm, LANES), lambda i: (i, 0))] * 2,
        out_shape=[jax.ShapeDtypeStruct((rows, LANES), F32)] * 2,
        compiler_params=_cparams("parallel"),
        name="router",
    )(xt, normg, mod, mod, rw)


def _ffn_kernel(x_ref, g2_ref, sh_ref, sc_ref, wg_ref, wu_ref, wo_ref, g3_ref, gate_ref, o_ref, h_scr, acc, *, n_f):
    f = pl.program_id(1)

    @pl.when(f == 0)
    def _():
        h_scr[...] = _modulate(x_ref[...], g2_ref[...], sh_ref[...], sc_ref[...]).astype(BF16)
        acc[...] = jnp.zeros_like(acc)

    h = h_scr[...]
    gp = _dot(h, wg_ref[...])
    up = _dot(h, wu_ref[...])
    a = gp * (1.0 / (1.0 + jnp.exp(-gp))) * up
    acc[...] += _dot(a.astype(BF16), wo_ref[...])

    @pl.when(f == n_f - 1)
    def _():
        o_ref[...] = x_ref[...] + gate_ref[...] * (_rms(acc[...]) * g3_ref[...])


def _ffn(xt, normg, mod, w_in, w_out, *, tm, tf, rows, seq, n_batch):
    d = xt.shape[1]
    fdim = w_out.shape[0]
    n_f = fdim // tf

    def mrow(i):
        return jnp.minimum(i * tm // seq, n_batch)

    in_specs = [
        pl.BlockSpec((tm, d), lambda i, f: (i, 0)),
        pl.BlockSpec((None, 1, d), lambda i, f: (2, 0, 0)),
        pl.BlockSpec((None, None, 1, d), lambda i, f: (mrow(i), 3, 0, 0)),
        pl.BlockSpec((None, None, 1, d), lambda i, f: (mrow(i), 4, 0, 0)),
        pl.BlockSpec((d, tf), lambda i, f: (0, f)),
        pl.BlockSpec((d, tf), lambda i, f: (0, n_f + f)),
        pl.BlockSpec((tf, d), lambda i, f: (f, 0)),
        pl.BlockSpec((None, 1, d), lambda i, f: (3, 0, 0)),
        pl.BlockSpec((None, None, 1, d), lambda i, f: (mrow(i), 5, 0, 0)),
    ]
    return pl.pallas_call(
        functools.partial(_ffn_kernel, n_f=n_f),
        grid=(rows // tm, n_f),
        in_specs=in_specs,
        out_specs=pl.BlockSpec((tm, d), lambda i, f: (i, 0)),
        out_shape=jax.ShapeDtypeStruct((rows, d), F32),
        scratch_shapes=[pltpu.VMEM((tm, d), BF16), pltpu.VMEM((tm, d), F32)],
        compiler_params=_cparams("parallel", "arbitrary"),
        name="ffn",
    )(xt, normg, mod, mod, w_in, w_in, w_out, normg, mod)


MOE_CHUNK = 128


def _moe_kernel(nchunk_ref, x_ref, g2_ref, sh_ref, sc_ref, wg_ref, wu_ref, wo_ref, g3_ref, gate_ref,
                comb_ref, sel_ref, o_ref, h_scr, key_scr, keyt_scr, hs_scr, y_scr, acc_scr, *, n_exp, n_f):
    b = pl.program_id(0)
    e = pl.program_id(1)
    f = pl.program_id(2)
    nb = x_ref.shape[0]
    ch = MOE_CHUNK
    n_chunks = nchunk_ref[b * n_exp + e]

    @pl.when((e == 0) & (f == 0))
    def _():
        h_scr[...] = _modulate(x_ref[...], g2_ref[...], sh_ref[...], sc_ref[...]).astype(BF16)
        o_ref[...] = jnp.zeros_like(o_ref)
        ti = lax.broadcasted_iota(jnp.int32, (nb, nb), 0)
        tj = lax.broadcasted_iota(jnp.int32, (nb, nb), 1)
        lower = jnp.where(tj < ti, 1.0, 0.0).astype(BF16)
        sel = sel_ref[...]
        rank = _dot(lower, sel.astype(BF16))
        key = jnp.where(sel > 0.0, rank, -1.0)
        key_scr[...] = key
        keyt_scr[...] = key.T

    def chunk_rows(c):
        return pl.ds(pl.multiple_of(c * ch, ch), ch)

    @pl.when(f == 0)
    def _():
        key_row = keyt_scr[pl.ds(e, 1), :]

        def gather(c, carry):
            r = (c * ch + lax.broadcasted_iota(jnp.int32, (ch, 1), 0)).astype(F32)
            onehot = jnp.where(key_row == r, 1.0, 0.0).astype(BF16)
            hs_scr[chunk_rows(c), :] = _dot(onehot, h_scr[...]).astype(BF16)
            return carry

        lax.fori_loop(0, n_chunks, gather, 0)

    def expert(c, carry):
        rows = chunk_rows(c)
        hs = hs_scr[rows, :]
        gp = _dot(hs, wg_ref[...])
        up = _dot(hs, wu_ref[...])
        a = gp * (1.0 / (1.0 + jnp.exp(-gp))) * up
        part = _dot(a.astype(BF16), wo_ref[...])

        @pl.when(f == 0)
        def _():
            y_scr[rows, :] = part

        @pl.when(f > 0)
        def _():
            y_scr[rows, :] += part

        return carry

    lax.fori_loop(0, n_chunks, expert, 0)

    @pl.when(f == n_f - 1)
    def _():
        lane = lax.broadcasted_iota(jnp.int32, (nb, LANES), 1)
        pick = lane == e
        key_col = jnp.sum(jnp.where(pick, key_scr[...], 0.0), axis=-1, keepdims=True)
        gate_col = jnp.sum(jnp.where(pick, comb_ref[...], 0.0), axis=-1, keepdims=True)
        acc_scr[...] = jnp.zeros_like(acc_scr)

        def scatter(c, carry):
            r = (c * ch + lax.broadcasted_iota(jnp.int32, (1, ch), 1)).astype(F32)
            onehot = jnp.where(key_col == r, 1.0, 0.0).astype(BF16)
            acc_scr[...] += _dot(onehot, y_scr[chunk_rows(c), :].astype(BF16))
            return carry

        lax.fori_loop(0, n_chunks, scatter, 0)
        o_ref[...] += gate_col * acc_scr[...]

    @pl.when((e == n_exp - 1) & (f == n_f - 1))
    def _():
        o_ref[...] = x_ref[...] + gate_ref[...] * (_rms(o_ref[...]) * g3_ref[...])


def _moe(xt, normg, mod, w_in, w_out, comb, sel, *, nb, tf, rows, seq, n_batch):
    d = xt.shape[1]
    n_exp, fdim = w_out.shape[0], w_out.shape[1]
    n_f = fdim // tf
    n_blk = rows // nb
    counts = jnp.sum(sel.reshape(n_blk, nb, LANES)[:, :, :n_exp], axis=1).astype(jnp.int32)
    nchunk = ((counts + (MOE_CHUNK - 1)) // MOE_CHUNK).reshape(n_blk * n_exp)

    def mrow(i):
        return jnp.minimum(i * nb // seq, n_batch)

    in_specs = [
        pl.BlockSpec((nb, d), lambda i, e, f, s: (i, 0)),
        pl.BlockSpec((None, 1, d), lambda i, e, f, s: (2, 0, 0)),
        pl.BlockSpec((None, None, 1, d), lambda i, e, f, s: (mrow(i), 3, 0, 0)),
        pl.BlockSpec((None, None, 1, d), lambda i, e, f, s: (mrow(i), 4, 0, 0)),
        pl.BlockSpec((None, d, tf), lambda i, e, f, s: (e, 0, f)),
        pl.BlockSpec((None, d, tf), lambda i, e, f, s: (e, 0, n_f + f)),
        pl.BlockSpec((None, tf, d), lambda i, e, f, s: (e, f, 0)),
        pl.BlockSpec((None, 1, d), lambda i, e, f, s: (3, 0, 0)),
        pl.BlockSpec((None, None, 1, d), lambda i, e, f, s: (mrow(i), 5, 0, 0)),
        pl.BlockSpec((nb, LANES), lambda i, e, f, s: (i, 0)),
        pl.BlockSpec((nb, LANES), lambda i, e, f, s: (i, 0)),
    ]
    return pl.pallas_call(
        functools.partial(_moe_kernel, n_exp=n_exp, n_f=n_f),
        grid_spec=pltpu.PrefetchScalarGridSpec(
            num_scalar_prefetch=1,
            grid=(n_blk, n_exp, n_f),
            in_specs=in_specs,
            out_specs=pl.BlockSpec((nb, d), lambda i, e, f, s: (i, 0)),
            scratch_shapes=[
                pltpu.VMEM((nb, d), BF16),
                pltpu.VMEM((nb, LANES), F32),
                pltpu.VMEM((LANES, nb), F32),
                pltpu.VMEM((nb, d), BF16),
                pltpu.VMEM((nb, d), F32),
                pltpu.VMEM((nb, d), F32),
            ],
        ),
        out_shape=jax.ShapeDtypeStruct((rows, d), F32),
        compiler_params=_cparams("parallel", "arbitrary", "arbitrary"),
        name="moe",
    )(nchunk, xt, normg, mod, mod, w_in, w_in, w_out, normg, mod, comb, sel)


def _lambda_init(layer):
    return 0.8 - 0.6 * math.exp(-0.3 * layer)


def kernel(x, c, ctx, c_ctx, ada_w, ada_b, norm_g, mix_in_w, mix_out_w, win_sink, diff_qkv_w, diff_out_w,
           diff_lambda, diff_subln_g, ffn_in_w, ffn_out_w, router_w, expert_in_w, expert_out_w):
    n_batch, seq, d = x.shape
    n_ctx = ctx.shape[1]
    depth = ada_w.shape[0]
    n_lat = n_batch * seq
    n_all = n_lat + n_batch * n_ctx
    fdim = mix_in_w.shape[2] - (d // 2 + 2 * LANES)
    q_cols = d // 2
    n_diff_heads = d // LANES

    tm_proj = _pick_tile(512, seq, n_batch * n_ctx)
    tm_out = _pick_tile(512, seq, n_batch * n_ctx)
    tm_ffn = _pick_tile(512, seq, n_batch * n_ctx)
    nb_moe = _pick_tile(1024, seq, n_batch * n_ctx)
    tq_diff = _pick_tile(256, seq)
    common = dict(seq=seq, n_batch=n_batch)

    xt = jnp.concatenate([x.reshape(n_lat, d), ctx.reshape(n_batch * n_ctx, d)], axis=0)
    n_mod = -(-(n_batch + 1) // 8) * 8
    cv = jnp.zeros((n_mod, d), F32).at[:n_batch].set(c).at[n_batch].set(c_ctx)
    mods = _modvec(cv, ada_w, ada_b).reshape(depth, n_mod, 6, 1, d)
    rope = _rope_tables(seq, tm_proj)
    f_tables = _fourier_tables(seq)
    fc_tables = _dense_fourier_tables(n_ctx)

    n_f = fdim // LANES
    plan_even = ([(0, i * LANES, "plain") for i in range(n_f)]
                 + [(1, i * LANES, "rope_q") for i in range(q_cols // LANES)]
                 + [(1, q_cols, "rope_k"), (1, q_cols + LANES, "plain")])
    plan_odd = ([(0, i * LANES, "rope_q") for i in range(n_diff_heads)]
                + [(0, d + i * LANES, "rope_k") for i in range(n_diff_heads)]
                + [(0, 2 * d + i * LANES, "plain") for i in range(n_diff_heads)])

    for layer in range(depth):
        j = layer // 2
        need_ctx = layer < depth - 1
        rows = n_all if need_ctx else n_lat
        mod = mods[layer]
        ng = norm_g[layer].reshape(4, 1, d)
        if layer % 2 == 0:
            f, z = _proj(xt, ng, mod, 0, 1, mix_in_w[j].astype(BF16), rope, plan_even,
                         [(fdim, F32), (q_cols + 2 * LANES, BF16)], tm=tm_proj, n_lat=n_lat, **common)
            k_col, v_col = q_cols // LANES, q_cols // LANES + 1
            att = dict(n_batch=n_batch, seq=seq, n_ctx=n_ctx, q_cols=q_cols, k_col=k_col, v_col=v_col)
            mix_f = _fourier(f, f_tables, n_batch=n_batch, seq=seq)
            mix_a = _win_attention(z, win_sink[j], **att)
            if need_ctx:
                mix_f = jnp.concatenate(
                    [mix_f, _dense_fourier(f, fc_tables, n_batch=n_batch, n_pos=n_ctx, row_block0=n_lat // n_ctx)])
                mix_a = jnp.concatenate([mix_a, _ctx_gqa_attention(z, win_sink[j], **att)])
            xt = _outproj(mix_f, 0, mix_a, 0, mix_out_w[j].astype(BF16), xt, ng, mod, 2, tm=tm_out, rows=rows,
                          **common)
            xt = _ffn(xt, ng, mod, ffn_in_w[j].astype(BF16), ffn_out_w[j].astype(BF16),
                      tm=tm_ffn, tf=ffn_out_w.shape[1] // 2, rows=rows, **common)
        else:
            lam_init = _lambda_init(layer)
            (z,) = _proj(xt, ng, mod, 0, 1, diff_qkv_w[j].astype(BF16), rope, plan_odd, [(3 * d, BF16)],
                         tm=tm_proj, n_lat=n_lat, **common)
            subg = diff_subln_g[j].reshape(1, LANES)
            att = dict(n_batch=n_batch, seq=seq, n_ctx=n_ctx, n_heads=n_diff_heads)
            mix = _diff_attention(z, diff_lambda[j], subg, lam_init, tq=tq_diff, **att)
            if need_ctx:
                mix = jnp.concatenate([mix, _diff_attention_ctx(z, diff_lambda[j], subg, lam_init, **att)])
            xt = _outproj(mix, 0, mix, 1, diff_out_w[j].astype(BF16), xt, ng, mod, 2, tm=tm_out, rows=rows,
                          **common)
            comb, sel = _router(xt, ng, mod, router_w[j], tm=tm_out, rows=rows, **common)
            xt = _moe(xt, ng, mod, expert_in_w[j].astype(BF16), expert_out_w[j].astype(BF16), comb, sel,
                      nb=nb_moe, tf=512, rows=rows, **common)
    return xt[:n_lat].reshape(n_batch, seq, d)
```

```python
import functools
import math

import numpy as np
import jax
import jax.numpy as jnp
from jax import lax
from jax.experimental import pallas as pl
from jax.experimental.pallas import tpu as pltpu

EPS = 1e-6
NEG = -1e30
HEAD_DIM = 64
LANES = 128
GRID_W = 64
BLOCK = 128
ROPE_THETA = 10000.0
N_EXPERTS = 8
F32 = jnp.float32
BF16 = jnp.bfloat16
HIGHEST = lax.Precision.HIGHEST
VMEM_LIMIT = 56 * 1024 * 1024


def _cparams(*sem):
    return pltpu.CompilerParams(dimension_semantics=sem, vmem_limit_bytes=VMEM_LIMIT)


def _dot(a, b):
    return jnp.dot(a, b, preferred_element_type=F32)


def _dot_nt(a, b):
    return lax.dot_general(a, b, (((1,), (1,)), ((), ())), preferred_element_type=F32)


def _dot_hi(a, b):
    return jnp.dot(a, b, precision=HIGHEST, preferred_element_type=F32)


def _rms(v):
    return v * lax.rsqrt(jnp.mean(v * v, axis=-1, keepdims=True) + EPS)


def _modulate(x, g, sh, sc):
    return _rms(x) * g * (1.0 + sc) + sh


def _pick_tile(pref, *dims):
    t = pref
    while any(d % t for d in dims):
        t //= 2
    return t


def _modvec_kernel(c_ref, w_ref, b_ref, o_ref):
    cv = c_ref[...]
    s = cv * (1.0 / (1.0 + jnp.exp(-cv)))
    o_ref[...] = _dot(s.astype(BF16), w_ref[...].astype(BF16)) + b_ref[...]


def _modvec(cv, ada_w, ada_b):
    depth, d, n = ada_w.shape
    r = cv.shape[0]
    tn = _pick_tile(1536, n)
    return pl.pallas_call(
        _modvec_kernel,
        grid=(depth, n // tn),
        in_specs=[
            pl.BlockSpec((r, d), lambda l, j: (0, 0)),
            pl.BlockSpec((None, d, tn), lambda l, j: (l, 0, j)),
            pl.BlockSpec((None, 1, tn), lambda l, j: (l, 0, j)),
        ],
        out_specs=pl.BlockSpec((None, r, tn), lambda l, j: (l, 0, j)),
        out_shape=jax.ShapeDtypeStruct((depth, r, n), F32),
        compiler_params=_cparams("parallel", "parallel"),
        name="modvec",
    )(cv, ada_w, ada_b.reshape(depth, 1, n))


def _proj_kernel(x_ref, g_ref, sh_ref, sc_ref, w_ref, cos_ref, sa_ref, sb_ref, *o_refs, plan, group):
    h = _modulate(x_ref[...], g_ref[...], sh_ref[...], sc_ref[...]).astype(BF16)
    n = w_ref.shape[1]
    for g0 in range(0, n, group):
        acc = _dot(h, w_ref[:, g0:g0 + group])
        for c0 in range(0, group, LANES):
            oi, oc, mode = plan[(g0 + c0) // LANES]
            v = acc[:, c0:c0 + LANES]
            if mode != "plain":
                v = (v * cos_ref[...] + pltpu.roll(v, LANES - 16, 1) * sa_ref[...]
                     + pltpu.roll(v, 16, 1) * sb_ref[...])
                if mode == "rope_q":
                    v = v * (HEAD_DIM ** -0.5)
                elif mode == "rope_q_log2":
                    v = v * (HEAD_DIM ** -0.5 * LOG2E)
            o_refs[oi][:, oc:oc + LANES] = v.astype(o_refs[oi].dtype)


def _proj(xt, normg, mod, k_sh, k_sc, w, rope, plan, outs, *, tm, n_lat, seq, n_batch):
    t, d = xt.shape
    n = w.shape[1]
    group = _pick_tile(512, n)
    nx = n_lat // tm
    per = seq // tm

    def mrow(i):
        return jnp.minimum(i * tm // seq, n_batch)

    def rrow(i):
        return jnp.where(i < nx, i % per, per)

    in_specs = [
        pl.BlockSpec((tm, d), lambda i: (i, 0)),
        pl.BlockSpec((None, 1, d), lambda i: (0, 0, 0)),
        pl.BlockSpec((None, None, 1, d), lambda i: (mrow(i), k_sh, 0, 0)),
        pl.BlockSpec((None, None, 1, d), lambda i: (mrow(i), k_sc, 0, 0)),
        pl.BlockSpec((d, n), lambda i: (0, 0)),
        pl.BlockSpec((tm, LANES), lambda i: (rrow(i), 0)),
        pl.BlockSpec((tm, LANES), lambda i: (rrow(i), 0)),
        pl.BlockSpec((tm, LANES), lambda i: (rrow(i), 0)),
    ]
    out_specs = [pl.BlockSpec((tm, wd), lambda i: (i, 0)) for wd, _ in outs]
    out_shape = [jax.ShapeDtypeStruct((t, wd), dt) for wd, dt in outs]
    return pl.pallas_call(
        functools.partial(_proj_kernel, plan=plan, group=group),
        grid=(t // tm,),
        in_specs=in_specs,
        out_specs=out_specs,
        out_shape=out_shape,
        compiler_params=_cparams("parallel"),
        name="proj",
    )(xt, normg, mod, mod, w, *rope)


def _rope_tables(seq, tm):
    rows_count = seq // GRID_W
    rows = jnp.repeat(jnp.arange(rows_count), GRID_W).astype(F32)
    cols = jnp.tile(jnp.arange(GRID_W), rows_count).astype(F32)
    axis_dim = HEAD_DIM // 2
    inv = ROPE_THETA ** (-jnp.arange(0, axis_dim, 2, dtype=F32) / axis_dim)
    ar = rows[:, None] * inv
    ac = cols[:, None] * inv
    cr, sr, cc, sc = jnp.cos(ar), jnp.sin(ar), jnp.cos(ac), jnp.sin(ac)
    z = jnp.zeros_like(sr)
    reps = LANES // HEAD_DIM
    cos = jnp.tile(jnp.concatenate([cr, cr, cc, cc], axis=1), (1, reps))
    sa = jnp.tile(jnp.concatenate([-sr, z, -sc, z], axis=1), (1, reps))
    sb = jnp.tile(jnp.concatenate([z, sr, z, sc], axis=1), (1, reps))
    ident = jnp.ones((tm, LANES), F32)
    zero = jnp.zeros((tm, LANES), F32)
    return (jnp.concatenate([cos, ident]), jnp.concatenate([sa, zero]), jnp.concatenate([sb, zero]))


def _fourier_tables(seq):
    n2 = GRID_W
    n1 = seq // n2
    norm = 1.0 / math.sqrt(seq * LANES)
    a = np.arange(n1)
    k1 = np.arange(n1)
    b = np.arange(n2)
    ang = (b[:, None, None] * k1[None, :, None] + (seq // n1) * k1[None, :, None] * a[None, None, :]) % seq
    th = 2.0 * np.pi * ang / seq
    m1 = np.concatenate([np.cos(th), -np.sin(th)], axis=1).astype(np.float32)
    ph = 2.0 * np.pi * ((b[:, None] * b[None, :]) % n2) / n2
    c2, s2 = np.cos(ph), np.sin(ph)
    g2 = np.block([[c2, s2], [-s2, c2]]).astype(np.float32)
    ch = np.arange(LANES)
    pc = 2.0 * np.pi * ((ch[:, None] * ch[None, :]) % LANES) / LANES
    cc = (np.cos(pc) * norm).astype(np.float32)
    sc = (np.sin(pc) * norm).astype(np.float32)
    return jnp.asarray(m1), jnp.asarray(g2), jnp.asarray(cc), jnp.asarray(sc)


def _fourier_kernel(u_ref, m1_ref, g2_ref, cc_ref, sc_ref, o_ref, b_scr, y_scr, *, n1, n2):
    for b in range(n2):
        xs = u_ref[pl.ds(b, n1, stride=n2), :]
        z = _dot_hi(m1_ref[b], xs)
        b_scr[pl.ds(b, n1, stride=2 * n2), :] = z[:n1]
        b_scr[pl.ds(n2 + b, n1, stride=2 * n2), :] = z[n1:]
    for k1 in range(n1):
        bk = b_scr[2 * n2 * k1:2 * n2 * (k1 + 1), :]
        xk = _dot_hi(g2_ref[...], bk)
        y = _dot_hi(xk[:n2], cc_ref[...]) + _dot_hi(xk[n2:], sc_ref[...])
        y_scr[pl.ds(k1, n2, stride=n1), :] = y
    o_ref[...] = y_scr[...].astype(o_ref.dtype)


def _fourier(f, tables, *, n_batch, seq):
    m1, g2, cc, sc = tables
    n2 = GRID_W
    n1 = seq // n2
    groups = f.shape[1] // LANES
    return pl.pallas_call(
        functools.partial(_fourier_kernel, n1=n1, n2=n2),
        grid=(n_batch, groups),
        in_specs=[
            pl.BlockSpec((seq, LANES), lambda b, g: (b, g)),
            pl.BlockSpec(m1.shape, lambda b, g: (0, 0, 0)),
            pl.BlockSpec(g2.shape, lambda b, g: (0, 0)),
            pl.BlockSpec(cc.shape, lambda b, g: (0, 0)),
            pl.BlockSpec(sc.shape, lambda b, g: (0, 0)),
        ],
        out_specs=pl.BlockSpec((seq, LANES), lambda b, g: (b, g)),
        out_shape=jax.ShapeDtypeStruct((n_batch * seq, f.shape[1]), BF16),
        scratch_shapes=[pltpu.VMEM((2 * seq, LANES), F32), pltpu.VMEM((seq, LANES), F32)],
        compiler_params=_cparams("parallel", "parallel"),
        name="fourier",
    )(f, m1, g2, cc, sc)


def _dense_fourier_tables(n):
    norm = 1.0 / math.sqrt(n * LANES)
    p = np.arange(n)
    ph = 2.0 * np.pi * ((p[:, None] * p[None, :]) % n) / n
    ch = np.arange(LANES)
    pc = 2.0 * np.pi * ((ch[:, None] * ch[None, :]) % LANES) / LANES
    return (jnp.asarray(np.cos(ph).astype(np.float32)), jnp.asarray(np.sin(ph).astype(np.float32)),
            jnp.asarray((np.cos(pc) * norm).astype(np.float32)), jnp.asarray((np.sin(pc) * norm).astype(np.float32)))


def _dense_fourier_kernel(u_ref, cl_ref, sl_ref, cc_ref, sc_ref, o_ref):
    u = u_ref[...]
    y = _dot_hi(cl_ref[...], _dot_hi(u, cc_ref[...])) - _dot_hi(sl_ref[...], _dot_hi(u, sc_ref[...]))
    o_ref[...] = y.astype(o_ref.dtype)


def _dense_fourier(f, tables, *, n_batch, n_pos, row_block0):
    cl, sl, cc, sc = tables
    groups = f.shape[1] // LANES
    return pl.pallas_call(
        _dense_fourier_kernel,
        grid=(n_batch, groups),
        in_specs=[
            pl.BlockSpec((n_pos, LANES), lambda b, g: (row_block0 + b, g)),
            pl.BlockSpec(cl.shape, lambda b, g: (0, 0)),
            pl.BlockSpec(sl.shape, lambda b, g: (0, 0)),
            pl.BlockSpec(cc.shape, lambda b, g: (0, 0)),
            pl.BlockSpec(sc.shape, lambda b, g: (0, 0)),
        ],
        out_specs=pl.BlockSpec((n_pos, LANES), lambda b, g: (b, g)),
        out_shape=jax.ShapeDtypeStruct((n_batch * n_pos, f.shape[1]), BF16),
        compiler_params=_cparams("parallel", "parallel"),
        name="fourier_ctx",
    )(f, cl, sl, cc, sc)


def _win_kernel(sink_ref, q_ref, *refs, n_blocks, has_local, n_heads, group_size):
    if has_local:
        kp_ref, kc_ref, kn_ref, vp_ref, vc_ref, vn_ref, kx_ref, vx_ref, o_ref = refs
    else:
        kx_ref, vx_ref, o_ref = refs
    tq = q_ref.shape[0]
    lane = lax.broadcasted_iota(jnp.int32, (1, LANES), 1)
    half_mask = [lane < HEAD_DIM, lane >= HEAD_DIM]
    kx = kx_ref[...]
    vx = vx_ref[...]
    if has_local:
        n = pl.program_id(1)
        qi = lax.broadcasted_iota(jnp.int32, (tq, BLOCK), 0)
        kj = lax.broadcasted_iota(jnp.int32, (tq, BLOCK), 1)
        valid_prev = (kj >= qi) & (n >= 1)
        valid_next = (kj <= qi) & (n <= n_blocks - 2)
        kp, kc, kn = kp_ref[...], kc_ref[...], kn_ref[...]
        vp, vc, vn = vp_ref[...], vc_ref[...], vn_ref[...]
    for pair in range(n_heads // 2):
        qp = q_ref[:, pair * LANES:(pair + 1) * LANES].astype(F32)
        qp_sw = pltpu.roll(qp, HEAD_DIM, 1)
        out_pair = jnp.zeros((tq, LANES), F32)
        for half in range(2):
            head = 2 * pair + half
            kv = head // group_size
            src = qp if half == kv else qp_sw
            qe = jnp.where(half_mask[kv], src, 0.0).astype(BF16)
            sink = sink_ref[head]
            s_x = _dot_nt(qe, kx)
            m = jnp.maximum(jnp.max(s_x, axis=-1, keepdims=True), sink)
            if has_local:
                s_p = jnp.where(valid_prev, _dot_nt(qe, kp), NEG)
                s_c = _dot_nt(qe, kc)
                s_n = jnp.where(valid_next, _dot_nt(qe, kn), NEG)
                m = jnp.maximum(m, jnp.max(s_p, axis=-1, keepdims=True))
                m = jnp.maximum(m, jnp.max(s_c, axis=-1, keepdims=True))
                m = jnp.maximum(m, jnp.max(s_n, axis=-1, keepdims=True))
            p_x = jnp.exp(s_x - m)
            den = jnp.sum(p_x, axis=-1, keepdims=True) + jnp.exp(sink - m)
            pv = _dot(p_x.astype(BF16), vx)
            if has_local:
                for s_l, v_l in ((s_p, vp), (s_c, vc), (s_n, vn)):
                    p_l = jnp.exp(s_l - m)
                    den = den + jnp.sum(p_l, axis=-1, keepdims=True)
                    pv = pv + _dot(p_l.astype(BF16), v_l)
            pv = pv / den
            if half != kv:
                pv = pltpu.roll(pv, HEAD_DIM, 1)
            out_pair = jnp.where(half_mask[half], pv, out_pair)
        o_ref[:, pair * LANES:(pair + 1) * LANES] = out_pair.astype(o_ref.dtype)


def _win_attention(z, sink, *, n_batch, seq, n_ctx, q_cols, k_col, v_col):
    nb = seq // BLOCK
    ctx_blk0 = n_batch * seq // n_ctx
    n_heads = q_cols // HEAD_DIM
    group_size = n_heads // (LANES // HEAD_DIM)

    def loc(col, off):
        return pl.BlockSpec(
            (BLOCK, LANES), lambda b, n: (b * nb + jnp.clip(n + off, 0, nb - 1), col))

    return pl.pallas_call(
        functools.partial(_win_kernel, n_blocks=nb, has_local=True, n_heads=n_heads, group_size=group_size),
        grid=(n_batch, nb),
        in_specs=[
            pl.BlockSpec(memory_space=pltpu.SMEM),
            pl.BlockSpec((BLOCK, q_cols), lambda b, n: (b * nb + n, 0)),
            loc(k_col, -1), loc(k_col, 0), loc(k_col, 1),
            loc(v_col, -1), loc(v_col, 0), loc(v_col, 1),
            pl.BlockSpec((n_ctx, LANES), lambda b, n: (ctx_blk0 + b, k_col)),
            pl.BlockSpec((n_ctx, LANES), lambda b, n: (ctx_blk0 + b, v_col)),
        ],
        out_specs=pl.BlockSpec((BLOCK, q_cols), lambda b, n: (b * nb + n, 0)),
        out_shape=jax.ShapeDtypeStruct((n_batch * seq, q_cols), BF16),
        compiler_params=_cparams("parallel", "parallel"),
        name="win_attn",
    )(sink, z, z, z, z, z, z, z, z, z)


def _ctx_gqa_attention(z, sink, *, n_batch, seq, n_ctx, q_cols, k_col, v_col):
    ctx_blk0 = n_batch * seq // n_ctx
    n_heads = q_cols // HEAD_DIM
    group_size = n_heads // (LANES // HEAD_DIM)
    return pl.pallas_call(
        functools.partial(_win_kernel, n_blocks=0, has_local=False, n_heads=n_heads, group_size=group_size),
        grid=(n_batch,),
        in_specs=[
            pl.BlockSpec(memory_space=pltpu.SMEM),
            pl.BlockSpec((n_ctx, q_cols), lambda b: (ctx_blk0 + b, 0)),
            pl.BlockSpec((n_ctx, LANES), lambda b: (ctx_blk0 + b, k_col)),
            pl.BlockSpec((n_ctx, LANES), lambda b: (ctx_blk0 + b, v_col)),
        ],
        out_specs=pl.BlockSpec((n_ctx, q_cols), lambda b: (b, 0)),
        out_shape=jax.ShapeDtypeStruct((n_batch * n_ctx, q_cols), BF16),
        compiler_params=_cparams("parallel"),
        name="ctx_gqa",
    )(sink, z, z, z)


LOG2E = math.log2(math.e)
DIFF_SUB_ROWS = 128


def _diff_kernel(lam_ref, g_ref, q_ref, *refs, lam_init, has_x):
    if has_x:
        kx_ref, vx_ref, kc_ref, vc_ref, o_ref, s_scr = refs
        nx = kx_ref.shape[0]
    else:
        kc_ref, vc_ref, o_ref, s_scr = refs
        nx = 0
    nc = kc_ref.shape[0]
    sub = s_scr.shape[1]
    n_sub = q_ref.shape[0] // sub
    lv = lam_ref[...]
    lam = (jnp.exp(jnp.sum(lv[0:1] * lv[1:2], axis=-1, keepdims=True))
           - jnp.exp(jnp.sum(lv[2:3] * lv[3:4], axis=-1, keepdims=True)) + lam_init)
    lane = lax.broadcasted_iota(jnp.int32, (1, LANES), 1)
    units = [(h, c) for h in range(n_sub) for c in range(2)]

    def scores(u):
        h, c = units[u]
        q = q_ref[h * sub:(h + 1) * sub, :]
        qm = jnp.where((lane < HEAD_DIM) if c == 0 else (lane >= HEAD_DIM), q, jnp.zeros_like(q))
        if has_x:
            s_scr[u % 2, :, 0:nx] = _dot_nt(qm, kx_ref[...])
        s_scr[u % 2, :, nx:nx + nc] = _dot_nt(qm, kc_ref[...])

    def attend(u):
        s = s_scr[u % 2]
        p = jnp.exp2(s - jnp.max(s, axis=-1, keepdims=True))
        den = jnp.sum(p, axis=-1, keepdims=True)
        pb = p.astype(BF16)
        acc = _dot(pb[:, nx:nx + nc], vc_ref[...])
        if has_x:
            acc = acc + _dot(pb[:, 0:nx], vx_ref[...])
        return acc / den

    res = []
    scores(0)
    for u in range(len(units)):
        if u + 1 < len(units):
            scores(u + 1)
        res.append(attend(u))
    for h in range(n_sub):
        o = res[2 * h] - lam * res[2 * h + 1]
        o = _rms(o) * g_ref[...] * (1.0 - lam_init)
        o_ref[h * sub:(h + 1) * sub, :] = o.astype(o_ref.dtype)


def _diff_attention(z, lam_vec, subln_g, lam_init, *, n_batch, seq, n_ctx, n_heads, tq):
    ctx_blk0 = n_batch * seq // n_ctx
    nq = seq // tq
    return pl.pallas_call(
        functools.partial(_diff_kernel, lam_init=lam_init, has_x=True),
        grid=(n_batch, n_heads, nq),
        in_specs=[
            pl.BlockSpec(lam_vec.shape, lambda b, h, i: (0, 0)),
            pl.BlockSpec((1, LANES), lambda b, h, i: (0, 0)),
            pl.BlockSpec((tq, LANES), lambda b, h, i: (b * nq + i, h)),
            pl.BlockSpec((seq, LANES), lambda b, h, i: (b, n_heads + h)),
            pl.BlockSpec((seq, LANES), lambda b, h, i: (b, 2 * n_heads + h)),
            pl.BlockSpec((n_ctx, LANES), lambda b, h, i: (ctx_blk0 + b, n_heads + h)),
            pl.BlockSpec((n_ctx, LANES), lambda b, h, i: (ctx_blk0 + b, 2 * n_heads + h)),
        ],
        out_specs=pl.BlockSpec((tq, LANES), lambda b, h, i: (b * nq + i, h)),
        out_shape=jax.ShapeDtypeStruct((n_batch * seq, n_heads * LANES), BF16),
        scratch_shapes=[pltpu.VMEM((2, min(tq, DIFF_SUB_ROWS), seq + n_ctx), F32)],
        compiler_params=_cparams("parallel", "parallel", "parallel"),
        name="diff_attn",
    )(lam_vec, subln_g, z, z, z, z, z)


def _diff_attention_ctx(z, lam_vec, subln_g, lam_init, *, n_batch, seq, n_ctx, n_heads):
    ctx_blk0 = n_batch * seq // n_ctx
    return pl.pallas_call(
        functools.partial(_diff_kernel, lam_init=lam_init, has_x=False),
        grid=(n_batch, n_heads),
        in_specs=[
            pl.BlockSpec(lam_vec.shape, lambda b, h: (0, 0)),
            pl.BlockSpec((1, LANES), lambda b, h: (0, 0)),
            pl.BlockSpec((n_ctx, LANES), lambda b, h: (ctx_blk0 + b, h)),
            pl.BlockSpec((n_ctx, LANES), lambda b, h: (ctx_blk0 + b, n_heads + h)),
            pl.BlockSpec((n_ctx, LANES), lambda b, h: (ctx_blk0 + b, 2 * n_heads + h)),
        ],
        out_specs=pl.BlockSpec((n_ctx, LANES), lambda b, h: (b, h)),
        out_shape=jax.ShapeDtypeStruct((n_batch * n_ctx, n_heads * LANES), BF16),
        scratch_shapes=[pltpu.VMEM((2, min(n_ctx, DIFF_SUB_ROWS), n_ctx), F32)],
        compiler_params=_cparams("parallel", "parallel"),
        name="diff_attn_ctx",
    )(lam_vec, subln_g, z, z, z)


def _outproj_kernel(a0_ref, a1_ref, w_ref, x_ref, g_ref, gate_ref, o_ref):
    half = a0_ref.shape[1]
    y = _dot(a0_ref[...], w_ref[:half, :]) + _dot(a1_ref[...], w_ref[half:, :])
    o_ref[...] = x_ref[...] + gate_ref[...] * (_rms(y) * g_ref[...])


def _outproj(a0, a0_col, a1, a1_col, w, xt, normg, mod, k_gate, *, tm, rows, seq, n_batch):
    d = xt.shape[1]
    half = d // 2

    def mrow(i):
        return jnp.minimum(i * tm // seq, n_batch)

    return pl.pallas_call(
        _outproj_kernel,
        grid=(rows // tm,),
        in_specs=[
            pl.BlockSpec((tm, half), lambda i: (i, a0_col)),
            pl.BlockSpec((tm, half), lambda i: (i, a1_col)),
            pl.BlockSpec((d, d), lambda i: (0, 0)),
            pl.BlockSpec((tm, d), lambda i: (i, 0)),
            pl.BlockSpec((None, 1, d), lambda i: (1, 0, 0)),
            pl.BlockSpec((None, None, 1, d), lambda i: (mrow(i), k_gate, 0, 0)),
        ],
        out_specs=pl.BlockSpec((tm, d), lambda i: (i, 0)),
        out_shape=jax.ShapeDtypeStruct((rows, d), F32),
        compiler_params=_cparams("parallel"),
        name="outproj",
    )(a0, a1, w, xt, normg, mod)


def _router_kernel(x_ref, g_ref, sh_ref, sc_ref, rw_ref, o_ref, sel_ref):
    h = _modulate(x_ref[...], g_ref[...], sh_ref[...], sc_ref[...])
    logits = _dot_hi(h, rw_ref[...])
    lane = lax.broadcasted_iota(jnp.int32, logits.shape, 1)
    ninf = -jnp.inf
    logits = jnp.where(lane < N_EXPERTS, logits, ninf)
    m1 = jnp.max(logits, axis=-1, keepdims=True)
    i1 = jnp.min(jnp.where(logits == m1, lane, LANES), axis=-1, keepdims=True)
    sel1 = lane == i1
    rest = jnp.where(sel1, ninf, logits)
    m2 = jnp.max(rest, axis=-1, keepdims=True)
    i2 = jnp.min(jnp.where(rest == m2, lane, LANES), axis=-1, keepdims=True)
    sel2 = lane == i2
    e2 = jnp.exp(m2 - m1)
    den = 1.0 + e2
    o_ref[...] = jnp.where(sel1, 1.0 / den, 0.0) + jnp.where(sel2, e2 / den, 0.0)
    sel_ref[...] = jnp.where(sel1 | sel2, 1.0, 0.0)


def _router(xt, normg, mod, router_w, *, tm, rows, seq, n_batch):
    d = xt.shape[1]
    rw = jnp.zeros((d, LANES), F32).at[:, :N_EXPERTS].set(router_w)

    def mrow(i):
        return jnp.minimum(i * tm // seq, n_batch)

    return pl.pallas_call(
        _router_kernel,
        grid=(rows // tm,),
        in_specs=[
            pl.BlockSpec((tm, d), lambda i: (i, 0)),
            pl.BlockSpec((None, 1, d), lambda i: (2, 0, 0)),
            pl.BlockSpec((None, None, 1, d), lambda i: (mrow(i), 3, 0, 0)),
            pl.BlockSpec((None, None, 1, d), lambda i: (mrow(i), 4, 0, 0)),
            pl.BlockSpec((d, LANES), lambda i: (0, 0)),
        ],
        out_specs=[pl.BlockSpec((tm, LANES), lambda i: (i, 0))] * 2,
        out_shape=[jax.ShapeDtypeStruct((rows, LANES), F32)] * 2,
        compiler_params=_cparams("parallel"),
        name="router",
    )(xt, normg, mod, mod, rw)


def _ffn_kernel(x_ref, g2_ref, sh_ref, sc_ref, wg_ref, wu_ref, wo_ref, g3_ref, gate_ref, o_ref, h_scr, acc, *, n_f):
    f = pl.program_id(1)

    @pl.when(f == 0)
    def _():
        h_scr[...] = _modulate(x_ref[...], g2_ref[...], sh_ref[...], sc_ref[...]).astype(BF16)
        acc[...] = jnp.zeros_like(acc)

    h = h_scr[...]
    gp = _dot(h, wg_ref[...])
    up = _dot(h, wu_ref[...])
    a = gp * (1.0 / (1.0 + jnp.exp(-gp))) * up
    acc[...] += _dot(a.astype(BF16), wo_ref[...])

    @pl.when(f == n_f - 1)
    def _():
        o_ref[...] = x_ref[...] + gate_ref[...] * (_rms(acc[...]) * g3_ref[...])


def _ffn(xt, normg, mod, w_in, w_out, *, tm, tf, rows, seq, n_batch):
    d = xt.shape[1]
    fdim = w_out.shape[0]
    n_f = fdim // tf

    def mrow(i):
        return jnp.minimum(i * tm // seq, n_batch)

    in_specs = [
        pl.BlockSpec((tm, d), lambda i, f: (i, 0)),
        pl.BlockSpec((None, 1, d), lambda i, f: (2, 0, 0)),
        pl.BlockSpec((None, None, 1, d), lambda i, f: (mrow(i), 3, 0, 0)),
        pl.BlockSpec((None, None, 1, d), lambda i, f: (mrow(i), 4, 0, 0)),
        pl.BlockSpec((d, tf), lambda i, f: (0, f)),
        pl.BlockSpec((d, tf), lambda i, f: (0, n_f + f)),
        pl.BlockSpec((tf, d), lambda i, f: (f, 0)),
        pl.BlockSpec((None, 1, d), lambda i, f: (3, 0, 0)),
        pl.BlockSpec((None, None, 1, d), lambda i, f: (mrow(i), 5, 0, 0)),
    ]
    return pl.pallas_call(
        functools.partial(_ffn_kernel, n_f=n_f),
        grid=(rows // tm, n_f),
        in_specs=in_specs,
        out_specs=pl.BlockSpec((tm, d), lambda i, f: (i, 0)),
        out_shape=jax.ShapeDtypeStruct((rows, d), F32),
        scratch_shapes=[pltpu.VMEM((tm, d), BF16), pltpu.VMEM((tm, d), F32)],
        compiler_params=_cparams("parallel", "arbitrary"),
        name="ffn",
    )(xt, normg, mod, mod, w_in, w_in, w_out, normg, mod)


MOE_CHUNK = 128


def _moe_kernel(nchunk_ref, x_ref, g2_ref, sh_ref, sc_ref, wg_ref, wu_ref, wo_ref, g3_ref, gate_ref,
                comb_ref, sel_ref, o_ref, h_scr, key_scr, keyt_scr, hs_scr, y_scr, acc_scr, *, n_exp, n_f):
    b = pl.program_id(0)
    e = pl.program_id(1)
    f = pl.program_id(2)
    nb = x_ref.shape[0]
    ch = MOE_CHUNK
    n_chunks = nchunk_ref[b * n_exp + e]

    @pl.when((e == 0) & (f == 0))
    def _():
        h_scr[...] = _modulate(x_ref[...], g2_ref[...], sh_ref[...], sc_ref[...]).astype(BF16)
        o_ref[...] = jnp.zeros_like(o_ref)
        ti = lax.broadcasted_iota(jnp.int32, (nb, nb), 0)
        tj = lax.broadcasted_iota(jnp.int32, (nb, nb), 1)
        lower = jnp.where(tj < ti, 1.0, 0.0).astype(BF16)
        sel = sel_ref[...]
        rank = _dot(lower, sel.astype(BF16))
        key = jnp.where(sel > 0.0, rank, -1.0)
        key_scr[...] = key
        keyt_scr[...] = key.T

    def for_rows(body):
        def pair(i, carry):
            body(pl.multiple_of(i * (2 * ch), 2 * ch), 2 * ch)
            return carry

        lax.fori_loop(0, n_chunks // 2, pair, 0)

        @pl.when(n_chunks % 2 == 1)
        def _():
            body(pl.multiple_of((n_chunks - 1) * ch, ch), ch)

    @pl.when(f == 0)
    def _():
        key_row = keyt_scr[pl.ds(e, 1), :]

        def gather(r0, n):
            r = (r0 + lax.broadcasted_iota(jnp.int32, (n, 1), 0)).astype(F32)
            onehot = jnp.where(key_row == r, 1.0, 0.0).astype(BF16)
            hs_scr[pl.ds(r0, n), :] = _dot(onehot, h_scr[...]).astype(BF16)

        for_rows(gather)

    def expert(r0, n):
        rows = pl.ds(r0, n)
        hs = hs_scr[rows, :]
        gp = _dot(hs, wg_ref[...])
        up = _dot(hs, wu_ref[...])
        a = gp * (1.0 / (1.0 + jnp.exp(-gp))) * up
        part = _dot(a.astype(BF16), wo_ref[...])

        @pl.when(f == 0)
        def _():
            y_scr[rows, :] = part

        @pl.when(f > 0)
        def _():
            y_scr[rows, :] += part

    for_rows(expert)

    @pl.when(f == n_f - 1)
    def _():
        lane = lax.broadcasted_iota(jnp.int32, (nb, LANES), 1)
        pick = lane == e
        key_col = jnp.sum(jnp.where(pick, key_scr[...], 0.0), axis=-1, keepdims=True)
        gate_col = jnp.sum(jnp.where(pick, comb_ref[...], 0.0), axis=-1, keepdims=True)
        acc_scr[...] = jnp.zeros_like(acc_scr)

        def scatter(r0, n):
            r = (r0 + lax.broadcasted_iota(jnp.int32, (1, n), 1)).astype(F32)
            onehot = jnp.where(key_col == r, 1.0, 0.0).astype(BF16)
            acc_scr[...] += _dot(onehot, y_scr[pl.ds(r0, n), :].astype(BF16))

        for_rows(scatter)
        o_ref[...] += gate_col * acc_scr[...]

    @pl.when((e == n_exp - 1) & (f == n_f - 1))
    def _():
        o_ref[...] = x_ref[...] + gate_ref[...] * (_rms(o_ref[...]) * g3_ref[...])


def _moe(xt, normg, mod, w_in, w_out, comb, sel, *, nb, tf, rows, seq, n_batch):
    d = xt.shape[1]
    n_exp, fdim = w_out.shape[0], w_out.shape[1]
    n_f = fdim // tf
    n_blk = rows // nb
    counts = jnp.sum(sel.reshape(n_blk, nb, LANES)[:, :, :n_exp], axis=1).astype(jnp.int32)
    nchunk = ((counts + (MOE_CHUNK - 1)) // MOE_CHUNK).reshape(n_blk * n_exp)

    def mrow(i):
        return jnp.minimum(i * nb // seq, n_batch)

    in_specs = [
        pl.BlockSpec((nb, d), lambda i, e, f, s: (i, 0)),
        pl.BlockSpec((None, 1, d), lambda i, e, f, s: (2, 0, 0)),
        pl.BlockSpec((None, None, 1, d), lambda i, e, f, s: (mrow(i), 3, 0, 0)),
        pl.BlockSpec((None, None, 1, d), lambda i, e, f, s: (mrow(i), 4, 0, 0)),
        pl.BlockSpec((None, d, tf), lambda i, e, f, s: (e, 0, f)),
        pl.BlockSpec((None, d, tf), lambda i, e, f, s: (e, 0, n_f + f)),
        pl.BlockSpec((None, tf, d), lambda i, e, f, s: (e, f, 0)),
        pl.BlockSpec((None, 1, d), lambda i, e, f, s: (3, 0, 0)),
        pl.BlockSpec((None, None, 1, d), lambda i, e, f, s: (mrow(i), 5, 0, 0)),
        pl.BlockSpec((nb, LANES), lambda i, e, f, s: (i, 0)),
        pl.BlockSpec((nb, LANES), lambda i, e, f, s: (i, 0)),
    ]
    return pl.pallas_call(
        functools.partial(_moe_kernel, n_exp=n_exp, n_f=n_f),
        grid_spec=pltpu.PrefetchScalarGridSpec(
            num_scalar_prefetch=1,
            grid=(n_blk, n_exp, n_f),
            in_specs=in_specs,
            out_specs=pl.BlockSpec((nb, d), lambda i, e, f, s: (i, 0)),
            scratch_shapes=[
                pltpu.VMEM((nb, d), BF16),
                pltpu.VMEM((nb, LANES), F32),
                pltpu.VMEM((LANES, nb), F32),
                pltpu.VMEM((nb, d), BF16),
                pltpu.VMEM((nb, d), F32),
                pltpu.VMEM((nb, d), F32),
            ],
        ),
        out_shape=jax.ShapeDtypeStruct((rows, d), F32),
        compiler_params=_cparams("parallel", "arbitrary", "arbitrary"),
        name="moe",
    )(nchunk, xt, normg, mod, mod, w_in, w_in, w_out, normg, mod, comb, sel)


def _lambda_init(layer):
    return 0.8 - 0.6 * math.exp(-0.3 * layer)


def kernel(x, c, ctx, c_ctx, ada_w, ada_b, norm_g, mix_in_w, mix_out_w, win_sink, diff_qkv_w, diff_out_w,
           diff_lambda, diff_subln_g, ffn_in_w, ffn_out_w, router_w, expert_in_w, expert_out_w):
    n_batch, seq, d = x.shape
    n_ctx = ctx.shape[1]
    depth = ada_w.shape[0]
    n_lat = n_batch * seq
    n_all = n_lat + n_batch * n_ctx
    fdim = mix_in_w.shape[2] - (d // 2 + 2 * LANES)
    q_cols = d // 2
    n_diff_heads = d // LANES

    tm_proj = _pick_tile(512, seq, n_batch * n_ctx)
    tm_out = _pick_tile(512, seq, n_batch * n_ctx)
    tm_ffn = _pick_tile(512, seq, n_batch * n_ctx)
    nb_moe = _pick_tile(1024, seq, n_batch * n_ctx)
    tq_diff = _pick_tile(512, seq)
    common = dict(seq=seq, n_batch=n_batch)

    xt = jnp.concatenate([x.reshape(n_lat, d), ctx.reshape(n_batch * n_ctx, d)], axis=0)
    n_mod = -(-(n_batch + 1) // 8) * 8
    cv = jnp.zeros((n_mod, d), F32).at[:n_batch].set(c).at[n_batch].set(c_ctx)
    mods = _modvec(cv, ada_w, ada_b).reshape(depth, n_mod, 6, 1, d)
    rope = _rope_tables(seq, tm_proj)
    f_tables = _fourier_tables(seq)
    fc_tables = _dense_fourier_tables(n_ctx)

    n_f = fdim // LANES
    plan_even = ([(0, i * LANES, "plain") for i in range(n_f)]
                 + [(1, i * LANES, "rope_q") for i in range(q_cols // LANES)]
                 + [(1, q_cols, "rope_k"), (1, q_cols + LANES, "plain")])
    plan_odd = ([(0, i * LANES, "rope_q_log2") for i in range(n_diff_heads)]
                + [(0, d + i * LANES, "rope_k") for i in range(n_diff_heads)]
                + [(0, 2 * d + i * LANES, "plain") for i in range(n_diff_heads)])

    for layer in range(depth):
        j = layer // 2
        need_ctx = layer < depth - 1
        rows = n_all if need_ctx else n_lat
        mod = mods[layer]
        ng = norm_g[layer].reshape(4, 1, d)
        if layer % 2 == 0:
            f, z = _proj(xt, ng, mod, 0, 1, mix_in_w[j].astype(BF16), rope, plan_even,
                         [(fdim, F32), (q_cols + 2 * LANES, BF16)], tm=tm_proj, n_lat=n_lat, **common)
            k_col, v_col = q_cols // LANES, q_cols // LANES + 1
            att = dict(n_batch=n_batch, seq=seq, n_ctx=n_ctx, q_cols=q_cols, k_col=k_col, v_col=v_col)
            mix_f = _fourier(f, f_tables, n_batch=n_batch, seq=seq)
            mix_a = _win_attention(z, win_sink[j], **att)
            if need_ctx:
                mix_f = jnp.concatenate(
                    [mix_f, _dense_fourier(f, fc_tables, n_batch=n_batch, n_pos=n_ctx, row_block0=n_lat // n_ctx)])
                mix_a = jnp.concatenate([mix_a, _ctx_gqa_attention(z, win_sink[j], **att)])
            xt = _outproj(mix_f, 0, mix_a, 0, mix_out_w[j].astype(BF16), xt, ng, mod, 2, tm=tm_out, rows=rows,
                          **common)
            xt = _ffn(xt, ng, mod, ffn_in_w[j].astype(BF16), ffn_out_w[j].astype(BF16),
                      tm=tm_ffn, tf=ffn_out_w.shape[1] // 2, rows=rows, **common)
        else:
            lam_init = _lambda_init(layer)
            (z,) = _proj(xt, ng, mod, 0, 1, diff_qkv_w[j].astype(BF16), rope, plan_odd, [(3 * d, BF16)],
                         tm=tm_proj, n_lat=n_lat, **common)
            subg = diff_subln_g[j].reshape(1, LANES)
            att = dict(n_batch=n_batch, seq=seq, n_ctx=n_ctx, n_heads=n_diff_heads)
            mix = _diff_attention(z, diff_lambda[j], subg, lam_init, tq=tq_diff, **att)
            if need_ctx:
                mix = jnp.concatenate([mix, _diff_attention_ctx(z, diff_lambda[j], subg, lam_init, **att)])
            xt = _outproj(mix, 0, mix, 1, diff_out_w[j].astype(BF16), xt, ng, mod, 2, tm=tm_out, rows=rows,
                          **common)
            comb, sel = _router(xt, ng, mod, router_w[j], tm=tm_out, rows=rows, **common)
            xt = _moe(xt, ng, mod, expert_in_w[j].astype(BF16), expert_out_w[j].astype(BF16), comb, sel,
                      nb=nb_moe, tf=896, rows=rows, **common)
    return xt[:n_lat].reshape(n_batch, seq, d)
```

```python
import functools
import math

import numpy as np
import jax
import jax.numpy as jnp
from jax import lax
from jax.experimental import pallas as pl
from jax.experimental.pallas import tpu as pltpu

EPS = 1e-6
NEG = -1e30
HEAD_DIM = 64
LANES = 128
GRID_W = 64
BLOCK = 128
WINDOW = 128
ROPE_THETA = 10000.0
N_EXPERTS = 8
F32 = jnp.float32
BF16 = jnp.bfloat16
HIGHEST = lax.Precision.HIGHEST
VMEM_LIMIT = 56 * 1024 * 1024


def _cparams(*sem):
    return pltpu.CompilerParams(dimension_semantics=sem, vmem_limit_bytes=VMEM_LIMIT)


def _dot(a, b):
    return jnp.dot(a, b, preferred_element_type=F32)


def _dot_nt(a, b):
    return lax.dot_general(a, b, (((1,), (1,)), ((), ())), preferred_element_type=F32)


def _dot_hi(a, b):
    return jnp.dot(a, b, precision=HIGHEST, preferred_element_type=F32)


def _rms(v):
    return v * lax.rsqrt(jnp.mean(v * v, axis=-1, keepdims=True) + EPS)


def _modulate(x, g, sh, sc):
    return _rms(x) * g * (1.0 + sc) + sh


def _pick_tile(pref, *dims):
    t = pref
    while any(d % t for d in dims):
        t //= 2
    return t


def _modvec_kernel(c_ref, w_ref, b_ref, o_ref):
    cv = c_ref[...]
    s = cv * (1.0 / (1.0 + jnp.exp(-cv)))
    o_ref[...] = _dot(s.astype(BF16), w_ref[...].astype(BF16)) + b_ref[...]


def _modvec(cv, ada_w, ada_b):
    depth, d, n = ada_w.shape
    r = cv.shape[0]
    tn = _pick_tile(1536, n)
    return pl.pallas_call(
        _modvec_kernel,
        grid=(depth, n // tn),
        in_specs=[
            pl.BlockSpec((r, d), lambda l, j: (0, 0)),
            pl.BlockSpec((None, d, tn), lambda l, j: (l, 0, j)),
            pl.BlockSpec((None, 1, tn), lambda l, j: (l, 0, j)),
        ],
        out_specs=pl.BlockSpec((None, r, tn), lambda l, j: (l, 0, j)),
        out_shape=jax.ShapeDtypeStruct((depth, r, n), F32),
        compiler_params=_cparams("parallel", "parallel"),
        name="modvec",
    )(cv, ada_w, ada_b.reshape(depth, 1, n))


def _proj_kernel(x_ref, g_ref, sh_ref, sc_ref, w_ref, cos_ref, sa_ref, sb_ref, *o_refs, plan, group):
    h = _modulate(x_ref[...], g_ref[...], sh_ref[...], sc_ref[...]).astype(BF16)
    n = w_ref.shape[1]
    for g0 in range(0, n, group):
        acc = _dot(h, w_ref[:, g0:g0 + group])
        for c0 in range(0, group, LANES):
            oi, oc, mode = plan[(g0 + c0) // LANES]
            v = acc[:, c0:c0 + LANES]
            if mode != "plain":
                v = (v * cos_ref[...] + pltpu.roll(v, LANES - 16, 1) * sa_ref[...]
                     + pltpu.roll(v, 16, 1) * sb_ref[...])
                if mode == "rope_q_log2":
                    v = v * (HEAD_DIM ** -0.5 * LOG2E)
            o_refs[oi][:, oc:oc + LANES] = v.astype(o_refs[oi].dtype)


def _proj(xt, normg, mod, k_sh, k_sc, w, rope, plan, outs, *, tm, n_lat, seq, n_batch):
    t, d = xt.shape
    n = w.shape[1]
    group = _pick_tile(512, n)
    nx = n_lat // tm
    per = seq // tm

    def mrow(i):
        return jnp.minimum(i * tm // seq, n_batch)

    def rrow(i):
        return jnp.where(i < nx, i % per, per)

    in_specs = [
        pl.BlockSpec((tm, d), lambda i: (i, 0)),
        pl.BlockSpec((None, 1, d), lambda i: (0, 0, 0)),
        pl.BlockSpec((None, None, 1, d), lambda i: (mrow(i), k_sh, 0, 0)),
        pl.BlockSpec((None, None, 1, d), lambda i: (mrow(i), k_sc, 0, 0)),
        pl.BlockSpec((d, n), lambda i: (0, 0)),
        pl.BlockSpec((tm, LANES), lambda i: (rrow(i), 0)),
        pl.BlockSpec((tm, LANES), lambda i: (rrow(i), 0)),
        pl.BlockSpec((tm, LANES), lambda i: (rrow(i), 0)),
    ]
    out_specs = [pl.BlockSpec((tm, wd), lambda i: (i, 0)) for wd, _ in outs]
    out_shape = [jax.ShapeDtypeStruct((t, wd), dt) for wd, dt in outs]
    return pl.pallas_call(
        functools.partial(_proj_kernel, plan=plan, group=group),
        grid=(t // tm,),
        in_specs=in_specs,
        out_specs=out_specs,
        out_shape=out_shape,
        compiler_params=_cparams("parallel"),
        name="proj",
    )(xt, normg, mod, mod, w, *rope)


def _rope_tables(seq, tm):
    rows_count = seq // GRID_W
    rows = jnp.repeat(jnp.arange(rows_count), GRID_W).astype(F32)
    cols = jnp.tile(jnp.arange(GRID_W), rows_count).astype(F32)
    axis_dim = HEAD_DIM // 2
    inv = ROPE_THETA ** (-jnp.arange(0, axis_dim, 2, dtype=F32) / axis_dim)
    ar = rows[:, None] * inv
    ac = cols[:, None] * inv
    cr, sr, cc, sc = jnp.cos(ar), jnp.sin(ar), jnp.cos(ac), jnp.sin(ac)
    z = jnp.zeros_like(sr)
    reps = LANES // HEAD_DIM
    cos = jnp.tile(jnp.concatenate([cr, cr, cc, cc], axis=1), (1, reps))
    sa = jnp.tile(jnp.concatenate([-sr, z, -sc, z], axis=1), (1, reps))
    sb = jnp.tile(jnp.concatenate([z, sr, z, sc], axis=1), (1, reps))
    ident = jnp.ones((tm, LANES), F32)
    zero = jnp.zeros((tm, LANES), F32)
    return (jnp.concatenate([cos, ident]), jnp.concatenate([sa, zero]), jnp.concatenate([sb, zero]))


FOURIER_LANES = 2 * LANES


def _split_bf16(t):
    hi = t.astype(BF16)
    return hi, (t - hi.astype(F32)).astype(BF16)


def _dot_split(a, b):
    return _dot(a[0], b[0]) + _dot(a[0], b[1]) + _dot(a[1], b[0])


def _fourier_tables(seq):
    n2 = GRID_W
    n1 = seq // n2
    norm = 1.0 / math.sqrt(seq * LANES)
    a = np.arange(n1)
    k1 = np.arange(n1)
    b = np.arange(n2)
    ang = (b[:, None, None] * k1[None, :, None] + (seq // n1) * k1[None, :, None] * a[None, None, :]) % seq
    th = 2.0 * np.pi * ang / seq
    m1 = np.concatenate([np.cos(th), -np.sin(th)], axis=1).astype(np.float32)
    ph = 2.0 * np.pi * ((b[:, None] * b[None, :]) % n2) / n2
    c2, s2 = np.cos(ph), np.sin(ph)
    g2 = np.block([[c2, s2], [-s2, c2]]).astype(np.float32)
    ch = np.arange(LANES)
    pc = 2.0 * np.pi * ((ch[:, None] * ch[None, :]) % LANES) / LANES
    cc = (np.cos(pc) * norm).astype(np.float32)
    sc = (np.sin(pc) * norm).astype(np.float32)
    cs = np.concatenate([cc, sc], axis=0)
    out = []
    for t in (m1, g2, cs):
        out.extend(_split_bf16(jnp.asarray(t)))
    return tuple(out)


FOURIER_ROW_CHUNK = 512


def _fourier_kernel(*refs, n1, n2, n_grp):
    u_refs = refs[:n_grp]
    m1h_ref, m1l_ref, g2h_ref, g2l_ref, csh_ref, csl_ref, o_ref, b_scr, xr_scr, xi_scr = refs[n_grp:]
    seq = n1 * n2
    for b in range(n2):
        xs = jnp.concatenate([u[pl.ds(b, n1, stride=n2), :] for u in u_refs], axis=1)
        z = _dot_split((m1h_ref[b], m1l_ref[b]), _split_bf16(xs))
        for g in range(n_grp):
            b_scr[g, 2 * n1 * b:2 * n1 * (b + 1), :] = z[:, g * LANES:(g + 1) * LANES]
    g2 = (g2h_ref[...], g2l_ref[...])
    for k1 in range(n1):
        bk = jnp.concatenate(
            [jnp.concatenate([b_scr.at[g][pl.ds(k1, n2, stride=2 * n1), :],
                              b_scr.at[g][pl.ds(n1 + k1, n2, stride=2 * n1), :]], axis=0)
             for g in range(n_grp)], axis=1)
        xk = _dot_split(g2, _split_bf16(bk))
        for g in range(n_grp):
            xr_scr[g, n2 * k1:n2 * (k1 + 1), :] = xk[:n2, g * LANES:(g + 1) * LANES]
            xi_scr[g, n2 * k1:n2 * (k1 + 1), :] = xk[n2:, g * LANES:(g + 1) * LANES]
    cs = (csh_ref[...], csl_ref[...])
    rc = min(FOURIER_ROW_CHUNK, seq)
    for g in range(n_grp):
        for r0 in range(0, seq, rc):
            x = jnp.concatenate([xr_scr[g, r0:r0 + rc, :], xi_scr[g, r0:r0 + rc, :]], axis=1)
            xr_scr[g, r0:r0 + rc, :] = _dot_split(_split_bf16(x), cs)
    for g in range(n_grp):
        for k2 in range(n2):
            o_ref[n1 * k2:n1 * (k2 + 1), g * LANES:(g + 1) * LANES] = (
                xr_scr.at[g][pl.ds(k2, n1, stride=n2), :].astype(o_ref.dtype))


def _fourier(f, tables, *, n_batch, seq, out_rows):
    n2 = GRID_W
    n1 = seq // n2
    n_grp = FOURIER_LANES // LANES
    table_specs = [pl.BlockSpec(t.shape, (lambda b, g, nd=t.ndim: (0,) * nd)) for t in tables]
    u_specs = [pl.BlockSpec((seq, LANES), (lambda b, g, k=k: (b, g * n_grp + k))) for k in range(n_grp)]
    return pl.pallas_call(
        functools.partial(_fourier_kernel, n1=n1, n2=n2, n_grp=n_grp),
        grid=(n_batch, f.shape[1] // FOURIER_LANES),
        in_specs=u_specs + table_specs,
        out_specs=pl.BlockSpec((seq, FOURIER_LANES), lambda b, g: (b, g)),
        out_shape=jax.ShapeDtypeStruct((out_rows, f.shape[1]), BF16),
        scratch_shapes=[pltpu.VMEM((n_grp, 2 * seq, LANES), F32), pltpu.VMEM((n_grp, seq, LANES), F32),
                        pltpu.VMEM((n_grp, seq, LANES), F32)],
        compiler_params=_cparams("parallel", "parallel"),
        name="fourier",
    )(*([f] * n_grp), *tables)


def _dense_fourier_tables(n):
    norm = 1.0 / math.sqrt(n * LANES)
    p = np.arange(n)
    ph = 2.0 * np.pi * ((p[:, None] * p[None, :]) % n) / n
    ch = np.arange(LANES)
    pc = 2.0 * np.pi * ((ch[:, None] * ch[None, :]) % LANES) / LANES
    return (jnp.asarray(np.cos(ph).astype(np.float32)), jnp.asarray(np.sin(ph).astype(np.float32)),
            jnp.asarray((np.cos(pc) * norm).astype(np.float32)), jnp.asarray((np.sin(pc) * norm).astype(np.float32)))


def _dense_fourier_kernel(u_ref, cl_ref, sl_ref, cc_ref, sc_ref, _, o_ref):
    u = u_ref[...]
    y = _dot_hi(cl_ref[...], _dot_hi(u, cc_ref[...])) - _dot_hi(sl_ref[...], _dot_hi(u, sc_ref[...]))
    o_ref[...] = y.astype(o_ref.dtype)


def _dense_fourier(f, tables, dst, *, n_batch, n_pos, row_block0):
    cl, sl, cc, sc = tables
    groups = f.shape[1] // LANES
    return pl.pallas_call(
        _dense_fourier_kernel,
        grid=(n_batch, groups),
        in_specs=[
            pl.BlockSpec((n_pos, LANES), lambda b, g: (row_block0 + b, g)),
            pl.BlockSpec(cl.shape, lambda b, g: (0, 0)),
            pl.BlockSpec(sl.shape, lambda b, g: (0, 0)),
            pl.BlockSpec(cc.shape, lambda b, g: (0, 0)),
            pl.BlockSpec(sc.shape, lambda b, g: (0, 0)),
            pl.BlockSpec(memory_space=pl.ANY),
        ],
        out_specs=pl.BlockSpec((n_pos, LANES), lambda b, g: (row_block0 + b, g)),
        out_shape=jax.ShapeDtypeStruct(dst.shape, dst.dtype),
        input_output_aliases={5: 0},
        compiler_params=_cparams("parallel", "parallel"),
        name="fourier_ctx",
    )(f, cl, sl, cc, sc, dst)


def _win_kernel(sink_ref, q_ref, *refs, n_qtiles, has_local, n_heads, group_size):
    if has_local:
        kp_ref, kc_ref, kn_ref, vp_ref, vc_ref, vn_ref, kx_ref, vx_ref, o_ref = refs
    else:
        kx_ref, vx_ref, _, o_ref = refs
    tq = q_ref.shape[0]
    lane = lax.broadcasted_iota(jnp.int32, (1, LANES), 1)
    half_mask = [lane < HEAD_DIM, lane >= HEAD_DIM]
    pieces = [(kx_ref[...], vx_ref[...], None)]
    if has_local:
        n = pl.program_id(1)
        qi = lax.broadcasted_iota(jnp.int32, (tq, BLOCK), 0)
        kj = lax.broadcasted_iota(jnp.int32, (tq, BLOCK), 1)
        valid_prev = (kj >= qi) & (n >= 1)
        valid_next = (kj <= qi - (tq - WINDOW)) & (n <= n_qtiles - 2)
        di = lax.broadcasted_iota(jnp.int32, (tq, tq), 0) - lax.broadcasted_iota(jnp.int32, (tq, tq), 1)
        valid_mid = (di <= WINDOW) & (di >= -WINDOW)
        pieces += [(kp_ref[...], vp_ref[...], valid_prev), (kc_ref[...], vc_ref[...], valid_mid),
                   (kn_ref[...], vn_ref[...], valid_next)]
    for pair in range(n_heads // 2):
        qp = q_ref[:, pair * LANES:(pair + 1) * LANES].astype(F32)
        qp_sw = pltpu.roll(qp, HEAD_DIM, 1)
        out_pair = jnp.zeros((tq, LANES), F32)
        for half in range(2):
            head = 2 * pair + half
            kv = head // group_size
            src = qp if half == kv else qp_sw
            qe = jnp.where(half_mask[kv], src, 0.0).astype(BF16)
            sink = sink_ref[head] * LOG2E
            scores = []
            m = jnp.zeros((tq, 1), F32) + sink
            for k, _, valid in pieces:
                s = _dot_nt(qe, k)
                if valid is not None:
                    s = jnp.where(valid, s, NEG)
                scores.append(s)
                m = jnp.maximum(m, jnp.max(s, axis=-1, keepdims=True))
            den = jnp.exp2(sink - m)
            pv = jnp.zeros((tq, LANES), F32)
            for s, (_, v, _) in zip(scores, pieces):
                p = jnp.exp2(s - m)
                den = den + jnp.sum(p, axis=-1, keepdims=True)
                pv = pv + _dot(p.astype(BF16), v)
            pv = pv / den
            if half != kv:
                pv = pltpu.roll(pv, HEAD_DIM, 1)
            out_pair = jnp.where(half_mask[half], pv, out_pair)
        o_ref[:, pair * LANES:(pair + 1) * LANES] = out_pair.astype(o_ref.dtype)


def _win_attention(z, sink, *, n_batch, seq, n_ctx, q_cols, k_col, v_col, tq, out_rows):
    nbk = seq // BLOCK
    nq = seq // tq
    per = tq // BLOCK
    ctx_blk0 = n_batch * seq // n_ctx
    n_heads = q_cols // HEAD_DIM
    group_size = n_heads // (LANES // HEAD_DIM)

    def edge(col, first):
        return pl.BlockSpec(
            (BLOCK, LANES), lambda b, n: (b * nbk + jnp.clip(n * per + first, 0, nbk - 1), col))

    def mid(col):
        return pl.BlockSpec((tq, LANES), lambda b, n: (b * nq + n, col))

    return pl.pallas_call(
        functools.partial(_win_kernel, n_qtiles=nq, has_local=True, n_heads=n_heads, group_size=group_size),
        grid=(n_batch, nq),
        in_specs=[
            pl.BlockSpec(memory_space=pltpu.SMEM),
            pl.BlockSpec((tq, q_cols), lambda b, n: (b * nq + n, 0)),
            edge(k_col, -1), mid(k_col), edge(k_col, per),
            edge(v_col, -1), mid(v_col), edge(v_col, per),
            pl.BlockSpec((n_ctx, LANES), lambda b, n: (ctx_blk0 + b, k_col)),
            pl.BlockSpec((n_ctx, LANES), lambda b, n: (ctx_blk0 + b, v_col)),
        ],
        out_specs=pl.BlockSpec((tq, q_cols), lambda b, n: (b * nq + n, 0)),
        out_shape=jax.ShapeDtypeStruct((out_rows, q_cols), BF16),
        compiler_params=_cparams("parallel", "parallel"),
        name="win_attn",
    )(sink, z, z, z, z, z, z, z, z, z)


def _ctx_gqa_attention(z, sink, dst, *, n_batch, seq, n_ctx, q_cols, k_col, v_col):
    ctx_blk0 = n_batch * seq // n_ctx
    n_heads = q_cols // HEAD_DIM
    group_size = n_heads // (LANES // HEAD_DIM)
    return pl.pallas_call(
        functools.partial(_win_kernel, n_qtiles=0, has_local=False, n_heads=n_heads, group_size=group_size),
        grid=(n_batch,),
        in_specs=[
            pl.BlockSpec(memory_space=pltpu.SMEM),
            pl.BlockSpec((n_ctx, q_cols), lambda b: (ctx_blk0 + b, 0)),
            pl.BlockSpec((n_ctx, LANES), lambda b: (ctx_blk0 + b, k_col)),
            pl.BlockSpec((n_ctx, LANES), lambda b: (ctx_blk0 + b, v_col)),
            pl.BlockSpec(memory_space=pl.ANY),
        ],
        out_specs=pl.BlockSpec((n_ctx, q_cols), lambda b: (ctx_blk0 + b, 0)),
        out_shape=jax.ShapeDtypeStruct(dst.shape, dst.dtype),
        input_output_aliases={4: 0},
        compiler_params=_cparams("parallel"),
        name="ctx_gqa",
    )(sink, z, z, z, dst)


LOG2E = math.log2(math.e)
DIFF_SUB_ROWS = 128


def _diff_kernel(lam_ref, g_ref, q_ref, *refs, lam_init, has_x):
    if has_x:
        kx_ref, vx_ref, kc_ref, vc_ref, o_ref, s_scr = refs
        nx = kx_ref.shape[0]
    else:
        kc_ref, vc_ref, _, o_ref, s_scr = refs
        nx = 0
    nc = kc_ref.shape[0]
    sub = s_scr.shape[1]
    n_sub = q_ref.shape[0] // sub
    lv = lam_ref[...]
    lam = (jnp.exp(jnp.sum(lv[0:1] * lv[1:2], axis=-1, keepdims=True))
           - jnp.exp(jnp.sum(lv[2:3] * lv[3:4], axis=-1, keepdims=True)) + lam_init)
    lane = lax.broadcasted_iota(jnp.int32, (1, LANES), 1)
    units = [(h, c) for h in range(n_sub) for c in range(2)]

    def scores(u):
        h, c = units[u]
        q = q_ref[h * sub:(h + 1) * sub, :]
        qm = jnp.where((lane < HEAD_DIM) if c == 0 else (lane >= HEAD_DIM), q, jnp.zeros_like(q))
        if has_x:
            s_scr[u % 2, :, 0:nx] = _dot_nt(qm, kx_ref[...])
        s_scr[u % 2, :, nx:nx + nc] = _dot_nt(qm, kc_ref[...])

    def attend(u):
        s = s_scr[u % 2]
        p = jnp.exp2(s - jnp.max(s, axis=-1, keepdims=True))
        den = jnp.sum(p, axis=-1, keepdims=True)
        pb = p.astype(BF16)
        acc = _dot(pb[:, nx:nx + nc], vc_ref[...])
        if has_x:
            acc = acc + _dot(pb[:, 0:nx], vx_ref[...])
        return acc / den

    res = []
    scores(0)
    for u in range(len(units)):
        if u + 1 < len(units):
            scores(u + 1)
        res.append(attend(u))
    for h in range(n_sub):
        o = res[2 * h] - lam * res[2 * h + 1]
        o = _rms(o) * g_ref[...] * (1.0 - lam_init)
        o_ref[h * sub:(h + 1) * sub, :] = o.astype(o_ref.dtype)


def _diff_attention(z, lam_vec, subln_g, lam_init, *, n_batch, seq, n_ctx, n_heads, tq, out_rows):
    ctx_blk0 = n_batch * seq // n_ctx
    nq = seq // tq
    return pl.pallas_call(
        functools.partial(_diff_kernel, lam_init=lam_init, has_x=True),
        grid=(n_batch, n_heads, nq),
        in_specs=[
            pl.BlockSpec(lam_vec.shape, lambda b, h, i: (0, 0)),
            pl.BlockSpec((1, LANES), lambda b, h, i: (0, 0)),
            pl.BlockSpec((tq, LANES), lambda b, h, i: (b * nq + i, h)),
            pl.BlockSpec((seq, LANES), lambda b, h, i: (b, n_heads + h)),
            pl.BlockSpec((seq, LANES), lambda b, h, i: (b, 2 * n_heads + h)),
            pl.BlockSpec((n_ctx, LANES), lambda b, h, i: (ctx_blk0 + b, n_heads + h)),
            pl.BlockSpec((n_ctx, LANES), lambda b, h, i: (ctx_blk0 + b, 2 * n_heads + h)),
        ],
        out_specs=pl.BlockSpec((tq, LANES), lambda b, h, i: (b * nq + i, h)),
        out_shape=jax.ShapeDtypeStruct((out_rows, n_heads * LANES), BF16),
        scratch_shapes=[pltpu.VMEM((2, min(tq, DIFF_SUB_ROWS), seq + n_ctx), F32)],
        compiler_params=_cparams("parallel", "parallel", "parallel"),
        name="diff_attn",
    )(lam_vec, subln_g, z, z, z, z, z)


def _diff_attention_ctx(z, lam_vec, subln_g, lam_init, dst, *, n_batch, seq, n_ctx, n_heads):
    ctx_blk0 = n_batch * seq // n_ctx
    return pl.pallas_call(
        functools.partial(_diff_kernel, lam_init=lam_init, has_x=False),
        grid=(n_batch, n_heads),
        in_specs=[
            pl.BlockSpec(lam_vec.shape, lambda b, h: (0, 0)),
            pl.BlockSpec((1, LANES), lambda b, h: (0, 0)),
            pl.BlockSpec((n_ctx, LANES), lambda b, h: (ctx_blk0 + b, h)),
            pl.BlockSpec((n_ctx, LANES), lambda b, h: (ctx_blk0 + b, n_heads + h)),
            pl.BlockSpec((n_ctx, LANES), lambda b, h: (ctx_blk0 + b, 2 * n_heads + h)),
            pl.BlockSpec(memory_space=pl.ANY),
        ],
        out_specs=pl.BlockSpec((n_ctx, LANES), lambda b, h: (ctx_blk0 + b, h)),
        out_shape=jax.ShapeDtypeStruct(dst.shape, dst.dtype),
        input_output_aliases={5: 0},
        scratch_shapes=[pltpu.VMEM((2, min(n_ctx, DIFF_SUB_ROWS), n_ctx), F32)],
        compiler_params=_cparams("parallel", "parallel"),
        name="diff_attn_ctx",
    )(lam_vec, subln_g, z, z, z, dst)


def _outproj_kernel(a0_ref, a1_ref, w_ref, x_ref, g_ref, gate_ref, o_ref):
    half = a0_ref.shape[1]
    y = _dot(a0_ref[...], w_ref[:half, :]) + _dot(a1_ref[...], w_ref[half:, :])
    o_ref[...] = x_ref[...] + gate_ref[...] * (_rms(y) * g_ref[...])


def _outproj(a0, a0_col, a1, a1_col, w, xt, normg, mod, k_gate, *, tm, rows, seq, n_batch):
    d = xt.shape[1]
    half = d // 2

    def mrow(i):
        return jnp.minimum(i * tm // seq, n_batch)

    return pl.pallas_call(
        _outproj_kernel,
        grid=(rows // tm,),
        in_specs=[
            pl.BlockSpec((tm, half), lambda i: (i, a0_col)),
            pl.BlockSpec((tm, half), lambda i: (i, a1_col)),
            pl.BlockSpec((d, d), lambda i: (0, 0)),
            pl.BlockSpec((tm, d), lambda i: (i, 0)),
            pl.BlockSpec((None, 1, d), lambda i: (1, 0, 0)),
            pl.BlockSpec((None, None, 1, d), lambda i: (mrow(i), k_gate, 0, 0)),
        ],
        out_specs=pl.BlockSpec((tm, d), lambda i: (i, 0)),
        out_shape=jax.ShapeDtypeStruct((rows, d), F32),
        compiler_params=_cparams("parallel"),
        name="outproj",
    )(a0, a1, w, xt, normg, mod)


def _router_kernel(x_ref, g_ref, sh_ref, sc_ref, rw_ref, o_ref, sel_ref):
    h = _modulate(x_ref[...], g_ref[...], sh_ref[...], sc_ref[...])
    logits = _dot_hi(h, rw_ref[...])
    lane = lax.broadcasted_iota(jnp.int32, logits.shape, 1)
    ninf = -jnp.inf
    logits = jnp.where(lane < N_EXPERTS, logits, ninf)
    m1 = jnp.max(logits, axis=-1, keepdims=True)
    i1 = jnp.min(jnp.where(logits == m1, lane, LANES), axis=-1, keepdims=True)
    sel1 = lane == i1
    rest = jnp.where(sel1, ninf, logits)
    m2 = jnp.max(rest, axis=-1, keepdims=True)
    i2 = jnp.min(jnp.where(rest == m2, lane, LANES), axis=-1, keepdims=True)
    sel2 = lane == i2
    e2 = jnp.exp(m2 - m1)
    den = 1.0 + e2
    o_ref[...] = jnp.where(sel1, 1.0 / den, 0.0) + jnp.where(sel2, e2 / den, 0.0)
    sel_ref[...] = jnp.where(sel1 | sel2, 1.0, 0.0)


def _router(xt, normg, mod, router_w, *, tm, rows, seq, n_batch):
    d = xt.shape[1]
    rw = jnp.zeros((d, LANES), F32).at[:, :N_EXPERTS].set(router_w)

    def mrow(i):
        return jnp.minimum(i * tm // seq, n_batch)

    return pl.pallas_call(
        _router_kernel,
        grid=(rows // tm,),
        in_specs=[
            pl.BlockSpec((tm, d), lambda i: (i, 0)),
            pl.BlockSpec((None, 1, d), lambda i: (2, 0, 0)),
            pl.BlockSpec((None, None, 1, d), lambda i: (mrow(i), 3, 0, 0)),
            pl.BlockSpec((None, None, 1, d), lambda i: (mrow(i), 4, 0, 0)),
            pl.BlockSpec((d, LANES), lambda i: (0, 0)),
        ],
        out_specs=[pl.BlockSpec((tm, LANES), lambda i: (i, 0))] * 2,
        out_shape=[jax.ShapeDtypeStruct((rows, LANES), F32)] * 2,
        compiler_params=_cparams("parallel"),
        name="router",
    )(xt, normg, mod, mod, rw)


def _ffn_kernel(x_ref, g2_ref, sh_ref, sc_ref, wg_ref, wu_ref, wo_ref, g3_ref, gate_ref, o_ref, h_scr, acc, *, n_f):
    f = pl.program_id(1)

    @pl.when(f == 0)
    def _():
        h_scr[...] = _modulate(x_ref[...], g2_ref[...], sh_ref[...], sc_ref[...]).astype(BF16)
        acc[...] = jnp.zeros_like(acc)

    h = h_scr[...]
    gp = _dot(h, wg_ref[...])
    up = _dot(h, wu_ref[...])
    a = gp * (1.0 / (1.0 + jnp.exp(-gp))) * up
    acc[...] += _dot(a.astype(BF16), wo_ref[...])

    @pl.when(f == n_f - 1)
    def _():
        o_ref[...] = x_ref[...] + gate_ref[...] * (_rms(acc[...]) * g3_ref[...])


def _ffn(xt, normg, mod, w_in, w_out, *, tm, tf, rows, seq, n_batch):
    d = xt.shape[1]
    fdim = w_out.shape[0]
    n_f = fdim // tf

    def mrow(i):
        return jnp.minimum(i * tm // seq, n_batch)

    in_specs = [
        pl.BlockSpec((tm, d), lambda i, f: (i, 0)),
        pl.BlockSpec((None, 1, d), lambda i, f: (2, 0, 0)),
        pl.BlockSpec((None, None, 1, d), lambda i, f: (mrow(i), 3, 0, 0)),
        pl.BlockSpec((None, None, 1, d), lambda i, f: (mrow(i), 4, 0, 0)),
        pl.BlockSpec((d, tf), lambda i, f: (0, f)),
        pl.BlockSpec((d, tf), lambda i, f: (0, n_f + f)),
        pl.BlockSpec((tf, d), lambda i, f: (f, 0)),
        pl.BlockSpec((None, 1, d), lambda i, f: (3, 0, 0)),
        pl.BlockSpec((None, None, 1, d), lambda i, f: (mrow(i), 5, 0, 0)),
    ]
    return pl.pallas_call(
        functools.partial(_ffn_kernel, n_f=n_f),
        grid=(rows // tm, n_f),
        in_specs=in_specs,
        out_specs=pl.BlockSpec((tm, d), lambda i, f: (i, 0)),
        out_shape=jax.ShapeDtypeStruct((rows, d), F32),
        scratch_shapes=[pltpu.VMEM((tm, d), BF16), pltpu.VMEM((tm, d), F32)],
        compiler_params=_cparams("parallel", "arbitrary"),
        name="ffn",
    )(xt, normg, mod, mod, w_in, w_in, w_out, normg, mod)


MOE_CHUNK = 128


def _moe_kernel(nchunk_ref, x_ref, g2_ref, sh_ref, sc_ref, wg_ref, wu_ref, wo_ref, g3_ref, gate_ref,
                comb_ref, sel_ref, o_ref, h_scr, key_scr, keyt_scr, hs_scr, y_scr, acc_scr, *, n_exp, n_f):
    b = pl.program_id(0)
    e = pl.program_id(1)
    f = pl.program_id(2)
    nb = x_ref.shape[0]
    ch = MOE_CHUNK
    n_chunks = nchunk_ref[b * n_exp + e]

    @pl.when((e == 0) & (f == 0))
    def _():
        h_scr[...] = _modulate(x_ref[...], g2_ref[...], sh_ref[...], sc_ref[...]).astype(BF16)
        o_ref[...] = jnp.zeros_like(o_ref)
        ti = lax.broadcasted_iota(jnp.int32, (nb, nb), 0)
        tj = lax.broadcasted_iota(jnp.int32, (nb, nb), 1)
        lower = jnp.where(tj < ti, 1.0, 0.0).astype(BF16)
        sel = sel_ref[...]
        rank = _dot(lower, sel.astype(BF16))
        key = jnp.where(sel > 0.0, rank, -1.0)
        key_scr[...] = key
        keyt_scr[...] = key.T

    def for_rows(body):
        def pair(i, carry):
            body(pl.multiple_of(i * (2 * ch), 2 * ch), 2 * ch)
            return carry

        lax.fori_loop(0, n_chunks // 2, pair, 0)

        @pl.when(n_chunks % 2 == 1)
        def _():
            body(pl.multiple_of((n_chunks - 1) * ch, ch), ch)

    @pl.when(f == 0)
    def _():
        key_row = keyt_scr[pl.ds(e, 1), :]

        def gather(r0, n):
            r = (r0 + lax.broadcasted_iota(jnp.int32, (n, 1), 0)).astype(F32)
            onehot = jnp.where(key_row == r, 1.0, 0.0).astype(BF16)
            hs_scr[pl.ds(r0, n), :] = _dot(onehot, h_scr[...]).astype(BF16)

        for_rows(gather)

    def expert(r0, n):
        rows = pl.ds(r0, n)
        hs = hs_scr[rows, :]
        gp = _dot(hs, wg_ref[...])
        up = _dot(hs, wu_ref[...])
        a = gp * (1.0 / (1.0 + jnp.exp(-gp))) * up
        part = _dot(a.astype(BF16), wo_ref[...])

        @pl.when(f == 0)
        def _():
            y_scr[rows, :] = part

        @pl.when(f > 0)
        def _():
            y_scr[rows, :] += part

    for_rows(expert)

    @pl.when(f == n_f - 1)
    def _():
        lane = lax.broadcasted_iota(jnp.int32, (nb, LANES), 1)
        pick = lane == e
        key_col = jnp.sum(jnp.where(pick, key_scr[...], 0.0), axis=-1, keepdims=True)
        gate_col = jnp.sum(jnp.where(pick, comb_ref[...], 0.0), axis=-1, keepdims=True)
        acc_scr[...] = jnp.zeros_like(acc_scr)

        def scatter(r0, n):
            r = (r0 + lax.broadcasted_iota(jnp.int32, (1, n), 1)).astype(F32)
            onehot = jnp.where(key_col == r, 1.0, 0.0).astype(BF16)
            acc_scr[...] += _dot(onehot, y_scr[pl.ds(r0, n), :].astype(BF16))

        for_rows(scatter)
        o_ref[...] += gate_col * acc_scr[...]

    @pl.when((e == n_exp - 1) & (f == n_f - 1))
    def _():
        o_ref[...] = x_ref[...] + gate_ref[...] * (_rms(o_ref[...]) * g3_ref[...])


def _moe(xt, normg, mod, w_in, w_out, comb, sel, *, nb, tf, rows, seq, n_batch):
    d = xt.shape[1]
    n_exp, fdim = w_out.shape[0], w_out.shape[1]
    n_f = fdim // tf
    n_blk = rows // nb
    counts = jnp.sum(sel.reshape(n_blk, nb, LANES)[:, :, :n_exp], axis=1).astype(jnp.int32)
    nchunk = ((counts + (MOE_CHUNK - 1)) // MOE_CHUNK).reshape(n_blk * n_exp)

    def mrow(i):
        return jnp.minimum(i * nb // seq, n_batch)

    in_specs = [
        pl.BlockSpec((nb, d), lambda i, e, f, s: (i, 0)),
        pl.BlockSpec((None, 1, d), lambda i, e, f, s: (2, 0, 0)),
        pl.BlockSpec((None, None, 1, d), lambda i, e, f, s: (mrow(i), 3, 0, 0)),
        pl.BlockSpec((None, None, 1, d), lambda i, e, f, s: (mrow(i), 4, 0, 0)),
        pl.BlockSpec((None, d, tf), lambda i, e, f, s: (e, 0, f)),
        pl.BlockSpec((None, d, tf), lambda i, e, f, s: (e, 0, n_f + f)),
        pl.BlockSpec((None, tf, d), lambda i, e, f, s: (e, f, 0)),
        pl.BlockSpec((None, 1, d), lambda i, e, f, s: (3, 0, 0)),
        pl.BlockSpec((None, None, 1, d), lambda i, e, f, s: (mrow(i), 5, 0, 0)),
        pl.BlockSpec((nb, LANES), lambda i, e, f, s: (i, 0)),
        pl.BlockSpec((nb, LANES), lambda i, e, f, s: (i, 0)),
    ]
    return pl.pallas_call(
        functools.partial(_moe_kernel, n_exp=n_exp, n_f=n_f),
        grid_spec=pltpu.PrefetchScalarGridSpec(
            num_scalar_prefetch=1,
            grid=(n_blk, n_exp, n_f),
            in_specs=in_specs,
            out_specs=pl.BlockSpec((nb, d), lambda i, e, f, s: (i, 0)),
            scratch_shapes=[
                pltpu.VMEM((nb, d), BF16),
                pltpu.VMEM((nb, LANES), F32),
                pltpu.VMEM((LANES, nb), F32),
                pltpu.VMEM((nb, d), BF16),
                pltpu.VMEM((nb, d), F32),
                pltpu.VMEM((nb, d), F32),
            ],
        ),
        out_shape=jax.ShapeDtypeStruct((rows, d), F32),
        compiler_params=_cparams("parallel", "arbitrary", "arbitrary"),
        name="moe",
    )(nchunk, xt, normg, mod, mod, w_in, w_in, w_out, normg, mod, comb, sel)


def _lambda_init(layer):
    return 0.8 - 0.6 * math.exp(-0.3 * layer)


def kernel(x, c, ctx, c_ctx, ada_w, ada_b, norm_g, mix_in_w, mix_out_w, win_sink, diff_qkv_w, diff_out_w,
           diff_lambda, diff_subln_g, ffn_in_w, ffn_out_w, router_w, expert_in_w, expert_out_w):
    n_batch, seq, d = x.shape
    n_ctx = ctx.shape[1]
    depth = ada_w.shape[0]
    n_lat = n_batch * seq
    n_all = n_lat + n_batch * n_ctx
    fdim = mix_in_w.shape[2] - (d // 2 + 2 * LANES)
    q_cols = d // 2
    n_diff_heads = d // LANES

    tm_proj = _pick_tile(512, seq, n_batch * n_ctx)
    tm_out = _pick_tile(512, seq, n_batch * n_ctx)
    tm_ffn = _pick_tile(512, seq, n_batch * n_ctx)
    nb_moe = _pick_tile(1024, seq, n_batch * n_ctx)
    tq_diff = _pick_tile(512, seq)
    tq_win = _pick_tile(512, seq)
    common = dict(seq=seq, n_batch=n_batch)

    xt = jnp.concatenate([x.reshape(n_lat, d), ctx.reshape(n_batch * n_ctx, d)], axis=0)
    n_mod = -(-(n_batch + 1) // 8) * 8
    cv = jnp.zeros((n_mod, d), F32).at[:n_batch].set(c).at[n_batch].set(c_ctx)
    mods = _modvec(cv, ada_w, ada_b).reshape(depth, n_mod, 6, 1, d)
    rope = _rope_tables(seq, tm_proj)
    f_tables = _fourier_tables(seq)
    fc_tables = _dense_fourier_tables(n_ctx)

    n_f = fdim // LANES
    plan_even = ([(0, i * LANES, "plain") for i in range(n_f)]
                 + [(1, i * LANES, "rope_q_log2") for i in range(q_cols // LANES)]
                 + [(1, q_cols, "rope_k"), (1, q_cols + LANES, "plain")])
    plan_odd = ([(0, i * LANES, "rope_q_log2") for i in range(n_diff_heads)]
                + [(0, d + i * LANES, "rope_k") for i in range(n_diff_heads)]
                + [(0, 2 * d + i * LANES, "plain") for i in range(n_diff_heads)])

    for layer in range(depth):
        j = layer // 2
        need_ctx = layer < depth - 1
        rows = n_all if need_ctx else n_lat
        mod = mods[layer]
        ng = norm_g[layer].reshape(4, 1, d)
        if layer % 2 == 0:
            f, z = _proj(xt, ng, mod, 0, 1, mix_in_w[j].astype(BF16), rope, plan_even,
                         [(fdim, F32), (q_cols + 2 * LANES, BF16)], tm=tm_proj, n_lat=n_lat, **common)
            k_col, v_col = q_cols // LANES, q_cols // LANES + 1
            att = dict(n_batch=n_batch, seq=seq, n_ctx=n_ctx, q_cols=q_cols, k_col=k_col, v_col=v_col)
            mix_f = _fourier(f, f_tables, n_batch=n_batch, seq=seq, out_rows=rows)
            mix_a = _win_attention(z, win_sink[j], tq=tq_win, out_rows=rows, **att)
            if need_ctx:
                mix_f = _dense_fourier(f, fc_tables, mix_f, n_batch=n_batch, n_pos=n_ctx,
                                       row_block0=n_lat // n_ctx)
                mix_a = _ctx_gqa_attention(z, win_sink[j], mix_a, **att)
            xt = _outproj(mix_f, 0, mix_a, 0, mix_out_w[j].astype(BF16), xt, ng, mod, 2, tm=tm_out, rows=rows,
                          **common)
            xt = _ffn(xt, ng, mod, ffn_in_w[j].astype(BF16), ffn_out_w[j].astype(BF16),
                      tm=tm_ffn, tf=ffn_out_w.shape[1] // 2, rows=rows, **common)
        else:
            lam_init = _lambda_init(layer)
            (z,) = _proj(xt, ng, mod, 0, 1, diff_qkv_w[j].astype(BF16), rope, plan_odd, [(3 * d, BF16)],
                         tm=tm_proj, n_lat=n_lat, **common)
            subg = diff_subln_g[j].reshape(1, LANES)
            att = dict(n_batch=n_batch, seq=seq, n_ctx=n_ctx, n_heads=n_diff_heads)
            mix = _diff_attention(z, diff_lambda[j], subg, lam_init, tq=tq_diff, out_rows=rows, **att)
            if need_ctx:
                mix = _diff_attention_ctx(z, diff_lambda[j], subg, lam_init, mix, **att)
            xt = _outproj(mix, 0, mix, 1, diff_out_w[j].astype(BF16), xt, ng, mod, 2, tm=tm_out, rows=rows,
                          **common)
            comb, sel = _router(xt, ng, mod, router_w[j], tm=tm_out, rows=rows, **common)
            xt = _moe(xt, ng, mod, expert_in_w[j].astype(BF16), expert_out_w[j].astype(BF16), comb, sel,
                      nb=nb_moe, tf=896, rows=rows, **common)
    return xt[:n_lat].reshape(n_batch, seq, d)
```

```python
import functools
import math

import numpy as np
import jax
import jax.numpy as jnp
from jax import lax
from jax.experimental import pallas as pl
from jax.experimental.pallas import tpu as pltpu

EPS = 1e-6
NEG = -1e30
HEAD_DIM = 64
LANES = 128
GRID_W = 64
BLOCK = 128
WINDOW = 128
ROPE_THETA = 10000.0
N_EXPERTS = 8
F32 = jnp.float32
BF16 = jnp.bfloat16
HIGHEST = lax.Precision.HIGHEST
VMEM_LIMIT = 56 * 1024 * 1024


def _cparams(*sem):
    return pltpu.CompilerParams(dimension_semantics=sem, vmem_limit_bytes=VMEM_LIMIT)


def _dot(a, b):
    return jnp.dot(a, b, preferred_element_type=F32)


def _dot_nt(a, b):
    return lax.dot_general(a, b, (((1,), (1,)), ((), ())), preferred_element_type=F32)


def _dot_hi(a, b):
    return jnp.dot(a, b, precision=HIGHEST, preferred_element_type=F32)


def _rms(v):
    return v * lax.rsqrt(jnp.mean(v * v, axis=-1, keepdims=True) + EPS)


def _modulate(x, g, sh, sc):
    return _rms(x) * g * (1.0 + sc) + sh


def _pick_tile(pref, *dims):
    t = pref
    while any(d % t for d in dims):
        t //= 2
    return t


def _modvec_kernel(c_ref, w_ref, b_ref, o_ref):
    cv = c_ref[...]
    s = cv * (1.0 / (1.0 + jnp.exp(-cv)))
    o_ref[...] = _dot(s.astype(BF16), w_ref[...].astype(BF16)) + b_ref[...]


def _modvec(cv, ada_w, ada_b):
    depth, d, n = ada_w.shape
    r = cv.shape[0]
    tn = _pick_tile(1536, n)
    return pl.pallas_call(
        _modvec_kernel,
        grid=(depth, n // tn),
        in_specs=[
            pl.BlockSpec((r, d), lambda l, j: (0, 0)),
            pl.BlockSpec((None, d, tn), lambda l, j: (l, 0, j)),
            pl.BlockSpec((None, 1, tn), lambda l, j: (l, 0, j)),
        ],
        out_specs=pl.BlockSpec((None, r, tn), lambda l, j: (l, 0, j)),
        out_shape=jax.ShapeDtypeStruct((depth, r, n), F32),
        compiler_params=_cparams("parallel", "parallel"),
        name="modvec",
    )(cv, ada_w, ada_b.reshape(depth, 1, n))


def _proj_kernel(x_ref, g_ref, sh_ref, sc_ref, w_ref, cos_ref, sa_ref, sb_ref, *o_refs, plan, group):
    h = _modulate(x_ref[...], g_ref[...], sh_ref[...], sc_ref[...]).astype(BF16)
    n = w_ref.shape[1]
    for g0 in range(0, n, group):
        acc = _dot(h, w_ref[:, g0:g0 + group])
        for c0 in range(0, group, LANES):
            oi, oc, mode = plan[(g0 + c0) // LANES]
            v = acc[:, c0:c0 + LANES]
            if mode != "plain":
                v = (v * cos_ref[...] + pltpu.roll(v, LANES - 16, 1) * sa_ref[...]
                     + pltpu.roll(v, 16, 1) * sb_ref[...])
                if mode == "rope_q_log2":
                    v = v * (HEAD_DIM ** -0.5 * LOG2E)
            o_refs[oi][:, oc:oc + LANES] = v.astype(o_refs[oi].dtype)


def _proj(xt, normg, mod, k_sh, k_sc, w, rope, plan, outs, *, tm, n_lat, seq, n_batch):
    t, d = xt.shape
    n = w.shape[1]
    group = _pick_tile(512, n)
    nx = n_lat // tm
    per = seq // tm

    def mrow(i):
        return jnp.minimum(i * tm // seq, n_batch)

    def rrow(i):
        return jnp.where(i < nx, i % per, per)

    in_specs = [
        pl.BlockSpec((tm, d), lambda i: (i, 0)),
        pl.BlockSpec((None, 1, d), lambda i: (0, 0, 0)),
        pl.BlockSpec((None, None, 1, d), lambda i: (mrow(i), k_sh, 0, 0)),
        pl.BlockSpec((None, None, 1, d), lambda i: (mrow(i), k_sc, 0, 0)),
        pl.BlockSpec((d, n), lambda i: (0, 0)),
        pl.BlockSpec((tm, LANES), lambda i: (rrow(i), 0)),
        pl.BlockSpec((tm, LANES), lambda i: (rrow(i), 0)),
        pl.BlockSpec((tm, LANES), lambda i: (rrow(i), 0)),
    ]
    out_specs = [pl.BlockSpec((tm, wd), lambda i: (i, 0)) for wd, _ in outs]
    out_shape = [jax.ShapeDtypeStruct((t, wd), dt) for wd, dt in outs]
    return pl.pallas_call(
        functools.partial(_proj_kernel, plan=plan, group=group),
        grid=(t // tm,),
        in_specs=in_specs,
        out_specs=out_specs,
        out_shape=out_shape,
        compiler_params=_cparams("parallel"),
        name="proj",
    )(xt, normg, mod, mod, w, *rope)


def _rope_tables(seq, tm):
    rows_count = seq // GRID_W
    rows = jnp.repeat(jnp.arange(rows_count), GRID_W).astype(F32)
    cols = jnp.tile(jnp.arange(GRID_W), rows_count).astype(F32)
    axis_dim = HEAD_DIM // 2
    inv = ROPE_THETA ** (-jnp.arange(0, axis_dim, 2, dtype=F32) / axis_dim)
    ar = rows[:, None] * inv
    ac = cols[:, None] * inv
    cr, sr, cc, sc = jnp.cos(ar), jnp.sin(ar), jnp.cos(ac), jnp.sin(ac)
    z = jnp.zeros_like(sr)
    reps = LANES // HEAD_DIM
    cos = jnp.tile(jnp.concatenate([cr, cr, cc, cc], axis=1), (1, reps))
    sa = jnp.tile(jnp.concatenate([-sr, z, -sc, z], axis=1), (1, reps))
    sb = jnp.tile(jnp.concatenate([z, sr, z, sc], axis=1), (1, reps))
    ident = jnp.ones((tm, LANES), F32)
    zero = jnp.zeros((tm, LANES), F32)
    return (jnp.concatenate([cos, ident]), jnp.concatenate([sa, zero]), jnp.concatenate([sb, zero]))


FOURIER_LANES = 2 * LANES


def _split_bf16(t):
    hi = t.astype(BF16)
    return hi, (t - hi.astype(F32)).astype(BF16)


def _dot_split(a, b):
    return _dot(a[0], b[0]) + _dot(a[0], b[1]) + _dot(a[1], b[0])


def _fourier_tables(seq):
    n2 = GRID_W
    n1 = seq // n2
    norm = 1.0 / math.sqrt(seq * LANES)
    a = np.arange(n1)
    k1 = np.arange(n1)
    b = np.arange(n2)
    ang = (b[:, None, None] * k1[None, :, None] + (seq // n1) * k1[None, :, None] * a[None, None, :]) % seq
    th = 2.0 * np.pi * ang / seq
    m1 = np.concatenate([np.cos(th), -np.sin(th)], axis=1).astype(np.float32)
    ph = 2.0 * np.pi * ((b[:, None] * b[None, :]) % n2) / n2
    c2, s2 = np.cos(ph), np.sin(ph)
    g2 = np.block([[c2, s2], [-s2, c2]]).astype(np.float32)
    ch = np.arange(LANES)
    pc = 2.0 * np.pi * ((ch[:, None] * ch[None, :]) % LANES) / LANES
    cc = (np.cos(pc) * norm).astype(np.float32)
    sc = (np.sin(pc) * norm).astype(np.float32)
    cs = np.concatenate([cc, sc], axis=0)
    out = []
    for t in (m1, g2, cs):
        out.extend(_split_bf16(jnp.asarray(t)))
    return tuple(out)


FOURIER_ROW_CHUNK = 512


def _fourier_kernel(*refs, n1, n2, n_grp):
    u_refs = refs[:n_grp]
    m1h_ref, m1l_ref, g2h_ref, g2l_ref, csh_ref, csl_ref, o_ref, b_scr, xr_scr, xi_scr = refs[n_grp:]
    seq = n1 * n2
    for b in range(n2):
        xs = jnp.concatenate([u[pl.ds(b, n1, stride=n2), :] for u in u_refs], axis=1)
        z = _dot_split((m1h_ref[b], m1l_ref[b]), _split_bf16(xs))
        for g in range(n_grp):
            b_scr[g, 2 * n1 * b:2 * n1 * (b + 1), :] = z[:, g * LANES:(g + 1) * LANES]
    g2 = (g2h_ref[...], g2l_ref[...])
    for k1 in range(n1):
        bk = jnp.concatenate(
            [jnp.concatenate([b_scr.at[g][pl.ds(k1, n2, stride=2 * n1), :],
                              b_scr.at[g][pl.ds(n1 + k1, n2, stride=2 * n1), :]], axis=0)
             for g in range(n_grp)], axis=1)
        xk = _dot_split(g2, _split_bf16(bk))
        for g in range(n_grp):
            xr_scr[g, n2 * k1:n2 * (k1 + 1), :] = xk[:n2, g * LANES:(g + 1) * LANES]
            xi_scr[g, n2 * k1:n2 * (k1 + 1), :] = xk[n2:, g * LANES:(g + 1) * LANES]
    cs = (csh_ref[...], csl_ref[...])
    rc = min(FOURIER_ROW_CHUNK, seq)
    for g in range(n_grp):
        for r0 in range(0, seq, rc):
            x = jnp.concatenate([xr_scr[g, r0:r0 + rc, :], xi_scr[g, r0:r0 + rc, :]], axis=1)
            xr_scr[g, r0:r0 + rc, :] = _dot_split(_split_bf16(x), cs)
    for g in range(n_grp):
        for k2 in range(n2):
            o_ref[n1 * k2:n1 * (k2 + 1), g * LANES:(g + 1) * LANES] = (
                xr_scr.at[g][pl.ds(k2, n1, stride=n2), :].astype(o_ref.dtype))


def _fourier(f, tables, *, n_batch, seq, out_rows):
    n2 = GRID_W
    n1 = seq // n2
    n_grp = FOURIER_LANES // LANES
    table_specs = [pl.BlockSpec(t.shape, (lambda b, g, nd=t.ndim: (0,) * nd)) for t in tables]
    u_specs = [pl.BlockSpec((seq, LANES), (lambda b, g, k=k: (b, g * n_grp + k))) for k in range(n_grp)]
    return pl.pallas_call(
        functools.partial(_fourier_kernel, n1=n1, n2=n2, n_grp=n_grp),
        grid=(n_batch, f.shape[1] // FOURIER_LANES),
        in_specs=u_specs + table_specs,
        out_specs=pl.BlockSpec((seq, FOURIER_LANES), lambda b, g: (b, g)),
        out_shape=jax.ShapeDtypeStruct((out_rows, f.shape[1]), BF16),
        scratch_shapes=[pltpu.VMEM((n_grp, 2 * seq, LANES), F32), pltpu.VMEM((n_grp, seq, LANES), F32),
                        pltpu.VMEM((n_grp, seq, LANES), F32)],
        compiler_params=_cparams("parallel", "parallel"),
        name="fourier",
    )(*([f] * n_grp), *tables)


def _dense_fourier_tables(n):
    norm = 1.0 / math.sqrt(n * LANES)
    p = np.arange(n)
    ph = 2.0 * np.pi * ((p[:, None] * p[None, :]) % n) / n
    ch = np.arange(LANES)
    pc = 2.0 * np.pi * ((ch[:, None] * ch[None, :]) % LANES) / LANES
    return (jnp.asarray(np.cos(ph).astype(np.float32)), jnp.asarray(np.sin(ph).astype(np.float32)),
            jnp.asarray((np.cos(pc) * norm).astype(np.float32)), jnp.asarray((np.sin(pc) * norm).astype(np.float32)))


def _dense_fourier_kernel(u_ref, cl_ref, sl_ref, cc_ref, sc_ref, _, o_ref):
    u = u_ref[...]
    y = _dot_hi(cl_ref[...], _dot_hi(u, cc_ref[...])) - _dot_hi(sl_ref[...], _dot_hi(u, sc_ref[...]))
    o_ref[...] = y.astype(o_ref.dtype)


def _dense_fourier(f, tables, dst, *, n_batch, n_pos, row_block0):
    cl, sl, cc, sc = tables
    groups = f.shape[1] // LANES
    return pl.pallas_call(
        _dense_fourier_kernel,
        grid=(n_batch, groups),
        in_specs=[
            pl.BlockSpec((n_pos, LANES), lambda b, g: (row_block0 + b, g)),
            pl.BlockSpec(cl.shape, lambda b, g: (0, 0)),
            pl.BlockSpec(sl.shape, lambda b, g: (0, 0)),
            pl.BlockSpec(cc.shape, lambda b, g: (0, 0)),
            pl.BlockSpec(sc.shape, lambda b, g: (0, 0)),
            pl.BlockSpec(memory_space=pl.ANY),
        ],
        out_specs=pl.BlockSpec((n_pos, LANES), lambda b, g: (row_block0 + b, g)),
        out_shape=jax.ShapeDtypeStruct(dst.shape, dst.dtype),
        input_output_aliases={5: 0},
        compiler_params=_cparams("parallel", "parallel"),
        name="fourier_ctx",
    )(f, cl, sl, cc, sc, dst)


def _win_kernel(sink_ref, q_ref, *refs, n_qtiles, has_local, n_heads, group_size):
    if has_local:
        kp_ref, kc_ref, kn_ref, vp_ref, vc_ref, vn_ref, kx_ref, vx_ref, o_ref = refs
    else:
        kx_ref, vx_ref, _, o_ref = refs
    tq = q_ref.shape[0]
    lane = lax.broadcasted_iota(jnp.int32, (1, LANES), 1)
    half_mask = [lane < HEAD_DIM, lane >= HEAD_DIM]
    pieces = [(kx_ref[...], vx_ref[...], None)]
    if has_local:
        n = pl.program_id(1)
        qi = lax.broadcasted_iota(jnp.int32, (tq, BLOCK), 0)
        kj = lax.broadcasted_iota(jnp.int32, (tq, BLOCK), 1)
        valid_prev = (kj >= qi) & (n >= 1)
        valid_next = (kj <= qi - (tq - WINDOW)) & (n <= n_qtiles - 2)
        di = lax.broadcasted_iota(jnp.int32, (tq, tq), 0) - lax.broadcasted_iota(jnp.int32, (tq, tq), 1)
        valid_mid = (di <= WINDOW) & (di >= -WINDOW)
        pieces += [(kp_ref[...], vp_ref[...], valid_prev), (kc_ref[...], vc_ref[...], valid_mid),
                   (kn_ref[...], vn_ref[...], valid_next)]
    for pair in range(n_heads // 2):
        qp = q_ref[:, pair * LANES:(pair + 1) * LANES].astype(F32)
        qp_sw = pltpu.roll(qp, HEAD_DIM, 1)
        out_pair = jnp.zeros((tq, LANES), F32)
        for half in range(2):
            head = 2 * pair + half
            kv = head // group_size
            src = qp if half == kv else qp_sw
            qe = jnp.where(half_mask[kv], src, 0.0).astype(BF16)
            sink = sink_ref[head] * LOG2E
            scores = []
            m = jnp.zeros((tq, 1), F32) + sink
            for k, _, valid in pieces:
                s = _dot_nt(qe, k)
                if valid is not None:
                    s = jnp.where(valid, s, NEG)
                scores.append(s)
                m = jnp.maximum(m, jnp.max(s, axis=-1, keepdims=True))
            den = jnp.exp2(sink - m)
            pv = jnp.zeros((tq, LANES), F32)
            for s, (_, v, _) in zip(scores, pieces):
                p = jnp.exp2(s - m)
                den = den + jnp.sum(p, axis=-1, keepdims=True)
                pv = pv + _dot(p.astype(BF16), v)
            pv = pv / den
            if half != kv:
                pv = pltpu.roll(pv, HEAD_DIM, 1)
            out_pair = jnp.where(half_mask[half], pv, out_pair)
        o_ref[:, pair * LANES:(pair + 1) * LANES] = out_pair.astype(o_ref.dtype)


def _win_attention(z, sink, *, n_batch, seq, n_ctx, q_cols, k_col, v_col, tq, out_rows):
    nbk = seq // BLOCK
    nq = seq // tq
    per = tq // BLOCK
    ctx_blk0 = n_batch * seq // n_ctx
    n_heads = q_cols // HEAD_DIM
    group_size = n_heads // (LANES // HEAD_DIM)

    def edge(col, first):
        return pl.BlockSpec(
            (BLOCK, LANES), lambda b, n: (b * nbk + jnp.clip(n * per + first, 0, nbk - 1), col))

    def mid(col):
        return pl.BlockSpec((tq, LANES), lambda b, n: (b * nq + n, col))

    return pl.pallas_call(
        functools.partial(_win_kernel, n_qtiles=nq, has_local=True, n_heads=n_heads, group_size=group_size),
        grid=(n_batch, nq),
        in_specs=[
            pl.BlockSpec(memory_space=pltpu.SMEM),
            pl.BlockSpec((tq, q_cols), lambda b, n: (b * nq + n, 0)),
            edge(k_col, -1), mid(k_col), edge(k_col, per),
            edge(v_col, -1), mid(v_col), edge(v_col, per),
            pl.BlockSpec((n_ctx, LANES), lambda b, n: (ctx_blk0 + b, k_col)),
            pl.BlockSpec((n_ctx, LANES), lambda b, n: (ctx_blk0 + b, v_col)),
        ],
        out_specs=pl.BlockSpec((tq, q_cols), lambda b, n: (b * nq + n, 0)),
        out_shape=jax.ShapeDtypeStruct((out_rows, q_cols), BF16),
        compiler_params=_cparams("parallel", "parallel"),
        name="win_attn",
    )(sink, z, z, z, z, z, z, z, z, z)


def _ctx_gqa_attention(z, sink, dst, *, n_batch, seq, n_ctx, q_cols, k_col, v_col):
    ctx_blk0 = n_batch * seq // n_ctx
    n_heads = q_cols // HEAD_DIM
    group_size = n_heads // (LANES // HEAD_DIM)
    return pl.pallas_call(
        functools.partial(_win_kernel, n_qtiles=0, has_local=False, n_heads=n_heads, group_size=group_size),
        grid=(n_batch,),
        in_specs=[
            pl.BlockSpec(memory_space=pltpu.SMEM),
            pl.BlockSpec((n_ctx, q_cols), lambda b: (ctx_blk0 + b, 0)),
            pl.BlockSpec((n_ctx, LANES), lambda b: (ctx_blk0 + b, k_col)),
            pl.BlockSpec((n_ctx, LANES), lambda b: (ctx_blk0 + b, v_col)),
            pl.BlockSpec(memory_space=pl.ANY),
        ],
        out_specs=pl.BlockSpec((n_ctx, q_cols), lambda b: (ctx_blk0 + b, 0)),
        out_shape=jax.ShapeDtypeStruct(dst.shape, dst.dtype),
        input_output_aliases={4: 0},
        compiler_params=_cparams("parallel"),
        name="ctx_gqa",
    )(sink, z, z, z, dst)


LOG2E = math.log2(math.e)
DIFF_SUB_ROWS = 128


def _diff_kernel(lam_ref, g_ref, q_ref, *refs, lam_init, has_x):
    if has_x:
        kx_ref, vx_ref, kc_ref, vc_ref, o_ref, s_scr = refs
        nx = kx_ref.shape[0]
    else:
        kc_ref, vc_ref, _, o_ref, s_scr = refs
        nx = 0
    nc = kc_ref.shape[0]
    sub = s_scr.shape[1]
    n_sub = q_ref.shape[0] // sub
    lv = lam_ref[...]
    lam = (jnp.exp(jnp.sum(lv[0:1] * lv[1:2], axis=-1, keepdims=True))
           - jnp.exp(jnp.sum(lv[2:3] * lv[3:4], axis=-1, keepdims=True)) + lam_init)
    lane = lax.broadcasted_iota(jnp.int32, (1, LANES), 1)
    units = [(h, c) for h in range(n_sub) for c in range(2)]

    def scores(u):
        h, c = units[u]
        q = q_ref[h * sub:(h + 1) * sub, :]
        qm = jnp.where((lane < HEAD_DIM) if c == 0 else (lane >= HEAD_DIM), q, jnp.zeros_like(q))
        if has_x:
            s_scr[u % 2, :, 0:nx] = _dot_nt(qm, kx_ref[...])
        s_scr[u % 2, :, nx:nx + nc] = _dot_nt(qm, kc_ref[...])

    def attend(u):
        s = s_scr[u % 2]
        p = jnp.exp2(s - jnp.max(s, axis=-1, keepdims=True))
        den = jnp.sum(p, axis=-1, keepdims=True)
        pb = p.astype(BF16)
        acc = _dot(pb[:, nx:nx + nc], vc_ref[...])
        if has_x:
            acc = acc + _dot(pb[:, 0:nx], vx_ref[...])
        return acc / den

    res = []
    scores(0)
    for u in range(len(units)):
        if u + 1 < len(units):
            scores(u + 1)
        res.append(attend(u))
    for h in range(n_sub):
        o = res[2 * h] - lam * res[2 * h + 1]
        o = _rms(o) * g_ref[...] * (1.0 - lam_init)
        o_ref[h * sub:(h + 1) * sub, :] = o.astype(o_ref.dtype)


def _diff_attention(z, lam_vec, subln_g, lam_init, *, n_batch, seq, n_ctx, n_heads, tq, out_rows):
    ctx_blk0 = n_batch * seq // n_ctx
    nq = seq // tq
    return pl.pallas_call(
        functools.partial(_diff_kernel, lam_init=lam_init, has_x=True),
        grid=(n_batch, n_heads, nq),
        in_specs=[
            pl.BlockSpec(lam_vec.shape, lambda b, h, i: (0, 0)),
            pl.BlockSpec((1, LANES), lambda b, h, i: (0, 0)),
            pl.BlockSpec((tq, LANES), lambda b, h, i: (b * nq + i, h)),
            pl.BlockSpec((seq, LANES), lambda b, h, i: (b, n_heads + h)),
            pl.BlockSpec((seq, LANES), lambda b, h, i: (b, 2 * n_heads + h)),
            pl.BlockSpec((n_ctx, LANES), lambda b, h, i: (ctx_blk0 + b, n_heads + h)),
            pl.BlockSpec((n_ctx, LANES), lambda b, h, i: (ctx_blk0 + b, 2 * n_heads + h)),
        ],
        out_specs=pl.BlockSpec((tq, LANES), lambda b, h, i: (b * nq + i, h)),
        out_shape=jax.ShapeDtypeStruct((out_rows, n_heads * LANES), BF16),
        scratch_shapes=[pltpu.VMEM((2, min(tq, DIFF_SUB_ROWS), seq + n_ctx), F32)],
        compiler_params=_cparams("parallel", "parallel", "parallel"),
        name="diff_attn",
    )(lam_vec, subln_g, z, z, z, z, z)


def _diff_attention_ctx(z, lam_vec, subln_g, lam_init, dst, *, n_batch, seq, n_ctx, n_heads):
    ctx_blk0 = n_batch * seq // n_ctx
    return pl.pallas_call(
        functools.partial(_diff_kernel, lam_init=lam_init, has_x=False),
        grid=(n_batch, n_heads),
        in_specs=[
            pl.BlockSpec(lam_vec.shape, lambda b, h: (0, 0)),
            pl.BlockSpec((1, LANES), lambda b, h: (0, 0)),
            pl.BlockSpec((n_ctx, LANES), lambda b, h: (ctx_blk0 + b, h)),
            pl.BlockSpec((n_ctx, LANES), lambda b, h: (ctx_blk0 + b, n_heads + h)),
            pl.BlockSpec((n_ctx, LANES), lambda b, h: (ctx_blk0 + b, 2 * n_heads + h)),
            pl.BlockSpec(memory_space=pl.ANY),
        ],
        out_specs=pl.BlockSpec((n_ctx, LANES), lambda b, h: (ctx_blk0 + b, h)),
        out_shape=jax.ShapeDtypeStruct(dst.shape, dst.dtype),
        input_output_aliases={5: 0},
        scratch_shapes=[pltpu.VMEM((2, min(n_ctx, DIFF_SUB_ROWS), n_ctx), F32)],
        compiler_params=_cparams("parallel", "parallel"),
        name="diff_attn_ctx",
    )(lam_vec, subln_g, z, z, z, dst)


def _outproj_kernel(a0_ref, a1_ref, w_ref, x_ref, g_ref, gate_ref, o_ref):
    half = a0_ref.shape[1]
    y = _dot(a0_ref[...], w_ref[:half, :]) + _dot(a1_ref[...], w_ref[half:, :])
    o_ref[...] = x_ref[...] + gate_ref[...] * (_rms(y) * g_ref[...])


def _outproj(a0, a0_col, a1, a1_col, w, xt, normg, mod, k_gate, *, tm, rows, seq, n_batch):
    d = xt.shape[1]
    half = d // 2

    def mrow(i):
        return jnp.minimum(i * tm // seq, n_batch)

    return pl.pallas_call(
        _outproj_kernel,
        grid=(rows // tm,),
        in_specs=[
            pl.BlockSpec((tm, half), lambda i: (i, a0_col)),
            pl.BlockSpec((tm, half), lambda i: (i, a1_col)),
            pl.BlockSpec((d, d), lambda i: (0, 0)),
            pl.BlockSpec((tm, d), lambda i: (i, 0)),
            pl.BlockSpec((None, 1, d), lambda i: (1, 0, 0)),
            pl.BlockSpec((None, None, 1, d), lambda i: (mrow(i), k_gate, 0, 0)),
        ],
        out_specs=pl.BlockSpec((tm, d), lambda i: (i, 0)),
        out_shape=jax.ShapeDtypeStruct((rows, d), F32),
        compiler_params=_cparams("parallel"),
        name="outproj",
    )(a0, a1, w, xt, normg, mod)


def _router_kernel(x_ref, g_ref, sh_ref, sc_ref, rw_ref, o_ref, sel_ref):
    h = _modulate(x_ref[...], g_ref[...], sh_ref[...], sc_ref[...])
    logits = _dot_hi(h, rw_ref[...])
    lane = lax.broadcasted_iota(jnp.int32, logits.shape, 1)
    ninf = -jnp.inf
    logits = jnp.where(lane < N_EXPERTS, logits, ninf)
    m1 = jnp.max(logits, axis=-1, keepdims=True)
    i1 = jnp.min(jnp.where(logits == m1, lane, LANES), axis=-1, keepdims=True)
    sel1 = lane == i1
    rest = jnp.where(sel1, ninf, logits)
    m2 = jnp.max(rest, axis=-1, keepdims=True)
    i2 = jnp.min(jnp.where(rest == m2, lane, LANES), axis=-1, keepdims=True)
    sel2 = lane == i2
    e2 = jnp.exp(m2 - m1)
    den = 1.0 + e2
    o_ref[...] = jnp.where(sel1, 1.0 / den, 0.0) + jnp.where(sel2, e2 / den, 0.0)
    sel_ref[...] = jnp.where(sel1 | sel2, 1.0, 0.0)


def _router(xt, normg, mod, router_w, *, tm, rows, seq, n_batch):
    d = xt.shape[1]
    rw = jnp.zeros((d, LANES), F32).at[:, :N_EXPERTS].set(router_w)

    def mrow(i):
        return jnp.minimum(i * tm // seq, n_batch)

    return pl.pallas_call(
        _router_kernel,
        grid=(rows // tm,),
        in_specs=[
            pl.BlockSpec((tm, d), lambda i: (i, 0)),
            pl.BlockSpec((None, 1, d), lambda i: (2, 0, 0)),
            pl.BlockSpec((None, None, 1, d), lambda i: (mrow(i), 3, 0, 0)),
            pl.BlockSpec((None, None, 1, d), lambda i: (mrow(i), 4, 0, 0)),
            pl.BlockSpec((d, LANES), lambda i: (0, 0)),
        ],
        out_specs=[pl.BlockSpec((tm, LANES), lambda i: (i, 0))] * 2,
        out_shape=[jax.ShapeDtypeStruct((rows, LANES), F32)] * 2,
        compiler_params=_cparams("parallel"),
        name="router",
    )(xt, normg, mod, mod, rw)


SWIGLU_CHUNK = 256


def _swiglu(h, wg_ref, wu_ref, wo_ref, u_off):
    width = wo_ref.shape[0]
    acc = None
    for c0 in range(0, width, SWIGLU_CHUNK):
        gp = _dot(h, wg_ref[:, c0:c0 + SWIGLU_CHUNK])
        up = _dot(h, wu_ref[:, u_off + c0:u_off + c0 + SWIGLU_CHUNK])
        a = gp * (1.0 / (1.0 + jnp.exp(-gp))) * up
        part = _dot(a.astype(BF16), wo_ref[c0:c0 + SWIGLU_CHUNK, :])
        acc = part if acc is None else acc + part
    return acc


def _ffn_kernel(x_ref, g2_ref, sh_ref, sc_ref, win_ref, wout_ref, g3_ref, gate_ref, o_ref):
    x = x_ref[...]
    h = _modulate(x, g2_ref[...], sh_ref[...], sc_ref[...]).astype(BF16)
    y = _swiglu(h, win_ref, win_ref, wout_ref, wout_ref.shape[0])
    o_ref[...] = x + gate_ref[...] * (_rms(y) * g3_ref[...])


def _ffn(xt, normg, mod, w_in, w_out, *, tm, rows, seq, n_batch):
    d = xt.shape[1]

    def mrow(i):
        return jnp.minimum(i * tm // seq, n_batch)

    in_specs = [
        pl.BlockSpec((tm, d), lambda i: (i, 0)),
        pl.BlockSpec((None, 1, d), lambda i: (2, 0, 0)),
        pl.BlockSpec((None, None, 1, d), lambda i: (mrow(i), 3, 0, 0)),
        pl.BlockSpec((None, None, 1, d), lambda i: (mrow(i), 4, 0, 0)),
        pl.BlockSpec(w_in.shape, lambda i: (0, 0)),
        pl.BlockSpec(w_out.shape, lambda i: (0, 0)),
        pl.BlockSpec((None, 1, d), lambda i: (3, 0, 0)),
        pl.BlockSpec((None, None, 1, d), lambda i: (mrow(i), 5, 0, 0)),
    ]
    return pl.pallas_call(
        _ffn_kernel,
        grid=(rows // tm,),
        in_specs=in_specs,
        out_specs=pl.BlockSpec((tm, d), lambda i: (i, 0)),
        out_shape=jax.ShapeDtypeStruct((rows, d), F32),
        compiler_params=_cparams("parallel"),
        name="ffn",
    )(xt, normg, mod, mod, w_in, w_out, normg, mod)


MOE_CHUNK = 128


def _moe_kernel(nchunk_ref, x_ref, g2_ref, sh_ref, sc_ref, wg_ref, wu_ref, wo_ref, g3_ref, gate_ref,
                comb_ref, sel_ref, o_ref, h_scr, key_scr, keyt_scr, hs_scr, y_scr, acc_scr, *, n_exp, n_f):
    b = pl.program_id(0)
    e = pl.program_id(1)
    f = pl.program_id(2)
    nb = x_ref.shape[0]
    ch = MOE_CHUNK
    n_chunks = nchunk_ref[b * n_exp + e]

    @pl.when((e == 0) & (f == 0))
    def _():
        h_scr[...] = _modulate(x_ref[...], g2_ref[...], sh_ref[...], sc_ref[...]).astype(BF16)
        o_ref[...] = jnp.zeros_like(o_ref)
        ti = lax.broadcasted_iota(jnp.int32, (nb, nb), 0)
        tj = lax.broadcasted_iota(jnp.int32, (nb, nb), 1)
        lower = jnp.where(tj < ti, 1.0, 0.0).astype(BF16)
        sel = sel_ref[...]
        rank = _dot(lower, sel.astype(BF16))
        key = jnp.where(sel > 0.0, rank, -1.0)
        key_scr[...] = key
        keyt_scr[...] = key.T

    def for_rows(body):
        def pair(i, carry):
            body(pl.multiple_of(i * (2 * ch), 2 * ch), 2 * ch)
            return carry

        lax.fori_loop(0, n_chunks // 2, pair, 0)

        @pl.when(n_chunks % 2 == 1)
        def _():
            body(pl.multiple_of((n_chunks - 1) * ch, ch), ch)

    @pl.when(f == 0)
    def _():
        key_row = keyt_scr[pl.ds(e, 1), :]

        def gather(r0, n):
            r = (r0 + lax.broadcasted_iota(jnp.int32, (n, 1), 0)).astype(F32)
            onehot = jnp.where(key_row == r, 1.0, 0.0).astype(BF16)
            hs_scr[pl.ds(r0, n), :] = _dot(onehot, h_scr[...]).astype(BF16)

        for_rows(gather)

    def expert(r0, n):
        rows = pl.ds(r0, n)
        part = _swiglu(hs_scr[rows, :], wg_ref, wu_ref, wo_ref, 0)

        @pl.when(f == 0)
        def _():
            y_scr[rows, :] = part

        @pl.when(f > 0)
        def _():
            y_scr[rows, :] += part

    for_rows(expert)

    @pl.when(f == n_f - 1)
    def _():
        lane = lax.broadcasted_iota(jnp.int32, (nb, LANES), 1)
        pick = lane == e
        key_col = jnp.sum(jnp.where(pick, key_scr[...], 0.0), axis=-1, keepdims=True)
        gate_col = jnp.sum(jnp.where(pick, comb_ref[...], 0.0), axis=-1, keepdims=True)
        acc_scr[...] = jnp.zeros_like(acc_scr)

        def scatter(r0, n):
            r = (r0 + lax.broadcasted_iota(jnp.int32, (1, n), 1)).astype(F32)
            onehot = jnp.where(key_col == r, 1.0, 0.0).astype(BF16)
            acc_scr[...] += _dot(onehot, y_scr[pl.ds(r0, n), :].astype(BF16))

        for_rows(scatter)
        o_ref[...] += gate_col * acc_scr[...]

    @pl.when((e == n_exp - 1) & (f == n_f - 1))
    def _():
        o_ref[...] = x_ref[...] + gate_ref[...] * (_rms(o_ref[...]) * g3_ref[...])


def _moe(xt, normg, mod, w_in, w_out, comb, sel, *, nb, tf, rows, seq, n_batch):
    d = xt.shape[1]
    n_exp, fdim = w_out.shape[0], w_out.shape[1]
    n_f = fdim // tf
    n_blk = rows // nb
    counts = jnp.sum(sel.reshape(n_blk, nb, LANES)[:, :, :n_exp], axis=1).astype(jnp.int32)
    nchunk = ((counts + (MOE_CHUNK - 1)) // MOE_CHUNK).reshape(n_blk * n_exp)

    def mrow(i):
        return jnp.minimum(i * nb // seq, n_batch)

    in_specs = [
        pl.BlockSpec((nb, d), lambda i, e, f, s: (i, 0)),
        pl.BlockSpec((None, 1, d), lambda i, e, f, s: (2, 0, 0)),
        pl.BlockSpec((None, None, 1, d), lambda i, e, f, s: (mrow(i), 3, 0, 0)),
        pl.BlockSpec((None, None, 1, d), lambda i, e, f, s: (mrow(i), 4, 0, 0)),
        pl.BlockSpec((None, d, tf), lambda i, e, f, s: (e, 0, f)),
        pl.BlockSpec((None, d, tf), lambda i, e, f, s: (e, 0, n_f + f)),
        pl.BlockSpec((None, tf, d), lambda i, e, f, s: (e, f, 0)),
        pl.BlockSpec((None, 1, d), lambda i, e, f, s: (3, 0, 0)),
        pl.BlockSpec((None, None, 1, d), lambda i, e, f, s: (mrow(i), 5, 0, 0)),
        pl.BlockSpec((nb, LANES), lambda i, e, f, s: (i, 0)),
        pl.BlockSpec((nb, LANES), lambda i, e, f, s: (i, 0)),
    ]
    return pl.pallas_call(
        functools.partial(_moe_kernel, n_exp=n_exp, n_f=n_f),
        grid_spec=pltpu.PrefetchScalarGridSpec(
            num_scalar_prefetch=1,
            grid=(n_blk, n_exp, n_f),
            in_specs=in_specs,
            out_specs=pl.BlockSpec((nb, d), lambda i, e, f, s: (i, 0)),
            scratch_shapes=[
                pltpu.VMEM((nb, d), BF16),
                pltpu.VMEM((nb, LANES), F32),
                pltpu.VMEM((LANES, nb), F32),
                pltpu.VMEM((nb, d), BF16),
                pltpu.VMEM((nb, d), F32),
                pltpu.VMEM((nb, d), F32),
            ],
        ),
        out_shape=jax.ShapeDtypeStruct((rows, d), F32),
        compiler_params=_cparams("parallel", "arbitrary", "arbitrary"),
        name="moe",
    )(nchunk, xt, normg, mod, mod, w_in, w_in, w_out, normg, mod, comb, sel)


def _lambda_init(layer):
    return 0.8 - 0.6 * math.exp(-0.3 * layer)


def kernel(x, c, ctx, c_ctx, ada_w, ada_b, norm_g, mix_in_w, mix_out_w, win_sink, diff_qkv_w, diff_out_w,
           diff_lambda, diff_subln_g, ffn_in_w, ffn_out_w, router_w, expert_in_w, expert_out_w):
    n_batch, seq, d = x.shape
    n_ctx = ctx.shape[1]
    depth = ada_w.shape[0]
    n_lat = n_batch * seq
    n_all = n_lat + n_batch * n_ctx
    fdim = mix_in_w.shape[2] - (d // 2 + 2 * LANES)
    q_cols = d // 2
    n_diff_heads = d // LANES

    tm_proj = _pick_tile(512, seq, n_batch * n_ctx)
    tm_out = _pick_tile(512, seq, n_batch * n_ctx)
    tm_ffn = _pick_tile(512, seq, n_batch * n_ctx)
    nb_moe = _pick_tile(1024, seq, n_batch * n_ctx)
    tq_diff = _pick_tile(512, seq)
    tq_win = _pick_tile(512, seq)
    common = dict(seq=seq, n_batch=n_batch)

    xt = jnp.concatenate([x.reshape(n_lat, d), ctx.reshape(n_batch * n_ctx, d)], axis=0)
    n_mod = -(-(n_batch + 1) // 8) * 8
    cv = jnp.zeros((n_mod, d), F32).at[:n_batch].set(c).at[n_batch].set(c_ctx)
    mods = _modvec(cv, ada_w, ada_b).reshape(depth, n_mod, 6, 1, d)
    rope = _rope_tables(seq, tm_proj)
    f_tables = _fourier_tables(seq)
    fc_tables = _dense_fourier_tables(n_ctx)

    n_f = fdim // LANES
    plan_even = ([(0, i * LANES, "plain") for i in range(n_f)]
                 + [(1, i * LANES, "rope_q_log2") for i in range(q_cols // LANES)]
                 + [(1, q_cols, "rope_k"), (1, q_cols + LANES, "plain")])
    plan_odd = ([(0, i * LANES, "rope_q_log2") for i in range(n_diff_heads)]
                + [(0, d + i * LANES, "rope_k") for i in range(n_diff_heads)]
                + [(0, 2 * d + i * LANES, "plain") for i in range(n_diff_heads)])

    for layer in range(depth):
        j = layer // 2
        need_ctx = layer < depth - 1
        rows = n_all if need_ctx else n_lat
        mod = mods[layer]
        ng = norm_g[layer].reshape(4, 1, d)
        if layer % 2 == 0:
            f, z = _proj(xt, ng, mod, 0, 1, mix_in_w[j].astype(BF16), rope, plan_even,
                         [(fdim, F32), (q_cols + 2 * LANES, BF16)], tm=tm_proj, n_lat=n_lat, **common)
            k_col, v_col = q_cols // LANES, q_cols // LANES + 1
            att = dict(n_batch=n_batch, seq=seq, n_ctx=n_ctx, q_cols=q_cols, k_col=k_col, v_col=v_col)
            mix_f = _fourier(f, f_tables, n_batch=n_batch, seq=seq, out_rows=rows)
            mix_a = _win_attention(z, win_sink[j], tq=tq_win, out_rows=rows, **att)
            if need_ctx:
                mix_f = _dense_fourier(f, fc_tables, mix_f, n_batch=n_batch, n_pos=n_ctx,
                                       row_block0=n_lat // n_ctx)
                mix_a = _ctx_gqa_attention(z, win_sink[j], mix_a, **att)
            xt = _outproj(mix_f, 0, mix_a, 0, mix_out_w[j].astype(BF16), xt, ng, mod, 2, tm=tm_out, rows=rows,
                          **common)
            xt = _ffn(xt, ng, mod, ffn_in_w[j].astype(BF16), ffn_out_w[j].astype(BF16),
                      tm=tm_ffn, rows=rows, **common)
        else:
            lam_init = _lambda_init(layer)
            (z,) = _proj(xt, ng, mod, 0, 1, diff_qkv_w[j].astype(BF16), rope, plan_odd, [(3 * d, BF16)],
                         tm=tm_proj, n_lat=n_lat, **common)
            subg = diff_subln_g[j].reshape(1, LANES)
            att = dict(n_batch=n_batch, seq=seq, n_ctx=n_ctx, n_heads=n_diff_heads)
            mix = _diff_attention(z, diff_lambda[j], subg, lam_init, tq=tq_diff, out_rows=rows, **att)
            if need_ctx:
                mix = _diff_attention_ctx(z, diff_lambda[j], subg, lam_init, mix, **att)
            xt = _outproj(mix, 0, mix, 1, diff_out_w[j].astype(BF16), xt, ng, mod, 2, tm=tm_out, rows=rows,
                          **common)
            comb, sel = _router(xt, ng, mod, router_w[j], tm=tm_out, rows=rows, **common)
            xt = _moe(xt, ng, mod, expert_in_w[j].astype(BF16), expert_out_w[j].astype(BF16), comb, sel,
                      nb=nb_moe, tf=1792, rows=rows, **common)
    return xt[:n_lat].reshape(n_batch, seq, d)
```

```python
import functools
import math

import numpy as np
import jax
import jax.numpy as jnp
from jax import lax
from jax.experimental import pallas as pl
from jax.experimental.pallas import tpu as pltpu

EPS = 1e-6
NEG = -1e30
HEAD_DIM = 64
LANES = 128
GRID_W = 64
BLOCK = 128
WINDOW = 128
ROPE_THETA = 10000.0
N_EXPERTS = 8
F32 = jnp.float32
BF16 = jnp.bfloat16
HIGHEST = lax.Precision.HIGHEST
VMEM_LIMIT = 56 * 1024 * 1024


def _cparams(*sem):
    return pltpu.CompilerParams(dimension_semantics=sem, vmem_limit_bytes=VMEM_LIMIT)


def _dot(a, b):
    return jnp.dot(a, b, preferred_element_type=F32)


def _dot_nt(a, b):
    return lax.dot_general(a, b, (((1,), (1,)), ((), ())), preferred_element_type=F32)


def _dot_hi(a, b):
    return jnp.dot(a, b, precision=HIGHEST, preferred_element_type=F32)


def _rms(v):
    return v * lax.rsqrt(jnp.mean(v * v, axis=-1, keepdims=True) + EPS)


def _modulate(x, g, sh, sc):
    return _rms(x) * g * (1.0 + sc) + sh


def _pick_tile(pref, *dims):
    t = pref
    while any(d % t for d in dims):
        t //= 2
    return t


def _modvec_kernel(c_ref, w_ref, b_ref, o_ref):
    cv = c_ref[...]
    s = cv * (1.0 / (1.0 + jnp.exp(-cv)))
    o_ref[...] = _dot(s.astype(BF16), w_ref[...].astype(BF16)) + b_ref[...]


def _modvec(cv, ada_w, ada_b):
    depth, d, n = ada_w.shape
    r = cv.shape[0]
    tn = _pick_tile(1536, n)
    return pl.pallas_call(
        _modvec_kernel,
        grid=(depth, n // tn),
        in_specs=[
            pl.BlockSpec((r, d), lambda l, j: (0, 0)),
            pl.BlockSpec((None, d, tn), lambda l, j: (l, 0, j)),
            pl.BlockSpec((None, 1, tn), lambda l, j: (l, 0, j)),
        ],
        out_specs=pl.BlockSpec((None, r, tn), lambda l, j: (l, 0, j)),
        out_shape=jax.ShapeDtypeStruct((depth, r, n), F32),
        compiler_params=_cparams("parallel", "parallel"),
        name="modvec",
    )(cv, ada_w, ada_b.reshape(depth, 1, n))


def _proj_kernel(x_ref, g_ref, sh_ref, sc_ref, w_ref, cos_ref, sa_ref, sb_ref, *o_refs, plan, group):
    h = _modulate(x_ref[...], g_ref[...], sh_ref[...], sc_ref[...]).astype(BF16)
    n = w_ref.shape[1]
    for g0 in range(0, n, group):
        acc = _dot(h, w_ref[:, g0:g0 + group])
        for c0 in range(0, group, LANES):
            oi, oc, mode = plan[(g0 + c0) // LANES]
            v = acc[:, c0:c0 + LANES]
            if mode != "plain":
                v = (v * cos_ref[...] + pltpu.roll(v, LANES - 16, 1) * sa_ref[...]
                     + pltpu.roll(v, 16, 1) * sb_ref[...])
                if mode == "rope_q_log2":
                    v = v * (HEAD_DIM ** -0.5 * LOG2E)
            o_refs[oi][:, oc:oc + LANES] = v.astype(o_refs[oi].dtype)


def _proj(xt, normg, mod, k_sh, k_sc, w, rope, plan, outs, *, tm, n_lat, seq, n_batch):
    t, d = xt.shape
    n = w.shape[1]
    group = _pick_tile(512, n)
    nx = n_lat // tm
    per = seq // tm

    def mrow(i):
        return jnp.minimum(i * tm // seq, n_batch)

    def rrow(i):
        return jnp.where(i < nx, i % per, per)

    in_specs = [
        pl.BlockSpec((tm, d), lambda i: (i, 0)),
        pl.BlockSpec((None, 1, d), lambda i: (0, 0, 0)),
        pl.BlockSpec((None, None, 1, d), lambda i: (mrow(i), k_sh, 0, 0)),
        pl.BlockSpec((None, None, 1, d), lambda i: (mrow(i), k_sc, 0, 0)),
        pl.BlockSpec((d, n), lambda i: (0, 0)),
        pl.BlockSpec((tm, LANES), lambda i: (rrow(i), 0)),
        pl.BlockSpec((tm, LANES), lambda i: (rrow(i), 0)),
        pl.BlockSpec((tm, LANES), lambda i: (rrow(i), 0)),
    ]
    out_specs = [pl.BlockSpec((tm, wd), lambda i: (i, 0)) for wd, _ in outs]
    out_shape = [jax.ShapeDtypeStruct((t, wd), dt) for wd, dt in outs]
    return pl.pallas_call(
        functools.partial(_proj_kernel, plan=plan, group=group),
        grid=(t // tm,),
        in_specs=in_specs,
        out_specs=out_specs,
        out_shape=out_shape,
        compiler_params=_cparams("parallel"),
        name="proj",
    )(xt, normg, mod, mod, w, *rope)


def _rope_tables(seq, tm):
    rows_count = seq // GRID_W
    rows = jnp.repeat(jnp.arange(rows_count), GRID_W).astype(F32)
    cols = jnp.tile(jnp.arange(GRID_W), rows_count).astype(F32)
    axis_dim = HEAD_DIM // 2
    inv = ROPE_THETA ** (-jnp.arange(0, axis_dim, 2, dtype=F32) / axis_dim)
    ar = rows[:, None] * inv
    ac = cols[:, None] * inv
    cr, sr, cc, sc = jnp.cos(ar), jnp.sin(ar), jnp.cos(ac), jnp.sin(ac)
    z = jnp.zeros_like(sr)
    reps = LANES // HEAD_DIM
    cos = jnp.tile(jnp.concatenate([cr, cr, cc, cc], axis=1), (1, reps))
    sa = jnp.tile(jnp.concatenate([-sr, z, -sc, z], axis=1), (1, reps))
    sb = jnp.tile(jnp.concatenate([z, sr, z, sc], axis=1), (1, reps))
    ident = jnp.ones((tm, LANES), F32)
    zero = jnp.zeros((tm, LANES), F32)
    return (jnp.concatenate([cos, ident]), jnp.concatenate([sa, zero]), jnp.concatenate([sb, zero]))


FOURIER_LANES = 2 * LANES


def _split_bf16(t):
    hi = t.astype(BF16)
    return hi, (t - hi.astype(F32)).astype(BF16)


def _dot_split(a, b):
    return _dot(a[0], b[0]) + _dot(a[0], b[1]) + _dot(a[1], b[0])


def _fourier_tables(seq):
    n2 = GRID_W
    n1 = seq // n2
    norm = 1.0 / math.sqrt(seq * LANES)
    a = np.arange(n1)
    k1 = np.arange(n1)
    b = np.arange(n2)
    ang = (b[:, None, None] * k1[None, :, None] + (seq // n1) * k1[None, :, None] * a[None, None, :]) % seq
    th = 2.0 * np.pi * ang / seq
    m1 = np.concatenate([np.cos(th), -np.sin(th)], axis=1).astype(np.float32)
    ph = 2.0 * np.pi * ((b[:, None] * b[None, :]) % n2) / n2
    c2, s2 = np.cos(ph), np.sin(ph)
    g2 = np.block([[c2, s2], [-s2, c2]]).astype(np.float32)
    ch = np.arange(LANES)
    pc = 2.0 * np.pi * ((ch[:, None] * ch[None, :]) % LANES) / LANES
    cc = (np.cos(pc) * norm).astype(np.float32)
    sc = (np.sin(pc) * norm).astype(np.float32)
    cs = np.concatenate([cc, sc], axis=0)
    out = []
    for t in (m1, g2, cs):
        out.extend(_split_bf16(jnp.asarray(t)))
    return tuple(out)


FOURIER_ROW_CHUNK = 512


def _fourier_kernel(*refs, n1, n2, n_grp):
    u_refs = refs[:n_grp]
    m1h_ref, m1l_ref, g2h_ref, g2l_ref, csh_ref, csl_ref, o_ref, b_scr, xr_scr, xi_scr = refs[n_grp:]
    seq = n1 * n2
    for b in range(n2):
        xs = jnp.concatenate([u[pl.ds(b, n1, stride=n2), :] for u in u_refs], axis=1)
        z = _dot_split((m1h_ref[b], m1l_ref[b]), _split_bf16(xs))
        for g in range(n_grp):
            b_scr[g, 2 * n1 * b:2 * n1 * (b + 1), :] = z[:, g * LANES:(g + 1) * LANES]
    g2 = (g2h_ref[...], g2l_ref[...])
    for k1 in range(n1):
        bk = jnp.concatenate(
            [jnp.concatenate([b_scr.at[g][pl.ds(k1, n2, stride=2 * n1), :],
                              b_scr.at[g][pl.ds(n1 + k1, n2, stride=2 * n1), :]], axis=0)
             for g in range(n_grp)], axis=1)
        xk = _dot_split(g2, _split_bf16(bk))
        for g in range(n_grp):
            xr_scr[g, n2 * k1:n2 * (k1 + 1), :] = xk[:n2, g * LANES:(g + 1) * LANES]
            xi_scr[g, n2 * k1:n2 * (k1 + 1), :] = xk[n2:, g * LANES:(g + 1) * LANES]
    cs = (csh_ref[...], csl_ref[...])
    rc = min(FOURIER_ROW_CHUNK, seq)
    for g in range(n_grp):
        for r0 in range(0, seq, rc):
            x = jnp.concatenate([xr_scr[g, r0:r0 + rc, :], xi_scr[g, r0:r0 + rc, :]], axis=1)
            xr_scr[g, r0:r0 + rc, :] = _dot_split(_split_bf16(x), cs)
    for g in range(n_grp):
        for k2 in range(n2):
            o_ref[n1 * k2:n1 * (k2 + 1), g * LANES:(g + 1) * LANES] = (
                xr_scr.at[g][pl.ds(k2, n1, stride=n2), :].astype(o_ref.dtype))


def _fourier(f, tables, *, n_batch, seq, out_rows):
    n2 = GRID_W
    n1 = seq // n2
    n_grp = FOURIER_LANES // LANES
    table_specs = [pl.BlockSpec(t.shape, (lambda b, g, nd=t.ndim: (0,) * nd)) for t in tables]
    u_specs = [pl.BlockSpec((seq, LANES), (lambda b, g, k=k: (b, g * n_grp + k))) for k in range(n_grp)]
    return pl.pallas_call(
        functools.partial(_fourier_kernel, n1=n1, n2=n2, n_grp=n_grp),
        grid=(n_batch, f.shape[1] // FOURIER_LANES),
        in_specs=u_specs + table_specs,
        out_specs=pl.BlockSpec((seq, FOURIER_LANES), lambda b, g: (b, g)),
        out_shape=jax.ShapeDtypeStruct((out_rows, f.shape[1]), BF16),
        scratch_shapes=[pltpu.VMEM((n_grp, 2 * seq, LANES), F32), pltpu.VMEM((n_grp, seq, LANES), F32),
                        pltpu.VMEM((n_grp, seq, LANES), F32)],
        compiler_params=_cparams("parallel", "parallel"),
        name="fourier",
    )(*([f] * n_grp), *tables)


def _dense_fourier_tables(n):
    norm = 1.0 / math.sqrt(n * LANES)
    p = np.arange(n)
    ph = 2.0 * np.pi * ((p[:, None] * p[None, :]) % n) / n
    ch = np.arange(LANES)
    pc = 2.0 * np.pi * ((ch[:, None] * ch[None, :]) % LANES) / LANES
    return (jnp.asarray(np.cos(ph).astype(np.float32)), jnp.asarray(np.sin(ph).astype(np.float32)),
            jnp.asarray((np.cos(pc) * norm).astype(np.float32)), jnp.asarray((np.sin(pc) * norm).astype(np.float32)))


def _dense_fourier_kernel(u_ref, cl_ref, sl_ref, cc_ref, sc_ref, _, o_ref):
    u = u_ref[...]
    y = _dot_hi(cl_ref[...], _dot_hi(u, cc_ref[...])) - _dot_hi(sl_ref[...], _dot_hi(u, sc_ref[...]))
    o_ref[...] = y.astype(o_ref.dtype)


def _dense_fourier(f, tables, dst, *, n_batch, n_pos, row_block0):
    cl, sl, cc, sc = tables
    groups = f.shape[1] // LANES
    return pl.pallas_call(
        _dense_fourier_kernel,
        grid=(n_batch, groups),
        in_specs=[
            pl.BlockSpec((n_pos, LANES), lambda b, g: (row_block0 + b, g)),
            pl.BlockSpec(cl.shape, lambda b, g: (0, 0)),
            pl.BlockSpec(sl.shape, lambda b, g: (0, 0)),
            pl.BlockSpec(cc.shape, lambda b, g: (0, 0)),
            pl.BlockSpec(sc.shape, lambda b, g: (0, 0)),
            pl.BlockSpec(memory_space=pl.ANY),
        ],
        out_specs=pl.BlockSpec((n_pos, LANES), lambda b, g: (row_block0 + b, g)),
        out_shape=jax.ShapeDtypeStruct(dst.shape, dst.dtype),
        input_output_aliases={5: 0},
        compiler_params=_cparams("parallel", "parallel"),
        name="fourier_ctx",
    )(f, cl, sl, cc, sc, dst)


def _win_kernel(sink_ref, q_ref, *refs, n_qtiles, has_local, n_heads, group_size):
    if has_local:
        kp_ref, kc_ref, kn_ref, vp_ref, vc_ref, vn_ref, kx_ref, vx_ref, o_ref = refs
    else:
        kx_ref, vx_ref, _, o_ref = refs
    tq = q_ref.shape[0]
    lane = lax.broadcasted_iota(jnp.int32, (1, LANES), 1)
    half_mask = [lane < HEAD_DIM, lane >= HEAD_DIM]
    pieces = [(kx_ref[...], vx_ref[...], None)]
    if has_local:
        n = pl.program_id(1)
        qi = lax.broadcasted_iota(jnp.int32, (tq, BLOCK), 0)
        kj = lax.broadcasted_iota(jnp.int32, (tq, BLOCK), 1)
        valid_prev = (kj >= qi) & (n >= 1)
        valid_next = (kj <= qi - (tq - WINDOW)) & (n <= n_qtiles - 2)
        di = lax.broadcasted_iota(jnp.int32, (tq, tq), 0) - lax.broadcasted_iota(jnp.int32, (tq, tq), 1)
        valid_mid = (di <= WINDOW) & (di >= -WINDOW)
        pieces += [(kp_ref[...], vp_ref[...], valid_prev), (kc_ref[...], vc_ref[...], valid_mid),
                   (kn_ref[...], vn_ref[...], valid_next)]
    for pair in range(n_heads // 2):
        qp = q_ref[:, pair * LANES:(pair + 1) * LANES].astype(F32)
        qp_sw = pltpu.roll(qp, HEAD_DIM, 1)
        out_pair = jnp.zeros((tq, LANES), F32)
        for half in range(2):
            head = 2 * pair + half
            kv = head // group_size
            src = qp if half == kv else qp_sw
            qe = jnp.where(half_mask[kv], src, 0.0).astype(BF16)
            sink = sink_ref[head] * LOG2E
            scores = []
            m = jnp.zeros((tq, 1), F32) + sink
            for k, _, valid in pieces:
                s = _dot_nt(qe, k)
                if valid is not None:
                    s = jnp.where(valid, s, NEG)
                scores.append(s)
                m = jnp.maximum(m, jnp.max(s, axis=-1, keepdims=True))
            den = jnp.exp2(sink - m)
            pv = jnp.zeros((tq, LANES), F32)
            for s, (_, v, _) in zip(scores, pieces):
                p = jnp.exp2(s - m)
                den = den + jnp.sum(p, axis=-1, keepdims=True)
                pv = pv + _dot(p.astype(BF16), v)
            pv = pv / den
            if half != kv:
                pv = pltpu.roll(pv, HEAD_DIM, 1)
            out_pair = jnp.where(half_mask[half], pv, out_pair)
        o_ref[:, pair * LANES:(pair + 1) * LANES] = out_pair.astype(o_ref.dtype)


def _win_attention(z, sink, *, n_batch, seq, n_ctx, q_cols, k_col, v_col, tq, out_rows):
    nbk = seq // BLOCK
    nq = seq // tq
    per = tq // BLOCK
    ctx_blk0 = n_batch * seq // n_ctx
    n_heads = q_cols // HEAD_DIM
    group_size = n_heads // (LANES // HEAD_DIM)

    def edge(col, first):
        return pl.BlockSpec(
            (BLOCK, LANES), lambda b, n: (b * nbk + jnp.clip(n * per + first, 0, nbk - 1), col))

    def mid(col):
        return pl.BlockSpec((tq, LANES), lambda b, n: (b * nq + n, col))

    return pl.pallas_call(
        functools.partial(_win_kernel, n_qtiles=nq, has_local=True, n_heads=n_heads, group_size=group_size),
        grid=(n_batch, nq),
        in_specs=[
            pl.BlockSpec(memory_space=pltpu.SMEM),
            pl.BlockSpec((tq, q_cols), lambda b, n: (b * nq + n, 0)),
            edge(k_col, -1), mid(k_col), edge(k_col, per),
            edge(v_col, -1), mid(v_col), edge(v_col, per),
            pl.BlockSpec((n_ctx, LANES), lambda b, n: (ctx_blk0 + b, k_col)),
            pl.BlockSpec((n_ctx, LANES), lambda b, n: (ctx_blk0 + b, v_col)),
        ],
        out_specs=pl.BlockSpec((tq, q_cols), lambda b, n: (b * nq + n, 0)),
        out_shape=jax.ShapeDtypeStruct((out_rows, q_cols), BF16),
        compiler_params=_cparams("parallel", "parallel"),
        name="win_attn",
    )(sink, z, z, z, z, z, z, z, z, z)


def _ctx_gqa_attention(z, sink, dst, *, n_batch, seq, n_ctx, q_cols, k_col, v_col):
    ctx_blk0 = n_batch * seq // n_ctx
    n_heads = q_cols // HEAD_DIM
    group_size = n_heads // (LANES // HEAD_DIM)
    return pl.pallas_call(
        functools.partial(_win_kernel, n_qtiles=0, has_local=False, n_heads=n_heads, group_size=group_size),
        grid=(n_batch,),
        in_specs=[
            pl.BlockSpec(memory_space=pltpu.SMEM),
            pl.BlockSpec((n_ctx, q_cols), lambda b: (ctx_blk0 + b, 0)),
            pl.BlockSpec((n_ctx, LANES), lambda b: (ctx_blk0 + b, k_col)),
            pl.BlockSpec((n_ctx, LANES), lambda b: (ctx_blk0 + b, v_col)),
            pl.BlockSpec(memory_space=pl.ANY),
        ],
        out_specs=pl.BlockSpec((n_ctx, q_cols), lambda b: (ctx_blk0 + b, 0)),
        out_shape=jax.ShapeDtypeStruct(dst.shape, dst.dtype),
        input_output_aliases={4: 0},
        compiler_params=_cparams("parallel"),
        name="ctx_gqa",
    )(sink, z, z, z, dst)


LOG2E = math.log2(math.e)
DIFF_SUB_ROWS = 128


def _diff_kernel(lam_ref, g_ref, q_ref, *refs, lam_init, has_x):
    if has_x:
        kx_ref, vx_ref, kc_ref, vc_ref, o_ref, s_scr = refs
        nx = kx_ref.shape[0]
    else:
        kc_ref, vc_ref, _, o_ref, s_scr = refs
        nx = 0
    nc = kc_ref.shape[0]
    sub = s_scr.shape[1]
    n_sub = q_ref.shape[0] // sub
    lv = lam_ref[...]
    lam = (jnp.exp(jnp.sum(lv[0:1] * lv[1:2], axis=-1, keepdims=True))
           - jnp.exp(jnp.sum(lv[2:3] * lv[3:4], axis=-1, keepdims=True)) + lam_init)
    lane = lax.broadcasted_iota(jnp.int32, (1, LANES), 1)
    units = [(h, c) for h in range(n_sub) for c in range(2)]

    def scores(u):
        h, c = units[u]
        q = q_ref[h * sub:(h + 1) * sub, :]
        qm = jnp.where((lane < HEAD_DIM) if c == 0 else (lane >= HEAD_DIM), q, jnp.zeros_like(q))
        if has_x:
            s_scr[u % 2, :, 0:nx] = _dot_nt(qm, kx_ref[...])
        s_scr[u % 2, :, nx:nx + nc] = _dot_nt(qm, kc_ref[...])

    def attend(u):
        s = s_scr[u % 2]
        p = jnp.exp2(s - jnp.max(s, axis=-1, keepdims=True))
        den = jnp.sum(p, axis=-1, keepdims=True)
        pb = p.astype(BF16)
        acc = _dot(pb[:, nx:nx + nc], vc_ref[...])
        if has_x:
            acc = acc + _dot(pb[:, 0:nx], vx_ref[...])
        return acc / den

    res = []
    scores(0)
    for u in range(len(units)):
        if u + 1 < len(units):
            scores(u + 1)
        res.append(attend(u))
    for h in range(n_sub):
        o = res[2 * h] - lam * res[2 * h + 1]
        o = _rms(o) * g_ref[...] * (1.0 - lam_init)
        o_ref[h * sub:(h + 1) * sub, :] = o.astype(o_ref.dtype)


def _diff_attention(z, lam_vec, subln_g, lam_init, *, n_batch, seq, n_ctx, n_heads, tq, out_rows):
    ctx_blk0 = n_batch * seq // n_ctx
    nq = seq // tq
    return pl.pallas_call(
        functools.partial(_diff_kernel, lam_init=lam_init, has_x=True),
        grid=(n_batch, n_heads, nq),
        in_specs=[
            pl.BlockSpec(lam_vec.shape, lambda b, h, i: (0, 0)),
            pl.BlockSpec((1, LANES), lambda b, h, i: (0, 0)),
            pl.BlockSpec((tq, LANES), lambda b, h, i: (b * nq + i, h)),
            pl.BlockSpec((seq, LANES), lambda b, h, i: (b, n_heads + h)),
            pl.BlockSpec((seq, LANES), lambda b, h, i: (b, 2 * n_heads + h)),
            pl.BlockSpec((n_ctx, LANES), lambda b, h, i: (ctx_blk0 + b, n_heads + h)),
            pl.BlockSpec((n_ctx, LANES), lambda b, h, i: (ctx_blk0 + b, 2 * n_heads + h)),
        ],
        out_specs=pl.BlockSpec((tq, LANES), lambda b, h, i: (b * nq + i, h)),
        out_shape=jax.ShapeDtypeStruct((out_rows, n_heads * LANES), BF16),
        scratch_shapes=[pltpu.VMEM((2, min(tq, DIFF_SUB_ROWS), seq + n_ctx), F32)],
        compiler_params=_cparams("parallel", "parallel", "parallel"),
        name="diff_attn",
    )(lam_vec, subln_g, z, z, z, z, z)


def _diff_attention_ctx(z, lam_vec, subln_g, lam_init, dst, *, n_batch, seq, n_ctx, n_heads):
    ctx_blk0 = n_batch * seq // n_ctx
    return pl.pallas_call(
        functools.partial(_diff_kernel, lam_init=lam_init, has_x=False),
        grid=(n_batch, n_heads),
        in_specs=[
            pl.BlockSpec(lam_vec.shape, lambda b, h: (0, 0)),
            pl.BlockSpec((1, LANES), lambda b, h: (0, 0)),
            pl.BlockSpec((n_ctx, LANES), lambda b, h: (ctx_blk0 + b, h)),
            pl.BlockSpec((n_ctx, LANES), lambda b, h: (ctx_blk0 + b, n_heads + h)),
            pl.BlockSpec((n_ctx, LANES), lambda b, h: (ctx_blk0 + b, 2 * n_heads + h)),
            pl.BlockSpec(memory_space=pl.ANY),
        ],
        out_specs=pl.BlockSpec((n_ctx, LANES), lambda b, h: (ctx_blk0 + b, h)),
        out_shape=jax.ShapeDtypeStruct(dst.shape, dst.dtype),
        input_output_aliases={5: 0},
        scratch_shapes=[pltpu.VMEM((2, min(n_ctx, DIFF_SUB_ROWS), n_ctx), F32)],
        compiler_params=_cparams("parallel", "parallel"),
        name="diff_attn_ctx",
    )(lam_vec, subln_g, z, z, z, dst)


def _outproj_kernel(a0_ref, a1_ref, w_ref, x_ref, g_ref, gate_ref, o_ref):
    half = a0_ref.shape[1]
    y = _dot(a0_ref[...], w_ref[:half, :]) + _dot(a1_ref[...], w_ref[half:, :])
    o_ref[...] = x_ref[...] + gate_ref[...] * (_rms(y) * g_ref[...])


def _outproj(a0, a0_col, a1, a1_col, w, xt, normg, mod, k_gate, *, tm, rows, seq, n_batch):
    d = xt.shape[1]
    half = d // 2

    def mrow(i):
        return jnp.minimum(i * tm // seq, n_batch)

    return pl.pallas_call(
        _outproj_kernel,
        grid=(rows // tm,),
        in_specs=[
            pl.BlockSpec((tm, half), lambda i: (i, a0_col)),
            pl.BlockSpec((tm, half), lambda i: (i, a1_col)),
            pl.BlockSpec((d, d), lambda i: (0, 0)),
            pl.BlockSpec((tm, d), lambda i: (i, 0)),
            pl.BlockSpec((None, 1, d), lambda i: (1, 0, 0)),
            pl.BlockSpec((None, None, 1, d), lambda i: (mrow(i), k_gate, 0, 0)),
        ],
        out_specs=pl.BlockSpec((tm, d), lambda i: (i, 0)),
        out_shape=jax.ShapeDtypeStruct((rows, d), F32),
        compiler_params=_cparams("parallel"),
        name="outproj",
    )(a0, a1, w, xt, normg, mod)


def _router_kernel(x_ref, g_ref, sh_ref, sc_ref, rw_ref, o_ref, sel_ref):
    h = _modulate(x_ref[...], g_ref[...], sh_ref[...], sc_ref[...])
    logits = _dot_hi(h, rw_ref[...])
    lane = lax.broadcasted_iota(jnp.int32, logits.shape, 1)
    ninf = -jnp.inf
    logits = jnp.where(lane < N_EXPERTS, logits, ninf)
    m1 = jnp.max(logits, axis=-1, keepdims=True)
    i1 = jnp.min(jnp.where(logits == m1, lane, LANES), axis=-1, keepdims=True)
    sel1 = lane == i1
    rest = jnp.where(sel1, ninf, logits)
    m2 = jnp.max(rest, axis=-1, keepdims=True)
    i2 = jnp.min(jnp.where(rest == m2, lane, LANES), axis=-1, keepdims=True)
    sel2 = lane == i2
    e2 = jnp.exp(m2 - m1)
    den = 1.0 + e2
    o_ref[...] = jnp.where(sel1, 1.0 / den, 0.0) + jnp.where(sel2, e2 / den, 0.0)
    sel_ref[...] = jnp.where(sel1 | sel2, 1.0, 0.0)


def _router(xt, normg, mod, router_w, *, tm, rows, seq, n_batch):
    d = xt.shape[1]
    rw = jnp.zeros((d, LANES), F32).at[:, :N_EXPERTS].set(router_w)

    def mrow(i):
        return jnp.minimum(i * tm // seq, n_batch)

    return pl.pallas_call(
        _router_kernel,
        grid=(rows // tm,),
        in_specs=[
            pl.BlockSpec((tm, d), lambda i: (i, 0)),
            pl.BlockSpec((None, 1, d), lambda i: (2, 0, 0)),
            pl.BlockSpec((None, None, 1, d), lambda i: (mrow(i), 3, 0, 0)),
            pl.BlockSpec((None, None, 1, d), lambda i: (mrow(i), 4, 0, 0)),
            pl.BlockSpec((d, LANES), lambda i: (0, 0)),
        ],
        out_specs=[pl.BlockSpec((tm, LANES), lambda i: (i, 0))] * 2,
        out_shape=[jax.ShapeDtypeStruct((rows, LANES), F32)] * 2,
        compiler_params=_cparams("parallel"),
        name="router",
    )(xt, normg, mod, mod, rw)


SWIGLU_CHUNK = 256


def _swiglu(h, wg_ref, wu_ref, wo_ref, u_off):
    width = wo_ref.shape[0]
    acc = None
    for c0 in range(0, width, SWIGLU_CHUNK):
        gp = _dot(h, wg_ref[:, c0:c0 + SWIGLU_CHUNK])
        up = _dot(h, wu_ref[:, u_off + c0:u_off + c0 + SWIGLU_CHUNK])
        a = gp * (1.0 / (1.0 + jnp.exp(-gp))) * up
        part = _dot(a.astype(BF16), wo_ref[c0:c0 + SWIGLU_CHUNK, :])
        acc = part if acc is None else acc + part
    return acc


def _ffn_kernel(x_ref, g2_ref, sh_ref, sc_ref, win_ref, wout_ref, g3_ref, gate_ref, o_ref):
    x = x_ref[...]
    h = _modulate(x, g2_ref[...], sh_ref[...], sc_ref[...]).astype(BF16)
    y = _swiglu(h, win_ref, win_ref, wout_ref, wout_ref.shape[0])
    o_ref[...] = x + gate_ref[...] * (_rms(y) * g3_ref[...])


def _ffn(xt, normg, mod, w_in, w_out, *, tm, rows, seq, n_batch):
    d = xt.shape[1]

    def mrow(i):
        return jnp.minimum(i * tm // seq, n_batch)

    in_specs = [
        pl.BlockSpec((tm, d), lambda i: (i, 0)),
        pl.BlockSpec((None, 1, d), lambda i: (2, 0, 0)),
        pl.BlockSpec((None, None, 1, d), lambda i: (mrow(i), 3, 0, 0)),
        pl.BlockSpec((None, None, 1, d), lambda i: (mrow(i), 4, 0, 0)),
        pl.BlockSpec(w_in.shape, lambda i: (0, 0)),
        pl.BlockSpec(w_out.shape, lambda i: (0, 0)),
        pl.BlockSpec((None, 1, d), lambda i: (3, 0, 0)),
        pl.BlockSpec((None, None, 1, d), lambda i: (mrow(i), 5, 0, 0)),
    ]
    return pl.pallas_call(
        _ffn_kernel,
        grid=(rows // tm,),
        in_specs=in_specs,
        out_specs=pl.BlockSpec((tm, d), lambda i: (i, 0)),
        out_shape=jax.ShapeDtypeStruct((rows, d), F32),
        compiler_params=_cparams("parallel"),
        name="ffn",
    )(xt, normg, mod, mod, w_in, w_out, normg, mod)


MOE_PIECE_SIGMAS = 2.3


def _moe_kernel(npiece_ref, x_ref, g2_ref, sh_ref, sc_ref, wg_ref, wu_ref, wo_ref, g3_ref, gate_ref,
                comb_ref, sel_ref, o_ref, h_scr, key_scr, keyt_scr, hs_scr, y_scr, acc_scr, *, n_exp, n_f, piece):
    b = pl.program_id(0)
    e = pl.program_id(1)
    f = pl.program_id(2)
    nb = x_ref.shape[0]
    n_pieces = npiece_ref[b * n_exp + e]

    @pl.when((e == 0) & (f == 0))
    def _():
        h_scr[...] = _modulate(x_ref[...], g2_ref[...], sh_ref[...], sc_ref[...]).astype(BF16)
        o_ref[...] = jnp.zeros_like(o_ref)
        ti = lax.broadcasted_iota(jnp.int32, (nb, nb), 0)
        tj = lax.broadcasted_iota(jnp.int32, (nb, nb), 1)
        lower = jnp.where(tj < ti, 1.0, 0.0).astype(BF16)
        sel = sel_ref[...]
        rank = _dot(lower, sel.astype(BF16))
        key = jnp.where(sel > 0.0, rank, -1.0)
        key_scr[...] = key
        keyt_scr[...] = key.T

    def for_rows(body):
        def one(i, carry):
            body(pl.multiple_of(i * piece, 16), piece)
            return carry

        lax.fori_loop(0, n_pieces, one, 0)

    @pl.when(f == 0)
    def _():
        key_row = keyt_scr[pl.ds(e, 1), :]

        def gather(r0, n):
            r = (r0 + lax.broadcasted_iota(jnp.int32, (n, 1), 0)).astype(F32)
            onehot = jnp.where(key_row == r, 1.0, 0.0).astype(BF16)
            hs_scr[pl.ds(r0, n), :] = _dot(onehot, h_scr[...]).astype(BF16)

        for_rows(gather)

    def expert(r0, n):
        rows = pl.ds(r0, n)
        part = _swiglu(hs_scr[rows, :], wg_ref, wu_ref, wo_ref, 0)

        @pl.when(f == 0)
        def _():
            y_scr[rows, :] = part

        @pl.when(f > 0)
        def _():
            y_scr[rows, :] += part

    for_rows(expert)

    @pl.when(f == n_f - 1)
    def _():
        lane = lax.broadcasted_iota(jnp.int32, (nb, LANES), 1)
        pick = lane == e
        key_col = jnp.sum(jnp.where(pick, key_scr[...], 0.0), axis=-1, keepdims=True)
        gate_col = jnp.sum(jnp.where(pick, comb_ref[...], 0.0), axis=-1, keepdims=True)
        acc_scr[...] = jnp.zeros_like(acc_scr)

        def scatter(r0, n):
            r = (r0 + lax.broadcasted_iota(jnp.int32, (1, n), 1)).astype(F32)
            onehot = jnp.where(key_col == r, 1.0, 0.0).astype(BF16)
            acc_scr[...] += _dot(onehot, y_scr[pl.ds(r0, n), :].astype(BF16))

        for_rows(scatter)
        o_ref[...] += gate_col * acc_scr[...]

    @pl.when((e == n_exp - 1) & (f == n_f - 1))
    def _():
        o_ref[...] = x_ref[...] + gate_ref[...] * (_rms(o_ref[...]) * g3_ref[...])


def _moe(xt, normg, mod, w_in, w_out, comb, sel, *, nb, tf, rows, seq, n_batch):
    d = xt.shape[1]
    n_exp, fdim = w_out.shape[0], w_out.shape[1]
    n_f = fdim // tf
    n_blk = rows // nb
    counts = jnp.sum(sel.reshape(n_blk, nb, LANES)[:, :, :n_exp], axis=1).astype(jnp.int32)
    p_sel = 2.0 / n_exp
    piece = nb * 2 // n_exp + int(math.ceil(MOE_PIECE_SIGMAS * math.sqrt(nb * p_sel * (1.0 - p_sel))))
    piece = min(nb, -(-piece // 16) * 16)
    cap = -(-nb // piece) * piece
    npiece = ((counts + (piece - 1)) // piece).reshape(n_blk * n_exp)

    def mrow(i):
        return jnp.minimum(i * nb // seq, n_batch)

    in_specs = [
        pl.BlockSpec((nb, d), lambda i, e, f, s: (i, 0)),
        pl.BlockSpec((None, 1, d), lambda i, e, f, s: (2, 0, 0)),
        pl.BlockSpec((None, None, 1, d), lambda i, e, f, s: (mrow(i), 3, 0, 0)),
        pl.BlockSpec((None, None, 1, d), lambda i, e, f, s: (mrow(i), 4, 0, 0)),
        pl.BlockSpec((None, d, tf), lambda i, e, f, s: (e, 0, f)),
        pl.BlockSpec((None, d, tf), lambda i, e, f, s: (e, 0, n_f + f)),
        pl.BlockSpec((None, tf, d), lambda i, e, f, s: (e, f, 0)),
        pl.BlockSpec((None, 1, d), lambda i, e, f, s: (3, 0, 0)),
        pl.BlockSpec((None, None, 1, d), lambda i, e, f, s: (mrow(i), 5, 0, 0)),
        pl.BlockSpec((nb, LANES), lambda i, e, f, s: (i, 0)),
        pl.BlockSpec((nb, LANES), lambda i, e, f, s: (i, 0)),
    ]
    return pl.pallas_call(
        functools.partial(_moe_kernel, n_exp=n_exp, n_f=n_f, piece=piece),
        grid_spec=pltpu.PrefetchScalarGridSpec(
            num_scalar_prefetch=1,
            grid=(n_blk, n_exp, n_f),
            in_specs=in_specs,
            out_specs=pl.BlockSpec((nb, d), lambda i, e, f, s: (i, 0)),
            scratch_shapes=[
                pltpu.VMEM((nb, d), BF16),
                pltpu.VMEM((nb, LANES), F32),
                pltpu.VMEM((LANES, nb), F32),
                pltpu.VMEM((cap, d), BF16),
                pltpu.VMEM((cap, d), F32),
                pltpu.VMEM((nb, d), F32),
            ],
        ),
        out_shape=jax.ShapeDtypeStruct((rows, d), F32),
        compiler_params=_cparams("parallel", "arbitrary", "arbitrary"),
        name="moe",
    )(npiece, xt, normg, mod, mod, w_in, w_in, w_out, normg, mod, comb, sel)


def _lambda_init(layer):
    return 0.8 - 0.6 * math.exp(-0.3 * layer)


def kernel(x, c, ctx, c_ctx, ada_w, ada_b, norm_g, mix_in_w, mix_out_w, win_sink, diff_qkv_w, diff_out_w,
           diff_lambda, diff_subln_g, ffn_in_w, ffn_out_w, router_w, expert_in_w, expert_out_w):
    n_batch, seq, d = x.shape
    n_ctx = ctx.shape[1]
    depth = ada_w.shape[0]
    n_lat = n_batch * seq
    n_all = n_lat + n_batch * n_ctx
    fdim = mix_in_w.shape[2] - (d // 2 + 2 * LANES)
    q_cols = d // 2
    n_diff_heads = d // LANES

    tm_proj = _pick_tile(512, seq, n_batch * n_ctx)
    tm_out = _pick_tile(512, seq, n_batch * n_ctx)
    tm_ffn = _pick_tile(512, seq, n_batch * n_ctx)
    nb_moe = _pick_tile(1024, seq, n_batch * n_ctx)
    tq_diff = _pick_tile(512, seq)
    tq_win = _pick_tile(512, seq)
    common = dict(seq=seq, n_batch=n_batch)

    xt = jnp.concatenate([x.reshape(n_lat, d), ctx.reshape(n_batch * n_ctx, d)], axis=0)
    n_mod = -(-(n_batch + 1) // 8) * 8
    cv = jnp.zeros((n_mod, d), F32).at[:n_batch].set(c).at[n_batch].set(c_ctx)
    mods = _modvec(cv, ada_w, ada_b).reshape(depth, n_mod, 6, 1, d)
    rope = _rope_tables(seq, tm_proj)
    f_tables = _fourier_tables(seq)
    fc_tables = _dense_fourier_tables(n_ctx)

    n_f = fdim // LANES
    plan_even = ([(0, i * LANES, "plain") for i in range(n_f)]
                 + [(1, i * LANES, "rope_q_log2") for i in range(q_cols // LANES)]
                 + [(1, q_cols, "rope_k"), (1, q_cols + LANES, "plain")])
    plan_odd = ([(0, i * LANES, "rope_q_log2") for i in range(n_diff_heads)]
                + [(0, d + i * LANES, "rope_k") for i in range(n_diff_heads)]
                + [(0, 2 * d + i * LANES, "plain") for i in range(n_diff_heads)])

    for layer in range(depth):
        j = layer // 2
        need_ctx = layer < depth - 1
        rows = n_all if need_ctx else n_lat
        mod = mods[layer]
        ng = norm_g[layer].reshape(4, 1, d)
        if layer % 2 == 0:
            f, z = _proj(xt, ng, mod, 0, 1, mix_in_w[j].astype(BF16), rope, plan_even,
                         [(fdim, F32), (q_cols + 2 * LANES, BF16)], tm=tm_proj, n_lat=n_lat, **common)
            k_col, v_col = q_cols // LANES, q_cols // LANES + 1
            att = dict(n_batch=n_batch, seq=seq, n_ctx=n_ctx, q_cols=q_cols, k_col=k_col, v_col=v_col)
            mix_f = _fourier(f, f_tables, n_batch=n_batch, seq=seq, out_rows=rows)
            mix_a = _win_attention(z, win_sink[j], tq=tq_win, out_rows=rows, **att)
            if need_ctx:
                mix_f = _dense_fourier(f, fc_tables, mix_f, n_batch=n_batch, n_pos=n_ctx,
                                       row_block0=n_lat // n_ctx)
                mix_a = _ctx_gqa_attention(z, win_sink[j], mix_a, **att)
            xt = _outproj(mix_f, 0, mix_a, 0, mix_out_w[j].astype(BF16), xt, ng, mod, 2, tm=tm_out, rows=rows,
                          **common)
            xt = _ffn(xt, ng, mod, ffn_in_w[j].astype(BF16), ffn_out_w[j].astype(BF16),
                      tm=tm_ffn, rows=rows, **common)
        else:
            lam_init = _lambda_init(layer)
            (z,) = _proj(xt, ng, mod, 0, 1, diff_qkv_w[j].astype(BF16), rope, plan_odd, [(3 * d, BF16)],
                         tm=tm_proj, n_lat=n_lat, **common)
            subg = diff_subln_g[j].reshape(1, LANES)
            att = dict(n_batch=n_batch, seq=seq, n_ctx=n_ctx, n_heads=n_diff_heads)
            mix = _diff_attention(z, diff_lambda[j], subg, lam_init, tq=tq_diff, out_rows=rows, **att)
            if need_ctx:
                mix = _diff_attention_ctx(z, diff_lambda[j], subg, lam_init, mix, **att)
            xt = _outproj(mix, 0, mix, 1, diff_out_w[j].astype(BF16), xt, ng, mod, 2, tm=tm_out, rows=rows,
                          **common)
            comb, sel = _router(xt, ng, mod, router_w[j], tm=tm_out, rows=rows, **common)
            xt = _moe(xt, ng, mod, expert_in_w[j].astype(BF16), expert_out_w[j].astype(BF16), comb, sel,
                      nb=nb_moe, tf=1792, rows=rows, **common)
    return xt[:n_lat].reshape(n_batch, seq, d)
```

```python
import functools
import math

import numpy as np
import jax
import jax.numpy as jnp
from jax import lax
from jax.experimental import pallas as pl
from jax.experimental.pallas import tpu as pltpu

EPS = 1e-6
NEG = -1e30
HEAD_DIM = 64
LANES = 128
GRID_W = 64
BLOCK = 128
WINDOW = 128
ROPE_THETA = 10000.0
N_EXPERTS = 8
F32 = jnp.float32
BF16 = jnp.bfloat16
HIGHEST = lax.Precision.HIGHEST
VMEM_LIMIT = 56 * 1024 * 1024


def _cparams(*sem):
    return pltpu.CompilerParams(dimension_semantics=sem, vmem_limit_bytes=VMEM_LIMIT)


def _dot(a, b):
    return jnp.dot(a, b, preferred_element_type=F32)


def _dot_nt(a, b):
    return lax.dot_general(a, b, (((1,), (1,)), ((), ())), preferred_element_type=F32)


def _dot_hi(a, b):
    return jnp.dot(a, b, precision=HIGHEST, preferred_element_type=F32)


def _rms(v):
    return v * lax.rsqrt(jnp.mean(v * v, axis=-1, keepdims=True) + EPS)


def _modulate(x, g, sh, sc):
    return _rms(x) * g * (1.0 + sc) + sh


def _pick_tile(pref, *dims):
    t = pref
    while any(d % t for d in dims):
        t //= 2
    return t


def _modvec_kernel(c_ref, w_ref, b_ref, o_ref):
    cv = c_ref[...]
    s = cv * (1.0 / (1.0 + jnp.exp(-cv)))
    o_ref[...] = _dot(s.astype(BF16), w_ref[...].astype(BF16)) + b_ref[...]


def _modvec(cv, ada_w, ada_b):
    depth, d, n = ada_w.shape
    r = cv.shape[0]
    tn = _pick_tile(1536, n)
    return pl.pallas_call(
        _modvec_kernel,
        grid=(depth, n // tn),
        in_specs=[
            pl.BlockSpec((r, d), lambda l, j: (0, 0)),
            pl.BlockSpec((None, d, tn), lambda l, j: (l, 0, j)),
            pl.BlockSpec((None, 1, tn), lambda l, j: (l, 0, j)),
        ],
        out_specs=pl.BlockSpec((None, r, tn), lambda l, j: (l, 0, j)),
        out_shape=jax.ShapeDtypeStruct((depth, r, n), F32),
        compiler_params=_cparams("parallel", "parallel"),
        name="modvec",
    )(cv, ada_w, ada_b.reshape(depth, 1, n))


def _proj_kernel(x_ref, g_ref, sh_ref, sc_ref, w_ref, cos_ref, sa_ref, sb_ref, *o_refs, plan, group):
    h = _modulate(x_ref[...], g_ref[...], sh_ref[...], sc_ref[...]).astype(BF16)
    n = w_ref.shape[1]
    for g0 in range(0, n, group):
        acc = _dot(h, w_ref[:, g0:g0 + group])
        for c0 in range(0, group, LANES):
            oi, oc, mode = plan[(g0 + c0) // LANES]
            v = acc[:, c0:c0 + LANES]
            if mode != "plain":
                v = (v * cos_ref[...] + pltpu.roll(v, LANES - 16, 1) * sa_ref[...]
                     + pltpu.roll(v, 16, 1) * sb_ref[...])
                if mode == "rope_q_log2":
                    v = v * (HEAD_DIM ** -0.5 * LOG2E)
            o_refs[oi][:, oc:oc + LANES] = v.astype(o_refs[oi].dtype)


def _proj(xt, normg, mod, k_sh, k_sc, w, rope, plan, outs, *, tm, n_lat, seq, n_batch):
    t, d = xt.shape
    n = w.shape[1]
    group = _pick_tile(512, n)
    nx = n_lat // tm
    per = seq // tm

    def mrow(i):
        return jnp.minimum(i * tm // seq, n_batch)

    def rrow(i):
        return jnp.where(i < nx, i % per, per)

    in_specs = [
        pl.BlockSpec((tm, d), lambda i: (i, 0)),
        pl.BlockSpec((None, 1, d), lambda i: (0, 0, 0)),
        pl.BlockSpec((None, None, 1, d), lambda i: (mrow(i), k_sh, 0, 0)),
        pl.BlockSpec((None, None, 1, d), lambda i: (mrow(i), k_sc, 0, 0)),
        pl.BlockSpec((d, n), lambda i: (0, 0)),
        pl.BlockSpec((tm, LANES), lambda i: (rrow(i), 0)),
        pl.BlockSpec((tm, LANES), lambda i: (rrow(i), 0)),
        pl.BlockSpec((tm, LANES), lambda i: (rrow(i), 0)),
    ]
    out_specs = [pl.BlockSpec((tm, wd), lambda i: (i, 0)) for wd, _ in outs]
    out_shape = [jax.ShapeDtypeStruct((t, wd), dt) for wd, dt in outs]
    return pl.pallas_call(
        functools.partial(_proj_kernel, plan=plan, group=group),
        grid=(t // tm,),
        in_specs=in_specs,
        out_specs=out_specs,
        out_shape=out_shape,
        compiler_params=_cparams("parallel"),
        name="proj",
    )(xt, normg, mod, mod, w, *rope)


def _rope_tables(seq, tm):
    rows_count = seq // GRID_W
    rows = jnp.repeat(jnp.arange(rows_count), GRID_W).astype(F32)
    cols = jnp.tile(jnp.arange(GRID_W), rows_count).astype(F32)
    axis_dim = HEAD_DIM // 2
    inv = ROPE_THETA ** (-jnp.arange(0, axis_dim, 2, dtype=F32) / axis_dim)
    ar = rows[:, None] * inv
    ac = cols[:, None] * inv
    cr, sr, cc, sc = jnp.cos(ar), jnp.sin(ar), jnp.cos(ac), jnp.sin(ac)
    z = jnp.zeros_like(sr)
    reps = LANES // HEAD_DIM
    cos = jnp.tile(jnp.concatenate([cr, cr, cc, cc], axis=1), (1, reps))
    sa = jnp.tile(jnp.concatenate([-sr, z, -sc, z], axis=1), (1, reps))
    sb = jnp.tile(jnp.concatenate([z, sr, z, sc], axis=1), (1, reps))
    ident = jnp.ones((tm, LANES), F32)
    zero = jnp.zeros((tm, LANES), F32)
    return (jnp.concatenate([cos, ident]), jnp.concatenate([sa, zero]), jnp.concatenate([sb, zero]))


FOURIER_LANES = 2 * LANES


def _split_bf16(t):
    hi = t.astype(BF16)
    return hi, (t - hi.astype(F32)).astype(BF16)


def _dot_split(a, b):
    return _dot(a[0], b[0]) + _dot(a[0], b[1]) + _dot(a[1], b[0])


def _fourier_tables(seq):
    n2 = GRID_W
    n1 = seq // n2
    norm = 1.0 / math.sqrt(seq * LANES)
    a = np.arange(n1)
    k1 = np.arange(n1)
    b = np.arange(n2)
    ang = (b[:, None, None] * k1[None, :, None] + (seq // n1) * k1[None, :, None] * a[None, None, :]) % seq
    th = 2.0 * np.pi * ang / seq
    m1 = np.concatenate([np.cos(th), -np.sin(th)], axis=1).astype(np.float32)
    ph = 2.0 * np.pi * ((b[:, None] * b[None, :]) % n2) / n2
    c2, s2 = np.cos(ph), np.sin(ph)
    g2 = np.block([[c2, s2], [-s2, c2]]).astype(np.float32)
    ch = np.arange(LANES)
    pc = 2.0 * np.pi * ((ch[:, None] * ch[None, :]) % LANES) / LANES
    cc = (np.cos(pc) * norm).astype(np.float32)
    sc = (np.sin(pc) * norm).astype(np.float32)
    cs = np.concatenate([cc, sc], axis=0)
    out = []
    for t in (m1, g2, cs):
        out.extend(_split_bf16(jnp.asarray(t)))
    return tuple(out)


FOURIER_ROW_CHUNK = 512


def _fourier_kernel(*refs, n1, n2, n_grp):
    u_refs = refs[:n_grp]
    m1h_ref, m1l_ref, g2h_ref, g2l_ref, csh_ref, csl_ref, _, o_ref, b_scr, xr_scr, xi_scr = refs[n_grp:]
    seq = n1 * n2
    for b in range(n2):
        xs = jnp.concatenate([u[pl.ds(b, n1, stride=n2), :] for u in u_refs], axis=1)
        z = _dot_split((m1h_ref[b], m1l_ref[b]), _split_bf16(xs))
        for g in range(n_grp):
            b_scr[g, 2 * n1 * b:2 * n1 * (b + 1), :] = z[:, g * LANES:(g + 1) * LANES]
    g2 = (g2h_ref[...], g2l_ref[...])
    for k1 in range(n1):
        bk = jnp.concatenate(
            [jnp.concatenate([b_scr.at[g][pl.ds(k1, n2, stride=2 * n1), :],
                              b_scr.at[g][pl.ds(n1 + k1, n2, stride=2 * n1), :]], axis=0)
             for g in range(n_grp)], axis=1)
        xk = _dot_split(g2, _split_bf16(bk))
        for g in range(n_grp):
            xr_scr[g, n2 * k1:n2 * (k1 + 1), :] = xk[:n2, g * LANES:(g + 1) * LANES]
            xi_scr[g, n2 * k1:n2 * (k1 + 1), :] = xk[n2:, g * LANES:(g + 1) * LANES]
    cs = (csh_ref[...], csl_ref[...])
    rc = min(FOURIER_ROW_CHUNK, seq)
    for g in range(n_grp):
        for r0 in range(0, seq, rc):
            x = jnp.concatenate([xr_scr[g, r0:r0 + rc, :], xi_scr[g, r0:r0 + rc, :]], axis=1)
            xr_scr[g, r0:r0 + rc, :] = _dot_split(_split_bf16(x), cs)
    for g in range(n_grp):
        for k2 in range(n2):
            o_ref[n1 * k2:n1 * (k2 + 1), g * LANES:(g + 1) * LANES] = (
                xr_scr.at[g][pl.ds(k2, n1, stride=n2), :].astype(o_ref.dtype))


def _fourier(f, tables, dst, *, n_batch, seq):
    n2 = GRID_W
    n1 = seq // n2
    n_grp = FOURIER_LANES // LANES
    table_specs = [pl.BlockSpec(t.shape, (lambda b, g, nd=t.ndim: (0,) * nd)) for t in tables]
    u_specs = [pl.BlockSpec((seq, LANES), (lambda b, g, k=k: (b, g * n_grp + k))) for k in range(n_grp)]
    return pl.pallas_call(
        functools.partial(_fourier_kernel, n1=n1, n2=n2, n_grp=n_grp),
        grid=(n_batch, f.shape[1] // FOURIER_LANES),
        in_specs=u_specs + table_specs + [pl.BlockSpec(memory_space=pl.ANY)],
        out_specs=pl.BlockSpec((seq, FOURIER_LANES), lambda b, g: (b, g)),
        out_shape=jax.ShapeDtypeStruct(dst.shape, dst.dtype),
        input_output_aliases={n_grp + len(tables): 0},
        scratch_shapes=[pltpu.VMEM((n_grp, 2 * seq, LANES), F32), pltpu.VMEM((n_grp, seq, LANES), F32),
                        pltpu.VMEM((n_grp, seq, LANES), F32)],
        compiler_params=_cparams("parallel", "parallel"),
        name="fourier",
    )(*([f] * n_grp), *tables, dst)


def _dense_fourier_tables(n):
    norm = 1.0 / math.sqrt(n * LANES)
    p = np.arange(n)
    ph = 2.0 * np.pi * ((p[:, None] * p[None, :]) % n) / n
    ch = np.arange(LANES)
    pc = 2.0 * np.pi * ((ch[:, None] * ch[None, :]) % LANES) / LANES
    return (jnp.asarray(np.cos(ph).astype(np.float32)), jnp.asarray(np.sin(ph).astype(np.float32)),
            jnp.asarray((np.cos(pc) * norm).astype(np.float32)), jnp.asarray((np.sin(pc) * norm).astype(np.float32)))


def _dense_fourier_kernel(u_ref, cl_ref, sl_ref, cc_ref, sc_ref, _, o_ref):
    u = u_ref[...]
    y = _dot_hi(cl_ref[...], _dot_hi(u, cc_ref[...])) - _dot_hi(sl_ref[...], _dot_hi(u, sc_ref[...]))
    o_ref[...] = y.astype(o_ref.dtype)


def _dense_fourier(f, tables, dst, *, n_batch, n_pos, row_block0):
    cl, sl, cc, sc = tables
    groups = f.shape[1] // LANES
    return pl.pallas_call(
        _dense_fourier_kernel,
        grid=(n_batch, groups),
        in_specs=[
            pl.BlockSpec((n_pos, LANES), lambda b, g: (row_block0 + b, g)),
            pl.BlockSpec(cl.shape, lambda b, g: (0, 0)),
            pl.BlockSpec(sl.shape, lambda b, g: (0, 0)),
            pl.BlockSpec(cc.shape, lambda b, g: (0, 0)),
            pl.BlockSpec(sc.shape, lambda b, g: (0, 0)),
            pl.BlockSpec(memory_space=pl.ANY),
        ],
        out_specs=pl.BlockSpec((n_pos, LANES), lambda b, g: (row_block0 + b, g)),
        out_shape=jax.ShapeDtypeStruct(dst.shape, dst.dtype),
        input_output_aliases={5: 0},
        compiler_params=_cparams("parallel", "parallel"),
        name="fourier_ctx",
    )(f, cl, sl, cc, sc, dst)


def _win_kernel(sink_ref, q_ref, *refs, n_qtiles, has_local, n_heads, group_size):
    if has_local:
        kp_ref, kc_ref, kn_ref, vp_ref, vc_ref, vn_ref, kx_ref, vx_ref, _, o_ref = refs
    else:
        kx_ref, vx_ref, _, o_ref = refs
    tq = q_ref.shape[0]
    lane = lax.broadcasted_iota(jnp.int32, (1, LANES), 1)
    half_mask = [lane < HEAD_DIM, lane >= HEAD_DIM]
    pieces = [(kx_ref[...], vx_ref[...], None)]
    if has_local:
        n = pl.program_id(1)
        qi = lax.broadcasted_iota(jnp.int32, (tq, BLOCK), 0)
        kj = lax.broadcasted_iota(jnp.int32, (tq, BLOCK), 1)
        valid_prev = (kj >= qi) & (n >= 1)
        valid_next = (kj <= qi - (tq - WINDOW)) & (n <= n_qtiles - 2)
        di = lax.broadcasted_iota(jnp.int32, (tq, tq), 0) - lax.broadcasted_iota(jnp.int32, (tq, tq), 1)
        valid_mid = (di <= WINDOW) & (di >= -WINDOW)
        pieces += [(kp_ref[...], vp_ref[...], valid_prev), (kc_ref[...], vc_ref[...], valid_mid),
                   (kn_ref[...], vn_ref[...], valid_next)]
    for pair in range(n_heads // 2):
        qp = q_ref[:, pair * LANES:(pair + 1) * LANES].astype(F32)
        qp_sw = pltpu.roll(qp, HEAD_DIM, 1)
        out_pair = jnp.zeros((tq, LANES), F32)
        for half in range(2):
            head = 2 * pair + half
            kv = head // group_size
            src = qp if half == kv else qp_sw
            qe = jnp.where(half_mask[kv], src, 0.0).astype(BF16)
            sink = sink_ref[head] * LOG2E
            scores = []
            m = jnp.zeros((tq, 1), F32) + sink
            for k, _, valid in pieces:
                s = _dot_nt(qe, k)
                if valid is not None:
                    s = jnp.where(valid, s, NEG)
                scores.append(s)
                m = jnp.maximum(m, jnp.max(s, axis=-1, keepdims=True))
            den = jnp.exp2(sink - m)
            pv = jnp.zeros((tq, LANES), F32)
            for s, (_, v, _) in zip(scores, pieces):
                p = jnp.exp2(s - m)
                den = den + jnp.sum(p, axis=-1, keepdims=True)
                pv = pv + _dot(p.astype(BF16), v)
            pv = pv / den
            if half != kv:
                pv = pltpu.roll(pv, HEAD_DIM, 1)
            out_pair = jnp.where(half_mask[half], pv, out_pair)
        o_ref[:, pair * LANES:(pair + 1) * LANES] = out_pair.astype(o_ref.dtype)


def _win_attention(z, sink, dst, *, n_batch, seq, n_ctx, q_cols, k_col, v_col, tq):
    nbk = seq // BLOCK
    nq = seq // tq
    per = tq // BLOCK
    ctx_blk0 = n_batch * seq // n_ctx
    n_heads = q_cols // HEAD_DIM
    group_size = n_heads // (LANES // HEAD_DIM)

    def edge(col, first):
        return pl.BlockSpec(
            (BLOCK, LANES), lambda b, n: (b * nbk + jnp.clip(n * per + first, 0, nbk - 1), col))

    def mid(col):
        return pl.BlockSpec((tq, LANES), lambda b, n: (b * nq + n, col))

    return pl.pallas_call(
        functools.partial(_win_kernel, n_qtiles=nq, has_local=True, n_heads=n_heads, group_size=group_size),
        grid=(n_batch, nq),
        in_specs=[
            pl.BlockSpec(memory_space=pltpu.SMEM),
            pl.BlockSpec((tq, q_cols), lambda b, n: (b * nq + n, 0)),
            edge(k_col, -1), mid(k_col), edge(k_col, per),
            edge(v_col, -1), mid(v_col), edge(v_col, per),
            pl.BlockSpec((n_ctx, LANES), lambda b, n: (ctx_blk0 + b, k_col)),
            pl.BlockSpec((n_ctx, LANES), lambda b, n: (ctx_blk0 + b, v_col)),
            pl.BlockSpec(memory_space=pl.ANY),
        ],
        out_specs=pl.BlockSpec((tq, q_cols), lambda b, n: (b * nq + n, 0)),
        out_shape=jax.ShapeDtypeStruct(dst.shape, dst.dtype),
        input_output_aliases={10: 0},
        compiler_params=_cparams("parallel", "parallel"),
        name="win_attn",
    )(sink, z, z, z, z, z, z, z, z, z, dst)


def _ctx_gqa_attention(z, sink, dst, *, n_batch, seq, n_ctx, q_cols, k_col, v_col):
    ctx_blk0 = n_batch * seq // n_ctx
    n_heads = q_cols // HEAD_DIM
    group_size = n_heads // (LANES // HEAD_DIM)
    return pl.pallas_call(
        functools.partial(_win_kernel, n_qtiles=0, has_local=False, n_heads=n_heads, group_size=group_size),
        grid=(n_batch,),
        in_specs=[
            pl.BlockSpec(memory_space=pltpu.SMEM),
            pl.BlockSpec((n_ctx, q_cols), lambda b: (ctx_blk0 + b, 0)),
            pl.BlockSpec((n_ctx, LANES), lambda b: (ctx_blk0 + b, k_col)),
            pl.BlockSpec((n_ctx, LANES), lambda b: (ctx_blk0 + b, v_col)),
            pl.BlockSpec(memory_space=pl.ANY),
        ],
        out_specs=pl.BlockSpec((n_ctx, q_cols), lambda b: (ctx_blk0 + b, 0)),
        out_shape=jax.ShapeDtypeStruct(dst.shape, dst.dtype),
        input_output_aliases={4: 0},
        compiler_params=_cparams("parallel"),
        name="ctx_gqa",
    )(sink, z, z, z, dst)


LOG2E = math.log2(math.e)
DIFF_SUB_ROWS = 128


def _diff_kernel(lam_ref, g_ref, q_ref, *refs, lam_init, has_x):
    if has_x:
        kx_ref, vx_ref, kc_ref, vc_ref, _, o_ref, s_scr = refs
        nx = kx_ref.shape[0]
    else:
        kc_ref, vc_ref, _, o_ref, s_scr = refs
        nx = 0
    nc = kc_ref.shape[0]
    sub = s_scr.shape[1]
    n_sub = q_ref.shape[0] // sub
    lv = lam_ref[...]
    lam = (jnp.exp(jnp.sum(lv[0:1] * lv[1:2], axis=-1, keepdims=True))
           - jnp.exp(jnp.sum(lv[2:3] * lv[3:4], axis=-1, keepdims=True)) + lam_init)
    lane = lax.broadcasted_iota(jnp.int32, (1, LANES), 1)
    units = [(h, c) for h in range(n_sub) for c in range(2)]

    def scores(u):
        h, c = units[u]
        q = q_ref[h * sub:(h + 1) * sub, :]
        qm = jnp.where((lane < HEAD_DIM) if c == 0 else (lane >= HEAD_DIM), q, jnp.zeros_like(q))
        if has_x:
            s_scr[u % 2, :, 0:nx] = _dot_nt(qm, kx_ref[...])
        s_scr[u % 2, :, nx:nx + nc] = _dot_nt(qm, kc_ref[...])

    def attend(u):
        s = s_scr[u % 2]
        p = jnp.exp2(s - jnp.max(s, axis=-1, keepdims=True))
        den = jnp.sum(p, axis=-1, keepdims=True)
        pb = p.astype(BF16)
        acc = _dot(pb[:, nx:nx + nc], vc_ref[...])
        if has_x:
            acc = acc + _dot(pb[:, 0:nx], vx_ref[...])
        return acc / den

    res = []
    scores(0)
    for u in range(len(units)):
        if u + 1 < len(units):
            scores(u + 1)
        res.append(attend(u))
    for h in range(n_sub):
        o = res[2 * h] - lam * res[2 * h + 1]
        o = _rms(o) * g_ref[...] * (1.0 - lam_init)
        o_ref[h * sub:(h + 1) * sub, :] = o.astype(o_ref.dtype)


def _diff_attention(z, lam_vec, subln_g, lam_init, dst, *, n_batch, seq, n_ctx, n_heads, tq):
    ctx_blk0 = n_batch * seq // n_ctx
    nq = seq // tq
    return pl.pallas_call(
        functools.partial(_diff_kernel, lam_init=lam_init, has_x=True),
        grid=(n_batch, n_heads, nq),
        in_specs=[
            pl.BlockSpec(lam_vec.shape, lambda b, h, i: (0, 0)),
            pl.BlockSpec((1, LANES), lambda b, h, i: (0, 0)),
            pl.BlockSpec((tq, LANES), lambda b, h, i: (b * nq + i, h)),
            pl.BlockSpec((seq, LANES), lambda b, h, i: (b, n_heads + h)),
            pl.BlockSpec((seq, LANES), lambda b, h, i: (b, 2 * n_heads + h)),
            pl.BlockSpec((n_ctx, LANES), lambda b, h, i: (ctx_blk0 + b, n_heads + h)),
            pl.BlockSpec((n_ctx, LANES), lambda b, h, i: (ctx_blk0 + b, 2 * n_heads + h)),
            pl.BlockSpec(memory_space=pl.ANY),
        ],
        out_specs=pl.BlockSpec((tq, LANES), lambda b, h, i: (b * nq + i, h)),
        out_shape=jax.ShapeDtypeStruct(dst.shape, dst.dtype),
        input_output_aliases={7: 0},
        scratch_shapes=[pltpu.VMEM((2, min(tq, DIFF_SUB_ROWS), seq + n_ctx), F32)],
        compiler_params=_cparams("parallel", "parallel", "parallel"),
        name="diff_attn",
    )(lam_vec, subln_g, z, z, z, z, z, dst)


def _diff_attention_ctx(z, lam_vec, subln_g, lam_init, dst, *, n_batch, seq, n_ctx, n_heads):
    ctx_blk0 = n_batch * seq // n_ctx
    return pl.pallas_call(
        functools.partial(_diff_kernel, lam_init=lam_init, has_x=False),
        grid=(n_batch, n_heads),
        in_specs=[
            pl.BlockSpec(lam_vec.shape, lambda b, h: (0, 0)),
            pl.BlockSpec((1, LANES), lambda b, h: (0, 0)),
            pl.BlockSpec((n_ctx, LANES), lambda b, h: (ctx_blk0 + b, h)),
            pl.BlockSpec((n_ctx, LANES), lambda b, h: (ctx_blk0 + b, n_heads + h)),
            pl.BlockSpec((n_ctx, LANES), lambda b, h: (ctx_blk0 + b, 2 * n_heads + h)),
            pl.BlockSpec(memory_space=pl.ANY),
        ],
        out_specs=pl.BlockSpec((n_ctx, LANES), lambda b, h: (ctx_blk0 + b, h)),
        out_shape=jax.ShapeDtypeStruct(dst.shape, dst.dtype),
        input_output_aliases={5: 0},
        scratch_shapes=[pltpu.VMEM((2, min(n_ctx, DIFF_SUB_ROWS), n_ctx), F32)],
        compiler_params=_cparams("parallel", "parallel"),
        name="diff_attn_ctx",
    )(lam_vec, subln_g, z, z, z, dst)


def _outproj_kernel(a0_ref, a1_ref, w_ref, x_ref, g_ref, gate_ref, o_ref):
    half = a0_ref.shape[1]
    y = _dot(a0_ref[...], w_ref[:half, :]) + _dot(a1_ref[...], w_ref[half:, :])
    o_ref[...] = x_ref[...] + gate_ref[...] * (_rms(y) * g_ref[...])


def _outproj(a0, a0_col, a1, a1_col, w, xt, normg, mod, k_gate, *, tm, rows, seq, n_batch):
    d = xt.shape[1]
    half = d // 2

    def mrow(i):
        return jnp.minimum(i * tm // seq, n_batch)

    return pl.pallas_call(
        _outproj_kernel,
        grid=(rows // tm,),
        in_specs=[
            pl.BlockSpec((tm, half), lambda i: (i, a0_col)),
            pl.BlockSpec((tm, half), lambda i: (i, a1_col)),
            pl.BlockSpec((d, d), lambda i: (0, 0)),
            pl.BlockSpec((tm, d), lambda i: (i, 0)),
            pl.BlockSpec((None, 1, d), lambda i: (1, 0, 0)),
            pl.BlockSpec((None, None, 1, d), lambda i: (mrow(i), k_gate, 0, 0)),
        ],
        out_specs=pl.BlockSpec((tm, d), lambda i: (i, 0)),
        out_shape=jax.ShapeDtypeStruct((rows, d), F32),
        compiler_params=_cparams("parallel"),
        name="outproj",
    )(a0, a1, w, xt, normg, mod)


def _router_kernel(x_ref, g_ref, sh_ref, sc_ref, rw_ref, o_ref, sel_ref):
    h = _modulate(x_ref[...], g_ref[...], sh_ref[...], sc_ref[...])
    logits = _dot_split(_split_bf16(h), _split_bf16(rw_ref[...]))
    lane = lax.broadcasted_iota(jnp.int32, logits.shape, 1)
    ninf = -jnp.inf
    logits = jnp.where(lane < N_EXPERTS, logits, ninf)
    m1 = jnp.max(logits, axis=-1, keepdims=True)
    i1 = jnp.min(jnp.where(logits == m1, lane, LANES), axis=-1, keepdims=True)
    sel1 = lane == i1
    rest = jnp.where(sel1, ninf, logits)
    m2 = jnp.max(rest, axis=-1, keepdims=True)
    i2 = jnp.min(jnp.where(rest == m2, lane, LANES), axis=-1, keepdims=True)
    sel2 = lane == i2
    e2 = jnp.exp(m2 - m1)
    den = 1.0 + e2
    o_ref[...] = jnp.where(sel1, 1.0 / den, 0.0) + jnp.where(sel2, e2 / den, 0.0)
    sel_ref[...] = jnp.where(sel1 | sel2, 1.0, 0.0)


def _router(xt, normg, mod, router_w, *, tm, rows, seq, n_batch):
    d = xt.shape[1]
    rw = jnp.zeros((d, LANES), F32).at[:, :N_EXPERTS].set(router_w)

    def mrow(i):
        return jnp.minimum(i * tm // seq, n_batch)

    return pl.pallas_call(
        _router_kernel,
        grid=(rows // tm,),
        in_specs=[
            pl.BlockSpec((tm, d), lambda i: (i, 0)),
            pl.BlockSpec((None, 1, d), lambda i: (2, 0, 0)),
            pl.BlockSpec((None, None, 1, d), lambda i: (mrow(i), 3, 0, 0)),
            pl.BlockSpec((None, None, 1, d), lambda i: (mrow(i), 4, 0, 0)),
            pl.BlockSpec((d, LANES), lambda i: (0, 0)),
        ],
        out_specs=[pl.BlockSpec((tm, LANES), lambda i: (i, 0))] * 2,
        out_shape=[jax.ShapeDtypeStruct((rows, LANES), F32)] * 2,
        compiler_params=_cparams("parallel"),
        name="router",
    )(xt, normg, mod, mod, rw)


SWIGLU_CHUNK = 256


def _swiglu(h, wg_ref, wu_ref, wo_ref, u_off):
    width = wo_ref.shape[0]
    acc = None
    for c0 in range(0, width, SWIGLU_CHUNK):
        gp = _dot(h, wg_ref[:, c0:c0 + SWIGLU_CHUNK])
        up = _dot(h, wu_ref[:, u_off + c0:u_off + c0 + SWIGLU_CHUNK])
        a = gp * (1.0 / (1.0 + jnp.exp(-gp))) * up
        part = _dot(a.astype(BF16), wo_ref[c0:c0 + SWIGLU_CHUNK, :])
        acc = part if acc is None else acc + part
    return acc


def _ffn_kernel(x_ref, g2_ref, sh_ref, sc_ref, win_ref, wout_ref, g3_ref, gate_ref, o_ref):
    x = x_ref[...]
    h = _modulate(x, g2_ref[...], sh_ref[...], sc_ref[...]).astype(BF16)
    y = _swiglu(h, win_ref, win_ref, wout_ref, wout_ref.shape[0])
    o_ref[...] = x + gate_ref[...] * (_rms(y) * g3_ref[...])


def _ffn(xt, normg, mod, w_in, w_out, *, tm, rows, seq, n_batch):
    d = xt.shape[1]

    def mrow(i):
        return jnp.minimum(i * tm // seq, n_batch)

    in_specs = [
        pl.BlockSpec((tm, d), lambda i: (i, 0)),
        pl.BlockSpec((None, 1, d), lambda i: (2, 0, 0)),
        pl.BlockSpec((None, None, 1, d), lambda i: (mrow(i), 3, 0, 0)),
        pl.BlockSpec((None, None, 1, d), lambda i: (mrow(i), 4, 0, 0)),
        pl.BlockSpec(w_in.shape, lambda i: (0, 0)),
        pl.BlockSpec(w_out.shape, lambda i: (0, 0)),
        pl.BlockSpec((None, 1, d), lambda i: (3, 0, 0)),
        pl.BlockSpec((None, None, 1, d), lambda i: (mrow(i), 5, 0, 0)),
    ]
    return pl.pallas_call(
        _ffn_kernel,
        grid=(rows // tm,),
        in_specs=in_specs,
        out_specs=pl.BlockSpec((tm, d), lambda i: (i, 0)),
        out_shape=jax.ShapeDtypeStruct((rows, d), F32),
        compiler_params=_cparams("parallel"),
        name="ffn",
    )(xt, normg, mod, mod, w_in, w_out, normg, mod)


MOE_CHUNK = 128


def _moe_kernel(nchunk_ref, x_ref, g2_ref, sh_ref, sc_ref, wg_ref, wu_ref, wo_ref, g3_ref, gate_ref,
                comb_ref, sel_ref, o_ref, h_scr, key_scr, keyt_scr, hs_scr, y_scr, acc_scr, *, n_exp, n_f):
    b = pl.program_id(0)
    e = pl.program_id(1)
    f = pl.program_id(2)
    nb = x_ref.shape[0]
    ch = MOE_CHUNK
    n_chunks = nchunk_ref[b * n_exp + e]

    @pl.when((e == 0) & (f == 0))
    def _():
        h_scr[...] = _modulate(x_ref[...], g2_ref[...], sh_ref[...], sc_ref[...]).astype(BF16)
        o_ref[...] = jnp.zeros_like(o_ref)
        ti = lax.broadcasted_iota(jnp.int32, (nb, nb), 0)
        tj = lax.broadcasted_iota(jnp.int32, (nb, nb), 1)
        lower = jnp.where(tj < ti, 1.0, 0.0).astype(BF16)
        sel = sel_ref[...]
        rank = _dot(lower, sel.astype(BF16))
        key = jnp.where(sel > 0.0, rank, -1.0)
        key_scr[...] = key
        keyt_scr[...] = key.T

    def for_rows(body):
        def pair(i, carry):
            body(pl.multiple_of(i * (2 * ch), 2 * ch), 2 * ch)
            return carry

        lax.fori_loop(0, n_chunks // 2, pair, 0)

        @pl.when(n_chunks % 2 == 1)
        def _():
            body(pl.multiple_of((n_chunks - 1) * ch, ch), ch)

    @pl.when(f == 0)
    def _():
        key_row = keyt_scr[pl.ds(e, 1), :]

        def gather(r0, n):
            r = (r0 + lax.broadcasted_iota(jnp.int32, (n, 1), 0)).astype(F32)
            onehot = jnp.where(key_row == r, 1.0, 0.0).astype(BF16)
            hs_scr[pl.ds(r0, n), :] = _dot(onehot, h_scr[...]).astype(BF16)

        for_rows(gather)

    def expert(r0, n):
        rows = pl.ds(r0, n)
        part = _swiglu(hs_scr[rows, :], wg_ref, wu_ref, wo_ref, 0)

        @pl.when(f == 0)
        def _():
            y_scr[rows, :] = part

        @pl.when(f > 0)
        def _():
            y_scr[rows, :] += part

    for_rows(expert)

    @pl.when(f == n_f - 1)
    def _():
        lane = lax.broadcasted_iota(jnp.int32, (nb, LANES), 1)
        pick = lane == e
        key_col = jnp.sum(jnp.where(pick, key_scr[...], 0.0), axis=-1, keepdims=True)
        gate_col = jnp.sum(jnp.where(pick, comb_ref[...], 0.0), axis=-1, keepdims=True)
        acc_scr[...] = jnp.zeros_like(acc_scr)

        def scatter(r0, n):
            r = (r0 + lax.broadcasted_iota(jnp.int32, (1, n), 1)).astype(F32)
            onehot = jnp.where(key_col == r, 1.0, 0.0).astype(BF16)
            acc_scr[...] += _dot(onehot, y_scr[pl.ds(r0, n), :].astype(BF16))

        for_rows(scatter)
        o_ref[...] += gate_col * acc_scr[...]

    @pl.when((e == n_exp - 1) & (f == n_f - 1))
    def _():
        o_ref[...] = x_ref[...] + gate_ref[...] * (_rms(o_ref[...]) * g3_ref[...])


def _moe(xt, normg, mod, w_in, w_out, comb, sel, *, nb, tf, rows, seq, n_batch):
    d = xt.shape[1]
    n_exp, fdim = w_out.shape[0], w_out.shape[1]
    n_f = fdim // tf
    n_blk = rows // nb
    counts = jnp.sum(sel.reshape(n_blk, nb, LANES)[:, :, :n_exp], axis=1).astype(jnp.int32)
    nchunk = ((counts + (MOE_CHUNK - 1)) // MOE_CHUNK).reshape(n_blk * n_exp)

    def mrow(i):
        return jnp.minimum(i * nb // seq, n_batch)

    in_specs = [
        pl.BlockSpec((nb, d), lambda i, e, f, s: (i, 0)),
        pl.BlockSpec((None, 1, d), lambda i, e, f, s: (2, 0, 0)),
        pl.BlockSpec((None, None, 1, d), lambda i, e, f, s: (mrow(i), 3, 0, 0)),
        pl.BlockSpec((None, None, 1, d), lambda i, e, f, s: (mrow(i), 4, 0, 0)),
        pl.BlockSpec((None, d, tf), lambda i, e, f, s: (e, 0, f)),
        pl.BlockSpec((None, d, tf), lambda i, e, f, s: (e, 0, n_f + f)),
        pl.BlockSpec((None, tf, d), lambda i, e, f, s: (e, f, 0)),
        pl.BlockSpec((None, 1, d), lambda i, e, f, s: (3, 0, 0)),
        pl.BlockSpec((None, None, 1, d), lambda i, e, f, s: (mrow(i), 5, 0, 0)),
        pl.BlockSpec((nb, LANES), lambda i, e, f, s: (i, 0)),
        pl.BlockSpec((nb, LANES), lambda i, e, f, s: (i, 0)),
    ]
    return pl.pallas_call(
        functools.partial(_moe_kernel, n_exp=n_exp, n_f=n_f),
        grid_spec=pltpu.PrefetchScalarGridSpec(
            num_scalar_prefetch=1,
            grid=(n_blk, n_exp, n_f),
            in_specs=in_specs,
            out_specs=pl.BlockSpec((nb, d), lambda i, e, f, s: (i, 0)),
            scratch_shapes=[
                pltpu.VMEM((nb, d), BF16),
                pltpu.VMEM((nb, LANES), F32),
                pltpu.VMEM((LANES, nb), F32),
                pltpu.VMEM((nb, d), BF16),
                pltpu.VMEM((nb, d), F32),
                pltpu.VMEM((nb, d), F32),
            ],
        ),
        out_shape=jax.ShapeDtypeStruct((rows, d), F32),
        compiler_params=_cparams("parallel", "arbitrary", "arbitrary"),
        name="moe",
    )(nchunk, xt, normg, mod, mod, w_in, w_in, w_out, normg, mod, comb, sel)


def _lambda_init(layer):
    return 0.8 - 0.6 * math.exp(-0.3 * layer)


def kernel(x, c, ctx, c_ctx, ada_w, ada_b, norm_g, mix_in_w, mix_out_w, win_sink, diff_qkv_w, diff_out_w,
           diff_lambda, diff_subln_g, ffn_in_w, ffn_out_w, router_w, expert_in_w, expert_out_w):
    n_batch, seq, d = x.shape
    n_ctx = ctx.shape[1]
    depth = ada_w.shape[0]
    n_lat = n_batch * seq
    n_all = n_lat + n_batch * n_ctx
    fdim = mix_in_w.shape[2] - (d // 2 + 2 * LANES)
    q_cols = d // 2
    n_diff_heads = d // LANES

    tm_proj = _pick_tile(512, seq, n_batch * n_ctx)
    tm_out = _pick_tile(512, seq, n_batch * n_ctx)
    tm_ffn = _pick_tile(512, seq, n_batch * n_ctx)
    nb_moe = _pick_tile(1024, seq, n_batch * n_ctx)
    tq_diff = _pick_tile(512, seq)
    tq_win = _pick_tile(512, seq)
    common = dict(seq=seq, n_batch=n_batch)

    xt = jnp.concatenate([x.reshape(n_lat, d), ctx.reshape(n_batch * n_ctx, d)], axis=0)
    n_mod = -(-(n_batch + 1) // 8) * 8
    cv = jnp.zeros((n_mod, d), F32).at[:n_batch].set(c).at[n_batch].set(c_ctx)
    mods = _modvec(cv, ada_w, ada_b).reshape(depth, n_mod, 6, 1, d)
    rope = _rope_tables(seq, tm_proj)
    f_tables = _fourier_tables(seq)
    fc_tables = _dense_fourier_tables(n_ctx)

    n_f = fdim // LANES
    plan_even = ([(0, i * LANES, "plain") for i in range(n_f)]
                 + [(1, i * LANES, "rope_q_log2") for i in range(q_cols // LANES)]
                 + [(1, q_cols, "rope_k"), (1, q_cols + LANES, "plain")])
    plan_odd = ([(0, i * LANES, "rope_q_log2") for i in range(n_diff_heads)]
                + [(0, d + i * LANES, "rope_k") for i in range(n_diff_heads)]
                + [(0, 2 * d + i * LANES, "plain") for i in range(n_diff_heads)])

    for layer in range(depth):
        j = layer // 2
        need_ctx = layer < depth - 1
        rows = n_all if need_ctx else n_lat
        mod = mods[layer]
        ng = norm_g[layer].reshape(4, 1, d)
        if layer % 2 == 0:
            f, z = _proj(xt, ng, mod, 0, 1, mix_in_w[j].astype(BF16), rope, plan_even,
                         [(fdim, F32), (q_cols + 2 * LANES, BF16)], tm=tm_proj, n_lat=n_lat, **common)
            k_col, v_col = q_cols // LANES, q_cols // LANES + 1
            att = dict(n_batch=n_batch, seq=seq, n_ctx=n_ctx, q_cols=q_cols, k_col=k_col, v_col=v_col)
            mix_f = _fourier(f, f_tables, jnp.zeros((rows, fdim), BF16), n_batch=n_batch, seq=seq)
            mix_a = _win_attention(z, win_sink[j], jnp.zeros((rows, q_cols), BF16), tq=tq_win, **att)
            if need_ctx:
                mix_f = _dense_fourier(f, fc_tables, mix_f, n_batch=n_batch, n_pos=n_ctx,
                                       row_block0=n_lat // n_ctx)
                mix_a = _ctx_gqa_attention(z, win_sink[j], mix_a, **att)
            xt = _outproj(mix_f, 0, mix_a, 0, mix_out_w[j].astype(BF16), xt, ng, mod, 2, tm=tm_out, rows=rows,
                          **common)
            xt = _ffn(xt, ng, mod, ffn_in_w[j].astype(BF16), ffn_out_w[j].astype(BF16),
                      tm=tm_ffn, rows=rows, **common)
        else:
            lam_init = _lambda_init(layer)
            (z,) = _proj(xt, ng, mod, 0, 1, diff_qkv_w[j].astype(BF16), rope, plan_odd, [(3 * d, BF16)],
                         tm=tm_proj, n_lat=n_lat, **common)
            subg = diff_subln_g[j].reshape(1, LANES)
            att = dict(n_batch=n_batch, seq=seq, n_ctx=n_ctx, n_heads=n_diff_heads)
            mix = _diff_attention(z, diff_lambda[j], subg, lam_init, jnp.zeros((rows, d), BF16), tq=tq_diff, **att)
            if need_ctx:
                mix = _diff_attention_ctx(z, diff_lambda[j], subg, lam_init, mix, **att)
            xt = _outproj(mix, 0, mix, 1, diff_out_w[j].astype(BF16), xt, ng, mod, 2, tm=tm_out, rows=rows,
                          **common)
            comb, sel = _router(xt, ng, mod, router_w[j], tm=tm_out, rows=rows, **common)
            xt = _moe(xt, ng, mod, expert_in_w[j].astype(BF16), expert_out_w[j].astype(BF16), comb, sel,
                      nb=nb_moe, tf=1792, rows=rows, **common)
    return xt[:n_lat].reshape(n_batch, seq, d)
```

```python
import functools
import math

import numpy as np
import jax
import jax.numpy as jnp
from jax import lax
from jax.experimental import pallas as pl
from jax.experimental.pallas import tpu as pltpu

EPS = 1e-6
NEG = -1e30
HEAD_DIM = 64
LANES = 128
GRID_W = 64
BLOCK = 128
WINDOW = 128
ROPE_THETA = 10000.0
N_EXPERTS = 8
F32 = jnp.float32
BF16 = jnp.bfloat16
HIGHEST = lax.Precision.HIGHEST
VMEM_LIMIT = 56 * 1024 * 1024


def _cparams(*sem):
    return pltpu.CompilerParams(dimension_semantics=sem, vmem_limit_bytes=VMEM_LIMIT)


def _dot(a, b):
    return jnp.dot(a, b, preferred_element_type=F32)


def _dot_nt(a, b):
    return lax.dot_general(a, b, (((1,), (1,)), ((), ())), preferred_element_type=F32)


def _dot_hi(a, b):
    return jnp.dot(a, b, precision=HIGHEST, preferred_element_type=F32)


def _rms(v):
    return v * lax.rsqrt(jnp.mean(v * v, axis=-1, keepdims=True) + EPS)


def _modulate(x, g, sh, sc):
    return _rms(x) * g * (1.0 + sc) + sh


def _pick_tile(pref, *dims):
    t = pref
    while any(d % t for d in dims):
        t //= 2
    return t


def _modvec_kernel(c_ref, w_ref, b_ref, o_ref):
    cv = c_ref[...]
    s = cv * (1.0 / (1.0 + jnp.exp(-cv)))
    o_ref[...] = _dot(s.astype(BF16), w_ref[...].astype(BF16)) + b_ref[...]


def _modvec(cv, ada_w, ada_b):
    depth, d, n = ada_w.shape
    r = cv.shape[0]
    tn = _pick_tile(1536, n)
    return pl.pallas_call(
        _modvec_kernel,
        grid=(depth, n // tn),
        in_specs=[
            pl.BlockSpec((r, d), lambda l, j: (0, 0)),
            pl.BlockSpec((None, d, tn), lambda l, j: (l, 0, j)),
            pl.BlockSpec((None, 1, tn), lambda l, j: (l, 0, j)),
        ],
        out_specs=pl.BlockSpec((None, r, tn), lambda l, j: (l, 0, j)),
        out_shape=jax.ShapeDtypeStruct((depth, r, n), F32),
        compiler_params=_cparams("parallel", "parallel"),
        name="modvec",
    )(cv, ada_w, ada_b.reshape(depth, 1, n))


def _proj_kernel(x_ref, g_ref, sh_ref, sc_ref, w_ref, cos_ref, sa_ref, sb_ref, *o_refs, plan, group):
    h = _modulate(x_ref[...], g_ref[...], sh_ref[...], sc_ref[...]).astype(BF16)
    n = w_ref.shape[1]
    for g0 in range(0, n, group):
        acc = _dot(h, w_ref[:, g0:g0 + group])
        for c0 in range(0, group, LANES):
            oi, oc, mode = plan[(g0 + c0) // LANES]
            v = acc[:, c0:c0 + LANES]
            if mode != "plain":
                v = (v * cos_ref[...] + pltpu.roll(v, LANES - 16, 1) * sa_ref[...]
                     + pltpu.roll(v, 16, 1) * sb_ref[...])
                if mode == "rope_q_log2":
                    v = v * (HEAD_DIM ** -0.5 * LOG2E)
            o_refs[oi][:, oc:oc + LANES] = v.astype(o_refs[oi].dtype)


def _proj(xt, normg, mod, k_sh, k_sc, w, rope, plan, outs, *, tm, n_lat, seq, n_batch):
    t, d = xt.shape
    n = w.shape[1]
    group = _pick_tile(512, n)
    nx = n_lat // tm
    per = seq // tm

    def mrow(i):
        return jnp.minimum(i * tm // seq, n_batch)

    def rrow(i):
        return jnp.where(i < nx, i % per, per)

    in_specs = [
        pl.BlockSpec((tm, d), lambda i: (i, 0)),
        pl.BlockSpec((None, 1, d), lambda i: (0, 0, 0)),
        pl.BlockSpec((None, None, 1, d), lambda i: (mrow(i), k_sh, 0, 0)),
        pl.BlockSpec((None, None, 1, d), lambda i: (mrow(i), k_sc, 0, 0)),
        pl.BlockSpec((d, n), lambda i: (0, 0)),
        pl.BlockSpec((tm, LANES), lambda i: (rrow(i), 0)),
        pl.BlockSpec((tm, LANES), lambda i: (rrow(i), 0)),
        pl.BlockSpec((tm, LANES), lambda i: (rrow(i), 0)),
    ]
    out_specs = [pl.BlockSpec((tm, wd), lambda i: (i, 0)) for wd, _ in outs]
    out_shape = [jax.ShapeDtypeStruct((t, wd), dt) for wd, dt in outs]
    return pl.pallas_call(
        functools.partial(_proj_kernel, plan=plan, group=group),
        grid=(t // tm,),
        in_specs=in_specs,
        out_specs=out_specs,
        out_shape=out_shape,
        compiler_params=_cparams("parallel"),
        name="proj",
    )(xt, normg, mod, mod, w, *rope)


def _rope_tables(seq, tm):
    rows_count = seq // GRID_W
    rows = jnp.repeat(jnp.arange(rows_count), GRID_W).astype(F32)
    cols = jnp.tile(jnp.arange(GRID_W), rows_count).astype(F32)
    axis_dim = HEAD_DIM // 2
    inv = ROPE_THETA ** (-jnp.arange(0, axis_dim, 2, dtype=F32) / axis_dim)
    ar = rows[:, None] * inv
    ac = cols[:, None] * inv
    cr, sr, cc, sc = jnp.cos(ar), jnp.sin(ar), jnp.cos(ac), jnp.sin(ac)
    z = jnp.zeros_like(sr)
    reps = LANES // HEAD_DIM
    cos = jnp.tile(jnp.concatenate([cr, cr, cc, cc], axis=1), (1, reps))
    sa = jnp.tile(jnp.concatenate([-sr, z, -sc, z], axis=1), (1, reps))
    sb = jnp.tile(jnp.concatenate([z, sr, z, sc], axis=1), (1, reps))
    ident = jnp.ones((tm, LANES), F32)
    zero = jnp.zeros((tm, LANES), F32)
    return (jnp.concatenate([cos, ident]), jnp.concatenate([sa, zero]), jnp.concatenate([sb, zero]))


FOURIER_LANES = 2 * LANES


def _split_bf16(t):
    hi = t.astype(BF16)
    return hi, (t - hi.astype(F32)).astype(BF16)


def _dot_split(a, b):
    return _dot(a[0], b[0]) + _dot(a[0], b[1]) + _dot(a[1], b[0])


def _fourier_tables(seq):
    n2 = GRID_W
    n1 = seq // n2
    norm = 1.0 / math.sqrt(seq * LANES)
    a = np.arange(n1)
    k1 = np.arange(n1)
    b = np.arange(n2)
    ang = (b[:, None, None] * k1[None, :, None] + (seq // n1) * k1[None, :, None] * a[None, None, :]) % seq
    th = 2.0 * np.pi * ang / seq
    m1 = np.concatenate([np.cos(th), -np.sin(th)], axis=1).astype(np.float32)
    ph = 2.0 * np.pi * ((b[:, None] * b[None, :]) % n2) / n2
    c2, s2 = np.cos(ph), np.sin(ph)
    g2 = np.block([[c2, s2], [-s2, c2]]).astype(np.float32)
    ch = np.arange(LANES)
    pc = 2.0 * np.pi * ((ch[:, None] * ch[None, :]) % LANES) / LANES
    cc = (np.cos(pc) * norm).astype(np.float32)
    sc = (np.sin(pc) * norm).astype(np.float32)
    cs = np.concatenate([cc, sc], axis=0)
    out = []
    for t in (m1, g2, cs):
        out.extend(_split_bf16(jnp.asarray(t)))
    return tuple(out)


FOURIER_ROW_CHUNK = 512


def _fourier_kernel(*refs, n1, n2, n_grp):
    u_refs = refs[:n_grp]
    m1h_ref, m1l_ref, g2h_ref, g2l_ref, csh_ref, csl_ref, _, o_ref, b_scr, xr_scr, xi_scr = refs[n_grp:]
    seq = n1 * n2
    for b in range(n2):
        xs = jnp.concatenate([u[pl.ds(b, n1, stride=n2), :] for u in u_refs], axis=1)
        z = _dot_split((m1h_ref[b], m1l_ref[b]), _split_bf16(xs))
        for g in range(n_grp):
            b_scr[g, 2 * n1 * b:2 * n1 * (b + 1), :] = z[:, g * LANES:(g + 1) * LANES]
    g2 = (g2h_ref[...], g2l_ref[...])
    for k1 in range(n1):
        bk = jnp.concatenate(
            [jnp.concatenate([b_scr.at[g][pl.ds(k1, n2, stride=2 * n1), :],
                              b_scr.at[g][pl.ds(n1 + k1, n2, stride=2 * n1), :]], axis=0)
             for g in range(n_grp)], axis=1)
        xk = _dot_split(g2, _split_bf16(bk))
        for g in range(n_grp):
            xr_scr[g, n2 * k1:n2 * (k1 + 1), :] = xk[:n2, g * LANES:(g + 1) * LANES]
            xi_scr[g, n2 * k1:n2 * (k1 + 1), :] = xk[n2:, g * LANES:(g + 1) * LANES]
    cs = (csh_ref[...], csl_ref[...])
    rc = min(FOURIER_ROW_CHUNK, seq)
    for g in range(n_grp):
        for r0 in range(0, seq, rc):
            x = jnp.concatenate([xr_scr[g, r0:r0 + rc, :], xi_scr[g, r0:r0 + rc, :]], axis=1)
            xr_scr[g, r0:r0 + rc, :] = _dot_split(_split_bf16(x), cs)
    for g in range(n_grp):
        for k2 in range(n2):
            o_ref[n1 * k2:n1 * (k2 + 1), g * LANES:(g + 1) * LANES] = (
                xr_scr.at[g][pl.ds(k2, n1, stride=n2), :].astype(o_ref.dtype))


def _fourier(f, tables, dst, *, n_batch, seq):
    n2 = GRID_W
    n1 = seq // n2
    n_grp = FOURIER_LANES // LANES
    table_specs = [pl.BlockSpec(t.shape, (lambda b, g, nd=t.ndim: (0,) * nd)) for t in tables]
    u_specs = [pl.BlockSpec((seq, LANES), (lambda b, g, k=k: (b, g * n_grp + k))) for k in range(n_grp)]
    return pl.pallas_call(
        functools.partial(_fourier_kernel, n1=n1, n2=n2, n_grp=n_grp),
        grid=(n_batch, f.shape[1] // FOURIER_LANES),
        in_specs=u_specs + table_specs + [pl.BlockSpec(memory_space=pl.ANY)],
        out_specs=pl.BlockSpec((seq, FOURIER_LANES), lambda b, g: (b, g)),
        out_shape=jax.ShapeDtypeStruct(dst.shape, dst.dtype),
        input_output_aliases={n_grp + len(tables): 0},
        scratch_shapes=[pltpu.VMEM((n_grp, 2 * seq, LANES), F32), pltpu.VMEM((n_grp, seq, LANES), F32),
                        pltpu.VMEM((n_grp, seq, LANES), F32)],
        compiler_params=_cparams("parallel", "parallel"),
        name="fourier",
    )(*([f] * n_grp), *tables, dst)


def _dense_fourier_tables(n):
    norm = 1.0 / math.sqrt(n * LANES)
    p = np.arange(n)
    ph = 2.0 * np.pi * ((p[:, None] * p[None, :]) % n) / n
    ch = np.arange(LANES)
    pc = 2.0 * np.pi * ((ch[:, None] * ch[None, :]) % LANES) / LANES
    return (jnp.asarray(np.cos(ph).astype(np.float32)), jnp.asarray(np.sin(ph).astype(np.float32)),
            jnp.asarray((np.cos(pc) * norm).astype(np.float32)), jnp.asarray((np.sin(pc) * norm).astype(np.float32)))


def _dense_fourier_kernel(u_ref, cl_ref, sl_ref, cc_ref, sc_ref, _, o_ref):
    u = u_ref[...]
    y = _dot_hi(cl_ref[...], _dot_hi(u, cc_ref[...])) - _dot_hi(sl_ref[...], _dot_hi(u, sc_ref[...]))
    o_ref[...] = y.astype(o_ref.dtype)


def _dense_fourier(f, tables, dst, *, n_batch, n_pos, row_block0):
    cl, sl, cc, sc = tables
    groups = f.shape[1] // LANES
    return pl.pallas_call(
        _dense_fourier_kernel,
        grid=(n_batch, groups),
        in_specs=[
            pl.BlockSpec((n_pos, LANES), lambda b, g: (row_block0 + b, g)),
            pl.BlockSpec(cl.shape, lambda b, g: (0, 0)),
            pl.BlockSpec(sl.shape, lambda b, g: (0, 0)),
            pl.BlockSpec(cc.shape, lambda b, g: (0, 0)),
            pl.BlockSpec(sc.shape, lambda b, g: (0, 0)),
            pl.BlockSpec(memory_space=pl.ANY),
        ],
        out_specs=pl.BlockSpec((n_pos, LANES), lambda b, g: (row_block0 + b, g)),
        out_shape=jax.ShapeDtypeStruct(dst.shape, dst.dtype),
        input_output_aliases={5: 0},
        compiler_params=_cparams("parallel", "parallel"),
        name="fourier_ctx",
    )(f, cl, sl, cc, sc, dst)


def _win_kernel(sink_ref, q_ref, *refs, n_qtiles, has_local, n_heads, group_size):
    if has_local:
        kp_ref, kc_ref, kn_ref, vp_ref, vc_ref, vn_ref, kx_ref, vx_ref, _, o_ref = refs
    else:
        kx_ref, vx_ref, _, o_ref = refs
    tq = q_ref.shape[0]
    lane = lax.broadcasted_iota(jnp.int32, (1, LANES), 1)
    half_mask = [lane < HEAD_DIM, lane >= HEAD_DIM]
    pieces = [(kx_ref[...], vx_ref[...], None)]
    if has_local:
        n = pl.program_id(1)
        qi = lax.broadcasted_iota(jnp.int32, (tq, BLOCK), 0)
        kj = lax.broadcasted_iota(jnp.int32, (tq, BLOCK), 1)
        valid_prev = (kj >= qi) & (n >= 1)
        valid_next = (kj <= qi - (tq - WINDOW)) & (n <= n_qtiles - 2)
        di = lax.broadcasted_iota(jnp.int32, (tq, tq), 0) - lax.broadcasted_iota(jnp.int32, (tq, tq), 1)
        valid_mid = (di <= WINDOW) & (di >= -WINDOW)
        pieces += [(kp_ref[...], vp_ref[...], valid_prev), (kc_ref[...], vc_ref[...], valid_mid),
                   (kn_ref[...], vn_ref[...], valid_next)]
    for pair in range(n_heads // 2):
        qp = q_ref[:, pair * LANES:(pair + 1) * LANES].astype(F32)
        qp_sw = pltpu.roll(qp, HEAD_DIM, 1)
        out_pair = jnp.zeros((tq, LANES), F32)
        for half in range(2):
            head = 2 * pair + half
            kv = head // group_size
            src = qp if half == kv else qp_sw
            qe = jnp.where(half_mask[kv], src, 0.0).astype(BF16)
            sink = sink_ref[head] * LOG2E
            scores = []
            m = jnp.zeros((tq, 1), F32) + sink
            for k, _, valid in pieces:
                s = _dot_nt(qe, k)
                if valid is not None:
                    s = jnp.where(valid, s, NEG)
                scores.append(s)
                m = jnp.maximum(m, jnp.max(s, axis=-1, keepdims=True))
            den = jnp.exp2(sink - m)
            pv = jnp.zeros((tq, LANES), F32)
            for s, (_, v, _) in zip(scores, pieces):
                p = jnp.exp2(s - m)
                den = den + jnp.sum(p, axis=-1, keepdims=True)
                pv = pv + _dot(p.astype(BF16), v)
            pv = pv / den
            if half != kv:
                pv = pltpu.roll(pv, HEAD_DIM, 1)
            out_pair = jnp.where(half_mask[half], pv, out_pair)
        o_ref[:, pair * LANES:(pair + 1) * LANES] = out_pair.astype(o_ref.dtype)


def _win_attention(z, sink, dst, *, n_batch, seq, n_ctx, q_cols, k_col, v_col, tq):
    nbk = seq // BLOCK
    nq = seq // tq
    per = tq // BLOCK
    ctx_blk0 = n_batch * seq // n_ctx
    n_heads = q_cols // HEAD_DIM
    group_size = n_heads // (LANES // HEAD_DIM)

    def edge(col, first):
        return pl.BlockSpec(
            (BLOCK, LANES), lambda b, n: (b * nbk + jnp.clip(n * per + first, 0, nbk - 1), col))

    def mid(col):
        return pl.BlockSpec((tq, LANES), lambda b, n: (b * nq + n, col))

    return pl.pallas_call(
        functools.partial(_win_kernel, n_qtiles=nq, has_local=True, n_heads=n_heads, group_size=group_size),
        grid=(n_batch, nq),
        in_specs=[
            pl.BlockSpec(memory_space=pltpu.SMEM),
            pl.BlockSpec((tq, q_cols), lambda b, n: (b * nq + n, 0)),
            edge(k_col, -1), mid(k_col), edge(k_col, per),
            edge(v_col, -1), mid(v_col), edge(v_col, per),
            pl.BlockSpec((n_ctx, LANES), lambda b, n: (ctx_blk0 + b, k_col)),
            pl.BlockSpec((n_ctx, LANES), lambda b, n: (ctx_blk0 + b, v_col)),
            pl.BlockSpec(memory_space=pl.ANY),
        ],
        out_specs=pl.BlockSpec((tq, q_cols), lambda b, n: (b * nq + n, 0)),
        out_shape=jax.ShapeDtypeStruct(dst.shape, dst.dtype),
        input_output_aliases={10: 0},
        compiler_params=_cparams("parallel", "parallel"),
        name="win_attn",
    )(sink, z, z, z, z, z, z, z, z, z, dst)


def _ctx_gqa_attention(z, sink, dst, *, n_batch, seq, n_ctx, q_cols, k_col, v_col):
    ctx_blk0 = n_batch * seq // n_ctx
    n_heads = q_cols // HEAD_DIM
    group_size = n_heads // (LANES // HEAD_DIM)
    return pl.pallas_call(
        functools.partial(_win_kernel, n_qtiles=0, has_local=False, n_heads=n_heads, group_size=group_size),
        grid=(n_batch,),
        in_specs=[
            pl.BlockSpec(memory_space=pltpu.SMEM),
            pl.BlockSpec((n_ctx, q_cols), lambda b: (ctx_blk0 + b, 0)),
            pl.BlockSpec((n_ctx, LANES), lambda b: (ctx_blk0 + b, k_col)),
            pl.BlockSpec((n_ctx, LANES), lambda b: (ctx_blk0 + b, v_col)),
            pl.BlockSpec(memory_space=pl.ANY),
        ],
        out_specs=pl.BlockSpec((n_ctx, q_cols), lambda b: (ctx_blk0 + b, 0)),
        out_shape=jax.ShapeDtypeStruct(dst.shape, dst.dtype),
        input_output_aliases={4: 0},
        compiler_params=_cparams("parallel"),
        name="ctx_gqa",
    )(sink, z, z, z, dst)


LOG2E = math.log2(math.e)
DIFF_SUB_ROWS = 128


def _diff_kernel(lam_ref, g_ref, q_ref, *refs, lam_init, has_x):
    if has_x:
        kx_ref, vx_ref, kc_ref, vc_ref, _, o_ref, s_scr = refs
        nx = kx_ref.shape[0]
    else:
        kc_ref, vc_ref, _, o_ref, s_scr = refs
        nx = 0
    nc = kc_ref.shape[0]
    sub = s_scr.shape[1]
    n_sub = q_ref.shape[0] // sub
    lv = lam_ref[...]
    lam = (jnp.exp(jnp.sum(lv[0:1] * lv[1:2], axis=-1, keepdims=True))
           - jnp.exp(jnp.sum(lv[2:3] * lv[3:4], axis=-1, keepdims=True)) + lam_init)
    lane = lax.broadcasted_iota(jnp.int32, (1, LANES), 1)
    units = [(h, c) for h in range(n_sub) for c in range(2)]

    def scores(u):
        h, c = units[u]
        q = q_ref[h * sub:(h + 1) * sub, :]
        qm = jnp.where((lane < HEAD_DIM) if c == 0 else (lane >= HEAD_DIM), q, jnp.zeros_like(q))
        if has_x:
            s_scr[u % 2, :, 0:nx] = _dot_nt(qm, kx_ref[...])
        s_scr[u % 2, :, nx:nx + nc] = _dot_nt(qm, kc_ref[...])

    def attend(u):
        s = s_scr[u % 2]
        p = jnp.exp2(s - jnp.max(s, axis=-1, keepdims=True))
        den = jnp.sum(p, axis=-1, keepdims=True)
        pb = p.astype(BF16)
        acc = _dot(pb[:, nx:nx + nc], vc_ref[...])
        if has_x:
            acc = acc + _dot(pb[:, 0:nx], vx_ref[...])
        return acc / den

    res = []
    scores(0)
    for u in range(len(units)):
        if u + 1 < len(units):
            scores(u + 1)
        res.append(attend(u))
    for h in range(n_sub):
        o = res[2 * h] - lam * res[2 * h + 1]
        o = _rms(o) * g_ref[...] * (1.0 - lam_init)
        o_ref[h * sub:(h + 1) * sub, :] = o.astype(o_ref.dtype)


def _diff_attention(z, lam_vec, subln_g, lam_init, dst, *, n_batch, seq, n_ctx, n_heads, tq):
    ctx_blk0 = n_batch * seq // n_ctx
    nq = seq // tq
    return pl.pallas_call(
        functools.partial(_diff_kernel, lam_init=lam_init, has_x=True),
        grid=(n_batch, n_heads, nq),
        in_specs=[
            pl.BlockSpec(lam_vec.shape, lambda b, h, i: (0, 0)),
            pl.BlockSpec((1, LANES), lambda b, h, i: (0, 0)),
            pl.BlockSpec((tq, LANES), lambda b, h, i: (b * nq + i, h)),
            pl.BlockSpec((seq, LANES), lambda b, h, i: (b, n_heads + h)),
            pl.BlockSpec((seq, LANES), lambda b, h, i: (b, 2 * n_heads + h)),
            pl.BlockSpec((n_ctx, LANES), lambda b, h, i: (ctx_blk0 + b, n_heads + h)),
            pl.BlockSpec((n_ctx, LANES), lambda b, h, i: (ctx_blk0 + b, 2 * n_heads + h)),
            pl.BlockSpec(memory_space=pl.ANY),
        ],
        out_specs=pl.BlockSpec((tq, LANES), lambda b, h, i: (b * nq + i, h)),
        out_shape=jax.ShapeDtypeStruct(dst.shape, dst.dtype),
        input_output_aliases={7: 0},
        scratch_shapes=[pltpu.VMEM((2, min(tq, DIFF_SUB_ROWS), seq + n_ctx), F32)],
        compiler_params=_cparams("parallel", "parallel", "parallel"),
        name="diff_attn",
    )(lam_vec, subln_g, z, z, z, z, z, dst)


def _diff_attention_ctx(z, lam_vec, subln_g, lam_init, dst, *, n_batch, seq, n_ctx, n_heads):
    ctx_blk0 = n_batch * seq // n_ctx
    return pl.pallas_call(
        functools.partial(_diff_kernel, lam_init=lam_init, has_x=False),
        grid=(n_batch, n_heads),
        in_specs=[
            pl.BlockSpec(lam_vec.shape, lambda b, h: (0, 0)),
            pl.BlockSpec((1, LANES), lambda b, h: (0, 0)),
            pl.BlockSpec((n_ctx, LANES), lambda b, h: (ctx_blk0 + b, h)),
            pl.BlockSpec((n_ctx, LANES), lambda b, h: (ctx_blk0 + b, n_heads + h)),
            pl.BlockSpec((n_ctx, LANES), lambda b, h: (ctx_blk0 + b, 2 * n_heads + h)),
            pl.BlockSpec(memory_space=pl.ANY),
        ],
        out_specs=pl.BlockSpec((n_ctx, LANES), lambda b, h: (ctx_blk0 + b, h)),
        out_shape=jax.ShapeDtypeStruct(dst.shape, dst.dtype),
        input_output_aliases={5: 0},
        scratch_shapes=[pltpu.VMEM((2, min(n_ctx, DIFF_SUB_ROWS), n_ctx), F32)],
        compiler_params=_cparams("parallel", "parallel"),
        name="diff_attn_ctx",
    )(lam_vec, subln_g, z, z, z, dst)


def _outproj_kernel(a0_ref, a1_ref, w_ref, x_ref, g_ref, gate_ref, o_ref):
    half = a0_ref.shape[1]
    y = _dot(a0_ref[...], w_ref[:half, :]) + _dot(a1_ref[...], w_ref[half:, :])
    o_ref[...] = x_ref[...] + gate_ref[...] * (_rms(y) * g_ref[...])


def _outproj(a0, a0_col, a1, a1_col, w, xt, normg, mod, k_gate, *, tm, rows, seq, n_batch):
    d = xt.shape[1]
    half = d // 2

    def mrow(i):
        return jnp.minimum(i * tm // seq, n_batch)

    return pl.pallas_call(
        _outproj_kernel,
        grid=(rows // tm,),
        in_specs=[
            pl.BlockSpec((tm, half), lambda i: (i, a0_col)),
            pl.BlockSpec((tm, half), lambda i: (i, a1_col)),
            pl.BlockSpec((d, d), lambda i: (0, 0)),
            pl.BlockSpec((tm, d), lambda i: (i, 0)),
            pl.BlockSpec((None, 1, d), lambda i: (1, 0, 0)),
            pl.BlockSpec((None, None, 1, d), lambda i: (mrow(i), k_gate, 0, 0)),
        ],
        out_specs=pl.BlockSpec((tm, d), lambda i: (i, 0)),
        out_shape=jax.ShapeDtypeStruct((rows, d), F32),
        compiler_params=_cparams("parallel"),
        name="outproj",
    )(a0, a1, w, xt, normg, mod)


def _router_kernel(x_ref, g_ref, sh_ref, sc_ref, rw_ref, o_ref, sel_ref):
    h = _modulate(x_ref[...], g_ref[...], sh_ref[...], sc_ref[...])
    logits = _dot_split(_split_bf16(h), _split_bf16(rw_ref[...]))
    lane = lax.broadcasted_iota(jnp.int32, logits.shape, 1)
    ninf = -jnp.inf
    logits = jnp.where(lane < N_EXPERTS, logits, ninf)
    m1 = jnp.max(logits, axis=-1, keepdims=True)
    i1 = jnp.min(jnp.where(logits == m1, lane, LANES), axis=-1, keepdims=True)
    sel1 = lane == i1
    rest = jnp.where(sel1, ninf, logits)
    m2 = jnp.max(rest, axis=-1, keepdims=True)
    i2 = jnp.min(jnp.where(rest == m2, lane, LANES), axis=-1, keepdims=True)
    sel2 = lane == i2
    e2 = jnp.exp(m2 - m1)
    den = 1.0 + e2
    o_ref[...] = jnp.where(sel1, 1.0 / den, 0.0) + jnp.where(sel2, e2 / den, 0.0)
    sel_ref[...] = jnp.where(sel1 | sel2, 1.0, 0.0)


def _router(xt, normg, mod, router_w, *, tm, rows, seq, n_batch):
    d = xt.shape[1]
    rw = jnp.zeros((d, LANES), F32).at[:, :N_EXPERTS].set(router_w)

    def mrow(i):
        return jnp.minimum(i * tm // seq, n_batch)

    return pl.pallas_call(
        _router_kernel,
        grid=(rows // tm,),
        in_specs=[
            pl.BlockSpec((tm, d), lambda i: (i, 0)),
            pl.BlockSpec((None, 1, d), lambda i: (2, 0, 0)),
            pl.BlockSpec((None, None, 1, d), lambda i: (mrow(i), 3, 0, 0)),
            pl.BlockSpec((None, None, 1, d), lambda i: (mrow(i), 4, 0, 0)),
            pl.BlockSpec((d, LANES), lambda i: (0, 0)),
        ],
        out_specs=[pl.BlockSpec((tm, LANES), lambda i: (i, 0))] * 2,
        out_shape=[jax.ShapeDtypeStruct((rows, LANES), F32)] * 2,
        compiler_params=_cparams("parallel"),
        name="router",
    )(xt, normg, mod, mod, rw)


SWIGLU_CHUNK = 256


def _swiglu(h, wg_ref, wu_ref, wo_ref, u_off):
    width = wo_ref.shape[0]
    acc = None
    for c0 in range(0, width, SWIGLU_CHUNK):
        gp = _dot(h, wg_ref[:, c0:c0 + SWIGLU_CHUNK])
        up = _dot(h, wu_ref[:, u_off + c0:u_off + c0 + SWIGLU_CHUNK])
        a = gp * (1.0 / (1.0 + jnp.exp(-gp))) * up
        part = _dot(a.astype(BF16), wo_ref[c0:c0 + SWIGLU_CHUNK, :])
        acc = part if acc is None else acc + part
    return acc


def _ffn_kernel(x_ref, g2_ref, sh_ref, sc_ref, win_ref, wout_ref, g3_ref, gate_ref, o_ref):
    x = x_ref[...]
    h = _modulate(x, g2_ref[...], sh_ref[...], sc_ref[...]).astype(BF16)
    y = _swiglu(h, win_ref, win_ref, wout_ref, wout_ref.shape[0])
    o_ref[...] = x + gate_ref[...] * (_rms(y) * g3_ref[...])


def _ffn(xt, normg, mod, w_in, w_out, *, tm, rows, seq, n_batch):
    d = xt.shape[1]

    def mrow(i):
        return jnp.minimum(i * tm // seq, n_batch)

    in_specs = [
        pl.BlockSpec((tm, d), lambda i: (i, 0)),
        pl.BlockSpec((None, 1, d), lambda i: (2, 0, 0)),
        pl.BlockSpec((None, None, 1, d), lambda i: (mrow(i), 3, 0, 0)),
        pl.BlockSpec((None, None, 1, d), lambda i: (mrow(i), 4, 0, 0)),
        pl.BlockSpec(w_in.shape, lambda i: (0, 0)),
        pl.BlockSpec(w_out.shape, lambda i: (0, 0)),
        pl.BlockSpec((None, 1, d), lambda i: (3, 0, 0)),
        pl.BlockSpec((None, None, 1, d), lambda i: (mrow(i), 5, 0, 0)),
    ]
    return pl.pallas_call(
        _ffn_kernel,
        grid=(rows // tm,),
        in_specs=in_specs,
        out_specs=pl.BlockSpec((tm, d), lambda i: (i, 0)),
        out_shape=jax.ShapeDtypeStruct((rows, d), F32),
        compiler_params=_cparams("parallel"),
        name="ffn",
    )(xt, normg, mod, mod, w_in, w_out, normg, mod)


MOE_CHUNK = 128


def _moe_kernel(nchunk_ref, x_ref, g2_ref, sh_ref, sc_ref, wg_ref, wu_ref, wo_ref, g3_ref, gate_ref,
                comb_ref, sel_ref, o_ref, h_scr, key_scr, keyt_scr, hs_scr, y_scr, acc_scr, *, n_exp, n_f):
    b = pl.program_id(0)
    e = pl.program_id(1)
    f = pl.program_id(2)
    nb = x_ref.shape[0]
    ch = MOE_CHUNK
    n_chunks = nchunk_ref[b * n_exp + e]

    @pl.when((e == 0) & (f == 0))
    def _():
        h_scr[...] = _modulate(x_ref[...], g2_ref[...], sh_ref[...], sc_ref[...]).astype(BF16)
        o_ref[...] = jnp.zeros_like(o_ref)
        ti = lax.broadcasted_iota(jnp.int32, (nb, nb), 0)
        tj = lax.broadcasted_iota(jnp.int32, (nb, nb), 1)
        lower = jnp.where(tj < ti, 1.0, 0.0).astype(BF16)
        sel = sel_ref[...]
        rank = _dot(lower, sel.astype(BF16))
        key = jnp.where(sel > 0.0, rank, -1.0)
        key_scr[...] = key
        keyt_scr[...] = key.T

    def for_rows(body):
        def pair(i, carry):
            body(pl.multiple_of(i * (2 * ch), 2 * ch), 2 * ch)
            return carry

        lax.fori_loop(0, n_chunks // 2, pair, 0)

        @pl.when(n_chunks % 2 == 1)
        def _():
            body(pl.multiple_of((n_chunks - 1) * ch, ch), ch)

    @pl.when(f == 0)
    def _():
        key_row = keyt_scr[pl.ds(e, 1), :]

        def gather(r0, n):
            r = (r0 + lax.broadcasted_iota(jnp.int32, (n, 1), 0)).astype(F32)
            onehot = jnp.where(key_row == r, 1.0, 0.0).astype(BF16)
            hs_scr[pl.ds(r0, n), :] = _dot(onehot, h_scr[...]).astype(BF16)

        for_rows(gather)

    def expert(r0, n):
        rows = pl.ds(r0, n)
        part = _swiglu(hs_scr[rows, :], wg_ref, wu_ref, wo_ref, 0)

        @pl.when(f == 0)
        def _():
            y_scr[rows, :] = part

        @pl.when(f > 0)
        def _():
            y_scr[rows, :] += part

    for_rows(expert)

    @pl.when(f == n_f - 1)
    def _():
        lane = lax.broadcasted_iota(jnp.int32, (nb, LANES), 1)
        pick = lane == e
        key_col = jnp.sum(jnp.where(pick, key_scr[...], 0.0), axis=-1, keepdims=True)
        gate_col = jnp.sum(jnp.where(pick, comb_ref[...], 0.0), axis=-1, keepdims=True)
        acc_scr[...] = jnp.zeros_like(acc_scr)

        def scatter(r0, n):
            r = (r0 + lax.broadcasted_iota(jnp.int32, (1, n), 1)).astype(F32)
            onehot = jnp.where(key_col == r, 1.0, 0.0).astype(BF16)
            acc_scr[...] += _dot(onehot, y_scr[pl.ds(r0, n), :].astype(BF16))

        for_rows(scatter)
        o_ref[...] += gate_col * acc_scr[...]

    @pl.when((e == n_exp - 1) & (f == n_f - 1))
    def _():
        o_ref[...] = x_ref[...] + gate_ref[...] * (_rms(o_ref[...]) * g3_ref[...])


def _moe(xt, normg, mod, w_in, w_out, comb, sel, *, nb, tf, rows, seq, n_batch):
    d = xt.shape[1]
    n_exp, fdim = w_out.shape[0], w_out.shape[1]
    n_f = fdim // tf
    n_blk = rows // nb
    counts = jnp.sum(sel.reshape(n_blk, nb, LANES)[:, :, :n_exp], axis=1).astype(jnp.int32)
    nchunk = ((counts + (MOE_CHUNK - 1)) // MOE_CHUNK).reshape(n_blk * n_exp)

    def mrow(i):
        return jnp.minimum(i * nb // seq, n_batch)

    in_specs = [
        pl.BlockSpec((nb, d), lambda i, e, f, s: (i, 0)),
        pl.BlockSpec((None, 1, d), lambda i, e, f, s: (2, 0, 0)),
        pl.BlockSpec((None, None, 1, d), lambda i, e, f, s: (mrow(i), 3, 0, 0)),
        pl.BlockSpec((None, None, 1, d), lambda i, e, f, s: (mrow(i), 4, 0, 0)),
        pl.BlockSpec((None, d, tf), lambda i, e, f, s: (e, 0, f)),
        pl.BlockSpec((None, d, tf), lambda i, e, f, s: (e, 0, n_f + f)),
        pl.BlockSpec((None, tf, d), lambda i, e, f, s: (e, f, 0)),
        pl.BlockSpec((None, 1, d), lambda i, e, f, s: (3, 0, 0)),
        pl.BlockSpec((None, None, 1, d), lambda i, e, f, s: (mrow(i), 5, 0, 0)),
        pl.BlockSpec((nb, LANES), lambda i, e, f, s: (i, 0)),
        pl.BlockSpec((nb, LANES), lambda i, e, f, s: (i, 0)),
    ]
    return pl.pallas_call(
        functools.partial(_moe_kernel, n_exp=n_exp, n_f=n_f),
        grid_spec=pltpu.PrefetchScalarGridSpec(
            num_scalar_prefetch=1,
            grid=(n_blk, n_exp, n_f),
            in_specs=in_specs,
            out_specs=pl.BlockSpec((nb, d), lambda i, e, f, s: (i, 0)),
            scratch_shapes=[
                pltpu.VMEM((nb, d), BF16),
                pltpu.VMEM((nb, LANES), F32),
                pltpu.VMEM((LANES, nb), F32),
                pltpu.VMEM((nb, d), BF16),
                pltpu.VMEM((nb, d), F32),
                pltpu.VMEM((nb, d), F32),
            ],
        ),
        out_shape=jax.ShapeDtypeStruct((rows, d), F32),
        compiler_params=_cparams("parallel", "arbitrary", "arbitrary"),
        name="moe",
    )(nchunk, xt, normg, mod, mod, w_in, w_in, w_out, normg, mod, comb, sel)


MOE_DMA_ROWS = 128
MOE_SEG_ALIGN = 16
MOE_TILE = 512


def _moe_keys(sel):
    nb = sel.shape[0]
    ti = lax.broadcasted_iota(jnp.int32, (nb, nb), 0)
    tj = lax.broadcasted_iota(jnp.int32, (nb, nb), 1)
    lower = jnp.where(tj < ti, 1.0, 0.0).astype(BF16)
    rank = _dot(lower, sel.astype(BF16))
    return jnp.where(sel > 0.0, rank, -1.0)


def _moe_dispatch_kernel(nchunk_ref, off_ref, x_ref, g2_ref, sh_ref, sc_ref, sel_ref, _, hs_hbm,
                         h_scr, keyt_scr, stage, sems, *, n_exp):
    b = pl.program_id(0)
    rows = MOE_DMA_ROWS
    h_scr[...] = _modulate(x_ref[...], g2_ref[...], sh_ref[...], sc_ref[...]).astype(BF16)
    keyt_scr[...] = _moe_keys(sel_ref[...]).T

    def copy(slot, row0):
        return pltpu.make_async_copy(stage.at[slot], hs_hbm.at[pl.ds(row0, rows), :], sems.at[slot])

    issued = jnp.int32(0)
    for e in range(n_exp):
        key_row = keyt_scr[e:e + 1, :]
        base = off_ref[b * n_exp + e]

        def chunk(c, k):
            slot = k % 2

            @pl.when(k >= 2)
            def _():
                copy(slot, 0).wait()

            r = (c * rows + lax.broadcasted_iota(jnp.int32, (rows, 1), 0)).astype(F32)
            onehot = jnp.where(key_row == r, 1.0, 0.0).astype(BF16)
            stage[slot] = _dot(onehot, h_scr[...]).astype(BF16)
            copy(slot, pl.multiple_of(base + c * rows, MOE_SEG_ALIGN)).start()
            return k + 1

        issued = lax.fori_loop(0, nchunk_ref[b * n_exp + e], chunk, issued)

    @pl.when(issued >= 1)
    def _():
        copy((issued - 1) % 2, 0).wait()

    @pl.when(issued >= 2)
    def _():
        copy(issued % 2, 0).wait()


def _moe_experts_kernel(tile_ref, exp_ref, valid_ref, hs_ref, wg_ref, wu_ref, wo_ref, y_ref, acc, *, n_f):
    t = pl.program_id(0)
    f = pl.program_id(1)

    @pl.when(valid_ref[t] == 1)
    def _():
        part = _swiglu(hs_ref[...], wg_ref, wu_ref, wo_ref, 0)

        @pl.when(f == 0)
        def _():
            acc[...] = part

        @pl.when(f > 0)
        def _():
            acc[...] += part

        @pl.when(f == n_f - 1)
        def _():
            y_ref[...] = acc[...].astype(y_ref.dtype)


def _moe_combine_kernel(nchunk_ref, off_ref, x_ref, g3_ref, gate_ref, comb_ref, sel_ref, y_hbm, o_ref,
                        key_scr, acc_scr, stage, sems, *, n_exp):
    b = pl.program_id(0)
    rows = MOE_DMA_ROWS
    nb = x_ref.shape[0]
    key_scr[...] = _moe_keys(sel_ref[...])
    o_ref[...] = jnp.zeros_like(o_ref)
    lane = lax.broadcasted_iota(jnp.int32, (nb, LANES), 1)

    def copy(slot, row0):
        return pltpu.make_async_copy(y_hbm.at[pl.ds(row0, rows), :], stage.at[slot], sems.at[slot])

    for e in range(n_exp):
        pick = lane == e
        key_col = jnp.sum(jnp.where(pick, key_scr[...], 0.0), axis=-1, keepdims=True)
        gate_col = jnp.sum(jnp.where(pick, comb_ref[...], 0.0), axis=-1, keepdims=True)
        base = off_ref[b * n_exp + e]
        n_chunks = nchunk_ref[b * n_exp + e]
        acc_scr[...] = jnp.zeros_like(acc_scr)

        def row0(c):
            return pl.multiple_of(base + c * rows, MOE_SEG_ALIGN)

        @pl.when(n_chunks >= 1)
        def _():
            copy(0, row0(0)).start()

        def chunk(c, carry):
            slot = c % 2
            copy(slot, 0).wait()

            @pl.when(c + 1 < n_chunks)
            def _():
                copy(1 - slot, row0(c + 1)).start()

            r = (c * rows + lax.broadcasted_iota(jnp.int32, (1, rows), 1)).astype(F32)
            onehot = jnp.where(key_col == r, 1.0, 0.0).astype(BF16)
            acc_scr[...] += _dot(onehot, stage[slot])
            return carry

        lax.fori_loop(0, n_chunks, chunk, 0)
        o_ref[...] += gate_col * acc_scr[...]

    o_ref[...] = x_ref[...] + gate_ref[...] * (_rms(o_ref[...]) * g3_ref[...])


def _moe_global(xt, normg, mod, w_in, w_out, comb, sel, *, nb, tf, rows, seq, n_batch):
    d = xt.shape[1]
    n_exp, fdim = w_out.shape[0], w_out.shape[1]
    n_f = fdim // tf
    n_blk = rows // nb
    i32 = jnp.int32

    counts = jnp.sum(sel.reshape(n_blk, nb, LANES)[:, :, :n_exp], axis=1).astype(i32)
    seg = -(-counts // MOE_SEG_ALIGN) * MOE_SEG_ALIGN
    region = -(-(jnp.sum(seg, axis=0) + MOE_DMA_ROWS) // MOE_TILE) * MOE_TILE
    region_end = jnp.cumsum(region)
    region_start = region_end - region
    off = (region_start[None, :] + jnp.cumsum(seg, axis=0) - seg).reshape(n_blk * n_exp).astype(i32)
    nchunk = (-(-counts // MOE_DMA_ROWS)).reshape(n_blk * n_exp).astype(i32)
    cap = 2 * rows + n_blk * n_exp * (MOE_SEG_ALIGN - 1) + n_exp * (MOE_DMA_ROWS + MOE_TILE - 1)
    cap = -(-cap // MOE_TILE) * MOE_TILE
    n_tiles = cap // MOE_TILE
    tiles = jnp.arange(n_tiles, dtype=i32)
    n_used = region_end[-1] // MOE_TILE
    tile_map = jnp.minimum(tiles, n_used - 1).astype(i32)
    tile_exp = jnp.minimum(jnp.searchsorted(region_end, tile_map * MOE_TILE, side="right"), n_exp - 1).astype(i32)
    tile_valid = (tiles < n_used).astype(i32)

    def mrow(i):
        return jnp.minimum(i * nb // seq, n_batch)

    hs = pl.pallas_call(
        functools.partial(_moe_dispatch_kernel, n_exp=n_exp),
        grid_spec=pltpu.PrefetchScalarGridSpec(
            num_scalar_prefetch=2,
            grid=(n_blk,),
            in_specs=[
                pl.BlockSpec((nb, d), lambda i, *_: (i, 0)),
                pl.BlockSpec((None, 1, d), lambda i, *_: (2, 0, 0)),
                pl.BlockSpec((None, None, 1, d), lambda i, *_: (mrow(i), 3, 0, 0)),
                pl.BlockSpec((None, None, 1, d), lambda i, *_: (mrow(i), 4, 0, 0)),
                pl.BlockSpec((nb, LANES), lambda i, *_: (i, 0)),
                pl.BlockSpec(memory_space=pl.ANY),
            ],
            out_specs=pl.BlockSpec(memory_space=pl.ANY),
            scratch_shapes=[
                pltpu.VMEM((nb, d), BF16),
                pltpu.VMEM((LANES, nb), F32),
                pltpu.VMEM((2, MOE_DMA_ROWS, d), BF16),
                pltpu.SemaphoreType.DMA((2,)),
            ],
        ),
        out_shape=jax.ShapeDtypeStruct((cap, d), BF16),
        input_output_aliases={7: 0},
        compiler_params=_cparams("arbitrary"),
        name="moe_dispatch",
    )(nchunk, off, xt, normg, mod, mod, sel, jnp.zeros((cap, d), BF16))

    ys = pl.pallas_call(
        functools.partial(_moe_experts_kernel, n_f=n_f),
        grid_spec=pltpu.PrefetchScalarGridSpec(
            num_scalar_prefetch=3,
            grid=(n_tiles, n_f),
            in_specs=[
                pl.BlockSpec((MOE_TILE, d), lambda t, f, tm, te, tv: (tm[t], 0)),
                pl.BlockSpec((None, d, tf), lambda t, f, tm, te, tv: (te[t], 0, f)),
                pl.BlockSpec((None, d, tf), lambda t, f, tm, te, tv: (te[t], 0, n_f + f)),
                pl.BlockSpec((None, tf, d), lambda t, f, tm, te, tv: (te[t], f, 0)),
            ],
            out_specs=pl.BlockSpec((MOE_TILE, d), lambda t, f, tm, te, tv: (tm[t], 0)),
            scratch_shapes=[pltpu.VMEM((MOE_TILE, d), F32)],
        ),
        out_shape=jax.ShapeDtypeStruct((cap, d), BF16),
        input_output_aliases={3: 0},
        compiler_params=_cparams("arbitrary", "arbitrary"),
        name="moe_experts",
    )(tile_map, tile_exp, tile_valid, hs, w_in, w_in, w_out)

    return pl.pallas_call(
        functools.partial(_moe_combine_kernel, n_exp=n_exp),
        grid_spec=pltpu.PrefetchScalarGridSpec(
            num_scalar_prefetch=2,
            grid=(n_blk,),
            in_specs=[
                pl.BlockSpec((nb, d), lambda i, *_: (i, 0)),
                pl.BlockSpec((None, 1, d), lambda i, *_: (3, 0, 0)),
                pl.BlockSpec((None, None, 1, d), lambda i, *_: (mrow(i), 5, 0, 0)),
                pl.BlockSpec((nb, LANES), lambda i, *_: (i, 0)),
                pl.BlockSpec((nb, LANES), lambda i, *_: (i, 0)),
                pl.BlockSpec(memory_space=pl.ANY),
            ],
            out_specs=pl.BlockSpec((nb, d), lambda i, *_: (i, 0)),
            scratch_shapes=[
                pltpu.VMEM((nb, LANES), F32),
                pltpu.VMEM((nb, d), F32),
                pltpu.VMEM((2, MOE_DMA_ROWS, d), BF16),
                pltpu.SemaphoreType.DMA((2,)),
            ],
        ),
        out_shape=jax.ShapeDtypeStruct((rows, d), F32),
        compiler_params=_cparams("parallel"),
        name="moe_combine",
    )(nchunk, off, xt, normg, mod, comb, sel, ys)


def _lambda_init(layer):
    return 0.8 - 0.6 * math.exp(-0.3 * layer)


def kernel(x, c, ctx, c_ctx, ada_w, ada_b, norm_g, mix_in_w, mix_out_w, win_sink, diff_qkv_w, diff_out_w,
           diff_lambda, diff_subln_g, ffn_in_w, ffn_out_w, router_w, expert_in_w, expert_out_w):
    n_batch, seq, d = x.shape
    n_ctx = ctx.shape[1]
    depth = ada_w.shape[0]
    n_lat = n_batch * seq
    n_all = n_lat + n_batch * n_ctx
    fdim = mix_in_w.shape[2] - (d // 2 + 2 * LANES)
    q_cols = d // 2
    n_diff_heads = d // LANES

    tm_proj = _pick_tile(512, seq, n_batch * n_ctx)
    tm_out = _pick_tile(512, seq, n_batch * n_ctx)
    tm_ffn = _pick_tile(512, seq, n_batch * n_ctx)
    nb_moe = _pick_tile(1024, seq, n_batch * n_ctx)
    tq_diff = _pick_tile(512, seq)
    tq_win = _pick_tile(512, seq)
    common = dict(seq=seq, n_batch=n_batch)

    xt = jnp.concatenate([x.reshape(n_lat, d), ctx.reshape(n_batch * n_ctx, d)], axis=0)
    n_mod = -(-(n_batch + 1) // 8) * 8
    cv = jnp.zeros((n_mod, d), F32).at[:n_batch].set(c).at[n_batch].set(c_ctx)
    mods = _modvec(cv, ada_w, ada_b).reshape(depth, n_mod, 6, 1, d)
    rope = _rope_tables(seq, tm_proj)
    f_tables = _fourier_tables(seq)
    fc_tables = _dense_fourier_tables(n_ctx)

    n_f = fdim // LANES
    plan_even = ([(0, i * LANES, "plain") for i in range(n_f)]
                 + [(1, i * LANES, "rope_q_log2") for i in range(q_cols // LANES)]
                 + [(1, q_cols, "rope_k"), (1, q_cols + LANES, "plain")])
    plan_odd = ([(0, i * LANES, "rope_q_log2") for i in range(n_diff_heads)]
                + [(0, d + i * LANES, "rope_k") for i in range(n_diff_heads)]
                + [(0, 2 * d + i * LANES, "plain") for i in range(n_diff_heads)])

    for layer in range(depth):
        j = layer // 2
        need_ctx = layer < depth - 1
        rows = n_all if need_ctx else n_lat
        mod = mods[layer]
        ng = norm_g[layer].reshape(4, 1, d)
        if layer % 2 == 0:
            f, z = _proj(xt, ng, mod, 0, 1, mix_in_w[j].astype(BF16), rope, plan_even,
                         [(fdim, F32), (q_cols + 2 * LANES, BF16)], tm=tm_proj, n_lat=n_lat, **common)
            k_col, v_col = q_cols // LANES, q_cols // LANES + 1
            att = dict(n_batch=n_batch, seq=seq, n_ctx=n_ctx, q_cols=q_cols, k_col=k_col, v_col=v_col)
            mix_f = _fourier(f, f_tables, jnp.zeros((rows, fdim), BF16), n_batch=n_batch, seq=seq)
            mix_a = _win_attention(z, win_sink[j], jnp.zeros((rows, q_cols), BF16), tq=tq_win, **att)
            if need_ctx:
                mix_f = _dense_fourier(f, fc_tables, mix_f, n_batch=n_batch, n_pos=n_ctx,
                                       row_block0=n_lat // n_ctx)
                mix_a = _ctx_gqa_attention(z, win_sink[j], mix_a, **att)
            xt = _outproj(mix_f, 0, mix_a, 0, mix_out_w[j].astype(BF16), xt, ng, mod, 2, tm=tm_out, rows=rows,
                          **common)
            xt = _ffn(xt, ng, mod, ffn_in_w[j].astype(BF16), ffn_out_w[j].astype(BF16),
                      tm=tm_ffn, rows=rows, **common)
        else:
            lam_init = _lambda_init(layer)
            (z,) = _proj(xt, ng, mod, 0, 1, diff_qkv_w[j].astype(BF16), rope, plan_odd, [(3 * d, BF16)],
                         tm=tm_proj, n_lat=n_lat, **common)
            subg = diff_subln_g[j].reshape(1, LANES)
            att = dict(n_batch=n_batch, seq=seq, n_ctx=n_ctx, n_heads=n_diff_heads)
            mix = _diff_attention(z, diff_lambda[j], subg, lam_init, jnp.zeros((rows, d), BF16), tq=tq_diff, **att)
            if need_ctx:
                mix = _diff_attention_ctx(z, diff_lambda[j], subg, lam_init, mix, **att)
            xt = _outproj(mix, 0, mix, 1, diff_out_w[j].astype(BF16), xt, ng, mod, 2, tm=tm_out, rows=rows,
                          **common)
            comb, sel = _router(xt, ng, mod, router_w[j], tm=tm_out, rows=rows, **common)
            xt = _moe_global(xt, ng, mod, expert_in_w[j].astype(BF16), expert_out_w[j].astype(BF16), comb, sel,
                      nb=nb_moe, tf=1792, rows=rows, **common)
    return xt[:n_lat].reshape(n_batch, seq, d)
```

```python
import functools
import math

import numpy as np
import jax
import jax.numpy as jnp
from jax import lax
from jax.experimental import pallas as pl
from jax.experimental.pallas import tpu as pltpu

EPS = 1e-6
NEG = -1e30
HEAD_DIM = 64
LANES = 128
GRID_W = 64
BLOCK = 128
WINDOW = 128
ROPE_THETA = 10000.0
N_EXPERTS = 8
F32 = jnp.float32
BF16 = jnp.bfloat16
HIGHEST = lax.Precision.HIGHEST
VMEM_LIMIT = 56 * 1024 * 1024


def _cparams(*sem):
    return pltpu.CompilerParams(dimension_semantics=sem, vmem_limit_bytes=VMEM_LIMIT)


def _dot(a, b):
    return jnp.dot(a, b, preferred_element_type=F32)


def _dot_nt(a, b):
    return lax.dot_general(a, b, (((1,), (1,)), ((), ())), preferred_element_type=F32)


def _dot_hi(a, b):
    return jnp.dot(a, b, precision=HIGHEST, preferred_element_type=F32)


def _rms(v):
    return v * lax.rsqrt(jnp.mean(v * v, axis=-1, keepdims=True) + EPS)


def _modulate(x, g, sh, sc):
    return _rms(x) * g * (1.0 + sc) + sh


def _pick_tile(pref, *dims):
    t = pref
    while any(d % t for d in dims):
        t //= 2
    return t


def _modvec_kernel(c_ref, w_ref, b_ref, o_ref):
    cv = c_ref[...]
    s = cv * (1.0 / (1.0 + jnp.exp(-cv)))
    o_ref[...] = _dot(s.astype(BF16), w_ref[...].astype(BF16)) + b_ref[...]


def _modvec(cv, ada_w, ada_b):
    depth, d, n = ada_w.shape
    r = cv.shape[0]
    tn = _pick_tile(1536, n)
    return pl.pallas_call(
        _modvec_kernel,
        grid=(depth, n // tn),
        in_specs=[
            pl.BlockSpec((r, d), lambda l, j: (0, 0)),
            pl.BlockSpec((None, d, tn), lambda l, j: (l, 0, j)),
            pl.BlockSpec((None, 1, tn), lambda l, j: (l, 0, j)),
        ],
        out_specs=pl.BlockSpec((None, r, tn), lambda l, j: (l, 0, j)),
        out_shape=jax.ShapeDtypeStruct((depth, r, n), F32),
        compiler_params=_cparams("parallel", "parallel"),
        name="modvec",
    )(cv, ada_w, ada_b.reshape(depth, 1, n))


def _proj_kernel(x_ref, g_ref, sh_ref, sc_ref, w_ref, cos_ref, sa_ref, sb_ref, *o_refs, plan, group):
    h = _modulate(x_ref[...], g_ref[...], sh_ref[...], sc_ref[...]).astype(BF16)
    n = w_ref.shape[1]
    for g0 in range(0, n, group):
        acc = _dot(h, w_ref[:, g0:g0 + group])
        for c0 in range(0, group, LANES):
            oi, oc, mode = plan[(g0 + c0) // LANES]
            v = acc[:, c0:c0 + LANES]
            if mode != "plain":
                v = (v * cos_ref[...] + pltpu.roll(v, LANES - 16, 1) * sa_ref[...]
                     + pltpu.roll(v, 16, 1) * sb_ref[...])
                if mode == "rope_q_log2":
                    v = v * (HEAD_DIM ** -0.5 * LOG2E)
            o_refs[oi][:, oc:oc + LANES] = v.astype(o_refs[oi].dtype)


def _proj(xt, normg, mod, k_sh, k_sc, w, rope, plan, outs, *, tm, n_lat, seq, n_batch):
    t, d = xt.shape
    n = w.shape[1]
    group = _pick_tile(512, n)
    nx = n_lat // tm
    per = seq // tm

    def mrow(i):
        return jnp.minimum(i * tm // seq, n_batch)

    def rrow(i):
        return jnp.where(i < nx, i % per, per)

    in_specs = [
        pl.BlockSpec((tm, d), lambda i: (i, 0)),
        pl.BlockSpec((None, 1, d), lambda i: (0, 0, 0)),
        pl.BlockSpec((None, None, 1, d), lambda i: (mrow(i), k_sh, 0, 0)),
        pl.BlockSpec((None, None, 1, d), lambda i: (mrow(i), k_sc, 0, 0)),
        pl.BlockSpec((d, n), lambda i: (0, 0)),
        pl.BlockSpec((tm, LANES), lambda i: (rrow(i), 0)),
        pl.BlockSpec((tm, LANES), lambda i: (rrow(i), 0)),
        pl.BlockSpec((tm, LANES), lambda i: (rrow(i), 0)),
    ]
    out_specs = [pl.BlockSpec((tm, wd), lambda i: (i, 0)) for wd, _ in outs]
    out_shape = [jax.ShapeDtypeStruct((t, wd), dt) for wd, dt in outs]
    return pl.pallas_call(
        functools.partial(_proj_kernel, plan=plan, group=group),
        grid=(t // tm,),
        in_specs=in_specs,
        out_specs=out_specs,
        out_shape=out_shape,
        compiler_params=_cparams("parallel"),
        name="proj",
    )(xt, normg, mod, mod, w, *rope)


def _rope_tables(seq, tm):
    rows_count = seq // GRID_W
    rows = jnp.repeat(jnp.arange(rows_count), GRID_W).astype(F32)
    cols = jnp.tile(jnp.arange(GRID_W), rows_count).astype(F32)
    axis_dim = HEAD_DIM // 2
    inv = ROPE_THETA ** (-jnp.arange(0, axis_dim, 2, dtype=F32) / axis_dim)
    ar = rows[:, None] * inv
    ac = cols[:, None] * inv
    cr, sr, cc, sc = jnp.cos(ar), jnp.sin(ar), jnp.cos(ac), jnp.sin(ac)
    z = jnp.zeros_like(sr)
    reps = LANES // HEAD_DIM
    cos = jnp.tile(jnp.concatenate([cr, cr, cc, cc], axis=1), (1, reps))
    sa = jnp.tile(jnp.concatenate([-sr, z, -sc, z], axis=1), (1, reps))
    sb = jnp.tile(jnp.concatenate([z, sr, z, sc], axis=1), (1, reps))
    ident = jnp.ones((tm, LANES), F32)
    zero = jnp.zeros((tm, LANES), F32)
    return (jnp.concatenate([cos, ident]), jnp.concatenate([sa, zero]), jnp.concatenate([sb, zero]))


FOURIER_LANES = 2 * LANES


def _split_bf16(t):
    hi = t.astype(BF16)
    return hi, (t - hi.astype(F32)).astype(BF16)


def _dot_split(a, b):
    return _dot(a[0], b[0]) + _dot(a[0], b[1]) + _dot(a[1], b[0])


def _fourier_tables(seq):
    n2 = GRID_W
    n1 = seq // n2
    norm = 1.0 / math.sqrt(seq * LANES)
    a = np.arange(n1)
    k1 = np.arange(n1)
    b = np.arange(n2)
    ang = (b[:, None, None] * k1[None, :, None] + (seq // n1) * k1[None, :, None] * a[None, None, :]) % seq
    th = 2.0 * np.pi * ang / seq
    m1 = np.concatenate([np.cos(th), -np.sin(th)], axis=1).astype(np.float32)
    ph = 2.0 * np.pi * ((b[:, None] * b[None, :]) % n2) / n2
    c2, s2 = np.cos(ph), np.sin(ph)
    g2 = np.block([[c2, s2], [-s2, c2]]).astype(np.float32)
    ch = np.arange(LANES)
    pc = 2.0 * np.pi * ((ch[:, None] * ch[None, :]) % LANES) / LANES
    cc = (np.cos(pc) * norm).astype(np.float32)
    sc = (np.sin(pc) * norm).astype(np.float32)
    cs = np.concatenate([cc, sc], axis=0)
    out = []
    for t in (m1, g2, cs):
        out.extend(_split_bf16(jnp.asarray(t)))
    return tuple(out)


FOURIER_ROW_CHUNK = 512


def _fourier_kernel(*refs, n1, n2, n_grp):
    u_refs = refs[:n_grp]
    m1h_ref, m1l_ref, g2h_ref, g2l_ref, csh_ref, csl_ref, _, o_ref, b_scr, xr_scr, xi_scr = refs[n_grp:]
    seq = n1 * n2
    for b in range(n2):
        xs = jnp.concatenate([u[pl.ds(b, n1, stride=n2), :] for u in u_refs], axis=1)
        z = _dot_split((m1h_ref[b], m1l_ref[b]), _split_bf16(xs))
        for g in range(n_grp):
            b_scr[g, 2 * n1 * b:2 * n1 * (b + 1), :] = z[:, g * LANES:(g + 1) * LANES]
    g2 = (g2h_ref[...], g2l_ref[...])
    for k1 in range(n1):
        bk = jnp.concatenate(
            [jnp.concatenate([b_scr.at[g][pl.ds(k1, n2, stride=2 * n1), :],
                              b_scr.at[g][pl.ds(n1 + k1, n2, stride=2 * n1), :]], axis=0)
             for g in range(n_grp)], axis=1)
        xk = _dot_split(g2, _split_bf16(bk))
        for g in range(n_grp):
            xr_scr[g, n2 * k1:n2 * (k1 + 1), :] = xk[:n2, g * LANES:(g + 1) * LANES]
            xi_scr[g, n2 * k1:n2 * (k1 + 1), :] = xk[n2:, g * LANES:(g + 1) * LANES]
    cs = (csh_ref[...], csl_ref[...])
    rc = min(FOURIER_ROW_CHUNK, seq)
    for g in range(n_grp):
        for r0 in range(0, seq, rc):
            x = jnp.concatenate([xr_scr[g, r0:r0 + rc, :], xi_scr[g, r0:r0 + rc, :]], axis=1)
            xr_scr[g, r0:r0 + rc, :] = _dot_split(_split_bf16(x), cs)
    for g in range(n_grp):
        for k2 in range(n2):
            o_ref[n1 * k2:n1 * (k2 + 1), g * LANES:(g + 1) * LANES] = (
                xr_scr.at[g][pl.ds(k2, n1, stride=n2), :].astype(o_ref.dtype))


def _fourier(f, tables, dst, *, n_batch, seq):
    n2 = GRID_W
    n1 = seq // n2
    n_grp = FOURIER_LANES // LANES
    table_specs = [pl.BlockSpec(t.shape, (lambda b, g, nd=t.ndim: (0,) * nd)) for t in tables]
    u_specs = [pl.BlockSpec((seq, LANES), (lambda b, g, k=k: (b, g * n_grp + k))) for k in range(n_grp)]
    return pl.pallas_call(
        functools.partial(_fourier_kernel, n1=n1, n2=n2, n_grp=n_grp),
        grid=(n_batch, f.shape[1] // FOURIER_LANES),
        in_specs=u_specs + table_specs + [pl.BlockSpec(memory_space=pl.ANY)],
        out_specs=pl.BlockSpec((seq, FOURIER_LANES), lambda b, g: (b, g)),
        out_shape=jax.ShapeDtypeStruct(dst.shape, dst.dtype),
        input_output_aliases={n_grp + len(tables): 0},
        scratch_shapes=[pltpu.VMEM((n_grp, 2 * seq, LANES), F32), pltpu.VMEM((n_grp, seq, LANES), F32),
                        pltpu.VMEM((n_grp, seq, LANES), F32)],
        compiler_params=_cparams("parallel", "parallel"),
        name="fourier",
    )(*([f] * n_grp), *tables, dst)


def _dense_fourier_tables(n):
    norm = 1.0 / math.sqrt(n * LANES)
    p = np.arange(n)
    ph = 2.0 * np.pi * ((p[:, None] * p[None, :]) % n) / n
    ch = np.arange(LANES)
    pc = 2.0 * np.pi * ((ch[:, None] * ch[None, :]) % LANES) / LANES
    return (jnp.asarray(np.cos(ph).astype(np.float32)), jnp.asarray(np.sin(ph).astype(np.float32)),
            jnp.asarray((np.cos(pc) * norm).astype(np.float32)), jnp.asarray((np.sin(pc) * norm).astype(np.float32)))


def _dense_fourier_kernel(u_ref, cl_ref, sl_ref, cc_ref, sc_ref, _, o_ref):
    u = u_ref[...]
    y = _dot_hi(cl_ref[...], _dot_hi(u, cc_ref[...])) - _dot_hi(sl_ref[...], _dot_hi(u, sc_ref[...]))
    o_ref[...] = y.astype(o_ref.dtype)


def _dense_fourier(f, tables, dst, *, n_batch, n_pos, row_block0):
    cl, sl, cc, sc = tables
    groups = f.shape[1] // LANES
    return pl.pallas_call(
        _dense_fourier_kernel,
        grid=(n_batch, groups),
        in_specs=[
            pl.BlockSpec((n_pos, LANES), lambda b, g: (row_block0 + b, g)),
            pl.BlockSpec(cl.shape, lambda b, g: (0, 0)),
            pl.BlockSpec(sl.shape, lambda b, g: (0, 0)),
            pl.BlockSpec(cc.shape, lambda b, g: (0, 0)),
            pl.BlockSpec(sc.shape, lambda b, g: (0, 0)),
            pl.BlockSpec(memory_space=pl.ANY),
        ],
        out_specs=pl.BlockSpec((n_pos, LANES), lambda b, g: (row_block0 + b, g)),
        out_shape=jax.ShapeDtypeStruct(dst.shape, dst.dtype),
        input_output_aliases={5: 0},
        compiler_params=_cparams("parallel", "parallel"),
        name="fourier_ctx",
    )(f, cl, sl, cc, sc, dst)


def _win_kernel(sink_ref, q_ref, *refs, n_qtiles, has_local, n_heads, group_size):
    if has_local:
        kp_ref, kc_ref, kn_ref, vp_ref, vc_ref, vn_ref, kx_ref, vx_ref, _, o_ref = refs
    else:
        kx_ref, vx_ref, _, o_ref = refs
    tq = q_ref.shape[0]
    lane = lax.broadcasted_iota(jnp.int32, (1, LANES), 1)
    half_mask = [lane < HEAD_DIM, lane >= HEAD_DIM]
    pieces = [(kx_ref[...], vx_ref[...], None)]
    if has_local:
        n = pl.program_id(1)
        qi = lax.broadcasted_iota(jnp.int32, (tq, BLOCK), 0)
        kj = lax.broadcasted_iota(jnp.int32, (tq, BLOCK), 1)
        valid_prev = (kj >= qi) & (n >= 1)
        valid_next = (kj <= qi - (tq - WINDOW)) & (n <= n_qtiles - 2)
        di = lax.broadcasted_iota(jnp.int32, (tq, tq), 0) - lax.broadcasted_iota(jnp.int32, (tq, tq), 1)
        valid_mid = (di <= WINDOW) & (di >= -WINDOW)
        pieces += [(kp_ref[...], vp_ref[...], valid_prev), (kc_ref[...], vc_ref[...], valid_mid),
                   (kn_ref[...], vn_ref[...], valid_next)]
    for pair in range(n_heads // 2):
        qp = q_ref[:, pair * LANES:(pair + 1) * LANES].astype(F32)
        qp_sw = pltpu.roll(qp, HEAD_DIM, 1)
        out_pair = jnp.zeros((tq, LANES), F32)
        for half in range(2):
            head = 2 * pair + half
            kv = head // group_size
            src = qp if half == kv else qp_sw
            qe = jnp.where(half_mask[kv], src, 0.0).astype(BF16)
            sink = sink_ref[head] * LOG2E
            scores = []
            m = jnp.zeros((tq, 1), F32) + sink
            for k, _, valid in pieces:
                s = _dot_nt(qe, k)
                if valid is not None:
                    s = jnp.where(valid, s, NEG)
                scores.append(s)
                m = jnp.maximum(m, jnp.max(s, axis=-1, keepdims=True))
            den = jnp.exp2(sink - m)
            pv = jnp.zeros((tq, LANES), F32)
            for s, (_, v, _) in zip(scores, pieces):
                p = jnp.exp2(s - m)
                den = den + jnp.sum(p, axis=-1, keepdims=True)
                pv = pv + _dot(p.astype(BF16), v)
            pv = pv / den
            if half != kv:
                pv = pltpu.roll(pv, HEAD_DIM, 1)
            out_pair = jnp.where(half_mask[half], pv, out_pair)
        o_ref[:, pair * LANES:(pair + 1) * LANES] = out_pair.astype(o_ref.dtype)


def _win_attention(z, sink, dst, *, n_batch, seq, n_ctx, q_cols, k_col, v_col, tq):
    nbk = seq // BLOCK
    nq = seq // tq
    per = tq // BLOCK
    ctx_blk0 = n_batch * seq // n_ctx
    n_heads = q_cols // HEAD_DIM
    group_size = n_heads // (LANES // HEAD_DIM)

    def edge(col, first):
        return pl.BlockSpec(
            (BLOCK, LANES), lambda b, n: (b * nbk + jnp.clip(n * per + first, 0, nbk - 1), col))

    def mid(col):
        return pl.BlockSpec((tq, LANES), lambda b, n: (b * nq + n, col))

    return pl.pallas_call(
        functools.partial(_win_kernel, n_qtiles=nq, has_local=True, n_heads=n_heads, group_size=group_size),
        grid=(n_batch, nq),
        in_specs=[
            pl.BlockSpec(memory_space=pltpu.SMEM),
            pl.BlockSpec((tq, q_cols), lambda b, n: (b * nq + n, 0)),
            edge(k_col, -1), mid(k_col), edge(k_col, per),
            edge(v_col, -1), mid(v_col), edge(v_col, per),
            pl.BlockSpec((n_ctx, LANES), lambda b, n: (ctx_blk0 + b, k_col)),
            pl.BlockSpec((n_ctx, LANES), lambda b, n: (ctx_blk0 + b, v_col)),
            pl.BlockSpec(memory_space=pl.ANY),
        ],
        out_specs=pl.BlockSpec((tq, q_cols), lambda b, n: (b * nq + n, 0)),
        out_shape=jax.ShapeDtypeStruct(dst.shape, dst.dtype),
        input_output_aliases={10: 0},
        compiler_params=_cparams("parallel", "parallel"),
        name="win_attn",
    )(sink, z, z, z, z, z, z, z, z, z, dst)


def _ctx_gqa_attention(z, sink, dst, *, n_batch, seq, n_ctx, q_cols, k_col, v_col):
    ctx_blk0 = n_batch * seq // n_ctx
    n_heads = q_cols // HEAD_DIM
    group_size = n_heads // (LANES // HEAD_DIM)
    return pl.pallas_call(
        functools.partial(_win_kernel, n_qtiles=0, has_local=False, n_heads=n_heads, group_size=group_size),
        grid=(n_batch,),
        in_specs=[
            pl.BlockSpec(memory_space=pltpu.SMEM),
            pl.BlockSpec((n_ctx, q_cols), lambda b: (ctx_blk0 + b, 0)),
            pl.BlockSpec((n_ctx, LANES), lambda b: (ctx_blk0 + b, k_col)),
            pl.BlockSpec((n_ctx, LANES), lambda b: (ctx_blk0 + b, v_col)),
            pl.BlockSpec(memory_space=pl.ANY),
        ],
        out_specs=pl.BlockSpec((n_ctx, q_cols), lambda b: (ctx_blk0 + b, 0)),
        out_shape=jax.ShapeDtypeStruct(dst.shape, dst.dtype),
        input_output_aliases={4: 0},
        compiler_params=_cparams("parallel"),
        name="ctx_gqa",
    )(sink, z, z, z, dst)


LOG2E = math.log2(math.e)
DIFF_SUB_ROWS = 128


def _diff_kernel(lam_ref, g_ref, q_ref, *refs, lam_init, has_x):
    if has_x:
        kx_ref, vx_ref, kc_ref, vc_ref, _, o_ref, s_scr = refs
        nx = kx_ref.shape[0]
    else:
        kc_ref, vc_ref, _, o_ref, s_scr = refs
        nx = 0
    nc = kc_ref.shape[0]
    sub = s_scr.shape[1]
    n_sub = q_ref.shape[0] // sub
    lv = lam_ref[...]
    lam = (jnp.exp(jnp.sum(lv[0:1] * lv[1:2], axis=-1, keepdims=True))
           - jnp.exp(jnp.sum(lv[2:3] * lv[3:4], axis=-1, keepdims=True)) + lam_init)
    lane = lax.broadcasted_iota(jnp.int32, (1, LANES), 1)
    units = [(h, c) for h in range(n_sub) for c in range(2)]

    def scores(u):
        h, c = units[u]
        q = q_ref[h * sub:(h + 1) * sub, :]
        qm = jnp.where((lane < HEAD_DIM) if c == 0 else (lane >= HEAD_DIM), q, jnp.zeros_like(q))
        if has_x:
            s_scr[u % 2, :, 0:nx] = _dot_nt(qm, kx_ref[...])
        s_scr[u % 2, :, nx:nx + nc] = _dot_nt(qm, kc_ref[...])

    def attend(u):
        s = s_scr[u % 2]
        p = jnp.exp2(s - jnp.max(s, axis=-1, keepdims=True))
        den = jnp.sum(p, axis=-1, keepdims=True)
        pb = p.astype(BF16)
        acc = _dot(pb[:, nx:nx + nc], vc_ref[...])
        if has_x:
            acc = acc + _dot(pb[:, 0:nx], vx_ref[...])
        return acc / den

    res = []
    scores(0)
    for u in range(len(units)):
        if u + 1 < len(units):
            scores(u + 1)
        res.append(attend(u))
    for h in range(n_sub):
        o = res[2 * h] - lam * res[2 * h + 1]
        o = _rms(o) * g_ref[...] * (1.0 - lam_init)
        o_ref[h * sub:(h + 1) * sub, :] = o.astype(o_ref.dtype)


def _diff_attention(z, lam_vec, subln_g, lam_init, dst, *, n_batch, seq, n_ctx, n_heads, tq):
    ctx_blk0 = n_batch * seq // n_ctx
    nq = seq // tq
    return pl.pallas_call(
        functools.partial(_diff_kernel, lam_init=lam_init, has_x=True),
        grid=(n_batch, n_heads, nq),
        in_specs=[
            pl.BlockSpec(lam_vec.shape, lambda b, h, i: (0, 0)),
            pl.BlockSpec((1, LANES), lambda b, h, i: (0, 0)),
            pl.BlockSpec((tq, LANES), lambda b, h, i: (b * nq + i, h)),
            pl.BlockSpec((seq, LANES), lambda b, h, i: (b, n_heads + h)),
            pl.BlockSpec((seq, LANES), lambda b, h, i: (b, 2 * n_heads + h)),
            pl.BlockSpec((n_ctx, LANES), lambda b, h, i: (ctx_blk0 + b, n_heads + h)),
            pl.BlockSpec((n_ctx, LANES), lambda b, h, i: (ctx_blk0 + b, 2 * n_heads + h)),
            pl.BlockSpec(memory_space=pl.ANY),
        ],
        out_specs=pl.BlockSpec((tq, LANES), lambda b, h, i: (b * nq + i, h)),
        out_shape=jax.ShapeDtypeStruct(dst.shape, dst.dtype),
        input_output_aliases={7: 0},
        scratch_shapes=[pltpu.VMEM((2, min(tq, DIFF_SUB_ROWS), seq + n_ctx), F32)],
        compiler_params=_cparams("parallel", "parallel", "parallel"),
        name="diff_attn",
    )(lam_vec, subln_g, z, z, z, z, z, dst)


def _diff_attention_ctx(z, lam_vec, subln_g, lam_init, dst, *, n_batch, seq, n_ctx, n_heads):
    ctx_blk0 = n_batch * seq // n_ctx
    return pl.pallas_call(
        functools.partial(_diff_kernel, lam_init=lam_init, has_x=False),
        grid=(n_batch, n_heads),
        in_specs=[
            pl.BlockSpec(lam_vec.shape, lambda b, h: (0, 0)),
            pl.BlockSpec((1, LANES), lambda b, h: (0, 0)),
            pl.BlockSpec((n_ctx, LANES), lambda b, h: (ctx_blk0 + b, h)),
            pl.BlockSpec((n_ctx, LANES), lambda b, h: (ctx_blk0 + b, n_heads + h)),
            pl.BlockSpec((n_ctx, LANES), lambda b, h: (ctx_blk0 + b, 2 * n_heads + h)),
            pl.BlockSpec(memory_space=pl.ANY),
        ],
        out_specs=pl.BlockSpec((n_ctx, LANES), lambda b, h: (ctx_blk0 + b, h)),
        out_shape=jax.ShapeDtypeStruct(dst.shape, dst.dtype),
        input_output_aliases={5: 0},
        scratch_shapes=[pltpu.VMEM((2, min(n_ctx, DIFF_SUB_ROWS), n_ctx), F32)],
        compiler_params=_cparams("parallel", "parallel"),
        name="diff_attn_ctx",
    )(lam_vec, subln_g, z, z, z, dst)


def _outproj_kernel(a0_ref, a1_ref, w_ref, x_ref, g_ref, gate_ref, o_ref):
    half = a0_ref.shape[1]
    y = _dot(a0_ref[...], w_ref[:half, :]) + _dot(a1_ref[...], w_ref[half:, :])
    o_ref[...] = x_ref[...] + gate_ref[...] * (_rms(y) * g_ref[...])


def _outproj(a0, a0_col, a1, a1_col, w, xt, normg, mod, k_gate, *, tm, rows, seq, n_batch):
    d = xt.shape[1]
    half = d // 2

    def mrow(i):
        return jnp.minimum(i * tm // seq, n_batch)

    return pl.pallas_call(
        _outproj_kernel,
        grid=(rows // tm,),
        in_specs=[
            pl.BlockSpec((tm, half), lambda i: (i, a0_col)),
            pl.BlockSpec((tm, half), lambda i: (i, a1_col)),
            pl.BlockSpec((d, d), lambda i: (0, 0)),
            pl.BlockSpec((tm, d), lambda i: (i, 0)),
            pl.BlockSpec((None, 1, d), lambda i: (1, 0, 0)),
            pl.BlockSpec((None, None, 1, d), lambda i: (mrow(i), k_gate, 0, 0)),
        ],
        out_specs=pl.BlockSpec((tm, d), lambda i: (i, 0)),
        out_shape=jax.ShapeDtypeStruct((rows, d), F32),
        compiler_params=_cparams("parallel"),
        name="outproj",
    )(a0, a1, w, xt, normg, mod)


def _router_kernel(x_ref, g_ref, sh_ref, sc_ref, rw_ref, o_ref, sel_ref):
    h = _modulate(x_ref[...], g_ref[...], sh_ref[...], sc_ref[...])
    logits = _dot_split(_split_bf16(h), _split_bf16(rw_ref[...]))
    lane = lax.broadcasted_iota(jnp.int32, logits.shape, 1)
    ninf = -jnp.inf
    logits = jnp.where(lane < N_EXPERTS, logits, ninf)
    m1 = jnp.max(logits, axis=-1, keepdims=True)
    i1 = jnp.min(jnp.where(logits == m1, lane, LANES), axis=-1, keepdims=True)
    sel1 = lane == i1
    rest = jnp.where(sel1, ninf, logits)
    m2 = jnp.max(rest, axis=-1, keepdims=True)
    i2 = jnp.min(jnp.where(rest == m2, lane, LANES), axis=-1, keepdims=True)
    sel2 = lane == i2
    e2 = jnp.exp(m2 - m1)
    den = 1.0 + e2
    o_ref[...] = jnp.where(sel1, 1.0 / den, 0.0) + jnp.where(sel2, e2 / den, 0.0)
    sel_ref[...] = jnp.where(sel1 | sel2, 1.0, 0.0)


def _router(xt, normg, mod, router_w, *, tm, rows, seq, n_batch):
    d = xt.shape[1]
    rw = jnp.zeros((d, LANES), F32).at[:, :N_EXPERTS].set(router_w)

    def mrow(i):
        return jnp.minimum(i * tm // seq, n_batch)

    return pl.pallas_call(
        _router_kernel,
        grid=(rows // tm,),
        in_specs=[
            pl.BlockSpec((tm, d), lambda i: (i, 0)),
            pl.BlockSpec((None, 1, d), lambda i: (2, 0, 0)),
            pl.BlockSpec((None, None, 1, d), lambda i: (mrow(i), 3, 0, 0)),
            pl.BlockSpec((None, None, 1, d), lambda i: (mrow(i), 4, 0, 0)),
            pl.BlockSpec((d, LANES), lambda i: (0, 0)),
        ],
        out_specs=[pl.BlockSpec((tm, LANES), lambda i: (i, 0))] * 2,
        out_shape=[jax.ShapeDtypeStruct((rows, LANES), F32)] * 2,
        compiler_params=_cparams("parallel"),
        name="router",
    )(xt, normg, mod, mod, rw)


SWIGLU_CHUNK = 256


def _swiglu(h, wg_ref, wu_ref, wo_ref, u_off):
    width = wo_ref.shape[0]
    acc = None
    for c0 in range(0, width, SWIGLU_CHUNK):
        gp = _dot(h, wg_ref[:, c0:c0 + SWIGLU_CHUNK])
        up = _dot(h, wu_ref[:, u_off + c0:u_off + c0 + SWIGLU_CHUNK])
        a = gp * (1.0 / (1.0 + jnp.exp(-gp))) * up
        part = _dot(a.astype(BF16), wo_ref[c0:c0 + SWIGLU_CHUNK, :])
        acc = part if acc is None else acc + part
    return acc


def _ffn_kernel(x_ref, g2_ref, sh_ref, sc_ref, win_ref, wout_ref, g3_ref, gate_ref, o_ref):
    x = x_ref[...]
    h = _modulate(x, g2_ref[...], sh_ref[...], sc_ref[...]).astype(BF16)
    y = _swiglu(h, win_ref, win_ref, wout_ref, wout_ref.shape[0])
    o_ref[...] = x + gate_ref[...] * (_rms(y) * g3_ref[...])


def _ffn(xt, normg, mod, w_in, w_out, *, tm, rows, seq, n_batch):
    d = xt.shape[1]

    def mrow(i):
        return jnp.minimum(i * tm // seq, n_batch)

    in_specs = [
        pl.BlockSpec((tm, d), lambda i: (i, 0)),
        pl.BlockSpec((None, 1, d), lambda i: (2, 0, 0)),
        pl.BlockSpec((None, None, 1, d), lambda i: (mrow(i), 3, 0, 0)),
        pl.BlockSpec((None, None, 1, d), lambda i: (mrow(i), 4, 0, 0)),
        pl.BlockSpec(w_in.shape, lambda i: (0, 0)),
        pl.BlockSpec(w_out.shape, lambda i: (0, 0)),
        pl.BlockSpec((None, 1, d), lambda i: (3, 0, 0)),
        pl.BlockSpec((None, None, 1, d), lambda i: (mrow(i), 5, 0, 0)),
    ]
    return pl.pallas_call(
        _ffn_kernel,
        grid=(rows // tm,),
        in_specs=in_specs,
        out_specs=pl.BlockSpec((tm, d), lambda i: (i, 0)),
        out_shape=jax.ShapeDtypeStruct((rows, d), F32),
        compiler_params=_cparams("parallel"),
        name="ffn",
    )(xt, normg, mod, mod, w_in, w_out, normg, mod)


MOE_CHUNK = 128


def _moe_kernel(nchunk_ref, x_ref, g2_ref, sh_ref, sc_ref, wg_ref, wu_ref, wo_ref, g3_ref, gate_ref,
                comb_ref, sel_ref, o_ref, h_scr, key_scr, keyt_scr, hs_scr, y_scr, acc_scr, *, n_exp, n_f):
    b = pl.program_id(0)
    e = pl.program_id(1)
    f = pl.program_id(2)
    nb = x_ref.shape[0]
    ch = MOE_CHUNK
    n_chunks = nchunk_ref[b * n_exp + e]

    @pl.when((e == 0) & (f == 0))
    def _():
        h_scr[...] = _modulate(x_ref[...], g2_ref[...], sh_ref[...], sc_ref[...]).astype(BF16)
        o_ref[...] = jnp.zeros_like(o_ref)
        ti = lax.broadcasted_iota(jnp.int32, (nb, nb), 0)
        tj = lax.broadcasted_iota(jnp.int32, (nb, nb), 1)
        lower = jnp.where(tj < ti, 1.0, 0.0).astype(BF16)
        sel = sel_ref[...]
        rank = _dot(lower, sel.astype(BF16))
        key = jnp.where(sel > 0.0, rank, -1.0)
        key_scr[...] = key
        keyt_scr[...] = key.T

    def for_rows(body):
        def pair(i, carry):
            body(pl.multiple_of(i * (2 * ch), 2 * ch), 2 * ch)
            return carry

        lax.fori_loop(0, n_chunks // 2, pair, 0)

        @pl.when(n_chunks % 2 == 1)
        def _():
            body(pl.multiple_of((n_chunks - 1) * ch, ch), ch)

    @pl.when(f == 0)
    def _():
        key_row = keyt_scr[pl.ds(e, 1), :]

        def gather(r0, n):
            r = (r0 + lax.broadcasted_iota(jnp.int32, (n, 1), 0)).astype(F32)
            onehot = jnp.where(key_row == r, 1.0, 0.0).astype(BF16)
            hs_scr[pl.ds(r0, n), :] = _dot(onehot, h_scr[...]).astype(BF16)

        for_rows(gather)

    def expert(r0, n):
        rows = pl.ds(r0, n)
        part = _swiglu(hs_scr[rows, :], wg_ref, wu_ref, wo_ref, 0)

        @pl.when(f == 0)
        def _():
            y_scr[rows, :] = part

        @pl.when(f > 0)
        def _():
            y_scr[rows, :] += part

    for_rows(expert)

    @pl.when(f == n_f - 1)
    def _():
        lane = lax.broadcasted_iota(jnp.int32, (nb, LANES), 1)
        pick = lane == e
        key_col = jnp.sum(jnp.where(pick, key_scr[...], 0.0), axis=-1, keepdims=True)
        gate_col = jnp.sum(jnp.where(pick, comb_ref[...], 0.0), axis=-1, keepdims=True)
        acc_scr[...] = jnp.zeros_like(acc_scr)

        def scatter(r0, n):
            r = (r0 + lax.broadcasted_iota(jnp.int32, (1, n), 1)).astype(F32)
            onehot = jnp.where(key_col == r, 1.0, 0.0).astype(BF16)
            acc_scr[...] += _dot(onehot, y_scr[pl.ds(r0, n), :].astype(BF16))

        for_rows(scatter)
        o_ref[...] += gate_col * acc_scr[...]

    @pl.when((e == n_exp - 1) & (f == n_f - 1))
    def _():
        o_ref[...] = x_ref[...] + gate_ref[...] * (_rms(o_ref[...]) * g3_ref[...])


def _moe(xt, normg, mod, w_in, w_out, comb, sel, *, nb, tf, rows, seq, n_batch):
    d = xt.shape[1]
    n_exp, fdim = w_out.shape[0], w_out.shape[1]
    n_f = fdim // tf
    n_blk = rows // nb
    counts = jnp.sum(sel.reshape(n_blk, nb, LANES)[:, :, :n_exp], axis=1).astype(jnp.int32)
    nchunk = ((counts + (MOE_CHUNK - 1)) // MOE_CHUNK).reshape(n_blk * n_exp)

    def mrow(i):
        return jnp.minimum(i * nb // seq, n_batch)

    in_specs = [
        pl.BlockSpec((nb, d), lambda i, e, f, s: (i, 0)),
        pl.BlockSpec((None, 1, d), lambda i, e, f, s: (2, 0, 0)),
        pl.BlockSpec((None, None, 1, d), lambda i, e, f, s: (mrow(i), 3, 0, 0)),
        pl.BlockSpec((None, None, 1, d), lambda i, e, f, s: (mrow(i), 4, 0, 0)),
        pl.BlockSpec((None, d, tf), lambda i, e, f, s: (e, 0, f)),
        pl.BlockSpec((None, d, tf), lambda i, e, f, s: (e, 0, n_f + f)),
        pl.BlockSpec((None, tf, d), lambda i, e, f, s: (e, f, 0)),
        pl.BlockSpec((None, 1, d), lambda i, e, f, s: (3, 0, 0)),
        pl.BlockSpec((None, None, 1, d), lambda i, e, f, s: (mrow(i), 5, 0, 0)),
        pl.BlockSpec((nb, LANES), lambda i, e, f, s: (i, 0)),
        pl.BlockSpec((nb, LANES), lambda i, e, f, s: (i, 0)),
    ]
    return pl.pallas_call(
        functools.partial(_moe_kernel, n_exp=n_exp, n_f=n_f),
        grid_spec=pltpu.PrefetchScalarGridSpec(
            num_scalar_prefetch=1,
            grid=(n_blk, n_exp, n_f),
            in_specs=in_specs,
            out_specs=pl.BlockSpec((nb, d), lambda i, e, f, s: (i, 0)),
            scratch_shapes=[
                pltpu.VMEM((nb, d), BF16),
                pltpu.VMEM((nb, LANES), F32),
                pltpu.VMEM((LANES, nb), F32),
                pltpu.VMEM((nb, d), BF16),
                pltpu.VMEM((nb, d), F32),
                pltpu.VMEM((nb, d), F32),
            ],
        ),
        out_shape=jax.ShapeDtypeStruct((rows, d), F32),
        compiler_params=_cparams("parallel", "arbitrary", "arbitrary"),
        name="moe",
    )(nchunk, xt, normg, mod, mod, w_in, w_in, w_out, normg, mod, comb, sel)


MOE_DISPATCH_ROWS = 256
MOE_COMBINE_ROWS = 512
MOE_SEG_ALIGN = 16
MOE_TILE = 512


def _moe_keys(sel):
    nb = sel.shape[0]
    ti = lax.broadcasted_iota(jnp.int32, (nb, nb), 0)
    tj = lax.broadcasted_iota(jnp.int32, (nb, nb), 1)
    lower = jnp.where(tj < ti, 1.0, 0.0).astype(BF16)
    rank = _dot(lower, sel.astype(BF16))
    return jnp.where(sel > 0.0, rank, -1.0)


def _moe_dispatch_kernel(nchunk_ref, off_ref, x_ref, g2_ref, sh_ref, sc_ref, sel_ref, _, hs_hbm,
                         h_scr, keyt_scr, stage, sems, *, n_exp):
    b = pl.program_id(0)
    rows = MOE_DISPATCH_ROWS
    h_scr[...] = _modulate(x_ref[...], g2_ref[...], sh_ref[...], sc_ref[...]).astype(BF16)
    keyt_scr[...] = _moe_keys(sel_ref[...]).T

    def copy(slot, row0):
        return pltpu.make_async_copy(stage.at[slot], hs_hbm.at[pl.ds(row0, rows), :], sems.at[slot])

    issued = jnp.int32(0)
    for e in range(n_exp):
        key_row = keyt_scr[e:e + 1, :]
        base = off_ref[b * n_exp + e]

        def chunk(c, k):
            slot = k % 2

            @pl.when(k >= 2)
            def _():
                copy(slot, 0).wait()

            r = (c * rows + lax.broadcasted_iota(jnp.int32, (rows, 1), 0)).astype(F32)
            onehot = jnp.where(key_row == r, 1.0, 0.0).astype(BF16)
            stage[slot] = _dot(onehot, h_scr[...]).astype(BF16)
            copy(slot, pl.multiple_of(base + c * rows, MOE_SEG_ALIGN)).start()
            return k + 1

        issued = lax.fori_loop(0, nchunk_ref[b * n_exp + e], chunk, issued)

    @pl.when(issued >= 1)
    def _():
        copy((issued - 1) % 2, 0).wait()

    @pl.when(issued >= 2)
    def _():
        copy(issued % 2, 0).wait()


def _moe_experts_kernel(tile_ref, exp_ref, valid_ref, hs_ref, wg_ref, wu_ref, wo_ref, y_ref, acc, *, n_f):
    t = pl.program_id(0)
    f = pl.program_id(1)

    @pl.when(valid_ref[t] == 1)
    def _():
        part = _swiglu(hs_ref[...], wg_ref, wu_ref, wo_ref, 0)

        @pl.when(f == 0)
        def _():
            acc[...] = part

        @pl.when(f > 0)
        def _():
            acc[...] += part

        @pl.when(f == n_f - 1)
        def _():
            y_ref[...] = acc[...].astype(y_ref.dtype)


def _moe_combine_kernel(npiece_ref, off_ref, x_ref, g3_ref, gate_ref, comb_ref, sel_ref, y_hbm, o_ref,
                        key_scr, stage, sems, *, n_exp):
    b = pl.program_id(0)
    rows = MOE_COMBINE_ROWS
    nb = x_ref.shape[0]
    key_scr[...] = _moe_keys(sel_ref[...])
    o_ref[...] = jnp.zeros_like(o_ref)
    lane = lax.broadcasted_iota(jnp.int32, (nb, LANES), 1)

    def copy(slot, row0):
        return pltpu.make_async_copy(y_hbm.at[pl.ds(row0, rows), :], stage.at[slot], sems.at[slot])

    def row0(e, c):
        return pl.multiple_of(off_ref[b * n_exp + e] + c * rows, MOE_SEG_ALIGN)

    copy(0, row0(0, 0)).start()
    done = jnp.int32(0)
    for e in range(n_exp):
        pick = lane == e
        key_col = jnp.sum(jnp.where(pick, key_scr[...], 0.0), axis=-1, keepdims=True)
        gate_col = jnp.sum(jnp.where(pick, comb_ref[...], 0.0), axis=-1, keepdims=True)
        n_pieces = npiece_ref[b * n_exp + e]

        def piece(c, k, e=e, n_pieces=n_pieces, key_col=key_col, gate_col=gate_col):
            slot = k % 2
            copy(slot, 0).wait()

            @pl.when(c + 1 < n_pieces)
            def _():
                copy(1 - slot, row0(e, c + 1)).start()

            if e + 1 < n_exp:
                @pl.when(c + 1 == n_pieces)
                def _():
                    copy(1 - slot, row0(e + 1, 0)).start()

            r = (c * rows + lax.broadcasted_iota(jnp.int32, (1, rows), 1)).astype(F32)
            onehot = jnp.where(key_col == r, 1.0, 0.0).astype(BF16)
            o_ref[...] += gate_col * _dot(onehot, stage[slot])
            return k + 1

        done = lax.fori_loop(0, n_pieces, piece, done)

    o_ref[...] = x_ref[...] + gate_ref[...] * (_rms(o_ref[...]) * g3_ref[...])


def _moe_global(xt, normg, mod, w_in, w_out, comb, sel, *, nb, tf, rows, seq, n_batch):
    d = xt.shape[1]
    n_exp, fdim = w_out.shape[0], w_out.shape[1]
    n_f = fdim // tf
    n_blk = rows // nb
    i32 = jnp.int32

    counts = jnp.sum(sel.reshape(n_blk, nb, LANES)[:, :, :n_exp], axis=1).astype(i32)
    seg = -(-counts // MOE_SEG_ALIGN) * MOE_SEG_ALIGN
    region = -(-(jnp.sum(seg, axis=0) + MOE_DISPATCH_ROWS) // MOE_TILE) * MOE_TILE
    region_end = jnp.cumsum(region)
    region_start = region_end - region
    off = (region_start[None, :] + jnp.cumsum(seg, axis=0) - seg).reshape(n_blk * n_exp).astype(i32)
    nchunk = (-(-counts // MOE_DISPATCH_ROWS)).reshape(n_blk * n_exp).astype(i32)
    npiece = jnp.maximum(-(-counts // MOE_COMBINE_ROWS), 1).reshape(n_blk * n_exp).astype(i32)
    cap = 2 * rows + n_blk * n_exp * (MOE_SEG_ALIGN - 1) + n_exp * (MOE_DISPATCH_ROWS + MOE_TILE - 1)
    n_tiles = -(-cap // MOE_TILE)
    cap = n_tiles * MOE_TILE + MOE_COMBINE_ROWS
    tiles = jnp.arange(n_tiles, dtype=i32)
    n_used = region_end[-1] // MOE_TILE
    tile_map = jnp.minimum(tiles, n_used - 1).astype(i32)
    tile_exp = jnp.minimum(jnp.searchsorted(region_end, tile_map * MOE_TILE, side="right"), n_exp - 1).astype(i32)
    tile_valid = (tiles < n_used).astype(i32)

    def mrow(i):
        return jnp.minimum(i * nb // seq, n_batch)

    hs = pl.pallas_call(
        functools.partial(_moe_dispatch_kernel, n_exp=n_exp),
        grid_spec=pltpu.PrefetchScalarGridSpec(
            num_scalar_prefetch=2,
            grid=(n_blk,),
            in_specs=[
                pl.BlockSpec((nb, d), lambda i, *_: (i, 0)),
                pl.BlockSpec((None, 1, d), lambda i, *_: (2, 0, 0)),
                pl.BlockSpec((None, None, 1, d), lambda i, *_: (mrow(i), 3, 0, 0)),
                pl.BlockSpec((None, None, 1, d), lambda i, *_: (mrow(i), 4, 0, 0)),
                pl.BlockSpec((nb, LANES), lambda i, *_: (i, 0)),
                pl.BlockSpec(memory_space=pl.ANY),
            ],
            out_specs=pl.BlockSpec(memory_space=pl.ANY),
            scratch_shapes=[
                pltpu.VMEM((nb, d), BF16),
                pltpu.VMEM((LANES, nb), F32),
                pltpu.VMEM((2, MOE_DISPATCH_ROWS, d), BF16),
                pltpu.SemaphoreType.DMA((2,)),
            ],
        ),
        out_shape=jax.ShapeDtypeStruct((cap, d), BF16),
        input_output_aliases={7: 0},
        compiler_params=_cparams("arbitrary"),
        name="moe_dispatch",
    )(nchunk, off, xt, normg, mod, mod, sel, jnp.zeros((cap, d), BF16))

    ys = pl.pallas_call(
        functools.partial(_moe_experts_kernel, n_f=n_f),
        grid_spec=pltpu.PrefetchScalarGridSpec(
            num_scalar_prefetch=3,
            grid=(n_tiles, n_f),
            in_specs=[
                pl.BlockSpec((MOE_TILE, d), lambda t, f, tm, te, tv: (tm[t], 0)),
                pl.BlockSpec((None, d, tf), lambda t, f, tm, te, tv: (te[t], 0, f)),
                pl.BlockSpec((None, d, tf), lambda t, f, tm, te, tv: (te[t], 0, n_f + f)),
                pl.BlockSpec((None, tf, d), lambda t, f, tm, te, tv: (te[t], f, 0)),
            ],
            out_specs=pl.BlockSpec((MOE_TILE, d), lambda t, f, tm, te, tv: (tm[t], 0)),
            scratch_shapes=[pltpu.VMEM((MOE_TILE, d), F32)],
        ),
        out_shape=jax.ShapeDtypeStruct((cap, d), BF16),
        input_output_aliases={3: 0},
        compiler_params=_cparams("arbitrary", "arbitrary"),
        name="moe_experts",
    )(tile_map, tile_exp, tile_valid, hs, w_in, w_in, w_out)

    return pl.pallas_call(
        functools.partial(_moe_combine_kernel, n_exp=n_exp),
        grid_spec=pltpu.PrefetchScalarGridSpec(
            num_scalar_prefetch=2,
            grid=(n_blk,),
            in_specs=[
                pl.BlockSpec((nb, d), lambda i, *_: (i, 0)),
                pl.BlockSpec((None, 1, d), lambda i, *_: (3, 0, 0)),
                pl.BlockSpec((None, None, 1, d), lambda i, *_: (mrow(i), 5, 0, 0)),
                pl.BlockSpec((nb, LANES), lambda i, *_: (i, 0)),
                pl.BlockSpec((nb, LANES), lambda i, *_: (i, 0)),
                pl.BlockSpec(memory_space=pl.ANY),
            ],
            out_specs=pl.BlockSpec((nb, d), lambda i, *_: (i, 0)),
            scratch_shapes=[
                pltpu.VMEM((nb, LANES), F32),
                pltpu.VMEM((2, MOE_COMBINE_ROWS, d), BF16),
                pltpu.SemaphoreType.DMA((2,)),
            ],
        ),
        out_shape=jax.ShapeDtypeStruct((rows, d), F32),
        compiler_params=_cparams("parallel"),
        name="moe_combine",
    )(npiece, off, xt, normg, mod, comb, sel, ys)


def _lambda_init(layer):
    return 0.8 - 0.6 * math.exp(-0.3 * layer)


def kernel(x, c, ctx, c_ctx, ada_w, ada_b, norm_g, mix_in_w, mix_out_w, win_sink, diff_qkv_w, diff_out_w,
           diff_lambda, diff_subln_g, ffn_in_w, ffn_out_w, router_w, expert_in_w, expert_out_w):
    n_batch, seq, d = x.shape
    n_ctx = ctx.shape[1]
    depth = ada_w.shape[0]
    n_lat = n_batch * seq
    n_all = n_lat + n_batch * n_ctx
    fdim = mix_in_w.shape[2] - (d // 2 + 2 * LANES)
    q_cols = d // 2
    n_diff_heads = d // LANES

    tm_proj = _pick_tile(512, seq, n_batch * n_ctx)
    tm_out = _pick_tile(512, seq, n_batch * n_ctx)
    tm_ffn = _pick_tile(512, seq, n_batch * n_ctx)
    nb_moe = _pick_tile(1024, seq, n_batch * n_ctx)
    tq_diff = _pick_tile(512, seq)
    tq_win = _pick_tile(512, seq)
    common = dict(seq=seq, n_batch=n_batch)

    xt = jnp.concatenate([x.reshape(n_lat, d), ctx.reshape(n_batch * n_ctx, d)], axis=0)
    n_mod = -(-(n_batch + 1) // 8) * 8
    cv = jnp.zeros((n_mod, d), F32).at[:n_batch].set(c).at[n_batch].set(c_ctx)
    mods = _modvec(cv, ada_w, ada_b).reshape(depth, n_mod, 6, 1, d)
    rope = _rope_tables(seq, tm_proj)
    f_tables = _fourier_tables(seq)
    fc_tables = _dense_fourier_tables(n_ctx)

    n_f = fdim // LANES
    plan_even = ([(0, i * LANES, "plain") for i in range(n_f)]
                 + [(1, i * LANES, "rope_q_log2") for i in range(q_cols // LANES)]
                 + [(1, q_cols, "rope_k"), (1, q_cols + LANES, "plain")])
    plan_odd = ([(0, i * LANES, "rope_q_log2") for i in range(n_diff_heads)]
                + [(0, d + i * LANES, "rope_k") for i in range(n_diff_heads)]
                + [(0, 2 * d + i * LANES, "plain") for i in range(n_diff_heads)])

    for layer in range(depth):
        j = layer // 2
        need_ctx = layer < depth - 1
        rows = n_all if need_ctx else n_lat
        mod = mods[layer]
        ng = norm_g[layer].reshape(4, 1, d)
        if layer % 2 == 0:
            f, z = _proj(xt, ng, mod, 0, 1, mix_in_w[j].astype(BF16), rope, plan_even,
                         [(fdim, F32), (q_cols + 2 * LANES, BF16)], tm=tm_proj, n_lat=n_lat, **common)
            k_col, v_col = q_cols // LANES, q_cols // LANES + 1
            att = dict(n_batch=n_batch, seq=seq, n_ctx=n_ctx, q_cols=q_cols, k_col=k_col, v_col=v_col)
            mix_f = _fourier(f, f_tables, jnp.zeros((rows, fdim), BF16), n_batch=n_batch, seq=seq)
            mix_a = _win_attention(z, win_sink[j], jnp.zeros((rows, q_cols), BF16), tq=tq_win, **att)
            if need_ctx:
                mix_f = _dense_fourier(f, fc_tables, mix_f, n_batch=n_batch, n_pos=n_ctx,
                                       row_block0=n_lat // n_ctx)
                mix_a = _ctx_gqa_attention(z, win_sink[j], mix_a, **att)
            xt = _outproj(mix_f, 0, mix_a, 0, mix_out_w[j].astype(BF16), xt, ng, mod, 2, tm=tm_out, rows=rows,
                          **common)
            xt = _ffn(xt, ng, mod, ffn_in_w[j].astype(BF16), ffn_out_w[j].astype(BF16),
                      tm=tm_ffn, rows=rows, **common)
        else:
            lam_init = _lambda_init(layer)
            (z,) = _proj(xt, ng, mod, 0, 1, diff_qkv_w[j].astype(BF16), rope, plan_odd, [(3 * d, BF16)],
                         tm=tm_proj, n_lat=n_lat, **common)
            subg = diff_subln_g[j].reshape(1, LANES)
            att = dict(n_batch=n_batch, seq=seq, n_ctx=n_ctx, n_heads=n_diff_heads)
            mix = _diff_attention(z, diff_lambda[j], subg, lam_init, jnp.zeros((rows, d), BF16), tq=tq_diff, **att)
            if need_ctx:
                mix = _diff_attention_ctx(z, diff_lambda[j], subg, lam_init, mix, **att)
            xt = _outproj(mix, 0, mix, 1, diff_out_w[j].astype(BF16), xt, ng, mod, 2, tm=tm_out, rows=rows,
                          **common)
            comb, sel = _router(xt, ng, mod, router_w[j], tm=tm_out, rows=rows, **common)
            xt = _moe_global(xt, ng, mod, expert_in_w[j].astype(BF16), expert_out_w[j].astype(BF16), comb, sel,
                      nb=nb_moe, tf=1792, rows=rows, **common)
    return xt[:n_lat].reshape(n_batch, seq, d)
```

```python
import functools
import math

import numpy as np
import jax
import jax.numpy as jnp
from jax import lax
from jax.experimental import pallas as pl
from jax.experimental.pallas import tpu as pltpu

EPS = 1e-6
NEG = -1e30
HEAD_DIM = 64
LANES = 128
GRID_W = 64
BLOCK = 128
WINDOW = 128
ROPE_THETA = 10000.0
N_EXPERTS = 8
F32 = jnp.float32
BF16 = jnp.bfloat16
HIGHEST = lax.Precision.HIGHEST
VMEM_LIMIT = 56 * 1024 * 1024


def _cparams(*sem):
    return pltpu.CompilerParams(dimension_semantics=sem, vmem_limit_bytes=VMEM_LIMIT)


def _dot(a, b):
    return jnp.dot(a, b, preferred_element_type=F32)


def _dot_nt(a, b):
    return lax.dot_general(a, b, (((1,), (1,)), ((), ())), preferred_element_type=F32)


def _dot_hi(a, b):
    return jnp.dot(a, b, precision=HIGHEST, preferred_element_type=F32)


def _rms(v):
    return v * lax.rsqrt(jnp.mean(v * v, axis=-1, keepdims=True) + EPS)


def _modulate(x, g, sh, sc):
    return _rms(x) * g * (1.0 + sc) + sh


def _pick_tile(pref, *dims):
    t = pref
    while any(d % t for d in dims):
        t //= 2
    return t


def _modvec_kernel(c_ref, w_ref, b_ref, o_ref):
    cv = c_ref[...]
    s = cv * (1.0 / (1.0 + jnp.exp(-cv)))
    o_ref[...] = _dot(s.astype(BF16), w_ref[...].astype(BF16)) + b_ref[...]


def _modvec(cv, ada_w, ada_b):
    depth, d, n = ada_w.shape
    r = cv.shape[0]
    tn = _pick_tile(1536, n)
    return pl.pallas_call(
        _modvec_kernel,
        grid=(depth, n // tn),
        in_specs=[
            pl.BlockSpec((r, d), lambda l, j: (0, 0)),
            pl.BlockSpec((None, d, tn), lambda l, j: (l, 0, j)),
            pl.BlockSpec((None, 1, tn), lambda l, j: (l, 0, j)),
        ],
        out_specs=pl.BlockSpec((None, r, tn), lambda l, j: (l, 0, j)),
        out_shape=jax.ShapeDtypeStruct((depth, r, n), F32),
        compiler_params=_cparams("parallel", "parallel"),
        name="modvec",
    )(cv, ada_w, ada_b.reshape(depth, 1, n))


def _proj_kernel(x_ref, g_ref, sh_ref, sc_ref, w_ref, cos_ref, sa_ref, sb_ref, *o_refs, plan, group):
    h = _modulate(x_ref[...], g_ref[...], sh_ref[...], sc_ref[...]).astype(BF16)
    n = w_ref.shape[1]
    for g0 in range(0, n, group):
        acc = _dot(h, w_ref[:, g0:g0 + group])
        for c0 in range(0, group, LANES):
            oi, oc, mode = plan[(g0 + c0) // LANES]
            v = acc[:, c0:c0 + LANES]
            if mode.startswith("rope"):
                v = (v * cos_ref[...] + pltpu.roll(v, LANES - 16, 1) * sa_ref[...]
                     + pltpu.roll(v, 16, 1) * sb_ref[...])
                if mode == "rope_q_log2":
                    v = v * (HEAD_DIM ** -0.5 * LOG2E)
            if mode == "plain_t":
                o_refs[oi][oc:oc + LANES, :] = v.T.astype(o_refs[oi].dtype)
            else:
                o_refs[oi][:, oc:oc + LANES] = v.astype(o_refs[oi].dtype)


def _proj(xt, normg, mod, k_sh, k_sc, w, rope, plan, outs, *, tm, n_lat, seq, n_batch):
    t, d = xt.shape
    n = w.shape[1]
    group = _pick_tile(512, n)
    nx = n_lat // tm
    per = seq // tm

    def mrow(i):
        return jnp.minimum(i * tm // seq, n_batch)

    def rrow(i):
        return jnp.where(i < nx, i % per, per)

    in_specs = [
        pl.BlockSpec((tm, d), lambda i: (i, 0)),
        pl.BlockSpec((None, 1, d), lambda i: (0, 0, 0)),
        pl.BlockSpec((None, None, 1, d), lambda i: (mrow(i), k_sh, 0, 0)),
        pl.BlockSpec((None, None, 1, d), lambda i: (mrow(i), k_sc, 0, 0)),
        pl.BlockSpec((d, n), lambda i: (0, 0)),
        pl.BlockSpec((tm, LANES), lambda i: (rrow(i), 0)),
        pl.BlockSpec((tm, LANES), lambda i: (rrow(i), 0)),
        pl.BlockSpec((tm, LANES), lambda i: (rrow(i), 0)),
    ]
    out_specs = [pl.BlockSpec((wd, tm), lambda i: (0, i)) if tr else pl.BlockSpec((tm, wd), lambda i: (i, 0))
                 for wd, _, tr in outs]
    out_shape = [jax.ShapeDtypeStruct((wd, t) if tr else (t, wd), dt) for wd, dt, tr in outs]
    return pl.pallas_call(
        functools.partial(_proj_kernel, plan=plan, group=group),
        grid=(t // tm,),
        in_specs=in_specs,
        out_specs=out_specs,
        out_shape=out_shape,
        compiler_params=_cparams("parallel"),
        name="proj",
    )(xt, normg, mod, mod, w, *rope)


def _rope_tables(seq, tm):
    rows_count = seq // GRID_W
    rows = jnp.repeat(jnp.arange(rows_count), GRID_W).astype(F32)
    cols = jnp.tile(jnp.arange(GRID_W), rows_count).astype(F32)
    axis_dim = HEAD_DIM // 2
    inv = ROPE_THETA ** (-jnp.arange(0, axis_dim, 2, dtype=F32) / axis_dim)
    ar = rows[:, None] * inv
    ac = cols[:, None] * inv
    cr, sr, cc, sc = jnp.cos(ar), jnp.sin(ar), jnp.cos(ac), jnp.sin(ac)
    z = jnp.zeros_like(sr)
    reps = LANES // HEAD_DIM
    cos = jnp.tile(jnp.concatenate([cr, cr, cc, cc], axis=1), (1, reps))
    sa = jnp.tile(jnp.concatenate([-sr, z, -sc, z], axis=1), (1, reps))
    sb = jnp.tile(jnp.concatenate([z, sr, z, sc], axis=1), (1, reps))
    ident = jnp.ones((tm, LANES), F32)
    zero = jnp.zeros((tm, LANES), F32)
    return (jnp.concatenate([cos, ident]), jnp.concatenate([sa, zero]), jnp.concatenate([sb, zero]))


FOURIER_LANES = 2 * LANES


def _split_bf16(t):
    hi = t.astype(BF16)
    return hi, (t - hi.astype(F32)).astype(BF16)


def _dot_split(a, b):
    return _dot(a[0], b[0]) + _dot(a[0], b[1]) + _dot(a[1], b[0])


def _fourier_tables(seq):
    n2 = GRID_W
    n1 = seq // n2
    norm = 1.0 / math.sqrt(seq * LANES)
    a = np.arange(n1)
    k1 = np.arange(n1)
    b = np.arange(n2)
    ang = (b[:, None, None] * k1[None, :, None] + (seq // n1) * k1[None, :, None] * a[None, None, :]) % seq
    th = 2.0 * np.pi * ang / seq
    m1 = np.concatenate([np.cos(th), -np.sin(th)], axis=1).astype(np.float32)
    ph = 2.0 * np.pi * ((b[:, None] * b[None, :]) % n2) / n2
    c2, s2 = np.cos(ph), np.sin(ph)
    g2 = np.block([[c2, s2], [-s2, c2]]).astype(np.float32)
    ch = np.arange(LANES)
    pc = 2.0 * np.pi * ((ch[:, None] * ch[None, :]) % LANES) / LANES
    cc = (np.cos(pc) * norm).astype(np.float32)
    sc = (np.sin(pc) * norm).astype(np.float32)
    cs = np.concatenate([cc, sc], axis=0)
    out = []
    for t in (m1, g2, cs):
        out.extend(_split_bf16(jnp.asarray(t)))
    return tuple(out)


FOURIER_ROW_CHUNK = 512


def _fourier_kernel(*refs, n1, n2, n_grp):
    u_refs = refs[:n_grp]
    m1h_ref, m1l_ref, g2h_ref, g2l_ref, csh_ref, csl_ref, _, o_ref, b_scr, xr_scr, xi_scr = refs[n_grp:]
    seq = n1 * n2
    for b in range(n2):
        xs = jnp.concatenate([u[pl.ds(b, n1, stride=n2), :] for u in u_refs], axis=1)
        z = _dot_split((m1h_ref[b], m1l_ref[b]), _split_bf16(xs))
        for g in range(n_grp):
            b_scr[g, 2 * n1 * b:2 * n1 * (b + 1), :] = z[:, g * LANES:(g + 1) * LANES]
    g2 = (g2h_ref[...], g2l_ref[...])
    for k1 in range(n1):
        bk = jnp.concatenate(
            [jnp.concatenate([b_scr.at[g][pl.ds(k1, n2, stride=2 * n1), :],
                              b_scr.at[g][pl.ds(n1 + k1, n2, stride=2 * n1), :]], axis=0)
             for g in range(n_grp)], axis=1)
        xk = _dot_split(g2, _split_bf16(bk))
        for g in range(n_grp):
            xr_scr[g, n2 * k1:n2 * (k1 + 1), :] = xk[:n2, g * LANES:(g + 1) * LANES]
            xi_scr[g, n2 * k1:n2 * (k1 + 1), :] = xk[n2:, g * LANES:(g + 1) * LANES]
    cs = (csh_ref[...], csl_ref[...])
    rc = min(FOURIER_ROW_CHUNK, seq)
    for g in range(n_grp):
        for r0 in range(0, seq, rc):
            x = jnp.concatenate([xr_scr[g, r0:r0 + rc, :], xi_scr[g, r0:r0 + rc, :]], axis=1)
            xr_scr[g, r0:r0 + rc, :] = _dot_split(_split_bf16(x), cs)
    for g in range(n_grp):
        for k2 in range(n2):
            o_ref[n1 * k2:n1 * (k2 + 1), g * LANES:(g + 1) * LANES] = (
                xr_scr.at[g][pl.ds(k2, n1, stride=n2), :].astype(o_ref.dtype))


def _fourier(f, tables, dst, *, n_batch, seq):
    n2 = GRID_W
    n1 = seq // n2
    n_grp = FOURIER_LANES // LANES
    table_specs = [pl.BlockSpec(t.shape, (lambda b, g, nd=t.ndim: (0,) * nd)) for t in tables]
    u_specs = [pl.BlockSpec((seq, LANES), (lambda b, g, k=k: (b, g * n_grp + k))) for k in range(n_grp)]
    return pl.pallas_call(
        functools.partial(_fourier_kernel, n1=n1, n2=n2, n_grp=n_grp),
        grid=(n_batch, f.shape[1] // FOURIER_LANES),
        in_specs=u_specs + table_specs + [pl.BlockSpec(memory_space=pl.ANY)],
        out_specs=pl.BlockSpec((seq, FOURIER_LANES), lambda b, g: (b, g)),
        out_shape=jax.ShapeDtypeStruct(dst.shape, dst.dtype),
        input_output_aliases={n_grp + len(tables): 0},
        scratch_shapes=[pltpu.VMEM((n_grp, 2 * seq, LANES), F32), pltpu.VMEM((n_grp, seq, LANES), F32),
                        pltpu.VMEM((n_grp, seq, LANES), F32)],
        compiler_params=_cparams("parallel", "parallel"),
        name="fourier",
    )(*([f] * n_grp), *tables, dst)


def _dense_fourier_tables(n):
    norm = 1.0 / math.sqrt(n * LANES)
    p = np.arange(n)
    ph = 2.0 * np.pi * ((p[:, None] * p[None, :]) % n) / n
    ch = np.arange(LANES)
    pc = 2.0 * np.pi * ((ch[:, None] * ch[None, :]) % LANES) / LANES
    return (jnp.asarray(np.cos(ph).astype(np.float32)), jnp.asarray(np.sin(ph).astype(np.float32)),
            jnp.asarray((np.cos(pc) * norm).astype(np.float32)), jnp.asarray((np.sin(pc) * norm).astype(np.float32)))


def _dense_fourier_kernel(u_ref, cl_ref, sl_ref, cc_ref, sc_ref, _, o_ref):
    u = u_ref[...]
    y = _dot_hi(cl_ref[...], _dot_hi(u, cc_ref[...])) - _dot_hi(sl_ref[...], _dot_hi(u, sc_ref[...]))
    o_ref[...] = y.astype(o_ref.dtype)


def _dense_fourier(f, tables, dst, *, n_batch, n_pos, row_block0):
    cl, sl, cc, sc = tables
    groups = f.shape[1] // LANES
    return pl.pallas_call(
        _dense_fourier_kernel,
        grid=(n_batch, groups),
        in_specs=[
            pl.BlockSpec((n_pos, LANES), lambda b, g: (row_block0 + b, g)),
            pl.BlockSpec(cl.shape, lambda b, g: (0, 0)),
            pl.BlockSpec(sl.shape, lambda b, g: (0, 0)),
            pl.BlockSpec(cc.shape, lambda b, g: (0, 0)),
            pl.BlockSpec(sc.shape, lambda b, g: (0, 0)),
            pl.BlockSpec(memory_space=pl.ANY),
        ],
        out_specs=pl.BlockSpec((n_pos, LANES), lambda b, g: (row_block0 + b, g)),
        out_shape=jax.ShapeDtypeStruct(dst.shape, dst.dtype),
        input_output_aliases={5: 0},
        compiler_params=_cparams("parallel", "parallel"),
        name="fourier_ctx",
    )(f, cl, sl, cc, sc, dst)


def _win_kernel(sink_ref, q_ref, *refs, n_qtiles, has_local, n_heads, group_size):
    if has_local:
        kp_ref, kc_ref, kn_ref, vp_ref, vc_ref, vn_ref, kx_ref, vx_ref, _, o_ref = refs
    else:
        kx_ref, vx_ref, _, o_ref = refs
    tq = q_ref.shape[0]
    lane = lax.broadcasted_iota(jnp.int32, (1, LANES), 1)
    half_mask = [lane < HEAD_DIM, lane >= HEAD_DIM]
    pieces = [(kx_ref[...], vx_ref[...], None)]
    if has_local:
        n = pl.program_id(1)
        qi = lax.broadcasted_iota(jnp.int32, (tq, BLOCK), 0)
        kj = lax.broadcasted_iota(jnp.int32, (tq, BLOCK), 1)
        valid_prev = (kj >= qi) & (n >= 1)
        valid_next = (kj <= qi - (tq - WINDOW)) & (n <= n_qtiles - 2)
        di = lax.broadcasted_iota(jnp.int32, (tq, tq), 0) - lax.broadcasted_iota(jnp.int32, (tq, tq), 1)
        valid_mid = (di <= WINDOW) & (di >= -WINDOW)
        pieces += [(kp_ref[...], vp_ref[...], valid_prev), (kc_ref[...], vc_ref[...], valid_mid),
                   (kn_ref[...], vn_ref[...], valid_next)]
    for pair in range(n_heads // 2):
        qp = q_ref[:, pair * LANES:(pair + 1) * LANES].astype(F32)
        qp_sw = pltpu.roll(qp, HEAD_DIM, 1)
        out_pair = jnp.zeros((tq, LANES), F32)
        for half in range(2):
            head = 2 * pair + half
            kv = head // group_size
            src = qp if half == kv else qp_sw
            qe = jnp.where(half_mask[kv], src, 0.0).astype(BF16)
            sink = sink_ref[head] * LOG2E
            scores = []
            m = jnp.zeros((tq, 1), F32) + sink
            for k, _, valid in pieces:
                s = _dot_nt(qe, k)
                if valid is not None:
                    s = jnp.where(valid, s, NEG)
                scores.append(s)
                m = jnp.maximum(m, jnp.max(s, axis=-1, keepdims=True))
            den = jnp.exp2(sink - m)
            pv = jnp.zeros((tq, LANES), F32)
            for s, (_, v, _) in zip(scores, pieces):
                p = jnp.exp2(s - m)
                den = den + jnp.sum(p, axis=-1, keepdims=True)
                pv = pv + _dot(p.astype(BF16), v)
            pv = pv / den
            if half != kv:
                pv = pltpu.roll(pv, HEAD_DIM, 1)
            out_pair = jnp.where(half_mask[half], pv, out_pair)
        o_ref[:, pair * LANES:(pair + 1) * LANES] = out_pair.astype(o_ref.dtype)


def _win_attention(z, sink, dst, *, n_batch, seq, n_ctx, q_cols, k_col, v_col, tq):
    nbk = seq // BLOCK
    nq = seq // tq
    per = tq // BLOCK
    ctx_blk0 = n_batch * seq // n_ctx
    n_heads = q_cols // HEAD_DIM
    group_size = n_heads // (LANES // HEAD_DIM)

    def edge(col, first):
        return pl.BlockSpec(
            (BLOCK, LANES), lambda b, n: (b * nbk + jnp.clip(n * per + first, 0, nbk - 1), col))

    def mid(col):
        return pl.BlockSpec((tq, LANES), lambda b, n: (b * nq + n, col))

    return pl.pallas_call(
        functools.partial(_win_kernel, n_qtiles=nq, has_local=True, n_heads=n_heads, group_size=group_size),
        grid=(n_batch, nq),
        in_specs=[
            pl.BlockSpec(memory_space=pltpu.SMEM),
            pl.BlockSpec((tq, q_cols), lambda b, n: (b * nq + n, 0)),
            edge(k_col, -1), mid(k_col), edge(k_col, per),
            edge(v_col, -1), mid(v_col), edge(v_col, per),
            pl.BlockSpec((n_ctx, LANES), lambda b, n: (ctx_blk0 + b, k_col)),
            pl.BlockSpec((n_ctx, LANES), lambda b, n: (ctx_blk0 + b, v_col)),
            pl.BlockSpec(memory_space=pl.ANY),
        ],
        out_specs=pl.BlockSpec((tq, q_cols), lambda b, n: (b * nq + n, 0)),
        out_shape=jax.ShapeDtypeStruct(dst.shape, dst.dtype),
        input_output_aliases={10: 0},
        compiler_params=_cparams("parallel", "parallel"),
        name="win_attn",
    )(sink, z, z, z, z, z, z, z, z, z, dst)


def _ctx_gqa_attention(z, sink, dst, *, n_batch, seq, n_ctx, q_cols, k_col, v_col):
    ctx_blk0 = n_batch * seq // n_ctx
    n_heads = q_cols // HEAD_DIM
    group_size = n_heads // (LANES // HEAD_DIM)
    return pl.pallas_call(
        functools.partial(_win_kernel, n_qtiles=0, has_local=False, n_heads=n_heads, group_size=group_size),
        grid=(n_batch,),
        in_specs=[
            pl.BlockSpec(memory_space=pltpu.SMEM),
            pl.BlockSpec((n_ctx, q_cols), lambda b: (ctx_blk0 + b, 0)),
            pl.BlockSpec((n_ctx, LANES), lambda b: (ctx_blk0 + b, k_col)),
            pl.BlockSpec((n_ctx, LANES), lambda b: (ctx_blk0 + b, v_col)),
            pl.BlockSpec(memory_space=pl.ANY),
        ],
        out_specs=pl.BlockSpec((n_ctx, q_cols), lambda b: (ctx_blk0 + b, 0)),
        out_shape=jax.ShapeDtypeStruct(dst.shape, dst.dtype),
        input_output_aliases={4: 0},
        compiler_params=_cparams("parallel"),
        name="ctx_gqa",
    )(sink, z, z, z, dst)


LOG2E = math.log2(math.e)
DIFF_SUB_ROWS = 128
DIFF_KEY_CHUNK = 1024


def _diff_kernel(lam_ref, g_ref, q_ref, *refs, lam_init, has_x):
    if has_x:
        kx_ref, vtx_ref, kc_ref, vtc_ref, _, o_ref, s_scr = refs
        nx = kx_ref.shape[0]
    else:
        kc_ref, vtc_ref, _, o_ref, s_scr = refs
        nx = 0
    nc = kc_ref.shape[0]
    mc = s_scr.shape[2]
    sub = mc // 2
    n_sub = q_ref.shape[0] // sub
    lv = lam_ref[...]
    lam = (jnp.exp(jnp.sum(lv[0:1] * lv[1:2], axis=-1, keepdims=True))
           - jnp.exp(jnp.sum(lv[2:3] * lv[3:4], axis=-1, keepdims=True)) + lam_init)
    lane = lax.broadcasted_iota(jnp.int32, (1, LANES), 1)
    segs = []
    if has_x:
        kw = min(DIFF_KEY_CHUNK, nx)
        segs += [(kx_ref, vtx_ref, r0, kw, r0) for r0 in range(0, nx, kw)]
    segs.append((kc_ref, vtc_ref, 0, nc, nx))

    def scores(u):
        q = q_ref[u * sub:(u + 1) * sub, :]
        zero = jnp.zeros_like(q)
        qs = jnp.concatenate([jnp.where(lane < HEAD_DIM, q, zero), jnp.where(lane >= HEAD_DIM, q, zero)], axis=0)
        m_run = jnp.full((8, mc), -jnp.inf, F32)
        for k_ref, _, r0, w, s0 in segs:
            st = _dot_nt(k_ref[r0:r0 + w, :], qs)
            s_scr[u % 2, s0:s0 + w, :] = st
            m_run = jnp.maximum(m_run, jnp.max(st.reshape(w // 8, 8, mc), axis=0))
        return jnp.broadcast_to(jnp.max(m_run, axis=0, keepdims=True), (8, mc))

    def attend(u, m8):
        l_run = jnp.zeros((8, mc), F32)
        acc = jnp.zeros((LANES, mc), F32)
        for _, vt_ref, r0, w, s0 in segs:
            p = jnp.exp2(s_scr[u % 2, s0:s0 + w, :].reshape(w // 8, 8, mc) - m8[None])
            l_run = l_run + jnp.sum(p, axis=0)
            acc = acc + _dot(vt_ref[:, r0:r0 + w], p.reshape(w, mc).astype(BF16))
        ot = acc / jnp.sum(l_run, axis=0, keepdims=True)
        o = (ot[:, :sub] - lam * ot[:, sub:]).T
        o = _rms(o) * g_ref[...] * (1.0 - lam_init)
        o_ref[u * sub:(u + 1) * sub, :] = o.astype(o_ref.dtype)

    m_next = scores(0)
    for u in range(n_sub):
        m_cur = m_next
        if u + 1 < n_sub:
            m_next = scores(u + 1)
        attend(u, m_cur)


def _diff_attention(z, vt, lam_vec, subln_g, lam_init, dst, *, n_batch, seq, n_ctx, n_heads, tq):
    ctx_blk0 = n_batch * seq // n_ctx
    nq = seq // tq
    return pl.pallas_call(
        functools.partial(_diff_kernel, lam_init=lam_init, has_x=True),
        grid=(n_batch, n_heads, nq),
        in_specs=[
            pl.BlockSpec(lam_vec.shape, lambda b, h, i: (0, 0)),
            pl.BlockSpec((1, LANES), lambda b, h, i: (0, 0)),
            pl.BlockSpec((tq, LANES), lambda b, h, i: (b * nq + i, h)),
            pl.BlockSpec((seq, LANES), lambda b, h, i: (b, n_heads + h)),
            pl.BlockSpec((LANES, seq), lambda b, h, i: (h, b)),
            pl.BlockSpec((n_ctx, LANES), lambda b, h, i: (ctx_blk0 + b, n_heads + h)),
            pl.BlockSpec((LANES, n_ctx), lambda b, h, i: (h, ctx_blk0 + b)),
            pl.BlockSpec(memory_space=pl.ANY),
        ],
        out_specs=pl.BlockSpec((tq, LANES), lambda b, h, i: (b * nq + i, h)),
        out_shape=jax.ShapeDtypeStruct(dst.shape, dst.dtype),
        input_output_aliases={7: 0},
        scratch_shapes=[pltpu.VMEM((2, seq + n_ctx, 2 * min(tq, DIFF_SUB_ROWS)), F32)],
        compiler_params=_cparams("parallel", "parallel", "parallel"),
        name="diff_attn",
    )(lam_vec, subln_g, z, z, vt, z, vt, dst)


def _diff_attention_ctx(z, vt, lam_vec, subln_g, lam_init, dst, *, n_batch, seq, n_ctx, n_heads):
    ctx_blk0 = n_batch * seq // n_ctx
    return pl.pallas_call(
        functools.partial(_diff_kernel, lam_init=lam_init, has_x=False),
        grid=(n_batch, n_heads),
        in_specs=[
            pl.BlockSpec(lam_vec.shape, lambda b, h: (0, 0)),
            pl.BlockSpec((1, LANES), lambda b, h: (0, 0)),
            pl.BlockSpec((n_ctx, LANES), lambda b, h: (ctx_blk0 + b, h)),
            pl.BlockSpec((n_ctx, LANES), lambda b, h: (ctx_blk0 + b, n_heads + h)),
            pl.BlockSpec((LANES, n_ctx), lambda b, h: (h, ctx_blk0 + b)),
            pl.BlockSpec(memory_space=pl.ANY),
        ],
        out_specs=pl.BlockSpec((n_ctx, LANES), lambda b, h: (ctx_blk0 + b, h)),
        out_shape=jax.ShapeDtypeStruct(dst.shape, dst.dtype),
        input_output_aliases={5: 0},
        scratch_shapes=[pltpu.VMEM((2, n_ctx, 2 * min(n_ctx, DIFF_SUB_ROWS)), F32)],
        compiler_params=_cparams("parallel", "parallel"),
        name="diff_attn_ctx",
    )(lam_vec, subln_g, z, z, vt, dst)


def _outproj_kernel(a0_ref, a1_ref, w_ref, x_ref, g_ref, gate_ref, o_ref):
    half = a0_ref.shape[1]
    y = _dot(a0_ref[...], w_ref[:half, :]) + _dot(a1_ref[...], w_ref[half:, :])
    o_ref[...] = x_ref[...] + gate_ref[...] * (_rms(y) * g_ref[...])


def _outproj(a0, a0_col, a1, a1_col, w, xt, normg, mod, k_gate, *, tm, rows, seq, n_batch):
    d = xt.shape[1]
    half = d // 2

    def mrow(i):
        return jnp.minimum(i * tm // seq, n_batch)

    return pl.pallas_call(
        _outproj_kernel,
        grid=(rows // tm,),
        in_specs=[
            pl.BlockSpec((tm, half), lambda i: (i, a0_col)),
            pl.BlockSpec((tm, half), lambda i: (i, a1_col)),
            pl.BlockSpec((d, d), lambda i: (0, 0)),
            pl.BlockSpec((tm, d), lambda i: (i, 0)),
            pl.BlockSpec((None, 1, d), lambda i: (1, 0, 0)),
            pl.BlockSpec((None, None, 1, d), lambda i: (mrow(i), k_gate, 0, 0)),
        ],
        out_specs=pl.BlockSpec((tm, d), lambda i: (i, 0)),
        out_shape=jax.ShapeDtypeStruct((rows, d), F32),
        compiler_params=_cparams("parallel"),
        name="outproj",
    )(a0, a1, w, xt, normg, mod)


def _router_kernel(x_ref, g_ref, sh_ref, sc_ref, rw_ref, o_ref, sel_ref):
    h = _modulate(x_ref[...], g_ref[...], sh_ref[...], sc_ref[...])
    logits = _dot_split(_split_bf16(h), _split_bf16(rw_ref[...]))
    lane = lax.broadcasted_iota(jnp.int32, logits.shape, 1)
    ninf = -jnp.inf
    logits = jnp.where(lane < N_EXPERTS, logits, ninf)
    m1 = jnp.max(logits, axis=-1, keepdims=True)
    i1 = jnp.min(jnp.where(logits == m1, lane, LANES), axis=-1, keepdims=True)
    sel1 = lane == i1
    rest = jnp.where(sel1, ninf, logits)
    m2 = jnp.max(rest, axis=-1, keepdims=True)
    i2 = jnp.min(jnp.where(rest == m2, lane, LANES), axis=-1, keepdims=True)
    sel2 = lane == i2
    e2 = jnp.exp(m2 - m1)
    den = 1.0 + e2
    o_ref[...] = jnp.where(sel1, 1.0 / den, 0.0) + jnp.where(sel2, e2 / den, 0.0)
    sel_ref[...] = jnp.where(sel1 | sel2, 1.0, 0.0)


def _router(xt, normg, mod, router_w, *, tm, rows, seq, n_batch):
    d = xt.shape[1]
    rw = jnp.zeros((d, LANES), F32).at[:, :N_EXPERTS].set(router_w)

    def mrow(i):
        return jnp.minimum(i * tm // seq, n_batch)

    return pl.pallas_call(
        _router_kernel,
        grid=(rows // tm,),
        in_specs=[
            pl.BlockSpec((tm, d), lambda i: (i, 0)),
            pl.BlockSpec((None, 1, d), lambda i: (2, 0, 0)),
            pl.BlockSpec((None, None, 1, d), lambda i: (mrow(i), 3, 0, 0)),
            pl.BlockSpec((None, None, 1, d), lambda i: (mrow(i), 4, 0, 0)),
            pl.BlockSpec((d, LANES), lambda i: (0, 0)),
        ],
        out_specs=[pl.BlockSpec((tm, LANES), lambda i: (i, 0))] * 2,
        out_shape=[jax.ShapeDtypeStruct((rows, LANES), F32)] * 2,
        compiler_params=_cparams("parallel"),
        name="router",
    )(xt, normg, mod, mod, rw)


SWIGLU_CHUNK = 256


def _swiglu(h, wg_ref, wu_ref, wo_ref, u_off):
    width = wo_ref.shape[0]
    acc = None
    for c0 in range(0, width, SWIGLU_CHUNK):
        gp = _dot(h, wg_ref[:, c0:c0 + SWIGLU_CHUNK])
        up = _dot(h, wu_ref[:, u_off + c0:u_off + c0 + SWIGLU_CHUNK])
        a = gp * (1.0 / (1.0 + jnp.exp(-gp))) * up
        part = _dot(a.astype(BF16), wo_ref[c0:c0 + SWIGLU_CHUNK, :])
        acc = part if acc is None else acc + part
    return acc


def _ffn_kernel(x_ref, g2_ref, sh_ref, sc_ref, win_ref, wout_ref, g3_ref, gate_ref, o_ref):
    x = x_ref[...]
    h = _modulate(x, g2_ref[...], sh_ref[...], sc_ref[...]).astype(BF16)
    y = _swiglu(h, win_ref, win_ref, wout_ref, wout_ref.shape[0])
    o_ref[...] = x + gate_ref[...] * (_rms(y) * g3_ref[...])


def _ffn(xt, normg, mod, w_in, w_out, *, tm, rows, seq, n_batch):
    d = xt.shape[1]

    def mrow(i):
        return jnp.minimum(i * tm // seq, n_batch)

    in_specs = [
        pl.BlockSpec((tm, d), lambda i: (i, 0)),
        pl.BlockSpec((None, 1, d), lambda i: (2, 0, 0)),
        pl.BlockSpec((None, None, 1, d), lambda i: (mrow(i), 3, 0, 0)),
        pl.BlockSpec((None, None, 1, d), lambda i: (mrow(i), 4, 0, 0)),
        pl.BlockSpec(w_in.shape, lambda i: (0, 0)),
        pl.BlockSpec(w_out.shape, lambda i: (0, 0)),
        pl.BlockSpec((None, 1, d), lambda i: (3, 0, 0)),
        pl.BlockSpec((None, None, 1, d), lambda i: (mrow(i), 5, 0, 0)),
    ]
    return pl.pallas_call(
        _ffn_kernel,
        grid=(rows // tm,),
        in_specs=in_specs,
        out_specs=pl.BlockSpec((tm, d), lambda i: (i, 0)),
        out_shape=jax.ShapeDtypeStruct((rows, d), F32),
        compiler_params=_cparams("parallel"),
        name="ffn",
    )(xt, normg, mod, mod, w_in, w_out, normg, mod)


MOE_DISPATCH_ROWS = 256
MOE_COMBINE_ROWS = 512
MOE_SEG_ALIGN = 16
MOE_TILE = 512


def _moe_keys(sel):
    nb = sel.shape[0]
    ti = lax.broadcasted_iota(jnp.int32, (nb, nb), 0)
    tj = lax.broadcasted_iota(jnp.int32, (nb, nb), 1)
    lower = jnp.where(tj < ti, 1.0, 0.0).astype(BF16)
    rank = _dot(lower, sel.astype(BF16))
    return jnp.where(sel > 0.0, rank, -1.0)


def _moe_dispatch_kernel(nchunk_ref, off_ref, x_ref, g2_ref, sh_ref, sc_ref, sel_ref, _, hs_hbm,
                         h_scr, keyt_scr, stage, sems, *, n_exp):
    b = pl.program_id(0)
    rows = MOE_DISPATCH_ROWS
    h_scr[...] = _modulate(x_ref[...], g2_ref[...], sh_ref[...], sc_ref[...]).astype(BF16)
    keyt_scr[...] = _moe_keys(sel_ref[...]).T

    def copy(slot, row0):
        return pltpu.make_async_copy(stage.at[slot], hs_hbm.at[pl.ds(row0, rows), :], sems.at[slot])

    issued = jnp.int32(0)
    for e in range(n_exp):
        key_row = keyt_scr[e:e + 1, :]
        base = off_ref[b * n_exp + e]

        def chunk(c, k):
            slot = k % 2

            @pl.when(k >= 2)
            def _():
                copy(slot, 0).wait()

            r = (c * rows + lax.broadcasted_iota(jnp.int32, (rows, 1), 0)).astype(F32)
            onehot = jnp.where(key_row == r, 1.0, 0.0).astype(BF16)
            stage[slot] = _dot(onehot, h_scr[...]).astype(BF16)
            copy(slot, pl.multiple_of(base + c * rows, MOE_SEG_ALIGN)).start()
            return k + 1

        issued = lax.fori_loop(0, nchunk_ref[b * n_exp + e], chunk, issued)

    @pl.when(issued >= 1)
    def _():
        copy((issued - 1) % 2, 0).wait()

    @pl.when(issued >= 2)
    def _():
        copy(issued % 2, 0).wait()


def _moe_experts_kernel(tile_ref, exp_ref, valid_ref, hs_ref, wg_ref, wu_ref, wo_ref, y_ref, acc, *, n_f):
    t = pl.program_id(0)
    f = pl.program_id(1)

    @pl.when(valid_ref[t] == 1)
    def _():
        part = _swiglu(hs_ref[...], wg_ref, wu_ref, wo_ref, 0)

        @pl.when(f == 0)
        def _():
            acc[...] = part

        @pl.when(f > 0)
        def _():
            acc[...] += part

        @pl.when(f == n_f - 1)
        def _():
            y_ref[...] = acc[...].astype(y_ref.dtype)


def _moe_combine_kernel(npiece_ref, off_ref, x_ref, g3_ref, gate_ref, comb_ref, sel_ref, y_hbm, o_ref,
                        key_scr, stage, sems, *, n_exp):
    b = pl.program_id(0)
    rows = MOE_COMBINE_ROWS
    nb = x_ref.shape[0]
    key_scr[...] = _moe_keys(sel_ref[...])
    o_ref[...] = jnp.zeros_like(o_ref)
    lane = lax.broadcasted_iota(jnp.int32, (nb, LANES), 1)

    def copy(slot, row0):
        return pltpu.make_async_copy(y_hbm.at[pl.ds(row0, rows), :], stage.at[slot], sems.at[slot])

    def row0(e, c):
        return pl.multiple_of(off_ref[b * n_exp + e] + c * rows, MOE_SEG_ALIGN)

    copy(0, row0(0, 0)).start()
    done = jnp.int32(0)
    for e in range(n_exp):
        pick = lane == e
        key_col = jnp.sum(jnp.where(pick, key_scr[...], 0.0), axis=-1, keepdims=True)
        gate_col = jnp.sum(jnp.where(pick, comb_ref[...], 0.0), axis=-1, keepdims=True)
        n_pieces = npiece_ref[b * n_exp + e]

        def piece(c, k, e=e, n_pieces=n_pieces, key_col=key_col, gate_col=gate_col):
            slot = k % 2
            copy(slot, 0).wait()

            @pl.when(c + 1 < n_pieces)
            def _():
                copy(1 - slot, row0(e, c + 1)).start()

            if e + 1 < n_exp:
                @pl.when(c + 1 == n_pieces)
                def _():
                    copy(1 - slot, row0(e + 1, 0)).start()

            r = (c * rows + lax.broadcasted_iota(jnp.int32, (1, rows), 1)).astype(F32)
            onehot = jnp.where(key_col == r, 1.0, 0.0).astype(BF16)
            o_ref[...] += gate_col * _dot(onehot, stage[slot])
            return k + 1

        done = lax.fori_loop(0, n_pieces, piece, done)

    o_ref[...] = x_ref[...] + gate_ref[...] * (_rms(o_ref[...]) * g3_ref[...])


def _moe(xt, normg, mod, w_in, w_out, comb, sel, *, nb, tf, rows, seq, n_batch):
    d = xt.shape[1]
    n_exp, fdim = w_out.shape[0], w_out.shape[1]
    n_f = fdim // tf
    n_blk = rows // nb
    i32 = jnp.int32

    counts = jnp.sum(sel.reshape(n_blk, nb, LANES)[:, :, :n_exp], axis=1).astype(i32)
    seg = -(-counts // MOE_SEG_ALIGN) * MOE_SEG_ALIGN
    region = -(-(jnp.sum(seg, axis=0) + MOE_DISPATCH_ROWS) // MOE_TILE) * MOE_TILE
    region_end = jnp.cumsum(region)
    region_start = region_end - region
    off = (region_start[None, :] + jnp.cumsum(seg, axis=0) - seg).reshape(n_blk * n_exp).astype(i32)
    nchunk = (-(-counts // MOE_DISPATCH_ROWS)).reshape(n_blk * n_exp).astype(i32)
    npiece = jnp.maximum(-(-counts // MOE_COMBINE_ROWS), 1).reshape(n_blk * n_exp).astype(i32)
    cap = 2 * rows + n_blk * n_exp * (MOE_SEG_ALIGN - 1) + n_exp * (MOE_DISPATCH_ROWS + MOE_TILE - 1)
    n_tiles = -(-cap // MOE_TILE)
    cap = n_tiles * MOE_TILE + MOE_COMBINE_ROWS
    tiles = jnp.arange(n_tiles, dtype=i32)
    n_used = region_end[-1] // MOE_TILE
    tile_map = jnp.minimum(tiles, n_used - 1).astype(i32)
    tile_exp = jnp.minimum(jnp.searchsorted(region_end, tile_map * MOE_TILE, side="right"), n_exp - 1).astype(i32)
    tile_valid = (tiles < n_used).astype(i32)

    def mrow(i):
        return jnp.minimum(i * nb // seq, n_batch)

    hs = pl.pallas_call(
        functools.partial(_moe_dispatch_kernel, n_exp=n_exp),
        grid_spec=pltpu.PrefetchScalarGridSpec(
            num_scalar_prefetch=2,
            grid=(n_blk,),
            in_specs=[
                pl.BlockSpec((nb, d), lambda i, *_: (i, 0)),
                pl.BlockSpec((None, 1, d), lambda i, *_: (2, 0, 0)),
                pl.BlockSpec((None, None, 1, d), lambda i, *_: (mrow(i), 3, 0, 0)),
                pl.BlockSpec((None, None, 1, d), lambda i, *_: (mrow(i), 4, 0, 0)),
                pl.BlockSpec((nb, LANES), lambda i, *_: (i, 0)),
                pl.BlockSpec(memory_space=pl.ANY),
            ],
            out_specs=pl.BlockSpec(memory_space=pl.ANY),
            scratch_shapes=[
                pltpu.VMEM((nb, d), BF16),
                pltpu.VMEM((LANES, nb), F32),
                pltpu.VMEM((2, MOE_DISPATCH_ROWS, d), BF16),
                pltpu.SemaphoreType.DMA((2,)),
            ],
        ),
        out_shape=jax.ShapeDtypeStruct((cap, d), BF16),
        input_output_aliases={7: 0},
        compiler_params=_cparams("arbitrary"),
        name="moe_dispatch",
    )(nchunk, off, xt, normg, mod, mod, sel, jnp.zeros((cap, d), BF16))

    ys = pl.pallas_call(
        functools.partial(_moe_experts_kernel, n_f=n_f),
        grid_spec=pltpu.PrefetchScalarGridSpec(
            num_scalar_prefetch=3,
            grid=(n_tiles, n_f),
            in_specs=[
                pl.BlockSpec((MOE_TILE, d), lambda t, f, tm, te, tv: (tm[t], 0)),
                pl.BlockSpec((None, d, tf), lambda t, f, tm, te, tv: (te[t], 0, f)),
                pl.BlockSpec((None, d, tf), lambda t, f, tm, te, tv: (te[t], 0, n_f + f)),
                pl.BlockSpec((None, tf, d), lambda t, f, tm, te, tv: (te[t], f, 0)),
            ],
            out_specs=pl.BlockSpec((MOE_TILE, d), lambda t, f, tm, te, tv: (tm[t], 0)),
            scratch_shapes=[pltpu.VMEM((MOE_TILE, d), F32)],
        ),
        out_shape=jax.ShapeDtypeStruct((cap, d), BF16),
        input_output_aliases={3: 0},
        compiler_params=_cparams("arbitrary", "arbitrary"),
        name="moe_experts",
    )(tile_map, tile_exp, tile_valid, hs, w_in, w_in, w_out)

    return pl.pallas_call(
        functools.partial(_moe_combine_kernel, n_exp=n_exp),
        grid_spec=pltpu.PrefetchScalarGridSpec(
            num_scalar_prefetch=2,
            grid=(n_blk,),
            in_specs=[
                pl.BlockSpec((nb, d), lambda i, *_: (i, 0)),
                pl.BlockSpec((None, 1, d), lambda i, *_: (3, 0, 0)),
                pl.BlockSpec((None, None, 1, d), lambda i, *_: (mrow(i), 5, 0, 0)),
                pl.BlockSpec((nb, LANES), lambda i, *_: (i, 0)),
                pl.BlockSpec((nb, LANES), lambda i, *_: (i, 0)),
                pl.BlockSpec(memory_space=pl.ANY),
            ],
            out_specs=pl.BlockSpec((nb, d), lambda i, *_: (i, 0)),
            scratch_shapes=[
                pltpu.VMEM((nb, LANES), F32),
                pltpu.VMEM((2, MOE_COMBINE_ROWS, d), BF16),
                pltpu.SemaphoreType.DMA((2,)),
            ],
        ),
        out_shape=jax.ShapeDtypeStruct((rows, d), F32),
        compiler_params=_cparams("parallel"),
        name="moe_combine",
    )(npiece, off, xt, normg, mod, comb, sel, ys)


def _lambda_init(layer):
    return 0.8 - 0.6 * math.exp(-0.3 * layer)


def kernel(x, c, ctx, c_ctx, ada_w, ada_b, norm_g, mix_in_w, mix_out_w, win_sink, diff_qkv_w, diff_out_w,
           diff_lambda, diff_subln_g, ffn_in_w, ffn_out_w, router_w, expert_in_w, expert_out_w):
    n_batch, seq, d = x.shape
    n_ctx = ctx.shape[1]
    depth = ada_w.shape[0]
    n_lat = n_batch * seq
    n_all = n_lat + n_batch * n_ctx
    fdim = mix_in_w.shape[2] - (d // 2 + 2 * LANES)
    q_cols = d // 2
    n_diff_heads = d // LANES

    tm_proj = _pick_tile(512, seq, n_batch * n_ctx)
    tm_out = _pick_tile(512, seq, n_batch * n_ctx)
    tm_ffn = _pick_tile(512, seq, n_batch * n_ctx)
    nb_moe = _pick_tile(1024, seq, n_batch * n_ctx)
    tq_diff = _pick_tile(1024, seq)
    tq_win = _pick_tile(512, seq)
    common = dict(seq=seq, n_batch=n_batch)

    xt = jnp.concatenate([x.reshape(n_lat, d), ctx.reshape(n_batch * n_ctx, d)], axis=0)
    n_mod = -(-(n_batch + 1) // 8) * 8
    cv = jnp.zeros((n_mod, d), F32).at[:n_batch].set(c).at[n_batch].set(c_ctx)
    mods = _modvec(cv, ada_w, ada_b).reshape(depth, n_mod, 6, 1, d)
    rope = _rope_tables(seq, tm_proj)
    f_tables = _fourier_tables(seq)
    fc_tables = _dense_fourier_tables(n_ctx)

    n_f = fdim // LANES
    plan_even = ([(0, i * LANES, "plain") for i in range(n_f)]
                 + [(1, i * LANES, "rope_q_log2") for i in range(q_cols // LANES)]
                 + [(1, q_cols, "rope_k"), (1, q_cols + LANES, "plain")])
    plan_odd = ([(0, i * LANES, "rope_q_log2") for i in range(n_diff_heads)]
                + [(0, d + i * LANES, "rope_k") for i in range(n_diff_heads)]
                + [(1, i * LANES, "plain_t") for i in range(n_diff_heads)])

    for layer in range(depth):
        j = layer // 2
        need_ctx = layer < depth - 1
        rows = n_all if need_ctx else n_lat
        mod = mods[layer]
        ng = norm_g[layer].reshape(4, 1, d)
        if layer % 2 == 0:
            f, z = _proj(xt, ng, mod, 0, 1, mix_in_w[j].astype(BF16), rope, plan_even,
                         [(fdim, F32, False), (q_cols + 2 * LANES, BF16, False)], tm=tm_proj, n_lat=n_lat,
                         **common)
            k_col, v_col = q_cols // LANES, q_cols // LANES + 1
            att = dict(n_batch=n_batch, seq=seq, n_ctx=n_ctx, q_cols=q_cols, k_col=k_col, v_col=v_col)
            mix_f = _fourier(f, f_tables, jnp.zeros((rows, fdim), BF16), n_batch=n_batch, seq=seq)
            mix_a = _win_attention(z, win_sink[j], jnp.zeros((rows, q_cols), BF16), tq=tq_win, **att)
            if need_ctx:
                mix_f = _dense_fourier(f, fc_tables, mix_f, n_batch=n_batch, n_pos=n_ctx,
                                       row_block0=n_lat // n_ctx)
                mix_a = _ctx_gqa_attention(z, win_sink[j], mix_a, **att)
            xt = _outproj(mix_f, 0, mix_a, 0, mix_out_w[j].astype(BF16), xt, ng, mod, 2, tm=tm_out, rows=rows,
                          **common)
            xt = _ffn(xt, ng, mod, ffn_in_w[j].astype(BF16), ffn_out_w[j].astype(BF16),
                      tm=tm_ffn, rows=rows, **common)
        else:
            lam_init = _lambda_init(layer)
            z, vt = _proj(xt, ng, mod, 0, 1, diff_qkv_w[j].astype(BF16), rope, plan_odd,
                          [(2 * d, BF16, False), (d, BF16, True)], tm=tm_proj, n_lat=n_lat, **common)
            subg = diff_subln_g[j].reshape(1, LANES)
            att = dict(n_batch=n_batch, seq=seq, n_ctx=n_ctx, n_heads=n_diff_heads)
            mix = _diff_attention(z, vt, diff_lambda[j], subg, lam_init, jnp.zeros((rows, d), BF16), tq=tq_diff,
                                  **att)
            if need_ctx:
                mix = _diff_attention_ctx(z, vt, diff_lambda[j], subg, lam_init, mix, **att)
            xt = _outproj(mix, 0, mix, 1, diff_out_w[j].astype(BF16), xt, ng, mod, 2, tm=tm_out, rows=rows,
                          **common)
            comb, sel = _router(xt, ng, mod, router_w[j], tm=tm_out, rows=rows, **common)
            xt = _moe(xt, ng, mod, expert_in_w[j].astype(BF16), expert_out_w[j].astype(BF16), comb, sel,
                      nb=nb_moe, tf=1792, rows=rows, **common)
    return xt[:n_lat].reshape(n_batch, seq, d)
```

```python
import functools
import math

import numpy as np
import jax
import jax.numpy as jnp
from jax import lax
from jax.experimental import pallas as pl
from jax.experimental.pallas import tpu as pltpu

EPS = 1e-6
NEG = -1e30
HEAD_DIM = 64
LANES = 128
GRID_W = 64
BLOCK = 128
WINDOW = 128
ROPE_THETA = 10000.0
N_EXPERTS = 8
F32 = jnp.float32
BF16 = jnp.bfloat16
HIGHEST = lax.Precision.HIGHEST
VMEM_LIMIT = 56 * 1024 * 1024


def _cparams(*sem):
    return pltpu.CompilerParams(dimension_semantics=sem, vmem_limit_bytes=VMEM_LIMIT)


def _dot(a, b):
    return jnp.dot(a, b, preferred_element_type=F32)


def _dot_nt(a, b):
    return lax.dot_general(a, b, (((1,), (1,)), ((), ())), preferred_element_type=F32)


def _dot_hi(a, b):
    return jnp.dot(a, b, precision=HIGHEST, preferred_element_type=F32)


def _rms(v):
    return v * lax.rsqrt(jnp.mean(v * v, axis=-1, keepdims=True) + EPS)


def _modulate(x, g, sh, sc):
    return _rms(x) * g * (1.0 + sc) + sh


def _pick_tile(pref, *dims):
    t = pref
    while any(d % t for d in dims):
        t //= 2
    return t


def _modvec_kernel(c_ref, w_ref, b_ref, o_ref):
    cv = c_ref[...]
    s = cv * (1.0 / (1.0 + jnp.exp(-cv)))
    o_ref[...] = _dot(s.astype(BF16), w_ref[...].astype(BF16)) + b_ref[...]


def _modvec(cv, ada_w, ada_b):
    depth, d, n = ada_w.shape
    r = cv.shape[0]
    tn = _pick_tile(1536, n)
    return pl.pallas_call(
        _modvec_kernel,
        grid=(depth, n // tn),
        in_specs=[
            pl.BlockSpec((r, d), lambda l, j: (0, 0)),
            pl.BlockSpec((None, d, tn), lambda l, j: (l, 0, j)),
            pl.BlockSpec((None, 1, tn), lambda l, j: (l, 0, j)),
        ],
        out_specs=pl.BlockSpec((None, r, tn), lambda l, j: (l, 0, j)),
        out_shape=jax.ShapeDtypeStruct((depth, r, n), F32),
        compiler_params=_cparams("parallel", "parallel"),
        name="modvec",
    )(cv, ada_w, ada_b.reshape(depth, 1, n))


def _proj_kernel(x_ref, g_ref, sh_ref, sc_ref, w_ref, cos_ref, sa_ref, sb_ref, *o_refs, plan, group):
    h = _modulate(x_ref[...], g_ref[...], sh_ref[...], sc_ref[...]).astype(BF16)
    n = w_ref.shape[1]
    for g0 in range(0, n, group):
        acc = _dot(h, w_ref[:, g0:g0 + group])
        for c0 in range(0, group, LANES):
            oi, oc, mode = plan[(g0 + c0) // LANES]
            v = acc[:, c0:c0 + LANES]
            if mode.startswith("rope"):
                v = (v * cos_ref[...] + pltpu.roll(v, LANES - 16, 1) * sa_ref[...]
                     + pltpu.roll(v, 16, 1) * sb_ref[...])
                if mode == "rope_q_log2":
                    v = v * (HEAD_DIM ** -0.5 * LOG2E)
            if mode == "plain_t":
                o_refs[oi][oc:oc + LANES, :] = v.T.astype(o_refs[oi].dtype)
            else:
                o_refs[oi][:, oc:oc + LANES] = v.astype(o_refs[oi].dtype)


def _proj(xt, normg, mod, k_sh, k_sc, w, rope, plan, outs, *, tm, n_lat, seq, n_batch):
    t, d = xt.shape
    n = w.shape[1]
    group = _pick_tile(512, n)
    nx = n_lat // tm
    per = seq // tm

    def mrow(i):
        return jnp.minimum(i * tm // seq, n_batch)

    def rrow(i):
        return jnp.where(i < nx, i % per, per)

    in_specs = [
        pl.BlockSpec((tm, d), lambda i: (i, 0)),
        pl.BlockSpec((None, 1, d), lambda i: (0, 0, 0)),
        pl.BlockSpec((None, None, 1, d), lambda i: (mrow(i), k_sh, 0, 0)),
        pl.BlockSpec((None, None, 1, d), lambda i: (mrow(i), k_sc, 0, 0)),
        pl.BlockSpec((d, n), lambda i: (0, 0)),
        pl.BlockSpec((tm, LANES), lambda i: (rrow(i), 0)),
        pl.BlockSpec((tm, LANES), lambda i: (rrow(i), 0)),
        pl.BlockSpec((tm, LANES), lambda i: (rrow(i), 0)),
    ]
    out_specs = [pl.BlockSpec((wd, tm), lambda i: (0, i)) if tr else pl.BlockSpec((tm, wd), lambda i: (i, 0))
                 for wd, _, tr in outs]
    out_shape = [jax.ShapeDtypeStruct((wd, t) if tr else (t, wd), dt) for wd, dt, tr in outs]
    return pl.pallas_call(
        functools.partial(_proj_kernel, plan=plan, group=group),
        grid=(t // tm,),
        in_specs=in_specs,
        out_specs=out_specs,
        out_shape=out_shape,
        compiler_params=_cparams("parallel"),
        name="proj",
    )(xt, normg, mod, mod, w, *rope)


def _rope_tables(seq, tm):
    rows_count = seq // GRID_W
    rows = jnp.repeat(jnp.arange(rows_count), GRID_W).astype(F32)
    cols = jnp.tile(jnp.arange(GRID_W), rows_count).astype(F32)
    axis_dim = HEAD_DIM // 2
    inv = ROPE_THETA ** (-jnp.arange(0, axis_dim, 2, dtype=F32) / axis_dim)
    ar = rows[:, None] * inv
    ac = cols[:, None] * inv
    cr, sr, cc, sc = jnp.cos(ar), jnp.sin(ar), jnp.cos(ac), jnp.sin(ac)
    z = jnp.zeros_like(sr)
    reps = LANES // HEAD_DIM
    cos = jnp.tile(jnp.concatenate([cr, cr, cc, cc], axis=1), (1, reps))
    sa = jnp.tile(jnp.concatenate([-sr, z, -sc, z], axis=1), (1, reps))
    sb = jnp.tile(jnp.concatenate([z, sr, z, sc], axis=1), (1, reps))
    ident = jnp.ones((tm, LANES), F32)
    zero = jnp.zeros((tm, LANES), F32)
    return (jnp.concatenate([cos, ident]), jnp.concatenate([sa, zero]), jnp.concatenate([sb, zero]))


FOURIER_LANES = 2 * LANES


def _split_bf16(t):
    hi = t.astype(BF16)
    return hi, (t - hi.astype(F32)).astype(BF16)


def _dot_split(a, b):
    return _dot(a[0], b[0]) + _dot(a[0], b[1]) + _dot(a[1], b[0])


def _fourier_tables(seq):
    n2 = GRID_W
    n1 = seq // n2
    norm = 1.0 / math.sqrt(seq * LANES)
    a = np.arange(n1)
    k1 = np.arange(n1)
    b = np.arange(n2)
    ang = (b[:, None, None] * k1[None, :, None] + (seq // n1) * k1[None, :, None] * a[None, None, :]) % seq
    th = 2.0 * np.pi * ang / seq
    m1 = np.concatenate([np.cos(th), -np.sin(th)], axis=1).astype(np.float32)
    ph = 2.0 * np.pi * ((b[:, None] * b[None, :]) % n2) / n2
    c2, s2 = np.cos(ph), np.sin(ph)
    g2 = np.block([[c2, s2], [-s2, c2]]).astype(np.float32)
    ch = np.arange(LANES)
    pc = 2.0 * np.pi * ((ch[:, None] * ch[None, :]) % LANES) / LANES
    cc = (np.cos(pc) * norm).astype(np.float32)
    sc = (np.sin(pc) * norm).astype(np.float32)
    cs = np.concatenate([cc, sc], axis=0)
    out = []
    for t in (m1, g2, cs):
        out.extend(_split_bf16(jnp.asarray(t)))
    return tuple(out)


FOURIER_ROW_CHUNK = 512


def _fourier_kernel(*refs, n1, n2, n_grp):
    u_refs = refs[:n_grp]
    m1h_ref, m1l_ref, g2h_ref, g2l_ref, csh_ref, csl_ref, _, o_ref, b_scr, xr_scr, xi_scr = refs[n_grp:]
    seq = n1 * n2
    for b in range(n2):
        xs = jnp.concatenate([u[pl.ds(b, n1, stride=n2), :] for u in u_refs], axis=1)
        z = _dot_split((m1h_ref[b], m1l_ref[b]), _split_bf16(xs))
        for g in range(n_grp):
            b_scr[g, 2 * n1 * b:2 * n1 * (b + 1), :] = z[:, g * LANES:(g + 1) * LANES]
    g2 = (g2h_ref[...], g2l_ref[...])
    for k1 in range(n1):
        bk = jnp.concatenate(
            [jnp.concatenate([b_scr.at[g][pl.ds(k1, n2, stride=2 * n1), :],
                              b_scr.at[g][pl.ds(n1 + k1, n2, stride=2 * n1), :]], axis=0)
             for g in range(n_grp)], axis=1)
        xk = _dot_split(g2, _split_bf16(bk))
        for g in range(n_grp):
            xr_scr[g, n2 * k1:n2 * (k1 + 1), :] = xk[:n2, g * LANES:(g + 1) * LANES]
            xi_scr[g, n2 * k1:n2 * (k1 + 1), :] = xk[n2:, g * LANES:(g + 1) * LANES]
    cs = (csh_ref[...], csl_ref[...])
    rc = min(FOURIER_ROW_CHUNK, seq)
    for g in range(n_grp):
        for r0 in range(0, seq, rc):
            x = jnp.concatenate([xr_scr[g, r0:r0 + rc, :], xi_scr[g, r0:r0 + rc, :]], axis=1)
            xr_scr[g, r0:r0 + rc, :] = _dot_split(_split_bf16(x), cs)
    for g in range(n_grp):
        for k2 in range(n2):
            o_ref[n1 * k2:n1 * (k2 + 1), g * LANES:(g + 1) * LANES] = (
                xr_scr.at[g][pl.ds(k2, n1, stride=n2), :].astype(o_ref.dtype))


def _fourier(f, tables, dst, *, n_batch, seq):
    n2 = GRID_W
    n1 = seq // n2
    n_grp = FOURIER_LANES // LANES
    table_specs = [pl.BlockSpec(t.shape, (lambda b, g, nd=t.ndim: (0,) * nd)) for t in tables]
    u_specs = [pl.BlockSpec((seq, LANES), (lambda b, g, k=k: (b, g * n_grp + k))) for k in range(n_grp)]
    return pl.pallas_call(
        functools.partial(_fourier_kernel, n1=n1, n2=n2, n_grp=n_grp),
        grid=(n_batch, f.shape[1] // FOURIER_LANES),
        in_specs=u_specs + table_specs + [pl.BlockSpec(memory_space=pl.ANY)],
        out_specs=pl.BlockSpec((seq, FOURIER_LANES), lambda b, g: (b, g)),
        out_shape=jax.ShapeDtypeStruct(dst.shape, dst.dtype),
        input_output_aliases={n_grp + len(tables): 0},
        scratch_shapes=[pltpu.VMEM((n_grp, 2 * seq, LANES), F32), pltpu.VMEM((n_grp, seq, LANES), F32),
                        pltpu.VMEM((n_grp, seq, LANES), F32)],
        compiler_params=_cparams("parallel", "parallel"),
        name="fourier",
    )(*([f] * n_grp), *tables, dst)


def _dense_fourier_tables(n):
    norm = 1.0 / math.sqrt(n * LANES)
    p = np.arange(n)
    ph = 2.0 * np.pi * ((p[:, None] * p[None, :]) % n) / n
    ch = np.arange(LANES)
    pc = 2.0 * np.pi * ((ch[:, None] * ch[None, :]) % LANES) / LANES
    return (jnp.asarray(np.cos(ph).astype(np.float32)), jnp.asarray(np.sin(ph).astype(np.float32)),
            jnp.asarray((np.cos(pc) * norm).astype(np.float32)), jnp.asarray((np.sin(pc) * norm).astype(np.float32)))


def _dense_fourier_kernel(u_ref, cl_ref, sl_ref, cc_ref, sc_ref, _, o_ref):
    u = u_ref[...]
    y = _dot_hi(cl_ref[...], _dot_hi(u, cc_ref[...])) - _dot_hi(sl_ref[...], _dot_hi(u, sc_ref[...]))
    o_ref[...] = y.astype(o_ref.dtype)


def _dense_fourier(f, tables, dst, *, n_batch, n_pos, row_block0):
    cl, sl, cc, sc = tables
    groups = f.shape[1] // LANES
    return pl.pallas_call(
        _dense_fourier_kernel,
        grid=(n_batch, groups),
        in_specs=[
            pl.BlockSpec((n_pos, LANES), lambda b, g: (row_block0 + b, g)),
            pl.BlockSpec(cl.shape, lambda b, g: (0, 0)),
            pl.BlockSpec(sl.shape, lambda b, g: (0, 0)),
            pl.BlockSpec(cc.shape, lambda b, g: (0, 0)),
            pl.BlockSpec(sc.shape, lambda b, g: (0, 0)),
            pl.BlockSpec(memory_space=pl.ANY),
        ],
        out_specs=pl.BlockSpec((n_pos, LANES), lambda b, g: (row_block0 + b, g)),
        out_shape=jax.ShapeDtypeStruct(dst.shape, dst.dtype),
        input_output_aliases={5: 0},
        compiler_params=_cparams("parallel", "parallel"),
        name="fourier_ctx",
    )(f, cl, sl, cc, sc, dst)


def _win_kernel(sink_ref, q_ref, *refs, n_qtiles, has_local, n_heads, group_size):
    if has_local:
        kp_ref, kc_ref, kn_ref, vp_ref, vc_ref, vn_ref, kx_ref, vx_ref, _, o_ref = refs
    else:
        kx_ref, vx_ref, _, o_ref = refs
    tq = q_ref.shape[0]
    lane = lax.broadcasted_iota(jnp.int32, (1, LANES), 1)
    half_mask = [lane < HEAD_DIM, lane >= HEAD_DIM]
    pieces = [(kx_ref[...], vx_ref[...], None)]
    if has_local:
        n = pl.program_id(1)
        qi = lax.broadcasted_iota(jnp.int32, (tq, BLOCK), 0)
        kj = lax.broadcasted_iota(jnp.int32, (tq, BLOCK), 1)
        valid_prev = (kj >= qi) & (n >= 1)
        valid_next = (kj <= qi - (tq - WINDOW)) & (n <= n_qtiles - 2)
        di = lax.broadcasted_iota(jnp.int32, (tq, tq), 0) - lax.broadcasted_iota(jnp.int32, (tq, tq), 1)
        valid_mid = (di <= WINDOW) & (di >= -WINDOW)
        pieces += [(kp_ref[...], vp_ref[...], valid_prev), (kc_ref[...], vc_ref[...], valid_mid),
                   (kn_ref[...], vn_ref[...], valid_next)]
    def scores(head):
        pair, half = divmod(head, 2)
        kv = head // group_size
        qp = q_ref[:, pair * LANES:(pair + 1) * LANES].astype(F32)
        src = qp if half == kv else pltpu.roll(qp, HEAD_DIM, 1)
        qe = jnp.where(half_mask[kv], src, 0.0).astype(BF16)
        out = []
        for k, _, valid in pieces:
            s = _dot_nt(qe, k)
            out.append(s if valid is None else jnp.where(valid, s, NEG))
        return out

    def attend(head, sc):
        half = head % 2
        kv = head // group_size
        sink = sink_ref[head] * LOG2E
        m = jnp.zeros((tq, 1), F32) + sink
        for s in sc:
            m = jnp.maximum(m, jnp.max(s, axis=-1, keepdims=True))
        den = jnp.exp2(sink - m)
        pv = jnp.zeros((tq, LANES), F32)
        for s, (_, v, _) in zip(sc, pieces):
            p = jnp.exp2(s - m)
            den = den + jnp.sum(p, axis=-1, keepdims=True)
            pv = pv + _dot(p.astype(BF16), v)
        pv = pv / den
        return pv if half == kv else pltpu.roll(pv, HEAD_DIM, 1)

    sc_next = scores(0)
    out_pair = None
    for head in range(n_heads):
        sc_cur = sc_next
        if head + 1 < n_heads:
            sc_next = scores(head + 1)
        pv = attend(head, sc_cur)
        if head % 2 == 0:
            out_pair = pv
        else:
            pair = head // 2
            o_ref[:, pair * LANES:(pair + 1) * LANES] = jnp.where(half_mask[1], pv, out_pair).astype(o_ref.dtype)


def _win_attention(z, sink, dst, *, n_batch, seq, n_ctx, q_cols, k_col, v_col, tq):
    nbk = seq // BLOCK
    nq = seq // tq
    per = tq // BLOCK
    ctx_blk0 = n_batch * seq // n_ctx
    n_heads = q_cols // HEAD_DIM
    group_size = n_heads // (LANES // HEAD_DIM)

    def edge(col, first):
        return pl.BlockSpec(
            (BLOCK, LANES), lambda b, n: (b * nbk + jnp.clip(n * per + first, 0, nbk - 1), col))

    def mid(col):
        return pl.BlockSpec((tq, LANES), lambda b, n: (b * nq + n, col))

    return pl.pallas_call(
        functools.partial(_win_kernel, n_qtiles=nq, has_local=True, n_heads=n_heads, group_size=group_size),
        grid=(n_batch, nq),
        in_specs=[
            pl.BlockSpec(memory_space=pltpu.SMEM),
            pl.BlockSpec((tq, q_cols), lambda b, n: (b * nq + n, 0)),
            edge(k_col, -1), mid(k_col), edge(k_col, per),
            edge(v_col, -1), mid(v_col), edge(v_col, per),
            pl.BlockSpec((n_ctx, LANES), lambda b, n: (ctx_blk0 + b, k_col)),
            pl.BlockSpec((n_ctx, LANES), lambda b, n: (ctx_blk0 + b, v_col)),
            pl.BlockSpec(memory_space=pl.ANY),
        ],
        out_specs=pl.BlockSpec((tq, q_cols), lambda b, n: (b * nq + n, 0)),
        out_shape=jax.ShapeDtypeStruct(dst.shape, dst.dtype),
        input_output_aliases={10: 0},
        compiler_params=_cparams("parallel", "parallel"),
        name="win_attn",
    )(sink, z, z, z, z, z, z, z, z, z, dst)


def _ctx_gqa_attention(z, sink, dst, *, n_batch, seq, n_ctx, q_cols, k_col, v_col):
    ctx_blk0 = n_batch * seq // n_ctx
    n_heads = q_cols // HEAD_DIM
    group_size = n_heads // (LANES // HEAD_DIM)
    return pl.pallas_call(
        functools.partial(_win_kernel, n_qtiles=0, has_local=False, n_heads=n_heads, group_size=group_size),
        grid=(n_batch,),
        in_specs=[
            pl.BlockSpec(memory_space=pltpu.SMEM),
            pl.BlockSpec((n_ctx, q_cols), lambda b: (ctx_blk0 + b, 0)),
            pl.BlockSpec((n_ctx, LANES), lambda b: (ctx_blk0 + b, k_col)),
            pl.BlockSpec((n_ctx, LANES), lambda b: (ctx_blk0 + b, v_col)),
            pl.BlockSpec(memory_space=pl.ANY),
        ],
        out_specs=pl.BlockSpec((n_ctx, q_cols), lambda b: (ctx_blk0 + b, 0)),
        out_shape=jax.ShapeDtypeStruct(dst.shape, dst.dtype),
        input_output_aliases={4: 0},
        compiler_params=_cparams("parallel"),
        name="ctx_gqa",
    )(sink, z, z, z, dst)


LOG2E = math.log2(math.e)
DIFF_SUB_ROWS = 128
DIFF_KEY_CHUNK = 1024


def _diff_kernel(lam_ref, g_ref, q_ref, *refs, lam_init, has_x):
    if has_x:
        kx_ref, vtx_ref, kc_ref, vtc_ref, _, o_ref, s_scr = refs
        nx = kx_ref.shape[0]
    else:
        kc_ref, vtc_ref, _, o_ref, s_scr = refs
        nx = 0
    nc = kc_ref.shape[0]
    mc = s_scr.shape[2]
    sub = mc // 2
    n_sub = q_ref.shape[0] // sub
    lv = lam_ref[...]
    lam = (jnp.exp(jnp.sum(lv[0:1] * lv[1:2], axis=-1, keepdims=True))
           - jnp.exp(jnp.sum(lv[2:3] * lv[3:4], axis=-1, keepdims=True)) + lam_init)
    lane = lax.broadcasted_iota(jnp.int32, (1, LANES), 1)
    segs = []
    if has_x:
        kw = min(DIFF_KEY_CHUNK, nx)
        segs += [(kx_ref, vtx_ref, r0, kw, r0) for r0 in range(0, nx, kw)]
    segs.append((kc_ref, vtc_ref, 0, nc, nx))

    def scores(u):
        q = q_ref[u * sub:(u + 1) * sub, :]
        zero = jnp.zeros_like(q)
        qs = jnp.concatenate([jnp.where(lane < HEAD_DIM, q, zero), jnp.where(lane >= HEAD_DIM, q, zero)], axis=0)
        m_run = jnp.full((8, mc), -jnp.inf, F32)
        for k_ref, _, r0, w, s0 in segs:
            st = _dot_nt(k_ref[r0:r0 + w, :], qs)
            s_scr[u % 2, s0:s0 + w, :] = st
            m_run = jnp.maximum(m_run, jnp.max(st.reshape(w // 8, 8, mc), axis=0))
        return jnp.broadcast_to(jnp.max(m_run, axis=0, keepdims=True), (8, mc))

    def attend(u, m8):
        l_run = jnp.zeros((8, mc), F32)
        acc = jnp.zeros((LANES, mc), F32)
        for _, vt_ref, r0, w, s0 in segs:
            p = jnp.exp2(s_scr[u % 2, s0:s0 + w, :].reshape(w // 8, 8, mc) - m8[None])
            l_run = l_run + jnp.sum(p, axis=0)
            acc = acc + _dot(vt_ref[:, r0:r0 + w], p.reshape(w, mc).astype(BF16))
        ot = acc / jnp.sum(l_run, axis=0, keepdims=True)
        o = (ot[:, :sub] - lam * ot[:, sub:]).T
        o = _rms(o) * g_ref[...] * (1.0 - lam_init)
        o_ref[u * sub:(u + 1) * sub, :] = o.astype(o_ref.dtype)

    m_next = scores(0)
    for u in range(n_sub):
        m_cur = m_next
        if u + 1 < n_sub:
            m_next = scores(u + 1)
        attend(u, m_cur)


def _diff_attention(z, vt, lam_vec, subln_g, lam_init, dst, *, n_batch, seq, n_ctx, n_heads, tq):
    ctx_blk0 = n_batch * seq // n_ctx
    nq = seq // tq
    return pl.pallas_call(
        functools.partial(_diff_kernel, lam_init=lam_init, has_x=True),
        grid=(n_batch, n_heads, nq),
        in_specs=[
            pl.BlockSpec(lam_vec.shape, lambda b, h, i: (0, 0)),
            pl.BlockSpec((1, LANES), lambda b, h, i: (0, 0)),
            pl.BlockSpec((tq, LANES), lambda b, h, i: (b * nq + i, h)),
            pl.BlockSpec((seq, LANES), lambda b, h, i: (b, n_heads + h)),
            pl.BlockSpec((LANES, seq), lambda b, h, i: (h, b)),
            pl.BlockSpec((n_ctx, LANES), lambda b, h, i: (ctx_blk0 + b, n_heads + h)),
            pl.BlockSpec((LANES, n_ctx), lambda b, h, i: (h, ctx_blk0 + b)),
            pl.BlockSpec(memory_space=pl.ANY),
        ],
        out_specs=pl.BlockSpec((tq, LANES), lambda b, h, i: (b * nq + i, h)),
        out_shape=jax.ShapeDtypeStruct(dst.shape, dst.dtype),
        input_output_aliases={7: 0},
        scratch_shapes=[pltpu.VMEM((2, seq + n_ctx, 2 * min(tq, DIFF_SUB_ROWS)), F32)],
        compiler_params=_cparams("parallel", "parallel", "parallel"),
        name="diff_attn",
    )(lam_vec, subln_g, z, z, vt, z, vt, dst)


def _diff_attention_ctx(z, vt, lam_vec, subln_g, lam_init, dst, *, n_batch, seq, n_ctx, n_heads):
    ctx_blk0 = n_batch * seq // n_ctx
    return pl.pallas_call(
        functools.partial(_diff_kernel, lam_init=lam_init, has_x=False),
        grid=(n_batch, n_heads),
        in_specs=[
            pl.BlockSpec(lam_vec.shape, lambda b, h: (0, 0)),
            pl.BlockSpec((1, LANES), lambda b, h: (0, 0)),
            pl.BlockSpec((n_ctx, LANES), lambda b, h: (ctx_blk0 + b, h)),
            pl.BlockSpec((n_ctx, LANES), lambda b, h: (ctx_blk0 + b, n_heads + h)),
            pl.BlockSpec((LANES, n_ctx), lambda b, h: (h, ctx_blk0 + b)),
            pl.BlockSpec(memory_space=pl.ANY),
        ],
        out_specs=pl.BlockSpec((n_ctx, LANES), lambda b, h: (ctx_blk0 + b, h)),
        out_shape=jax.ShapeDtypeStruct(dst.shape, dst.dtype),
        input_output_aliases={5: 0},
        scratch_shapes=[pltpu.VMEM((2, n_ctx, 2 * min(n_ctx, DIFF_SUB_ROWS)), F32)],
        compiler_params=_cparams("parallel", "parallel"),
        name="diff_attn_ctx",
    )(lam_vec, subln_g, z, z, vt, dst)


def _outproj_kernel(a0_ref, a1_ref, w_ref, x_ref, g_ref, gate_ref, o_ref):
    half = a0_ref.shape[1]
    y = _dot(a0_ref[...], w_ref[:half, :]) + _dot(a1_ref[...], w_ref[half:, :])
    o_ref[...] = x_ref[...] + gate_ref[...] * (_rms(y) * g_ref[...])


def _outproj(a0, a0_col, a1, a1_col, w, xt, normg, mod, k_gate, *, tm, rows, seq, n_batch):
    d = xt.shape[1]
    half = d // 2

    def mrow(i):
        return jnp.minimum(i * tm // seq, n_batch)

    return pl.pallas_call(
        _outproj_kernel,
        grid=(rows // tm,),
        in_specs=[
            pl.BlockSpec((tm, half), lambda i: (i, a0_col)),
            pl.BlockSpec((tm, half), lambda i: (i, a1_col)),
            pl.BlockSpec((d, d), lambda i: (0, 0)),
            pl.BlockSpec((tm, d), lambda i: (i, 0)),
            pl.BlockSpec((None, 1, d), lambda i: (1, 0, 0)),
            pl.BlockSpec((None, None, 1, d), lambda i: (mrow(i), k_gate, 0, 0)),
        ],
        out_specs=pl.BlockSpec((tm, d), lambda i: (i, 0)),
        out_shape=jax.ShapeDtypeStruct((rows, d), F32),
        compiler_params=_cparams("parallel"),
        name="outproj",
    )(a0, a1, w, xt, normg, mod)


def _router_kernel(x_ref, g_ref, sh_ref, sc_ref, rw_ref, o_ref, sel_ref):
    h = _modulate(x_ref[...], g_ref[...], sh_ref[...], sc_ref[...])
    logits = _dot_split(_split_bf16(h), _split_bf16(rw_ref[...]))
    lane = lax.broadcasted_iota(jnp.int32, logits.shape, 1)
    ninf = -jnp.inf
    logits = jnp.where(lane < N_EXPERTS, logits, ninf)
    m1 = jnp.max(logits, axis=-1, keepdims=True)
    i1 = jnp.min(jnp.where(logits == m1, lane, LANES), axis=-1, keepdims=True)
    sel1 = lane == i1
    rest = jnp.where(sel1, ninf, logits)
    m2 = jnp.max(rest, axis=-1, keepdims=True)
    i2 = jnp.min(jnp.where(rest == m2, lane, LANES), axis=-1, keepdims=True)
    sel2 = lane == i2
    e2 = jnp.exp(m2 - m1)
    den = 1.0 + e2
    o_ref[...] = jnp.where(sel1, 1.0 / den, 0.0) + jnp.where(sel2, e2 / den, 0.0)
    sel_ref[...] = jnp.where(sel1 | sel2, 1.0, 0.0)


def _router(xt, normg, mod, router_w, *, tm, rows, seq, n_batch):
    d = xt.shape[1]
    rw = jnp.zeros((d, LANES), F32).at[:, :N_EXPERTS].set(router_w)

    def mrow(i):
        return jnp.minimum(i * tm // seq, n_batch)

    return pl.pallas_call(
        _router_kernel,
        grid=(rows // tm,),
        in_specs=[
            pl.BlockSpec((tm, d), lambda i: (i, 0)),
            pl.BlockSpec((None, 1, d), lambda i: (2, 0, 0)),
            pl.BlockSpec((None, None, 1, d), lambda i: (mrow(i), 3, 0, 0)),
            pl.BlockSpec((None, None, 1, d), lambda i: (mrow(i), 4, 0, 0)),
            pl.BlockSpec((d, LANES), lambda i: (0, 0)),
        ],
        out_specs=[pl.BlockSpec((tm, LANES), lambda i: (i, 0))] * 2,
        out_shape=[jax.ShapeDtypeStruct((rows, LANES), F32)] * 2,
        compiler_params=_cparams("parallel"),
        name="router",
    )(xt, normg, mod, mod, rw)


SWIGLU_CHUNK = 256


def _swiglu(h, wg_ref, wu_ref, wo_ref, u_off):
    width = wo_ref.shape[0]
    acc = None
    for c0 in range(0, width, SWIGLU_CHUNK):
        gp = _dot(h, wg_ref[:, c0:c0 + SWIGLU_CHUNK])
        up = _dot(h, wu_ref[:, u_off + c0:u_off + c0 + SWIGLU_CHUNK])
        a = gp * (1.0 / (1.0 + jnp.exp(-gp))) * up
        part = _dot(a.astype(BF16), wo_ref[c0:c0 + SWIGLU_CHUNK, :])
        acc = part if acc is None else acc + part
    return acc


def _ffn_kernel(x_ref, g2_ref, sh_ref, sc_ref, win_ref, wout_ref, g3_ref, gate_ref, o_ref):
    x = x_ref[...]
    h = _modulate(x, g2_ref[...], sh_ref[...], sc_ref[...]).astype(BF16)
    y = _swiglu(h, win_ref, win_ref, wout_ref, wout_ref.shape[0])
    o_ref[...] = x + gate_ref[...] * (_rms(y) * g3_ref[...])


def _ffn(xt, normg, mod, w_in, w_out, *, tm, rows, seq, n_batch):
    d = xt.shape[1]

    def mrow(i):
        return jnp.minimum(i * tm // seq, n_batch)

    in_specs = [
        pl.BlockSpec((tm, d), lambda i: (i, 0)),
        pl.BlockSpec((None, 1, d), lambda i: (2, 0, 0)),
        pl.BlockSpec((None, None, 1, d), lambda i: (mrow(i), 3, 0, 0)),
        pl.BlockSpec((None, None, 1, d), lambda i: (mrow(i), 4, 0, 0)),
        pl.BlockSpec(w_in.shape, lambda i: (0, 0)),
        pl.BlockSpec(w_out.shape, lambda i: (0, 0)),
        pl.BlockSpec((None, 1, d), lambda i: (3, 0, 0)),
        pl.BlockSpec((None, None, 1, d), lambda i: (mrow(i), 5, 0, 0)),
    ]
    return pl.pallas_call(
        _ffn_kernel,
        grid=(rows // tm,),
        in_specs=in_specs,
        out_specs=pl.BlockSpec((tm, d), lambda i: (i, 0)),
        out_shape=jax.ShapeDtypeStruct((rows, d), F32),
        compiler_params=_cparams("parallel"),
        name="ffn",
    )(xt, normg, mod, mod, w_in, w_out, normg, mod)


MOE_DISPATCH_ROWS = 256
MOE_COMBINE_ROWS = 512
MOE_SEG_ALIGN = 16
MOE_TILE = 512


def _moe_keys(sel):
    nb = sel.shape[0]
    ti = lax.broadcasted_iota(jnp.int32, (nb, nb), 0)
    tj = lax.broadcasted_iota(jnp.int32, (nb, nb), 1)
    lower = jnp.where(tj < ti, 1.0, 0.0).astype(BF16)
    rank = _dot(lower, sel.astype(BF16))
    return jnp.where(sel > 0.0, rank, -1.0)


def _moe_dispatch_kernel(nchunk_ref, off_ref, x_ref, g2_ref, sh_ref, sc_ref, sel_ref, _, hs_hbm,
                         h_scr, keyt_scr, stage, sems, *, n_exp):
    b = pl.program_id(0)
    rows = MOE_DISPATCH_ROWS
    h_scr[...] = _modulate(x_ref[...], g2_ref[...], sh_ref[...], sc_ref[...]).astype(BF16)
    keyt_scr[...] = _moe_keys(sel_ref[...]).T

    def copy(slot, row0):
        return pltpu.make_async_copy(stage.at[slot], hs_hbm.at[pl.ds(row0, rows), :], sems.at[slot])

    issued = jnp.int32(0)
    for e in range(n_exp):
        key_row = keyt_scr[e:e + 1, :]
        base = off_ref[b * n_exp + e]

        def chunk(c, k):
            slot = k % 2

            @pl.when(k >= 2)
            def _():
                copy(slot, 0).wait()

            r = (c * rows + lax.broadcasted_iota(jnp.int32, (rows, 1), 0)).astype(F32)
            onehot = jnp.where(key_row == r, 1.0, 0.0).astype(BF16)
            stage[slot] = _dot(onehot, h_scr[...]).astype(BF16)
            copy(slot, pl.multiple_of(base + c * rows, MOE_SEG_ALIGN)).start()
            return k + 1

        issued = lax.fori_loop(0, nchunk_ref[b * n_exp + e], chunk, issued)

    @pl.when(issued >= 1)
    def _():
        copy((issued - 1) % 2, 0).wait()

    @pl.when(issued >= 2)
    def _():
        copy(issued % 2, 0).wait()


def _moe_experts_kernel(tile_ref, exp_ref, valid_ref, hs_ref, wg_ref, wu_ref, wo_ref, y_ref, acc, *, n_f):
    t = pl.program_id(0)
    f = pl.program_id(1)

    @pl.when(valid_ref[t] == 1)
    def _():
        part = _swiglu(hs_ref[...], wg_ref, wu_ref, wo_ref, 0)

        @pl.when(f == 0)
        def _():
            acc[...] = part

        @pl.when(f > 0)
        def _():
            acc[...] += part

        @pl.when(f == n_f - 1)
        def _():
            y_ref[...] = acc[...].astype(y_ref.dtype)


def _moe_combine_kernel(npiece_ref, off_ref, x_ref, g3_ref, gate_ref, comb_ref, sel_ref, y_hbm, o_ref,
                        key_scr, stage, sems, *, n_exp):
    b = pl.program_id(0)
    rows = MOE_COMBINE_ROWS
    nb = x_ref.shape[0]
    key_scr[...] = _moe_keys(sel_ref[...])
    o_ref[...] = jnp.zeros_like(o_ref)
    lane = lax.broadcasted_iota(jnp.int32, (nb, LANES), 1)

    def copy(slot, row0):
        return pltpu.make_async_copy(y_hbm.at[pl.ds(row0, rows), :], stage.at[slot], sems.at[slot])

    def row0(e, c):
        return pl.multiple_of(off_ref[b * n_exp + e] + c * rows, MOE_SEG_ALIGN)

    copy(0, row0(0, 0)).start()
    done = jnp.int32(0)
    for e in range(n_exp):
        pick = lane == e
        key_col = jnp.sum(jnp.where(pick, key_scr[...], 0.0), axis=-1, keepdims=True)
        gate_col = jnp.sum(jnp.where(pick, comb_ref[...], 0.0), axis=-1, keepdims=True)
        n_pieces = npiece_ref[b * n_exp + e]

        def piece(c, k, e=e, n_pieces=n_pieces, key_col=key_col, gate_col=gate_col):
            slot = k % 2
            copy(slot, 0).wait()

            @pl.when(c + 1 < n_pieces)
            def _():
                copy(1 - slot, row0(e, c + 1)).start()

            if e + 1 < n_exp:
                @pl.when(c + 1 == n_pieces)
                def _():
                    copy(1 - slot, row0(e + 1, 0)).start()

            r = (c * rows + lax.broadcasted_iota(jnp.int32, (1, rows), 1)).astype(F32)
            onehot = jnp.where(key_col == r, 1.0, 0.0).astype(BF16)
            o_ref[...] += gate_col * _dot(onehot, stage[slot])
            return k + 1

        done = lax.fori_loop(0, n_pieces, piece, done)

    o_ref[...] = x_ref[...] + gate_ref[...] * (_rms(o_ref[...]) * g3_ref[...])


def _moe(xt, normg, mod, w_in, w_out, j, comb, sel, *, nb, tf, rows, seq, n_batch):
    d = xt.shape[1]
    n_exp, fdim = w_out.shape[1], w_out.shape[2]
    n_f = fdim // tf
    n_blk = rows // nb
    i32 = jnp.int32

    counts = jnp.sum(sel.reshape(n_blk, nb, LANES)[:, :, :n_exp], axis=1).astype(i32)
    seg = -(-counts // MOE_SEG_ALIGN) * MOE_SEG_ALIGN
    region = -(-(jnp.sum(seg, axis=0) + MOE_DISPATCH_ROWS) // MOE_TILE) * MOE_TILE
    region_end = jnp.cumsum(region)
    region_start = region_end - region
    off = (region_start[None, :] + jnp.cumsum(seg, axis=0) - seg).reshape(n_blk * n_exp).astype(i32)
    nchunk = (-(-counts // MOE_DISPATCH_ROWS)).reshape(n_blk * n_exp).astype(i32)
    npiece = jnp.maximum(-(-counts // MOE_COMBINE_ROWS), 1).reshape(n_blk * n_exp).astype(i32)
    cap = 2 * rows + n_blk * n_exp * (MOE_SEG_ALIGN - 1) + n_exp * (MOE_DISPATCH_ROWS + MOE_TILE - 1)
    n_tiles = -(-cap // MOE_TILE)
    cap = n_tiles * MOE_TILE + MOE_COMBINE_ROWS
    tiles = jnp.arange(n_tiles, dtype=i32)
    n_used = region_end[-1] // MOE_TILE
    tile_map = jnp.minimum(tiles, n_used - 1).astype(i32)
    tile_exp = jnp.minimum(jnp.searchsorted(region_end, tile_map * MOE_TILE, side="right"), n_exp - 1).astype(i32)
    tile_valid = (tiles < n_used).astype(i32)

    def mrow(i):
        return jnp.minimum(i * nb // seq, n_batch)

    hs = pl.pallas_call(
        functools.partial(_moe_dispatch_kernel, n_exp=n_exp),
        grid_spec=pltpu.PrefetchScalarGridSpec(
            num_scalar_prefetch=2,
            grid=(n_blk,),
            in_specs=[
                pl.BlockSpec((nb, d), lambda i, *_: (i, 0)),
                pl.BlockSpec((None, 1, d), lambda i, *_: (2, 0, 0)),
                pl.BlockSpec((None, None, 1, d), lambda i, *_: (mrow(i), 3, 0, 0)),
                pl.BlockSpec((None, None, 1, d), lambda i, *_: (mrow(i), 4, 0, 0)),
                pl.BlockSpec((nb, LANES), lambda i, *_: (i, 0)),
                pl.BlockSpec(memory_space=pl.ANY),
            ],
            out_specs=pl.BlockSpec(memory_space=pl.ANY),
            scratch_shapes=[
                pltpu.VMEM((nb, d), BF16),
                pltpu.VMEM((LANES, nb), F32),
                pltpu.VMEM((2, MOE_DISPATCH_ROWS, d), BF16),
                pltpu.SemaphoreType.DMA((2,)),
            ],
        ),
        out_shape=jax.ShapeDtypeStruct((cap, d), BF16),
        input_output_aliases={7: 0},
        compiler_params=_cparams("arbitrary"),
        name="moe_dispatch",
    )(nchunk, off, xt, normg, mod, mod, sel, jnp.zeros((cap, d), BF16))

    ys = pl.pallas_call(
        functools.partial(_moe_experts_kernel, n_f=n_f),
        grid_spec=pltpu.PrefetchScalarGridSpec(
            num_scalar_prefetch=3,
            grid=(n_tiles, n_f),
            in_specs=[
                pl.BlockSpec((MOE_TILE, d), lambda t, f, tm, te, tv: (tm[t], 0)),
                pl.BlockSpec((None, None, d, tf), lambda t, f, tm, te, tv: (j, te[t], 0, f)),
                pl.BlockSpec((None, None, d, tf), lambda t, f, tm, te, tv: (j, te[t], 0, n_f + f)),
                pl.BlockSpec((None, None, tf, d), lambda t, f, tm, te, tv: (j, te[t], f, 0)),
            ],
            out_specs=pl.BlockSpec((MOE_TILE, d), lambda t, f, tm, te, tv: (tm[t], 0)),
            scratch_shapes=[pltpu.VMEM((MOE_TILE, d), F32)],
        ),
        out_shape=jax.ShapeDtypeStruct((cap, d), BF16),
        input_output_aliases={3: 0},
        compiler_params=_cparams("arbitrary", "arbitrary"),
        name="moe_experts",
    )(tile_map, tile_exp, tile_valid, hs, w_in, w_in, w_out)

    return pl.pallas_call(
        functools.partial(_moe_combine_kernel, n_exp=n_exp),
        grid_spec=pltpu.PrefetchScalarGridSpec(
            num_scalar_prefetch=2,
            grid=(n_blk,),
            in_specs=[
                pl.BlockSpec((nb, d), lambda i, *_: (i, 0)),
                pl.BlockSpec((None, 1, d), lambda i, *_: (3, 0, 0)),
                pl.BlockSpec((None, None, 1, d), lambda i, *_: (mrow(i), 5, 0, 0)),
                pl.BlockSpec((nb, LANES), lambda i, *_: (i, 0)),
                pl.BlockSpec((nb, LANES), lambda i, *_: (i, 0)),
                pl.BlockSpec(memory_space=pl.ANY),
            ],
            out_specs=pl.BlockSpec((nb, d), lambda i, *_: (i, 0)),
            scratch_shapes=[
                pltpu.VMEM((nb, LANES), F32),
                pltpu.VMEM((2, MOE_COMBINE_ROWS, d), BF16),
                pltpu.SemaphoreType.DMA((2,)),
            ],
        ),
        out_shape=jax.ShapeDtypeStruct((rows, d), F32),
        compiler_params=_cparams("parallel"),
        name="moe_combine",
    )(npiece, off, xt, normg, mod, comb, sel, ys)


def _lambda_init(layer):
    return 0.8 - 0.6 * math.exp(-0.3 * layer)


def kernel(x, c, ctx, c_ctx, ada_w, ada_b, norm_g, mix_in_w, mix_out_w, win_sink, diff_qkv_w, diff_out_w,
           diff_lambda, diff_subln_g, ffn_in_w, ffn_out_w, router_w, expert_in_w, expert_out_w):
    n_batch, seq, d = x.shape
    n_ctx = ctx.shape[1]
    depth = ada_w.shape[0]
    n_lat = n_batch * seq
    n_all = n_lat + n_batch * n_ctx
    fdim = mix_in_w.shape[2] - (d // 2 + 2 * LANES)
    q_cols = d // 2
    n_diff_heads = d // LANES

    tm_proj = _pick_tile(512, seq, n_batch * n_ctx)
    tm_out = _pick_tile(512, seq, n_batch * n_ctx)
    tm_ffn = _pick_tile(512, seq, n_batch * n_ctx)
    nb_moe = _pick_tile(1024, seq, n_batch * n_ctx)
    tq_diff = _pick_tile(1024, seq)
    tq_win = _pick_tile(512, seq)
    common = dict(seq=seq, n_batch=n_batch)

    xt = jnp.concatenate([x.reshape(n_lat, d), ctx.reshape(n_batch * n_ctx, d)], axis=0)
    n_mod = -(-(n_batch + 1) // 8) * 8
    cv = jnp.zeros((n_mod, d), F32).at[:n_batch].set(c).at[n_batch].set(c_ctx)
    mods = _modvec(cv, ada_w, ada_b).reshape(depth, n_mod, 6, 1, d)
    rope = _rope_tables(seq, tm_proj)
    f_tables = _fourier_tables(seq)
    fc_tables = _dense_fourier_tables(n_ctx)

    n_f = fdim // LANES
    plan_even = ([(0, i * LANES, "plain") for i in range(n_f)]
                 + [(1, i * LANES, "rope_q_log2") for i in range(q_cols // LANES)]
                 + [(1, q_cols, "rope_k"), (1, q_cols + LANES, "plain")])
    plan_odd = ([(0, i * LANES, "rope_q_log2") for i in range(n_diff_heads)]
                + [(0, d + i * LANES, "rope_k") for i in range(n_diff_heads)]
                + [(1, i * LANES, "plain_t") for i in range(n_diff_heads)])

    expert_in_bf16 = expert_in_w.astype(BF16)
    expert_out_bf16 = expert_out_w.astype(BF16)

    for layer in range(depth):
        j = layer // 2
        need_ctx = layer < depth - 1
        rows = n_all if need_ctx else n_lat
        mod = mods[layer]
        ng = norm_g[layer].reshape(4, 1, d)
        if layer % 2 == 0:
            f, z = _proj(xt, ng, mod, 0, 1, mix_in_w[j].astype(BF16), rope, plan_even,
                         [(fdim, F32, False), (q_cols + 2 * LANES, BF16, False)], tm=tm_proj, n_lat=n_lat,
                         **common)
            k_col, v_col = q_cols // LANES, q_cols // LANES + 1
            att = dict(n_batch=n_batch, seq=seq, n_ctx=n_ctx, q_cols=q_cols, k_col=k_col, v_col=v_col)
            mix_f = _fourier(f, f_tables, jnp.zeros((rows, fdim), BF16), n_batch=n_batch, seq=seq)
            mix_a = _win_attention(z, win_sink[j], jnp.zeros((rows, q_cols), BF16), tq=tq_win, **att)
            if need_ctx:
                mix_f = _dense_fourier(f, fc_tables, mix_f, n_batch=n_batch, n_pos=n_ctx,
                                       row_block0=n_lat // n_ctx)
                mix_a = _ctx_gqa_attention(z, win_sink[j], mix_a, **att)
            xt = _outproj(mix_f, 0, mix_a, 0, mix_out_w[j].astype(BF16), xt, ng, mod, 2, tm=tm_out, rows=rows,
                          **common)
            xt = _ffn(xt, ng, mod, ffn_in_w[j].astype(BF16), ffn_out_w[j].astype(BF16),
                      tm=tm_ffn, rows=rows, **common)
        else:
            lam_init = _lambda_init(layer)
            z, vt = _proj(xt, ng, mod, 0, 1, diff_qkv_w[j].astype(BF16), rope, plan_odd,
                          [(2 * d, BF16, False), (d, BF16, True)], tm=tm_proj, n_lat=n_lat, **common)
            subg = diff_subln_g[j].reshape(1, LANES)
            att = dict(n_batch=n_batch, seq=seq, n_ctx=n_ctx, n_heads=n_diff_heads)
            mix = _diff_attention(z, vt, diff_lambda[j], subg, lam_init, jnp.zeros((rows, d), BF16), tq=tq_diff,
                                  **att)
            if need_ctx:
                mix = _diff_attention_ctx(z, vt, diff_lambda[j], subg, lam_init, mix, **att)
            xt = _outproj(mix, 0, mix, 1, diff_out_w[j].astype(BF16), xt, ng, mod, 2, tm=tm_out, rows=rows,
                          **common)
            comb, sel = _router(xt, ng, mod, router_w[j], tm=tm_out, rows=rows, **common)
            xt = _moe(xt, ng, mod, expert_in_bf16, expert_out_bf16, j, comb, sel,
                      nb=nb_moe, tf=1792, rows=rows, **common)
    return xt[:n_lat].reshape(n_batch, seq, d)
```

```python
import functools
import math

import numpy as np
import jax
import jax.numpy as jnp
from jax import lax
from jax.experimental import pallas as pl
from jax.experimental.pallas import tpu as pltpu

EPS = 1e-6
NEG = -1e30
HEAD_DIM = 64
LANES = 128
GRID_W = 64
BLOCK = 128
WINDOW = 128
ROPE_THETA = 10000.0
N_EXPERTS = 8
F32 = jnp.float32
BF16 = jnp.bfloat16
HIGHEST = lax.Precision.HIGHEST
VMEM_LIMIT = 56 * 1024 * 1024


def _cparams(*sem):
    return pltpu.CompilerParams(dimension_semantics=sem, vmem_limit_bytes=VMEM_LIMIT)


def _dot(a, b):
    return jnp.dot(a, b, preferred_element_type=F32)


def _dot_nt(a, b):
    return lax.dot_general(a, b, (((1,), (1,)), ((), ())), preferred_element_type=F32)


def _dot_hi(a, b):
    return jnp.dot(a, b, precision=HIGHEST, preferred_element_type=F32)


def _rms(v):
    return v * lax.rsqrt(jnp.mean(v * v, axis=-1, keepdims=True) + EPS)


def _modulate(x, g, sh, sc):
    return _rms(x) * g * (1.0 + sc) + sh


def _pick_tile(pref, *dims):
    t = pref
    while any(d % t for d in dims):
        t //= 2
    return t


def _modvec_kernel(c_ref, w_ref, b_ref, o_ref):
    cv = c_ref[...]
    s = cv * (1.0 / (1.0 + jnp.exp(-cv)))
    o_ref[...] = _dot(s.astype(BF16), w_ref[...].astype(BF16)) + b_ref[...]


def _modvec(cv, ada_w, ada_b):
    depth, d, n = ada_w.shape
    r = cv.shape[0]
    tn = _pick_tile(1536, n)
    return pl.pallas_call(
        _modvec_kernel,
        grid=(depth, n // tn),
        in_specs=[
            pl.BlockSpec((r, d), lambda l, j: (0, 0)),
            pl.BlockSpec((None, d, tn), lambda l, j: (l, 0, j)),
            pl.BlockSpec((None, 1, tn), lambda l, j: (l, 0, j)),
        ],
        out_specs=pl.BlockSpec((None, r, tn), lambda l, j: (l, 0, j)),
        out_shape=jax.ShapeDtypeStruct((depth, r, n), F32),
        compiler_params=_cparams("parallel", "parallel"),
        name="modvec",
    )(cv, ada_w, ada_b.reshape(depth, 1, n))


def _proj_kernel(x_ref, g_ref, sh_ref, sc_ref, w_ref, cos_ref, sa_ref, sb_ref, *o_refs, plan, group):
    h = _modulate(x_ref[...], g_ref[...], sh_ref[...], sc_ref[...]).astype(BF16)
    n = w_ref.shape[1]
    for g0 in range(0, n, group):
        acc = _dot(h, w_ref[:, g0:g0 + group])
        for c0 in range(0, group, LANES):
            oi, oc, mode = plan[(g0 + c0) // LANES]
            v = acc[:, c0:c0 + LANES]
            if mode.startswith("rope"):
                v = (v * cos_ref[...] + pltpu.roll(v, LANES - 16, 1) * sa_ref[...]
                     + pltpu.roll(v, 16, 1) * sb_ref[...])
                if mode == "rope_q_log2":
                    v = v * (HEAD_DIM ** -0.5 * LOG2E)
            if mode == "plain_t":
                o_refs[oi][oc:oc + LANES, :] = v.T.astype(o_refs[oi].dtype)
            else:
                o_refs[oi][:, oc:oc + LANES] = v.astype(o_refs[oi].dtype)


def _proj(xt, normg, mod, k_sh, k_sc, w, rope, plan, outs, *, tm, n_lat, seq, n_batch):
    t, d = xt.shape
    n = w.shape[1]
    group = _pick_tile(512, n)
    nx = n_lat // tm
    per = seq // tm

    def mrow(i):
        return jnp.minimum(i * tm // seq, n_batch)

    def rrow(i):
        return jnp.where(i < nx, i % per, per)

    in_specs = [
        pl.BlockSpec((tm, d), lambda i: (i, 0)),
        pl.BlockSpec((None, 1, d), lambda i: (0, 0, 0)),
        pl.BlockSpec((None, None, 1, d), lambda i: (mrow(i), k_sh, 0, 0)),
        pl.BlockSpec((None, None, 1, d), lambda i: (mrow(i), k_sc, 0, 0)),
        pl.BlockSpec((d, n), lambda i: (0, 0)),
        pl.BlockSpec((tm, LANES), lambda i: (rrow(i), 0)),
        pl.BlockSpec((tm, LANES), lambda i: (rrow(i), 0)),
        pl.BlockSpec((tm, LANES), lambda i: (rrow(i), 0)),
    ]
    out_specs = [pl.BlockSpec((wd, tm), lambda i: (0, i)) if tr else pl.BlockSpec((tm, wd), lambda i: (i, 0))
                 for wd, _, tr in outs]
    out_shape = [jax.ShapeDtypeStruct((wd, t) if tr else (t, wd), dt) for wd, dt, tr in outs]
    return pl.pallas_call(
        functools.partial(_proj_kernel, plan=plan, group=group),
        grid=(t // tm,),
        in_specs=in_specs,
        out_specs=out_specs,
        out_shape=out_shape,
        compiler_params=_cparams("parallel"),
        name="proj",
    )(xt, normg, mod, mod, w, *rope)


def _rope_tables(seq, tm):
    rows_count = seq // GRID_W
    rows = jnp.repeat(jnp.arange(rows_count), GRID_W).astype(F32)
    cols = jnp.tile(jnp.arange(GRID_W), rows_count).astype(F32)
    axis_dim = HEAD_DIM // 2
    inv = ROPE_THETA ** (-jnp.arange(0, axis_dim, 2, dtype=F32) / axis_dim)
    ar = rows[:, None] * inv
    ac = cols[:, None] * inv
    cr, sr, cc, sc = jnp.cos(ar), jnp.sin(ar), jnp.cos(ac), jnp.sin(ac)
    z = jnp.zeros_like(sr)
    reps = LANES // HEAD_DIM
    cos = jnp.tile(jnp.concatenate([cr, cr, cc, cc], axis=1), (1, reps))
    sa = jnp.tile(jnp.concatenate([-sr, z, -sc, z], axis=1), (1, reps))
    sb = jnp.tile(jnp.concatenate([z, sr, z, sc], axis=1), (1, reps))
    ident = jnp.ones((tm, LANES), F32)
    zero = jnp.zeros((tm, LANES), F32)
    return (jnp.concatenate([cos, ident]), jnp.concatenate([sa, zero]), jnp.concatenate([sb, zero]))


FOURIER_LANES = 2 * LANES


def _split_bf16(t):
    hi = t.astype(BF16)
    return hi, (t - hi.astype(F32)).astype(BF16)


def _dot_split(a, b):
    return _dot(a[0], b[0]) + _dot(a[0], b[1]) + _dot(a[1], b[0])


def _fourier_tables(seq):
    n2 = GRID_W
    n1 = seq // n2
    norm = 1.0 / math.sqrt(seq * LANES)
    a = np.arange(n1)
    k1 = np.arange(n1)
    b = np.arange(n2)
    ang = (b[:, None, None] * k1[None, :, None] + (seq // n1) * k1[None, :, None] * a[None, None, :]) % seq
    th = 2.0 * np.pi * ang / seq
    m1 = np.concatenate([np.cos(th), -np.sin(th)], axis=1).astype(np.float32)
    ph = 2.0 * np.pi * ((b[:, None] * b[None, :]) % n2) / n2
    c2, s2 = np.cos(ph), np.sin(ph)
    g2 = np.block([[c2, s2], [-s2, c2]]).astype(np.float32)
    ch = np.arange(LANES)
    pc = 2.0 * np.pi * ((ch[:, None] * ch[None, :]) % LANES) / LANES
    cc = (np.cos(pc) * norm).astype(np.float32)
    sc = (np.sin(pc) * norm).astype(np.float32)
    cs = np.concatenate([cc, sc], axis=0)
    out = []
    for t in (m1, g2, cs):
        out.extend(_split_bf16(jnp.asarray(t)))
    return tuple(out)


FOURIER_ROW_CHUNK = 512


def _fourier_kernel(*refs, n1, n2, n_grp):
    u_refs = refs[:n_grp]
    m1h_ref, m1l_ref, g2h_ref, g2l_ref, csh_ref, csl_ref, _, o_ref, b_scr, xr_scr, xi_scr = refs[n_grp:]
    seq = n1 * n2
    for b in range(n2):
        xs = jnp.concatenate([u[pl.ds(b, n1, stride=n2), :] for u in u_refs], axis=1)
        z = _dot_split((m1h_ref[b], m1l_ref[b]), _split_bf16(xs))
        for g in range(n_grp):
            b_scr[g, 2 * n1 * b:2 * n1 * (b + 1), :] = z[:, g * LANES:(g + 1) * LANES]
    g2 = (g2h_ref[...], g2l_ref[...])
    for k1 in range(n1):
        bk = jnp.concatenate(
            [jnp.concatenate([b_scr.at[g][pl.ds(k1, n2, stride=2 * n1), :],
                              b_scr.at[g][pl.ds(n1 + k1, n2, stride=2 * n1), :]], axis=0)
             for g in range(n_grp)], axis=1)
        xk = _dot_split(g2, _split_bf16(bk))
        for g in range(n_grp):
            xr_scr[g, n2 * k1:n2 * (k1 + 1), :] = xk[:n2, g * LANES:(g + 1) * LANES]
            xi_scr[g, n2 * k1:n2 * (k1 + 1), :] = xk[n2:, g * LANES:(g + 1) * LANES]
    cs = (csh_ref[...], csl_ref[...])
    rc = min(FOURIER_ROW_CHUNK, seq)
    for g in range(n_grp):
        for r0 in range(0, seq, rc):
            x = jnp.concatenate([xr_scr[g, r0:r0 + rc, :], xi_scr[g, r0:r0 + rc, :]], axis=1)
            xr_scr[g, r0:r0 + rc, :] = _dot_split(_split_bf16(x), cs)
    for g in range(n_grp):
        for k2 in range(n2):
            o_ref[n1 * k2:n1 * (k2 + 1), g * LANES:(g + 1) * LANES] = (
                xr_scr.at[g][pl.ds(k2, n1, stride=n2), :].astype(o_ref.dtype))


def _fourier(f, tables, dst, *, n_batch, seq):
    n2 = GRID_W
    n1 = seq // n2
    n_grp = FOURIER_LANES // LANES
    table_specs = [pl.BlockSpec(t.shape, (lambda b, g, nd=t.ndim: (0,) * nd)) for t in tables]
    u_specs = [pl.BlockSpec((seq, LANES), (lambda b, g, k=k: (b, g * n_grp + k))) for k in range(n_grp)]
    return pl.pallas_call(
        functools.partial(_fourier_kernel, n1=n1, n2=n2, n_grp=n_grp),
        grid=(n_batch, f.shape[1] // FOURIER_LANES),
        in_specs=u_specs + table_specs + [pl.BlockSpec(memory_space=pl.ANY)],
        out_specs=pl.BlockSpec((seq, FOURIER_LANES), lambda b, g: (b, g)),
        out_shape=jax.ShapeDtypeStruct(dst.shape, dst.dtype),
        input_output_aliases={n_grp + len(tables): 0},
        scratch_shapes=[pltpu.VMEM((n_grp, 2 * seq, LANES), F32), pltpu.VMEM((n_grp, seq, LANES), F32),
                        pltpu.VMEM((n_grp, seq, LANES), F32)],
        compiler_params=_cparams("parallel", "parallel"),
        name="fourier",
    )(*([f] * n_grp), *tables, dst)


def _dense_fourier_tables(n):
    norm = 1.0 / math.sqrt(n * LANES)
    p = np.arange(n)
    ph = 2.0 * np.pi * ((p[:, None] * p[None, :]) % n) / n
    ch = np.arange(LANES)
    pc = 2.0 * np.pi * ((ch[:, None] * ch[None, :]) % LANES) / LANES
    return (jnp.asarray(np.cos(ph).astype(np.float32)), jnp.asarray(np.sin(ph).astype(np.float32)),
            jnp.asarray((np.cos(pc) * norm).astype(np.float32)), jnp.asarray((np.sin(pc) * norm).astype(np.float32)))


def _dense_fourier_kernel(u_ref, cl_ref, sl_ref, cc_ref, sc_ref, _, o_ref):
    u = u_ref[...]
    y = _dot_hi(cl_ref[...], _dot_hi(u, cc_ref[...])) - _dot_hi(sl_ref[...], _dot_hi(u, sc_ref[...]))
    o_ref[...] = y.astype(o_ref.dtype)


def _dense_fourier(f, tables, dst, *, n_batch, n_pos, row_block0):
    cl, sl, cc, sc = tables
    groups = f.shape[1] // LANES
    return pl.pallas_call(
        _dense_fourier_kernel,
        grid=(n_batch, groups),
        in_specs=[
            pl.BlockSpec((n_pos, LANES), lambda b, g: (row_block0 + b, g)),
            pl.BlockSpec(cl.shape, lambda b, g: (0, 0)),
            pl.BlockSpec(sl.shape, lambda b, g: (0, 0)),
            pl.BlockSpec(cc.shape, lambda b, g: (0, 0)),
            pl.BlockSpec(sc.shape, lambda b, g: (0, 0)),
            pl.BlockSpec(memory_space=pl.ANY),
        ],
        out_specs=pl.BlockSpec((n_pos, LANES), lambda b, g: (row_block0 + b, g)),
        out_shape=jax.ShapeDtypeStruct(dst.shape, dst.dtype),
        input_output_aliases={5: 0},
        compiler_params=_cparams("parallel", "parallel"),
        name="fourier_ctx",
    )(f, cl, sl, cc, sc, dst)


def _win_kernel(sink_ref, q_ref, *refs, n_qtiles, has_local, n_heads, group_size):
    if has_local:
        kp_ref, kc_ref, kn_ref, vp_ref, vc_ref, vn_ref, kx_ref, vx_ref, _, o_ref = refs
    else:
        kx_ref, vx_ref, _, o_ref = refs
    tq = q_ref.shape[0]
    lane = lax.broadcasted_iota(jnp.int32, (1, LANES), 1)
    half_mask = [lane < HEAD_DIM, lane >= HEAD_DIM]
    pieces = [(kx_ref[...], vx_ref[...], None)]
    if has_local:
        n = pl.program_id(1)
        qi = lax.broadcasted_iota(jnp.int32, (tq, BLOCK), 0)
        kj = lax.broadcasted_iota(jnp.int32, (tq, BLOCK), 1)
        valid_prev = (kj >= qi) & (n >= 1)
        valid_next = (kj <= qi - (tq - WINDOW)) & (n <= n_qtiles - 2)
        di = lax.broadcasted_iota(jnp.int32, (tq, tq), 0) - lax.broadcasted_iota(jnp.int32, (tq, tq), 1)
        valid_mid = (di <= WINDOW) & (di >= -WINDOW)
        pieces += [(kp_ref[...], vp_ref[...], valid_prev), (kc_ref[...], vc_ref[...], valid_mid),
                   (kn_ref[...], vn_ref[...], valid_next)]
    def scores(head):
        pair, half = divmod(head, 2)
        kv = head // group_size
        qp = q_ref[:, pair * LANES:(pair + 1) * LANES].astype(F32)
        src = qp if half == kv else pltpu.roll(qp, HEAD_DIM, 1)
        qe = jnp.where(half_mask[kv], src, 0.0).astype(BF16)
        out = []
        for k, _, valid in pieces:
            s = _dot_nt(qe, k)
            out.append(s if valid is None else jnp.where(valid, s, NEG))
        return out

    def attend(head, sc):
        half = head % 2
        kv = head // group_size
        sink = sink_ref[head] * LOG2E
        m = jnp.zeros((tq, 1), F32) + sink
        for s in sc:
            m = jnp.maximum(m, jnp.max(s, axis=-1, keepdims=True))
        den = jnp.exp2(sink - m)
        pv = jnp.zeros((tq, LANES), F32)
        for s, (_, v, _) in zip(sc, pieces):
            p = jnp.exp2(s - m)
            den = den + jnp.sum(p, axis=-1, keepdims=True)
            pv = pv + _dot(p.astype(BF16), v)
        pv = pv / den
        return pv if half == kv else pltpu.roll(pv, HEAD_DIM, 1)

    sc_next = scores(0)
    out_pair = None
    for head in range(n_heads):
        sc_cur = sc_next
        if head + 1 < n_heads:
            sc_next = scores(head + 1)
        pv = attend(head, sc_cur)
        if head % 2 == 0:
            out_pair = pv
        else:
            pair = head // 2
            o_ref[:, pair * LANES:(pair + 1) * LANES] = jnp.where(half_mask[1], pv, out_pair).astype(o_ref.dtype)


def _win_attention(z, sink, dst, *, n_batch, seq, n_ctx, q_cols, k_col, v_col, tq):
    nbk = seq // BLOCK
    nq = seq // tq
    per = tq // BLOCK
    ctx_blk0 = n_batch * seq // n_ctx
    n_heads = q_cols // HEAD_DIM
    group_size = n_heads // (LANES // HEAD_DIM)

    def edge(col, first):
        return pl.BlockSpec(
            (BLOCK, LANES), lambda b, n: (b * nbk + jnp.clip(n * per + first, 0, nbk - 1), col))

    def mid(col):
        return pl.BlockSpec((tq, LANES), lambda b, n: (b * nq + n, col))

    return pl.pallas_call(
        functools.partial(_win_kernel, n_qtiles=nq, has_local=True, n_heads=n_heads, group_size=group_size),
        grid=(n_batch, nq),
        in_specs=[
            pl.BlockSpec(memory_space=pltpu.SMEM),
            pl.BlockSpec((tq, q_cols), lambda b, n: (b * nq + n, 0)),
            edge(k_col, -1), mid(k_col), edge(k_col, per),
            edge(v_col, -1), mid(v_col), edge(v_col, per),
            pl.BlockSpec((n_ctx, LANES), lambda b, n: (ctx_blk0 + b, k_col)),
            pl.BlockSpec((n_ctx, LANES), lambda b, n: (ctx_blk0 + b, v_col)),
            pl.BlockSpec(memory_space=pl.ANY),
        ],
        out_specs=pl.BlockSpec((tq, q_cols), lambda b, n: (b * nq + n, 0)),
        out_shape=jax.ShapeDtypeStruct(dst.shape, dst.dtype),
        input_output_aliases={10: 0},
        compiler_params=_cparams("parallel", "parallel"),
        name="win_attn",
    )(sink, z, z, z, z, z, z, z, z, z, dst)


def _ctx_gqa_attention(z, sink, dst, *, n_batch, seq, n_ctx, q_cols, k_col, v_col):
    ctx_blk0 = n_batch * seq // n_ctx
    n_heads = q_cols // HEAD_DIM
    group_size = n_heads // (LANES // HEAD_DIM)
    return pl.pallas_call(
        functools.partial(_win_kernel, n_qtiles=0, has_local=False, n_heads=n_heads, group_size=group_size),
        grid=(n_batch,),
        in_specs=[
            pl.BlockSpec(memory_space=pltpu.SMEM),
            pl.BlockSpec((n_ctx, q_cols), lambda b: (ctx_blk0 + b, 0)),
            pl.BlockSpec((n_ctx, LANES), lambda b: (ctx_blk0 + b, k_col)),
            pl.BlockSpec((n_ctx, LANES), lambda b: (ctx_blk0 + b, v_col)),
            pl.BlockSpec(memory_space=pl.ANY),
        ],
        out_specs=pl.BlockSpec((n_ctx, q_cols), lambda b: (ctx_blk0 + b, 0)),
        out_shape=jax.ShapeDtypeStruct(dst.shape, dst.dtype),
        input_output_aliases={4: 0},
        compiler_params=_cparams("parallel"),
        name="ctx_gqa",
    )(sink, z, z, z, dst)


LOG2E = math.log2(math.e)
DIFF_SUB_ROWS = 128
DIFF_KEY_CHUNK = 1024


def _diff_kernel(lam_ref, g_ref, q_ref, *refs, lam_init, has_x):
    if has_x:
        kx_ref, vtx_ref, kc_ref, vtc_ref, _, o_ref, s_scr = refs
        nx = kx_ref.shape[0]
    else:
        kc_ref, vtc_ref, _, o_ref, s_scr = refs
        nx = 0
    nc = kc_ref.shape[0]
    mc = s_scr.shape[2]
    sub = mc // 2
    n_sub = q_ref.shape[0] // sub
    lv = lam_ref[...]
    lam = (jnp.exp(jnp.sum(lv[0:1] * lv[1:2], axis=-1, keepdims=True))
           - jnp.exp(jnp.sum(lv[2:3] * lv[3:4], axis=-1, keepdims=True)) + lam_init)
    lane = lax.broadcasted_iota(jnp.int32, (1, LANES), 1)
    segs = []
    if has_x:
        kw = min(DIFF_KEY_CHUNK, nx)
        segs += [(kx_ref, vtx_ref, r0, kw, r0) for r0 in range(0, nx, kw)]
    segs.append((kc_ref, vtc_ref, 0, nc, nx))

    def scores(u):
        q = q_ref[u * sub:(u + 1) * sub, :]
        zero = jnp.zeros_like(q)
        qs = jnp.concatenate([jnp.where(lane < HEAD_DIM, q, zero), jnp.where(lane >= HEAD_DIM, q, zero)], axis=0)
        m_run = jnp.full((8, mc), -jnp.inf, F32)
        for k_ref, _, r0, w, s0 in segs:
            st = _dot_nt(k_ref[r0:r0 + w, :], qs)
            s_scr[u % 2, s0:s0 + w, :] = st
            m_run = jnp.maximum(m_run, jnp.max(st.reshape(w // 8, 8, mc), axis=0))
        return jnp.broadcast_to(jnp.max(m_run, axis=0, keepdims=True), (8, mc))

    def attend(u, m8):
        l_run = jnp.zeros((8, mc), F32)
        acc = jnp.zeros((LANES, mc), F32)
        for _, vt_ref, r0, w, s0 in segs:
            p = jnp.exp2(s_scr[u % 2, s0:s0 + w, :].reshape(w // 8, 8, mc) - m8[None])
            l_run = l_run + jnp.sum(p, axis=0)
            acc = acc + _dot(vt_ref[:, r0:r0 + w], p.reshape(w, mc).astype(BF16))
        ot = acc / jnp.sum(l_run, axis=0, keepdims=True)
        o = (ot[:, :sub] - lam * ot[:, sub:]).T
        o = _rms(o) * g_ref[...] * (1.0 - lam_init)
        o_ref[u * sub:(u + 1) * sub, :] = o.astype(o_ref.dtype)

    m_next = scores(0)
    for u in range(n_sub):
        m_cur = m_next
        if u + 1 < n_sub:
            m_next = scores(u + 1)
        attend(u, m_cur)


def _diff_attention(z, vt, lam_vec, subln_g, lam_init, dst, *, n_batch, seq, n_ctx, n_heads, tq):
    ctx_blk0 = n_batch * seq // n_ctx
    nq = seq // tq
    return pl.pallas_call(
        functools.partial(_diff_kernel, lam_init=lam_init, has_x=True),
        grid=(n_batch, n_heads, nq),
        in_specs=[
            pl.BlockSpec(lam_vec.shape, lambda b, h, i: (0, 0)),
            pl.BlockSpec((1, LANES), lambda b, h, i: (0, 0)),
            pl.BlockSpec((tq, LANES), lambda b, h, i: (b * nq + i, h)),
            pl.BlockSpec((seq, LANES), lambda b, h, i: (b, n_heads + h)),
            pl.BlockSpec((LANES, seq), lambda b, h, i: (h, b)),
            pl.BlockSpec((n_ctx, LANES), lambda b, h, i: (ctx_blk0 + b, n_heads + h)),
            pl.BlockSpec((LANES, n_ctx), lambda b, h, i: (h, ctx_blk0 + b)),
            pl.BlockSpec(memory_space=pl.ANY),
        ],
        out_specs=pl.BlockSpec((tq, LANES), lambda b, h, i: (b * nq + i, h)),
        out_shape=jax.ShapeDtypeStruct(dst.shape, dst.dtype),
        input_output_aliases={7: 0},
        scratch_shapes=[pltpu.VMEM((2, seq + n_ctx, 2 * min(tq, DIFF_SUB_ROWS)), F32)],
        compiler_params=_cparams("parallel", "parallel", "parallel"),
        name="diff_attn",
    )(lam_vec, subln_g, z, z, vt, z, vt, dst)


def _diff_attention_ctx(z, vt, lam_vec, subln_g, lam_init, dst, *, n_batch, seq, n_ctx, n_heads):
    ctx_blk0 = n_batch * seq // n_ctx
    return pl.pallas_call(
        functools.partial(_diff_kernel, lam_init=lam_init, has_x=False),
        grid=(n_batch, n_heads),
        in_specs=[
            pl.BlockSpec(lam_vec.shape, lambda b, h: (0, 0)),
            pl.BlockSpec((1, LANES), lambda b, h: (0, 0)),
            pl.BlockSpec((n_ctx, LANES), lambda b, h: (ctx_blk0 + b, h)),
            pl.BlockSpec((n_ctx, LANES), lambda b, h: (ctx_blk0 + b, n_heads + h)),
            pl.BlockSpec((LANES, n_ctx), lambda b, h: (h, ctx_blk0 + b)),
            pl.BlockSpec(memory_space=pl.ANY),
        ],
        out_specs=pl.BlockSpec((n_ctx, LANES), lambda b, h: (ctx_blk0 + b, h)),
        out_shape=jax.ShapeDtypeStruct(dst.shape, dst.dtype),
        input_output_aliases={5: 0},
        scratch_shapes=[pltpu.VMEM((2, n_ctx, 2 * min(n_ctx, DIFF_SUB_ROWS)), F32)],
        compiler_params=_cparams("parallel", "parallel"),
        name="diff_attn_ctx",
    )(lam_vec, subln_g, z, z, vt, dst)


def _outproj_kernel(a0_ref, a1_ref, w_ref, x_ref, g_ref, gate_ref, *refs, with_router):
    half = a0_ref.shape[1]
    y = _dot(a0_ref[...], w_ref[:half, :]) + _dot(a1_ref[...], w_ref[half:, :])
    x_new = x_ref[...] + gate_ref[...] * (_rms(y) * g_ref[...])
    if with_router:
        g2_ref, sh_ref, sc_ref, rw_ref, o_ref, comb_ref, sel_ref = refs
        comb_ref[...], sel_ref[...] = _route(_modulate(x_new, g2_ref[...], sh_ref[...], sc_ref[...]), rw_ref[...])
    else:
        (o_ref,) = refs
    o_ref[...] = x_new


def _outproj(a0, a0_col, a1, a1_col, w, xt, normg, mod, k_gate, router_w=None, *, tm, rows, seq, n_batch):
    d = xt.shape[1]
    half = d // 2
    with_router = router_w is not None

    def mrow(i):
        return jnp.minimum(i * tm // seq, n_batch)

    in_specs = [
        pl.BlockSpec((tm, half), lambda i: (i, a0_col)),
        pl.BlockSpec((tm, half), lambda i: (i, a1_col)),
        pl.BlockSpec((d, d), lambda i: (0, 0)),
        pl.BlockSpec((tm, d), lambda i: (i, 0)),
        pl.BlockSpec((None, 1, d), lambda i: (1, 0, 0)),
        pl.BlockSpec((None, None, 1, d), lambda i: (mrow(i), k_gate, 0, 0)),
    ]
    args = [a0, a1, w, xt, normg, mod]
    out_specs = [pl.BlockSpec((tm, d), lambda i: (i, 0))]
    out_shape = [jax.ShapeDtypeStruct((rows, d), F32)]
    if with_router:
        in_specs += [
            pl.BlockSpec((None, 1, d), lambda i: (2, 0, 0)),
            pl.BlockSpec((None, None, 1, d), lambda i: (mrow(i), 3, 0, 0)),
            pl.BlockSpec((None, None, 1, d), lambda i: (mrow(i), 4, 0, 0)),
            pl.BlockSpec((d, LANES), lambda i: (0, 0)),
        ]
        args += [normg, mod, mod, jnp.zeros((d, LANES), F32).at[:, :N_EXPERTS].set(router_w)]
        out_specs += [pl.BlockSpec((tm, LANES), lambda i: (i, 0))] * 2
        out_shape += [jax.ShapeDtypeStruct((rows, LANES), F32)] * 2
    res = pl.pallas_call(
        functools.partial(_outproj_kernel, with_router=with_router),
        grid=(rows // tm,),
        in_specs=in_specs,
        out_specs=out_specs,
        out_shape=out_shape,
        compiler_params=_cparams("parallel"),
        name="outproj",
    )(*args)
    return res if with_router else res[0]


def _route(h, rw):
    logits = _dot_split(_split_bf16(h), _split_bf16(rw))
    lane = lax.broadcasted_iota(jnp.int32, logits.shape, 1)
    ninf = -jnp.inf
    logits = jnp.where(lane < N_EXPERTS, logits, ninf)
    m1 = jnp.max(logits, axis=-1, keepdims=True)
    i1 = jnp.min(jnp.where(logits == m1, lane, LANES), axis=-1, keepdims=True)
    sel1 = lane == i1
    rest = jnp.where(sel1, ninf, logits)
    m2 = jnp.max(rest, axis=-1, keepdims=True)
    i2 = jnp.min(jnp.where(rest == m2, lane, LANES), axis=-1, keepdims=True)
    sel2 = lane == i2
    e2 = jnp.exp(m2 - m1)
    den = 1.0 + e2
    comb = jnp.where(sel1, 1.0 / den, 0.0) + jnp.where(sel2, e2 / den, 0.0)
    return comb, jnp.where(sel1 | sel2, 1.0, 0.0)


SWIGLU_CHUNK = 256


def _swiglu(h, wg_ref, wu_ref, wo_ref, u_off):
    width = wo_ref.shape[0]
    acc = None
    for c0 in range(0, width, SWIGLU_CHUNK):
        gp = _dot(h, wg_ref[:, c0:c0 + SWIGLU_CHUNK])
        up = _dot(h, wu_ref[:, u_off + c0:u_off + c0 + SWIGLU_CHUNK])
        a = gp * (1.0 / (1.0 + jnp.exp(-gp))) * up
        part = _dot(a.astype(BF16), wo_ref[c0:c0 + SWIGLU_CHUNK, :])
        acc = part if acc is None else acc + part
    return acc


def _ffn_kernel(x_ref, g2_ref, sh_ref, sc_ref, win_ref, wout_ref, g3_ref, gate_ref, o_ref):
    x = x_ref[...]
    h = _modulate(x, g2_ref[...], sh_ref[...], sc_ref[...]).astype(BF16)
    y = _swiglu(h, win_ref, win_ref, wout_ref, wout_ref.shape[0])
    o_ref[...] = x + gate_ref[...] * (_rms(y) * g3_ref[...])


def _ffn(xt, normg, mod, w_in, w_out, *, tm, rows, seq, n_batch):
    d = xt.shape[1]

    def mrow(i):
        return jnp.minimum(i * tm // seq, n_batch)

    in_specs = [
        pl.BlockSpec((tm, d), lambda i: (i, 0)),
        pl.BlockSpec((None, 1, d), lambda i: (2, 0, 0)),
        pl.BlockSpec((None, None, 1, d), lambda i: (mrow(i), 3, 0, 0)),
        pl.BlockSpec((None, None, 1, d), lambda i: (mrow(i), 4, 0, 0)),
        pl.BlockSpec(w_in.shape, lambda i: (0, 0)),
        pl.BlockSpec(w_out.shape, lambda i: (0, 0)),
        pl.BlockSpec((None, 1, d), lambda i: (3, 0, 0)),
        pl.BlockSpec((None, None, 1, d), lambda i: (mrow(i), 5, 0, 0)),
    ]
    return pl.pallas_call(
        _ffn_kernel,
        grid=(rows // tm,),
        in_specs=in_specs,
        out_specs=pl.BlockSpec((tm, d), lambda i: (i, 0)),
        out_shape=jax.ShapeDtypeStruct((rows, d), F32),
        compiler_params=_cparams("parallel"),
        name="ffn",
    )(xt, normg, mod, mod, w_in, w_out, normg, mod)


MOE_DISPATCH_ROWS = 256
MOE_COMBINE_ROWS = 512
MOE_SEG_ALIGN = 16
MOE_TILE = 512


def _moe_keys(sel):
    nb = sel.shape[0]
    ti = lax.broadcasted_iota(jnp.int32, (nb, nb), 0)
    tj = lax.broadcasted_iota(jnp.int32, (nb, nb), 1)
    lower = jnp.where(tj < ti, 1.0, 0.0).astype(BF16)
    rank = _dot(lower, sel.astype(BF16))
    return jnp.where(sel > 0.0, rank, -1.0)


def _moe_dispatch_kernel(nchunk_ref, off_ref, x_ref, g2_ref, sh_ref, sc_ref, sel_ref, _, hs_hbm,
                         h_scr, keyt_scr, stage, sems, *, n_exp):
    b = pl.program_id(0)
    rows = MOE_DISPATCH_ROWS
    h_scr[...] = _modulate(x_ref[...], g2_ref[...], sh_ref[...], sc_ref[...]).astype(BF16)
    keyt_scr[...] = _moe_keys(sel_ref[...]).T

    def copy(slot, row0):
        return pltpu.make_async_copy(stage.at[slot], hs_hbm.at[pl.ds(row0, rows), :], sems.at[slot])

    issued = jnp.int32(0)
    for e in range(n_exp):
        key_row = keyt_scr[e:e + 1, :]
        base = off_ref[b * n_exp + e]

        def chunk(c, k):
            slot = k % 2

            @pl.when(k >= 2)
            def _():
                copy(slot, 0).wait()

            r = (c * rows + lax.broadcasted_iota(jnp.int32, (rows, 1), 0)).astype(F32)
            onehot = jnp.where(key_row == r, 1.0, 0.0).astype(BF16)
            stage[slot] = _dot(onehot, h_scr[...]).astype(BF16)
            copy(slot, pl.multiple_of(base + c * rows, MOE_SEG_ALIGN)).start()
            return k + 1

        issued = lax.fori_loop(0, nchunk_ref[b * n_exp + e], chunk, issued)

    @pl.when(issued >= 1)
    def _():
        copy((issued - 1) % 2, 0).wait()

    @pl.when(issued >= 2)
    def _():
        copy(issued % 2, 0).wait()


def _moe_experts_kernel(tile_ref, exp_ref, valid_ref, hs_ref, wg_ref, wu_ref, wo_ref, y_ref, acc, *, n_f):
    t = pl.program_id(0)
    f = pl.program_id(1)

    @pl.when(valid_ref[t] == 1)
    def _():
        part = _swiglu(hs_ref[...], wg_ref, wu_ref, wo_ref, 0)

        @pl.when(f == 0)
        def _():
            acc[...] = part

        @pl.when(f > 0)
        def _():
            acc[...] += part

        @pl.when(f == n_f - 1)
        def _():
            y_ref[...] = acc[...].astype(y_ref.dtype)


def _moe_combine_kernel(npiece_ref, off_ref, x_ref, g3_ref, gate_ref, comb_ref, sel_ref, y_hbm, o_ref,
                        key_scr, stage, sems, *, n_exp):
    b = pl.program_id(0)
    rows = MOE_COMBINE_ROWS
    nb = x_ref.shape[0]
    key_scr[...] = _moe_keys(sel_ref[...])
    o_ref[...] = jnp.zeros_like(o_ref)
    lane = lax.broadcasted_iota(jnp.int32, (nb, LANES), 1)

    def copy(slot, row0):
        return pltpu.make_async_copy(y_hbm.at[pl.ds(row0, rows), :], stage.at[slot], sems.at[slot])

    def row0(e, c):
        return pl.multiple_of(off_ref[b * n_exp + e] + c * rows, MOE_SEG_ALIGN)

    copy(0, row0(0, 0)).start()
    done = jnp.int32(0)
    for e in range(n_exp):
        pick = lane == e
        key_col = jnp.sum(jnp.where(pick, key_scr[...], 0.0), axis=-1, keepdims=True)
        gate_col = jnp.sum(jnp.where(pick, comb_ref[...], 0.0), axis=-1, keepdims=True)
        n_pieces = npiece_ref[b * n_exp + e]

        def piece(c, k, e=e, n_pieces=n_pieces, key_col=key_col, gate_col=gate_col):
            slot = k % 2
            copy(slot, 0).wait()

            @pl.when(c + 1 < n_pieces)
            def _():
                copy(1 - slot, row0(e, c + 1)).start()

            if e + 1 < n_exp:
                @pl.when(c + 1 == n_pieces)
                def _():
                    copy(1 - slot, row0(e + 1, 0)).start()

            r = (c * rows + lax.broadcasted_iota(jnp.int32, (1, rows), 1)).astype(F32)
            onehot = jnp.where(key_col == r, 1.0, 0.0).astype(BF16)
            o_ref[...] += gate_col * _dot(onehot, stage[slot])
            return k + 1

        done = lax.fori_loop(0, n_pieces, piece, done)

    o_ref[...] = x_ref[...] + gate_ref[...] * (_rms(o_ref[...]) * g3_ref[...])


def _moe(xt, normg, mod, w_in, w_out, j, comb, sel, *, nb, tf, rows, seq, n_batch):
    d = xt.shape[1]
    n_exp, fdim = w_out.shape[1], w_out.shape[2]
    n_f = fdim // tf
    n_blk = rows // nb
    i32 = jnp.int32

    counts = jnp.sum(sel.reshape(n_blk, nb, LANES)[:, :, :n_exp], axis=1).astype(i32)
    seg = -(-counts // MOE_SEG_ALIGN) * MOE_SEG_ALIGN
    region = -(-(jnp.sum(seg, axis=0) + MOE_DISPATCH_ROWS) // MOE_TILE) * MOE_TILE
    region_end = jnp.cumsum(region)
    region_start = region_end - region
    off = (region_start[None, :] + jnp.cumsum(seg, axis=0) - seg).reshape(n_blk * n_exp).astype(i32)
    nchunk = (-(-counts // MOE_DISPATCH_ROWS)).reshape(n_blk * n_exp).astype(i32)
    npiece = jnp.maximum(-(-counts // MOE_COMBINE_ROWS), 1).reshape(n_blk * n_exp).astype(i32)
    cap = 2 * rows + n_blk * n_exp * (MOE_SEG_ALIGN - 1) + n_exp * (MOE_DISPATCH_ROWS + MOE_TILE - 1)
    n_tiles = -(-cap // MOE_TILE)
    cap = n_tiles * MOE_TILE + MOE_COMBINE_ROWS
    tiles = jnp.arange(n_tiles, dtype=i32)
    n_used = region_end[-1] // MOE_TILE
    tile_map = jnp.minimum(tiles, n_used - 1).astype(i32)
    tile_exp = jnp.minimum(jnp.searchsorted(region_end, tile_map * MOE_TILE, side="right"), n_exp - 1).astype(i32)
    tile_valid = (tiles < n_used).astype(i32)

    def mrow(i):
        return jnp.minimum(i * nb // seq, n_batch)

    hs = pl.pallas_call(
        functools.partial(_moe_dispatch_kernel, n_exp=n_exp),
        grid_spec=pltpu.PrefetchScalarGridSpec(
            num_scalar_prefetch=2,
            grid=(n_blk,),
            in_specs=[
                pl.BlockSpec((nb, d), lambda i, *_: (i, 0)),
                pl.BlockSpec((None, 1, d), lambda i, *_: (2, 0, 0)),
                pl.BlockSpec((None, None, 1, d), lambda i, *_: (mrow(i), 3, 0, 0)),
                pl.BlockSpec((None, None, 1, d), lambda i, *_: (mrow(i), 4, 0, 0)),
                pl.BlockSpec((nb, LANES), lambda i, *_: (i, 0)),
                pl.BlockSpec(memory_space=pl.ANY),
            ],
            out_specs=pl.BlockSpec(memory_space=pl.ANY),
            scratch_shapes=[
                pltpu.VMEM((nb, d), BF16),
                pltpu.VMEM((LANES, nb), F32),
                pltpu.VMEM((2, MOE_DISPATCH_ROWS, d), BF16),
                pltpu.SemaphoreType.DMA((2,)),
            ],
        ),
        out_shape=jax.ShapeDtypeStruct((cap, d), BF16),
        input_output_aliases={7: 0},
        compiler_params=_cparams("arbitrary"),
        name="moe_dispatch",
    )(nchunk, off, xt, normg, mod, mod, sel, jnp.zeros((cap, d), BF16))

    ys = pl.pallas_call(
        functools.partial(_moe_experts_kernel, n_f=n_f),
        grid_spec=pltpu.PrefetchScalarGridSpec(
            num_scalar_prefetch=3,
            grid=(n_tiles, n_f),
            in_specs=[
                pl.BlockSpec((MOE_TILE, d), lambda t, f, tm, te, tv: (tm[t], 0)),
                pl.BlockSpec((None, None, d, tf), lambda t, f, tm, te, tv: (j, te[t], 0, f)),
                pl.BlockSpec((None, None, d, tf), lambda t, f, tm, te, tv: (j, te[t], 0, n_f + f)),
                pl.BlockSpec((None, None, tf, d), lambda t, f, tm, te, tv: (j, te[t], f, 0)),
            ],
            out_specs=pl.BlockSpec((MOE_TILE, d), lambda t, f, tm, te, tv: (tm[t], 0)),
            scratch_shapes=[pltpu.VMEM((MOE_TILE, d), F32)],
        ),
        out_shape=jax.ShapeDtypeStruct((cap, d), BF16),
        input_output_aliases={3: 0},
        compiler_params=_cparams("arbitrary", "arbitrary"),
        name="moe_experts",
    )(tile_map, tile_exp, tile_valid, hs, w_in, w_in, w_out)

    return pl.pallas_call(
        functools.partial(_moe_combine_kernel, n_exp=n_exp),
        grid_spec=pltpu.PrefetchScalarGridSpec(
            num_scalar_prefetch=2,
            grid=(n_blk,),
            in_specs=[
                pl.BlockSpec((nb, d), lambda i, *_: (i, 0)),
                pl.BlockSpec((None, 1, d), lambda i, *_: (3, 0, 0)),
                pl.BlockSpec((None, None, 1, d), lambda i, *_: (mrow(i), 5, 0, 0)),
                pl.BlockSpec((nb, LANES), lambda i, *_: (i, 0)),
                pl.BlockSpec((nb, LANES), lambda i, *_: (i, 0)),
                pl.BlockSpec(memory_space=pl.ANY),
            ],
            out_specs=pl.BlockSpec((nb, d), lambda i, *_: (i, 0)),
            scratch_shapes=[
                pltpu.VMEM((nb, LANES), F32),
                pltpu.VMEM((2, MOE_COMBINE_ROWS, d), BF16),
                pltpu.SemaphoreType.DMA((2,)),
            ],
        ),
        out_shape=jax.ShapeDtypeStruct((rows, d), F32),
        compiler_params=_cparams("parallel"),
        name="moe_combine",
    )(npiece, off, xt, normg, mod, comb, sel, ys)


def _lambda_init(layer):
    return 0.8 - 0.6 * math.exp(-0.3 * layer)


def kernel(x, c, ctx, c_ctx, ada_w, ada_b, norm_g, mix_in_w, mix_out_w, win_sink, diff_qkv_w, diff_out_w,
           diff_lambda, diff_subln_g, ffn_in_w, ffn_out_w, router_w, expert_in_w, expert_out_w):
    n_batch, seq, d = x.shape
    n_ctx = ctx.shape[1]
    depth = ada_w.shape[0]
    n_lat = n_batch * seq
    n_all = n_lat + n_batch * n_ctx
    fdim = mix_in_w.shape[2] - (d // 2 + 2 * LANES)
    q_cols = d // 2
    n_diff_heads = d // LANES

    tm_proj = _pick_tile(512, seq, n_batch * n_ctx)
    tm_out = _pick_tile(512, seq, n_batch * n_ctx)
    tm_ffn = _pick_tile(512, seq, n_batch * n_ctx)
    nb_moe = _pick_tile(1024, seq, n_batch * n_ctx)
    tq_diff = _pick_tile(1024, seq)
    tq_win = _pick_tile(512, seq)
    common = dict(seq=seq, n_batch=n_batch)

    xt = jnp.concatenate([x.reshape(n_lat, d), ctx.reshape(n_batch * n_ctx, d)], axis=0)
    n_mod = -(-(n_batch + 1) // 8) * 8
    cv = jnp.zeros((n_mod, d), F32).at[:n_batch].set(c).at[n_batch].set(c_ctx)
    mods = _modvec(cv, ada_w, ada_b).reshape(depth, n_mod, 6, 1, d)
    rope = _rope_tables(seq, tm_proj)
    f_tables = _fourier_tables(seq)
    fc_tables = _dense_fourier_tables(n_ctx)

    n_f = fdim // LANES
    plan_even = ([(0, i * LANES, "plain") for i in range(n_f)]
                 + [(1, i * LANES, "rope_q_log2") for i in range(q_cols // LANES)]
                 + [(1, q_cols, "rope_k"), (1, q_cols + LANES, "plain")])
    plan_odd = ([(0, i * LANES, "rope_q_log2") for i in range(n_diff_heads)]
                + [(0, d + i * LANES, "rope_k") for i in range(n_diff_heads)]
                + [(1, i * LANES, "plain_t") for i in range(n_diff_heads)])

    expert_in_bf16 = expert_in_w.astype(BF16)
    expert_out_bf16 = expert_out_w.astype(BF16)

    for layer in range(depth):
        j = layer // 2
        need_ctx = layer < depth - 1
        rows = n_all if need_ctx else n_lat
        mod = mods[layer]
        ng = norm_g[layer].reshape(4, 1, d)
        if layer % 2 == 0:
            f, z = _proj(xt, ng, mod, 0, 1, mix_in_w[j].astype(BF16), rope, plan_even,
                         [(fdim, F32, False), (q_cols + 2 * LANES, BF16, False)], tm=tm_proj, n_lat=n_lat,
                         **common)
            k_col, v_col = q_cols // LANES, q_cols // LANES + 1
            att = dict(n_batch=n_batch, seq=seq, n_ctx=n_ctx, q_cols=q_cols, k_col=k_col, v_col=v_col)
            mix_f = _fourier(f, f_tables, jnp.zeros((rows, fdim), BF16), n_batch=n_batch, seq=seq)
            mix_a = _win_attention(z, win_sink[j], jnp.zeros((rows, q_cols), BF16), tq=tq_win, **att)
            if need_ctx:
                mix_f = _dense_fourier(f, fc_tables, mix_f, n_batch=n_batch, n_pos=n_ctx,
                                       row_block0=n_lat // n_ctx)
                mix_a = _ctx_gqa_attention(z, win_sink[j], mix_a, **att)
            xt = _outproj(mix_f, 0, mix_a, 0, mix_out_w[j].astype(BF16), xt, ng, mod, 2, tm=tm_out, rows=rows,
                          **common)
            xt = _ffn(xt, ng, mod, ffn_in_w[j].astype(BF16), ffn_out_w[j].astype(BF16),
                      tm=tm_ffn, rows=rows, **common)
        else:
            lam_init = _lambda_init(layer)
            z, vt = _proj(xt, ng, mod, 0, 1, diff_qkv_w[j].astype(BF16), rope, plan_odd,
                          [(2 * d, BF16, False), (d, BF16, True)], tm=tm_proj, n_lat=n_lat, **common)
            subg = diff_subln_g[j].reshape(1, LANES)
            att = dict(n_batch=n_batch, seq=seq, n_ctx=n_ctx, n_heads=n_diff_heads)
            mix = _diff_attention(z, vt, diff_lambda[j], subg, lam_init, jnp.zeros((rows, d), BF16), tq=tq_diff,
                                  **att)
            if need_ctx:
                mix = _diff_attention_ctx(z, vt, diff_lambda[j], subg, lam_init, mix, **att)
            xt, comb, sel = _outproj(mix, 0, mix, 1, diff_out_w[j].astype(BF16), xt, ng, mod, 2, router_w[j],
                                     tm=tm_out, rows=rows, **common)
            xt = _moe(xt, ng, mod, expert_in_bf16, expert_out_bf16, j, comb, sel,
                      nb=nb_moe, tf=1792, rows=rows, **common)
    return xt[:n_lat].reshape(n_batch, seq, d)
```

```python
import functools
import math

import numpy as np
import jax
import jax.numpy as jnp
from jax import lax
from jax.experimental import pallas as pl
from jax.experimental.pallas import tpu as pltpu

EPS = 1e-6
NEG = -1e30
HEAD_DIM = 64
LANES = 128
GRID_W = 64
BLOCK = 128
WINDOW = 128
ROPE_THETA = 10000.0
N_EXPERTS = 8
F32 = jnp.float32
BF16 = jnp.bfloat16
HIGHEST = lax.Precision.HIGHEST
VMEM_LIMIT = 56 * 1024 * 1024


def _cparams(*sem):
    return pltpu.CompilerParams(dimension_semantics=sem, vmem_limit_bytes=VMEM_LIMIT)


def _dot(a, b):
    return jnp.dot(a, b, preferred_element_type=F32)


def _dot_nt(a, b):
    return lax.dot_general(a, b, (((1,), (1,)), ((), ())), preferred_element_type=F32)


def _dot_hi(a, b):
    return jnp.dot(a, b, precision=HIGHEST, preferred_element_type=F32)


def _rms(v):
    return v * lax.rsqrt(jnp.mean(v * v, axis=-1, keepdims=True) + EPS)


def _modulate(x, g, sh, sc):
    return _rms(x) * g * (1.0 + sc) + sh


def _pick_tile(pref, *dims):
    t = pref
    while any(d % t for d in dims):
        t //= 2
    return t


def _modvec_kernel(c_ref, w_ref, b_ref, o_ref):
    cv = c_ref[...]
    s = cv * (1.0 / (1.0 + jnp.exp(-cv)))
    o_ref[...] = _dot(s.astype(BF16), w_ref[...].astype(BF16)) + b_ref[...]


def _modvec(cv, ada_w, ada_b):
    depth, d, n = ada_w.shape
    r = cv.shape[0]
    tn = _pick_tile(1536, n)
    return pl.pallas_call(
        _modvec_kernel,
        grid=(depth, n // tn),
        in_specs=[
            pl.BlockSpec((r, d), lambda l, j: (0, 0)),
            pl.BlockSpec((None, d, tn), lambda l, j: (l, 0, j)),
            pl.BlockSpec((None, 1, tn), lambda l, j: (l, 0, j)),
        ],
        out_specs=pl.BlockSpec((None, r, tn), lambda l, j: (l, 0, j)),
        out_shape=jax.ShapeDtypeStruct((depth, r, n), F32),
        compiler_params=_cparams("parallel", "parallel"),
        name="modvec",
    )(cv, ada_w, ada_b.reshape(depth, 1, n))


def _proj_kernel(x_ref, g_ref, sh_ref, sc_ref, w_ref, cos_ref, sa_ref, sb_ref, *o_refs, plan, group):
    h = _modulate(x_ref[...], g_ref[...], sh_ref[...], sc_ref[...]).astype(BF16)
    n = w_ref.shape[1]
    for g0 in range(0, n, group):
        acc = _dot(h, w_ref[:, g0:g0 + group])
        for c0 in range(0, group, LANES):
            oi, oc, mode = plan[(g0 + c0) // LANES]
            v = acc[:, c0:c0 + LANES]
            if mode.startswith("rope"):
                v = (v * cos_ref[...] + pltpu.roll(v, LANES - 16, 1) * sa_ref[...]
                     + pltpu.roll(v, 16, 1) * sb_ref[...])
                if mode == "rope_q_log2":
                    v = v * (HEAD_DIM ** -0.5 * LOG2E)
            if mode == "plain_t":
                o_refs[oi][oc:oc + LANES, :] = v.T.astype(o_refs[oi].dtype)
            else:
                o_refs[oi][:, oc:oc + LANES] = v.astype(o_refs[oi].dtype)


def _proj(xt, normg, mod, k_sh, k_sc, w, rope, plan, outs, *, tm, n_lat, seq, n_batch):
    t, d = xt.shape
    n = w.shape[1]
    group = _pick_tile(512, n)
    nx = n_lat // tm
    per = seq // tm

    def mrow(i):
        return jnp.minimum(i * tm // seq, n_batch)

    def rrow(i):
        return jnp.where(i < nx, i % per, per)

    in_specs = [
        pl.BlockSpec((tm, d), lambda i: (i, 0)),
        pl.BlockSpec((None, 1, d), lambda i: (0, 0, 0)),
        pl.BlockSpec((None, None, 1, d), lambda i: (mrow(i), k_sh, 0, 0)),
        pl.BlockSpec((None, None, 1, d), lambda i: (mrow(i), k_sc, 0, 0)),
        pl.BlockSpec((d, n), lambda i: (0, 0)),
        pl.BlockSpec((tm, LANES), lambda i: (rrow(i), 0)),
        pl.BlockSpec((tm, LANES), lambda i: (rrow(i), 0)),
        pl.BlockSpec((tm, LANES), lambda i: (rrow(i), 0)),
    ]
    out_specs = [pl.BlockSpec((wd, tm), lambda i: (0, i)) if tr else pl.BlockSpec((tm, wd), lambda i: (i, 0))
                 for wd, _, tr in outs]
    out_shape = [jax.ShapeDtypeStruct((wd, t) if tr else (t, wd), dt) for wd, dt, tr in outs]
    return pl.pallas_call(
        functools.partial(_proj_kernel, plan=plan, group=group),
        grid=(t // tm,),
        in_specs=in_specs,
        out_specs=out_specs,
        out_shape=out_shape,
        compiler_params=_cparams("parallel"),
        name="proj",
    )(xt, normg, mod, mod, w, *rope)


def _rope_tables(seq, tm):
    rows_count = seq // GRID_W
    rows = jnp.repeat(jnp.arange(rows_count), GRID_W).astype(F32)
    cols = jnp.tile(jnp.arange(GRID_W), rows_count).astype(F32)
    axis_dim = HEAD_DIM // 2
    inv = ROPE_THETA ** (-jnp.arange(0, axis_dim, 2, dtype=F32) / axis_dim)
    ar = rows[:, None] * inv
    ac = cols[:, None] * inv
    cr, sr, cc, sc = jnp.cos(ar), jnp.sin(ar), jnp.cos(ac), jnp.sin(ac)
    z = jnp.zeros_like(sr)
    reps = LANES // HEAD_DIM
    cos = jnp.tile(jnp.concatenate([cr, cr, cc, cc], axis=1), (1, reps))
    sa = jnp.tile(jnp.concatenate([-sr, z, -sc, z], axis=1), (1, reps))
    sb = jnp.tile(jnp.concatenate([z, sr, z, sc], axis=1), (1, reps))
    ident = jnp.ones((tm, LANES), F32)
    zero = jnp.zeros((tm, LANES), F32)
    return (jnp.concatenate([cos, ident]), jnp.concatenate([sa, zero]), jnp.concatenate([sb, zero]))


FOURIER_LANES = 2 * LANES


def _split_bf16(t):
    hi = t.astype(BF16)
    return hi, (t - hi.astype(F32)).astype(BF16)


def _dot_split(a, b):
    return _dot(a[0], b[0]) + _dot(a[0], b[1]) + _dot(a[1], b[0])


def _fourier_tables(seq):
    n2 = GRID_W
    n1 = seq // n2
    norm = 1.0 / math.sqrt(seq * LANES)
    a = np.arange(n1)
    k1 = np.arange(n1)
    b = np.arange(n2)
    ang = (b[:, None, None] * k1[None, :, None] + (seq // n1) * k1[None, :, None] * a[None, None, :]) % seq
    th = 2.0 * np.pi * ang / seq
    m1 = np.concatenate([np.cos(th), -np.sin(th)], axis=1).astype(np.float32)
    ph = 2.0 * np.pi * ((b[:, None] * b[None, :]) % n2) / n2
    c2, s2 = np.cos(ph), np.sin(ph)
    g2 = np.block([[c2, s2], [-s2, c2]]).astype(np.float32)
    ch = np.arange(LANES)
    pc = 2.0 * np.pi * ((ch[:, None] * ch[None, :]) % LANES) / LANES
    cc = (np.cos(pc) * norm).astype(np.float32)
    sc = (np.sin(pc) * norm).astype(np.float32)
    cs = np.concatenate([cc, sc], axis=0)
    out = []
    for t in (m1, g2, cs):
        out.extend(_split_bf16(jnp.asarray(t)))
    return tuple(out)


FOURIER_ROW_CHUNK = 512


def _fourier_kernel(*refs, n1, n2, n_grp):
    u_refs = refs[:n_grp]
    m1h_ref, m1l_ref, g2h_ref, g2l_ref, csh_ref, csl_ref, _, o_ref, b_scr, xr_scr, xi_scr = refs[n_grp:]
    seq = n1 * n2
    for b in range(n2):
        xs = jnp.concatenate([u[pl.ds(b, n1, stride=n2), :] for u in u_refs], axis=1)
        z = _dot_split((m1h_ref[b], m1l_ref[b]), _split_bf16(xs))
        for g in range(n_grp):
            b_scr[g, 2 * n1 * b:2 * n1 * (b + 1), :] = z[:, g * LANES:(g + 1) * LANES]
    g2 = (g2h_ref[...], g2l_ref[...])
    for k1 in range(n1):
        bk = jnp.concatenate(
            [jnp.concatenate([b_scr.at[g][pl.ds(k1, n2, stride=2 * n1), :],
                              b_scr.at[g][pl.ds(n1 + k1, n2, stride=2 * n1), :]], axis=0)
             for g in range(n_grp)], axis=1)
        xk = _dot_split(g2, _split_bf16(bk))
        for g in range(n_grp):
            xr_scr[g, n2 * k1:n2 * (k1 + 1), :] = xk[:n2, g * LANES:(g + 1) * LANES]
            xi_scr[g, n2 * k1:n2 * (k1 + 1), :] = xk[n2:, g * LANES:(g + 1) * LANES]
    cs = (csh_ref[...], csl_ref[...])
    rc = min(FOURIER_ROW_CHUNK, seq)
    for g in range(n_grp):
        for r0 in range(0, seq, rc):
            x = jnp.concatenate([xr_scr[g, r0:r0 + rc, :], xi_scr[g, r0:r0 + rc, :]], axis=1)
            xr_scr[g, r0:r0 + rc, :] = _dot_split(_split_bf16(x), cs)
    for g in range(n_grp):
        for k2 in range(n2):
            o_ref[n1 * k2:n1 * (k2 + 1), g * LANES:(g + 1) * LANES] = (
                xr_scr.at[g][pl.ds(k2, n1, stride=n2), :].astype(o_ref.dtype))


def _fourier(f, tables, dst, *, n_batch, seq):
    n2 = GRID_W
    n1 = seq // n2
    n_grp = FOURIER_LANES // LANES
    table_specs = [pl.BlockSpec(t.shape, (lambda b, g, nd=t.ndim: (0,) * nd)) for t in tables]
    u_specs = [pl.BlockSpec((seq, LANES), (lambda b, g, k=k: (b, g * n_grp + k))) for k in range(n_grp)]
    return pl.pallas_call(
        functools.partial(_fourier_kernel, n1=n1, n2=n2, n_grp=n_grp),
        grid=(n_batch, f.shape[1] // FOURIER_LANES),
        in_specs=u_specs + table_specs + [pl.BlockSpec(memory_space=pl.ANY)],
        out_specs=pl.BlockSpec((seq, FOURIER_LANES), lambda b, g: (b, g)),
        out_shape=jax.ShapeDtypeStruct(dst.shape, dst.dtype),
        input_output_aliases={n_grp + len(tables): 0},
        scratch_shapes=[pltpu.VMEM((n_grp, 2 * seq, LANES), F32), pltpu.VMEM((n_grp, seq, LANES), F32),
                        pltpu.VMEM((n_grp, seq, LANES), F32)],
        compiler_params=_cparams("parallel", "parallel"),
        name="fourier",
    )(*([f] * n_grp), *tables, dst)


def _dense_fourier_tables(n):
    norm = 1.0 / math.sqrt(n * LANES)
    p = np.arange(n)
    ph = 2.0 * np.pi * ((p[:, None] * p[None, :]) % n) / n
    ch = np.arange(LANES)
    pc = 2.0 * np.pi * ((ch[:, None] * ch[None, :]) % LANES) / LANES
    return (jnp.asarray(np.cos(ph).astype(np.float32)), jnp.asarray(np.sin(ph).astype(np.float32)),
            jnp.asarray((np.cos(pc) * norm).astype(np.float32)), jnp.asarray((np.sin(pc) * norm).astype(np.float32)))


def _dense_fourier_kernel(u_ref, cl_ref, sl_ref, cc_ref, sc_ref, _, o_ref):
    u = u_ref[...]
    y = _dot_hi(cl_ref[...], _dot_hi(u, cc_ref[...])) - _dot_hi(sl_ref[...], _dot_hi(u, sc_ref[...]))
    o_ref[...] = y.astype(o_ref.dtype)


def _dense_fourier(f, tables, dst, *, n_batch, n_pos, row_block0):
    cl, sl, cc, sc = tables
    groups = f.shape[1] // LANES
    return pl.pallas_call(
        _dense_fourier_kernel,
        grid=(n_batch, groups),
        in_specs=[
            pl.BlockSpec((n_pos, LANES), lambda b, g: (row_block0 + b, g)),
            pl.BlockSpec(cl.shape, lambda b, g: (0, 0)),
            pl.BlockSpec(sl.shape, lambda b, g: (0, 0)),
            pl.BlockSpec(cc.shape, lambda b, g: (0, 0)),
            pl.BlockSpec(sc.shape, lambda b, g: (0, 0)),
            pl.BlockSpec(memory_space=pl.ANY),
        ],
        out_specs=pl.BlockSpec((n_pos, LANES), lambda b, g: (row_block0 + b, g)),
        out_shape=jax.ShapeDtypeStruct(dst.shape, dst.dtype),
        input_output_aliases={5: 0},
        compiler_params=_cparams("parallel", "parallel"),
        name="fourier_ctx",
    )(f, cl, sl, cc, sc, dst)


def _win_kernel(sink_ref, q_ref, *refs, n_qtiles, has_local, n_heads, group_size):
    if has_local:
        kp_ref, kc_ref, kn_ref, vp_ref, vc_ref, vn_ref, kx_ref, vx_ref, _, o_ref = refs
    else:
        kx_ref, vx_ref, _, o_ref = refs
    tq = q_ref.shape[0]
    lane = lax.broadcasted_iota(jnp.int32, (1, LANES), 1)
    half_mask = [lane < HEAD_DIM, lane >= HEAD_DIM]
    pieces = [(kx_ref[...], vx_ref[...], None)]
    if has_local:
        n = pl.program_id(1)
        qi = lax.broadcasted_iota(jnp.int32, (tq, BLOCK), 0)
        kj = lax.broadcasted_iota(jnp.int32, (tq, BLOCK), 1)
        valid_prev = (kj >= qi) & (n >= 1)
        valid_next = (kj <= qi - (tq - WINDOW)) & (n <= n_qtiles - 2)
        di = lax.broadcasted_iota(jnp.int32, (tq, tq), 0) - lax.broadcasted_iota(jnp.int32, (tq, tq), 1)
        valid_mid = (di <= WINDOW) & (di >= -WINDOW)
        pieces += [(kp_ref[...], vp_ref[...], valid_prev), (kc_ref[...], vc_ref[...], valid_mid),
                   (kn_ref[...], vn_ref[...], valid_next)]
    def scores(head):
        pair, half = divmod(head, 2)
        kv = head // group_size
        qp = q_ref[:, pair * LANES:(pair + 1) * LANES].astype(F32)
        src = qp if half == kv else pltpu.roll(qp, HEAD_DIM, 1)
        qe = jnp.where(half_mask[kv], src, 0.0).astype(BF16)
        out = []
        for k, _, valid in pieces:
            s = _dot_nt(qe, k)
            out.append(s if valid is None else jnp.where(valid, s, NEG))
        return out

    def attend(head, sc):
        half = head % 2
        kv = head // group_size
        sink = sink_ref[head] * LOG2E
        m = jnp.zeros((tq, 1), F32) + sink
        for s in sc:
            m = jnp.maximum(m, jnp.max(s, axis=-1, keepdims=True))
        den = jnp.exp2(sink - m)
        pv = jnp.zeros((tq, LANES), F32)
        for s, (_, v, _) in zip(sc, pieces):
            p = jnp.exp2(s - m)
            den = den + jnp.sum(p, axis=-1, keepdims=True)
            pv = pv + _dot(p.astype(BF16), v)
        pv = pv / den
        return pv if half == kv else pltpu.roll(pv, HEAD_DIM, 1)

    sc_next = scores(0)
    out_pair = None
    for head in range(n_heads):
        sc_cur = sc_next
        if head + 1 < n_heads:
            sc_next = scores(head + 1)
        pv = attend(head, sc_cur)
        if head % 2 == 0:
            out_pair = pv
        else:
            pair = head // 2
            o_ref[:, pair * LANES:(pair + 1) * LANES] = jnp.where(half_mask[1], pv, out_pair).astype(o_ref.dtype)


def _win_attention(z, sink, dst, *, n_batch, seq, n_ctx, q_cols, k_col, v_col, tq):
    nbk = seq // BLOCK
    nq = seq // tq
    per = tq // BLOCK
    ctx_blk0 = n_batch * seq // n_ctx
    n_heads = q_cols // HEAD_DIM
    group_size = n_heads // (LANES // HEAD_DIM)

    def edge(col, first):
        return pl.BlockSpec(
            (BLOCK, LANES), lambda b, n: (b * nbk + jnp.clip(n * per + first, 0, nbk - 1), col))

    def mid(col):
        return pl.BlockSpec((tq, LANES), lambda b, n: (b * nq + n, col))

    return pl.pallas_call(
        functools.partial(_win_kernel, n_qtiles=nq, has_local=True, n_heads=n_heads, group_size=group_size),
        grid=(n_batch, nq),
        in_specs=[
            pl.BlockSpec(memory_space=pltpu.SMEM),
            pl.BlockSpec((tq, q_cols), lambda b, n: (b * nq + n, 0)),
            edge(k_col, -1), mid(k_col), edge(k_col, per),
            edge(v_col, -1), mid(v_col), edge(v_col, per),
            pl.BlockSpec((n_ctx, LANES), lambda b, n: (ctx_blk0 + b, k_col)),
            pl.BlockSpec((n_ctx, LANES), lambda b, n: (ctx_blk0 + b, v_col)),
            pl.BlockSpec(memory_space=pl.ANY),
        ],
        out_specs=pl.BlockSpec((tq, q_cols), lambda b, n: (b * nq + n, 0)),
        out_shape=jax.ShapeDtypeStruct(dst.shape, dst.dtype),
        input_output_aliases={10: 0},
        compiler_params=_cparams("parallel", "parallel"),
        name="win_attn",
    )(sink, z, z, z, z, z, z, z, z, z, dst)


def _ctx_gqa_attention(z, sink, dst, *, n_batch, seq, n_ctx, q_cols, k_col, v_col):
    ctx_blk0 = n_batch * seq // n_ctx
    n_heads = q_cols // HEAD_DIM
    group_size = n_heads // (LANES // HEAD_DIM)
    return pl.pallas_call(
        functools.partial(_win_kernel, n_qtiles=0, has_local=False, n_heads=n_heads, group_size=group_size),
        grid=(n_batch,),
        in_specs=[
            pl.BlockSpec(memory_space=pltpu.SMEM),
            pl.BlockSpec((n_ctx, q_cols), lambda b: (ctx_blk0 + b, 0)),
            pl.BlockSpec((n_ctx, LANES), lambda b: (ctx_blk0 + b, k_col)),
            pl.BlockSpec((n_ctx, LANES), lambda b: (ctx_blk0 + b, v_col)),
            pl.BlockSpec(memory_space=pl.ANY),
        ],
        out_specs=pl.BlockSpec((n_ctx, q_cols), lambda b: (ctx_blk0 + b, 0)),
        out_shape=jax.ShapeDtypeStruct(dst.shape, dst.dtype),
        input_output_aliases={4: 0},
        compiler_params=_cparams("parallel"),
        name="ctx_gqa",
    )(sink, z, z, z, dst)


LOG2E = math.log2(math.e)
DIFF_SUB_ROWS = 128
DIFF_KEY_CHUNK = 1024


def _diff_kernel(lam_ref, g_ref, q_ref, *refs, lam_init, has_x):
    if has_x:
        kx_ref, vtx_ref, kc_ref, vtc_ref, _, o_ref, s_scr = refs
        nx = kx_ref.shape[0]
    else:
        kc_ref, vtc_ref, _, o_ref, s_scr = refs
        nx = 0
    nc = kc_ref.shape[0]
    mc = s_scr.shape[2]
    sub = mc // 2
    n_sub = q_ref.shape[0] // sub
    lv = lam_ref[...]
    lam = (jnp.exp(jnp.sum(lv[0:1] * lv[1:2], axis=-1, keepdims=True))
           - jnp.exp(jnp.sum(lv[2:3] * lv[3:4], axis=-1, keepdims=True)) + lam_init)
    lane = lax.broadcasted_iota(jnp.int32, (1, LANES), 1)
    segs = []
    if has_x:
        kw = min(DIFF_KEY_CHUNK, nx)
        segs += [(kx_ref, vtx_ref, r0, kw, r0) for r0 in range(0, nx, kw)]
    segs.append((kc_ref, vtc_ref, 0, nc, nx))

    def scores(u):
        q = q_ref[u * sub:(u + 1) * sub, :]
        zero = jnp.zeros_like(q)
        qs = jnp.concatenate([jnp.where(lane < HEAD_DIM, q, zero), jnp.where(lane >= HEAD_DIM, q, zero)], axis=0)
        m_run = jnp.full((8, mc), -jnp.inf, F32)
        for k_ref, _, r0, w, s0 in segs:
            st = _dot_nt(k_ref[r0:r0 + w, :], qs)
            s_scr[u % 2, s0:s0 + w, :] = st
            m_run = jnp.maximum(m_run, jnp.max(st.reshape(w // 8, 8, mc), axis=0))
        return jnp.broadcast_to(jnp.max(m_run, axis=0, keepdims=True), (8, mc))

    def attend(u, m8):
        l_run = jnp.zeros((8, mc), F32)
        acc = jnp.zeros((LANES, mc), F32)
        for _, vt_ref, r0, w, s0 in segs:
            p = jnp.exp2(s_scr[u % 2, s0:s0 + w, :].reshape(w // 8, 8, mc) - m8[None])
            l_run = l_run + jnp.sum(p, axis=0)
            acc = acc + _dot(vt_ref[:, r0:r0 + w], p.reshape(w, mc).astype(BF16))
        ot = acc / jnp.sum(l_run, axis=0, keepdims=True)
        o = (ot[:, :sub] - lam * ot[:, sub:]).T
        o = _rms(o) * g_ref[...] * (1.0 - lam_init)
        o_ref[u * sub:(u + 1) * sub, :] = o.astype(o_ref.dtype)

    m_next = scores(0)
    for u in range(n_sub):
        m_cur = m_next
        if u + 1 < n_sub:
            m_next = scores(u + 1)
        attend(u, m_cur)


def _diff_attention(z, vt, lam_vec, subln_g, lam_init, dst, *, n_batch, seq, n_ctx, n_heads, tq):
    ctx_blk0 = n_batch * seq // n_ctx
    nq = seq // tq
    return pl.pallas_call(
        functools.partial(_diff_kernel, lam_init=lam_init, has_x=True),
        grid=(n_batch, n_heads, nq),
        in_specs=[
            pl.BlockSpec(lam_vec.shape, lambda b, h, i: (0, 0)),
            pl.BlockSpec((1, LANES), lambda b, h, i: (0, 0)),
            pl.BlockSpec((tq, LANES), lambda b, h, i: (b * nq + i, h)),
            pl.BlockSpec((seq, LANES), lambda b, h, i: (b, n_heads + h)),
            pl.BlockSpec((LANES, seq), lambda b, h, i: (h, b)),
            pl.BlockSpec((n_ctx, LANES), lambda b, h, i: (ctx_blk0 + b, n_heads + h)),
            pl.BlockSpec((LANES, n_ctx), lambda b, h, i: (h, ctx_blk0 + b)),
            pl.BlockSpec(memory_space=pl.ANY),
        ],
        out_specs=pl.BlockSpec((tq, LANES), lambda b, h, i: (b * nq + i, h)),
        out_shape=jax.ShapeDtypeStruct(dst.shape, dst.dtype),
        input_output_aliases={7: 0},
        scratch_shapes=[pltpu.VMEM((2, seq + n_ctx, 2 * min(tq, DIFF_SUB_ROWS)), F32)],
        compiler_params=_cparams("parallel", "parallel", "parallel"),
        name="diff_attn",
    )(lam_vec, subln_g, z, z, vt, z, vt, dst)


def _diff_attention_ctx(z, vt, lam_vec, subln_g, lam_init, dst, *, n_batch, seq, n_ctx, n_heads):
    ctx_blk0 = n_batch * seq // n_ctx
    return pl.pallas_call(
        functools.partial(_diff_kernel, lam_init=lam_init, has_x=False),
        grid=(n_batch, n_heads),
        in_specs=[
            pl.BlockSpec(lam_vec.shape, lambda b, h: (0, 0)),
            pl.BlockSpec((1, LANES), lambda b, h: (0, 0)),
            pl.BlockSpec((n_ctx, LANES), lambda b, h: (ctx_blk0 + b, h)),
            pl.BlockSpec((n_ctx, LANES), lambda b, h: (ctx_blk0 + b, n_heads + h)),
            pl.BlockSpec((LANES, n_ctx), lambda b, h: (h, ctx_blk0 + b)),
            pl.BlockSpec(memory_space=pl.ANY),
        ],
        out_specs=pl.BlockSpec((n_ctx, LANES), lambda b, h: (ctx_blk0 + b, h)),
        out_shape=jax.ShapeDtypeStruct(dst.shape, dst.dtype),
        input_output_aliases={5: 0},
        scratch_shapes=[pltpu.VMEM((2, n_ctx, 2 * min(n_ctx, DIFF_SUB_ROWS)), F32)],
        compiler_params=_cparams("parallel", "parallel"),
        name="diff_attn_ctx",
    )(lam_vec, subln_g, z, z, vt, dst)


def _outproj_kernel(a0_ref, a1_ref, w_ref, x_ref, g_ref, gate_ref, o_ref):
    half = a0_ref.shape[1]
    y = _dot(a0_ref[...], w_ref[:half, :]) + _dot(a1_ref[...], w_ref[half:, :])
    o_ref[...] = x_ref[...] + gate_ref[...] * (_rms(y) * g_ref[...])


def _outproj(a0, a0_col, a1, a1_col, w, xt, normg, mod, k_gate, *, tm, rows, seq, n_batch):
    d = xt.shape[1]
    half = d // 2

    def mrow(i):
        return jnp.minimum(i * tm // seq, n_batch)

    return pl.pallas_call(
        _outproj_kernel,
        grid=(rows // tm,),
        in_specs=[
            pl.BlockSpec((tm, half), lambda i: (i, a0_col)),
            pl.BlockSpec((tm, half), lambda i: (i, a1_col)),
            pl.BlockSpec((d, d), lambda i: (0, 0)),
            pl.BlockSpec((tm, d), lambda i: (i, 0)),
            pl.BlockSpec((None, 1, d), lambda i: (1, 0, 0)),
            pl.BlockSpec((None, None, 1, d), lambda i: (mrow(i), k_gate, 0, 0)),
        ],
        out_specs=pl.BlockSpec((tm, d), lambda i: (i, 0)),
        out_shape=jax.ShapeDtypeStruct((rows, d), F32),
        compiler_params=_cparams("parallel"),
        name="outproj",
    )(a0, a1, w, xt, normg, mod)


def _router_kernel(x_ref, g_ref, sh_ref, sc_ref, rw_ref, o_ref, sel_ref):
    h = _modulate(x_ref[...], g_ref[...], sh_ref[...], sc_ref[...])
    logits = _dot_split(_split_bf16(h), _split_bf16(rw_ref[...]))
    lane = lax.broadcasted_iota(jnp.int32, logits.shape, 1)
    ninf = -jnp.inf
    logits = jnp.where(lane < N_EXPERTS, logits, ninf)
    m1 = jnp.max(logits, axis=-1, keepdims=True)
    i1 = jnp.min(jnp.where(logits == m1, lane, LANES), axis=-1, keepdims=True)
    sel1 = lane == i1
    rest = jnp.where(sel1, ninf, logits)
    m2 = jnp.max(rest, axis=-1, keepdims=True)
    i2 = jnp.min(jnp.where(rest == m2, lane, LANES), axis=-1, keepdims=True)
    sel2 = lane == i2
    e2 = jnp.exp(m2 - m1)
    den = 1.0 + e2
    o_ref[...] = jnp.where(sel1, 1.0 / den, 0.0) + jnp.where(sel2, e2 / den, 0.0)
    sel_ref[...] = jnp.where(sel1 | sel2, 1.0, 0.0)


def _router(xt, normg, mod, router_w, *, tm, rows, seq, n_batch):
    d = xt.shape[1]
    rw = jnp.zeros((d, LANES), F32).at[:, :N_EXPERTS].set(router_w)

    def mrow(i):
        return jnp.minimum(i * tm // seq, n_batch)

    return pl.pallas_call(
        _router_kernel,
        grid=(rows // tm,),
        in_specs=[
            pl.BlockSpec((tm, d), lambda i: (i, 0)),
            pl.BlockSpec((None, 1, d), lambda i: (2, 0, 0)),
            pl.BlockSpec((None, None, 1, d), lambda i: (mrow(i), 3, 0, 0)),
            pl.BlockSpec((None, None, 1, d), lambda i: (mrow(i), 4, 0, 0)),
            pl.BlockSpec((d, LANES), lambda i: (0, 0)),
        ],
        out_specs=[pl.BlockSpec((tm, LANES), lambda i: (i, 0))] * 2,
        out_shape=[jax.ShapeDtypeStruct((rows, LANES), F32)] * 2,
        compiler_params=_cparams("parallel"),
        name="router",
    )(xt, normg, mod, mod, rw)


SWIGLU_CHUNK = 256


def _swiglu(h, wg_ref, wu_ref, wo_ref, u_off):
    width = wo_ref.shape[0]
    acc = None
    for c0 in range(0, width, SWIGLU_CHUNK):
        gp = _dot(h, wg_ref[:, c0:c0 + SWIGLU_CHUNK])
        up = _dot(h, wu_ref[:, u_off + c0:u_off + c0 + SWIGLU_CHUNK])
        a = gp * (1.0 / (1.0 + jnp.exp(-gp))) * up
        part = _dot(a.astype(BF16), wo_ref[c0:c0 + SWIGLU_CHUNK, :])
        acc = part if acc is None else acc + part
    return acc


def _ffn_kernel(x_ref, g2_ref, sh_ref, sc_ref, win_ref, wout_ref, g3_ref, gate_ref, o_ref):
    x = x_ref[...]
    h = _modulate(x, g2_ref[...], sh_ref[...], sc_ref[...]).astype(BF16)
    y = _swiglu(h, win_ref, win_ref, wout_ref, wout_ref.shape[0])
    o_ref[...] = x + gate_ref[...] * (_rms(y) * g3_ref[...])


def _ffn(xt, normg, mod, w_in, w_out, *, tm, rows, seq, n_batch):
    d = xt.shape[1]

    def mrow(i):
        return jnp.minimum(i * tm // seq, n_batch)

    in_specs = [
        pl.BlockSpec((tm, d), lambda i: (i, 0)),
        pl.BlockSpec((None, 1, d), lambda i: (2, 0, 0)),
        pl.BlockSpec((None, None, 1, d), lambda i: (mrow(i), 3, 0, 0)),
        pl.BlockSpec((None, None, 1, d), lambda i: (mrow(i), 4, 0, 0)),
        pl.BlockSpec(w_in.shape, lambda i: (0, 0)),
        pl.BlockSpec(w_out.shape, lambda i: (0, 0)),
        pl.BlockSpec((None, 1, d), lambda i: (3, 0, 0)),
        pl.BlockSpec((None, None, 1, d), lambda i: (mrow(i), 5, 0, 0)),
    ]
    return pl.pallas_call(
        _ffn_kernel,
        grid=(rows // tm,),
        in_specs=in_specs,
        out_specs=pl.BlockSpec((tm, d), lambda i: (i, 0)),
        out_shape=jax.ShapeDtypeStruct((rows, d), F32),
        compiler_params=_cparams("parallel"),
        name="ffn",
    )(xt, normg, mod, mod, w_in, w_out, normg, mod)


MOE_DISPATCH_ROWS = 128
MOE_COMBINE_ROWS = 256
MOE_SEG_ALIGN = 16
MOE_TILE = 512


def _moe_keys(sel):
    nb = sel.shape[0]
    ti = lax.broadcasted_iota(jnp.int32, (nb, nb), 0)
    tj = lax.broadcasted_iota(jnp.int32, (nb, nb), 1)
    lower = jnp.where(tj < ti, 1.0, 0.0).astype(BF16)
    rank = _dot(lower, sel.astype(BF16))
    return jnp.where(sel > 0.0, rank, -1.0)


def _moe_dispatch_kernel(nchunk_ref, off_ref, x_ref, g2_ref, sh_ref, sc_ref, sel_ref, _, hs_hbm,
                         h_scr, keyt_scr, stage, sems, *, n_exp):
    b = pl.program_id(0)
    rows = MOE_DISPATCH_ROWS
    h_scr[...] = _modulate(x_ref[...], g2_ref[...], sh_ref[...], sc_ref[...]).astype(BF16)
    keyt_scr[...] = _moe_keys(sel_ref[...]).T

    def copy(slot, row0):
        return pltpu.make_async_copy(stage.at[slot], hs_hbm.at[pl.ds(row0, rows), :], sems.at[slot])

    issued = jnp.int32(0)
    for e in range(n_exp):
        key_row = keyt_scr[e:e + 1, :]
        base = off_ref[b * n_exp + e]

        def chunk(c, k):
            slot = k % 2

            @pl.when(k >= 2)
            def _():
                copy(slot, 0).wait()

            r = (c * rows + lax.broadcasted_iota(jnp.int32, (rows, 1), 0)).astype(F32)
            onehot = jnp.where(key_row == r, 1.0, 0.0).astype(BF16)
            stage[slot] = _dot(onehot, h_scr[...]).astype(BF16)
            copy(slot, pl.multiple_of(base + c * rows, MOE_SEG_ALIGN)).start()
            return k + 1

        issued = lax.fori_loop(0, nchunk_ref[b * n_exp + e], chunk, issued)

    @pl.when(issued >= 1)
    def _():
        copy((issued - 1) % 2, 0).wait()

    @pl.when(issued >= 2)
    def _():
        copy(issued % 2, 0).wait()


def _moe_experts_kernel(tile_ref, exp_ref, valid_ref, hs_ref, wg_ref, wu_ref, wo_ref, y_ref, acc, *, n_f):
    t = pl.program_id(0)
    f = pl.program_id(1)

    @pl.when(valid_ref[t] == 1)
    def _():
        part = _swiglu(hs_ref[...], wg_ref, wu_ref, wo_ref, 0)

        @pl.when(f == 0)
        def _():
            acc[...] = part

        @pl.when(f > 0)
        def _():
            acc[...] += part

        @pl.when(f == n_f - 1)
        def _():
            y_ref[...] = acc[...].astype(y_ref.dtype)


def _moe_combine_kernel(npiece_ref, off_ref, x_ref, g3_ref, gate_ref, comb_ref, sel_ref, y_hbm, o_ref,
                        key_scr, stage, sems, *, n_exp):
    b = pl.program_id(0)
    rows = MOE_COMBINE_ROWS
    nb = x_ref.shape[0]
    key_scr[...] = _moe_keys(sel_ref[...])
    o_ref[...] = jnp.zeros_like(o_ref)
    lane = lax.broadcasted_iota(jnp.int32, (nb, LANES), 1)

    def copy(slot, row0):
        return pltpu.make_async_copy(y_hbm.at[pl.ds(row0, rows), :], stage.at[slot], sems.at[slot])

    def row0(e, c):
        return pl.multiple_of(off_ref[b * n_exp + e] + c * rows, MOE_SEG_ALIGN)

    copy(0, row0(0, 0)).start()
    done = jnp.int32(0)
    for e in range(n_exp):
        pick = lane == e
        key_col = jnp.sum(jnp.where(pick, key_scr[...], 0.0), axis=-1, keepdims=True)
        gate_col = jnp.sum(jnp.where(pick, comb_ref[...], 0.0), axis=-1, keepdims=True)
        n_pieces = npiece_ref[b * n_exp + e]

        def piece(c, k, e=e, n_pieces=n_pieces, key_col=key_col, gate_col=gate_col):
            slot = k % 2
            copy(slot, 0).wait()

            @pl.when(c + 1 < n_pieces)
            def _():
                copy(1 - slot, row0(e, c + 1)).start()

            if e + 1 < n_exp:
                @pl.when(c + 1 == n_pieces)
                def _():
                    copy(1 - slot, row0(e + 1, 0)).start()

            r = (c * rows + lax.broadcasted_iota(jnp.int32, (1, rows), 1)).astype(F32)
            onehot = jnp.where(key_col == r, 1.0, 0.0).astype(BF16)
            o_ref[...] += gate_col * _dot(onehot, stage[slot])
            return k + 1

        done = lax.fori_loop(0, n_pieces, piece, done)

    o_ref[...] = x_ref[...] + gate_ref[...] * (_rms(o_ref[...]) * g3_ref[...])


def _moe(xt, normg, mod, w_in, w_out, j, comb, sel, *, nb, tf, rows, seq, n_batch):
    d = xt.shape[1]
    n_exp, fdim = w_out.shape[1], w_out.shape[2]
    n_f = fdim // tf
    n_blk = rows // nb
    i32 = jnp.int32

    counts = jnp.sum(sel.reshape(n_blk, nb, LANES)[:, :, :n_exp], axis=1).astype(i32)
    seg = -(-counts // MOE_SEG_ALIGN) * MOE_SEG_ALIGN
    region = -(-(jnp.sum(seg, axis=0) + MOE_DISPATCH_ROWS) // MOE_TILE) * MOE_TILE
    region_end = jnp.cumsum(region)
    region_start = region_end - region
    off = (region_start[None, :] + jnp.cumsum(seg, axis=0) - seg).reshape(n_blk * n_exp).astype(i32)
    nchunk = (-(-counts // MOE_DISPATCH_ROWS)).reshape(n_blk * n_exp).astype(i32)
    npiece = jnp.maximum(-(-counts // MOE_COMBINE_ROWS), 1).reshape(n_blk * n_exp).astype(i32)
    cap = 2 * rows + n_blk * n_exp * (MOE_SEG_ALIGN - 1) + n_exp * (MOE_DISPATCH_ROWS + MOE_TILE - 1)
    n_tiles = -(-cap // MOE_TILE)
    cap = n_tiles * MOE_TILE + MOE_COMBINE_ROWS
    tiles = jnp.arange(n_tiles, dtype=i32)
    n_used = region_end[-1] // MOE_TILE
    tile_map = jnp.minimum(tiles, n_used - 1).astype(i32)
    tile_exp = jnp.minimum(jnp.searchsorted(region_end, tile_map * MOE_TILE, side="right"), n_exp - 1).astype(i32)
    tile_valid = (tiles < n_used).astype(i32)

    def mrow(i):
        return jnp.minimum(i * nb // seq, n_batch)

    hs = pl.pallas_call(
        functools.partial(_moe_dispatch_kernel, n_exp=n_exp),
        grid_spec=pltpu.PrefetchScalarGridSpec(
            num_scalar_prefetch=2,
            grid=(n_blk,),
            in_specs=[
                pl.BlockSpec((nb, d), lambda i, *_: (i, 0)),
                pl.BlockSpec((None, 1, d), lambda i, *_: (2, 0, 0)),
                pl.BlockSpec((None, None, 1, d), lambda i, *_: (mrow(i), 3, 0, 0)),
                pl.BlockSpec((None, None, 1, d), lambda i, *_: (mrow(i), 4, 0, 0)),
                pl.BlockSpec((nb, LANES), lambda i, *_: (i, 0)),
                pl.BlockSpec(memory_space=pl.ANY),
            ],
            out_specs=pl.BlockSpec(memory_space=pl.ANY),
            scratch_shapes=[
                pltpu.VMEM((nb, d), BF16),
                pltpu.VMEM((LANES, nb), F32),
                pltpu.VMEM((2, MOE_DISPATCH_ROWS, d), BF16),
                pltpu.SemaphoreType.DMA((2,)),
            ],
        ),
        out_shape=jax.ShapeDtypeStruct((cap, d), BF16),
        input_output_aliases={7: 0},
        compiler_params=_cparams("arbitrary"),
        name="moe_dispatch",
    )(nchunk, off, xt, normg, mod, mod, sel, jnp.zeros((cap, d), BF16))

    ys = pl.pallas_call(
        functools.partial(_moe_experts_kernel, n_f=n_f),
        grid_spec=pltpu.PrefetchScalarGridSpec(
            num_scalar_prefetch=3,
            grid=(n_tiles, n_f),
            in_specs=[
                pl.BlockSpec((MOE_TILE, d), lambda t, f, tm, te, tv: (tm[t], 0)),
                pl.BlockSpec((None, None, d, tf), lambda t, f, tm, te, tv: (j, te[t], 0, f)),
                pl.BlockSpec((None, None, d, tf), lambda t, f, tm, te, tv: (j, te[t], 0, n_f + f)),
                pl.BlockSpec((None, None, tf, d), lambda t, f, tm, te, tv: (j, te[t], f, 0)),
            ],
            out_specs=pl.BlockSpec((MOE_TILE, d), lambda t, f, tm, te, tv: (tm[t], 0)),
            scratch_shapes=[pltpu.VMEM((MOE_TILE, d), F32)],
        ),
        out_shape=jax.ShapeDtypeStruct((cap, d), BF16),
        input_output_aliases={3: 0},
        compiler_params=_cparams("arbitrary", "arbitrary"),
        name="moe_experts",
    )(tile_map, tile_exp, tile_valid, hs, w_in, w_in, w_out)

    return pl.pallas_call(
        functools.partial(_moe_combine_kernel, n_exp=n_exp),
        grid_spec=pltpu.PrefetchScalarGridSpec(
            num_scalar_prefetch=2,
            grid=(n_blk,),
            in_specs=[
                pl.BlockSpec((nb, d), lambda i, *_: (i, 0)),
                pl.BlockSpec((None, 1, d), lambda i, *_: (3, 0, 0)),
                pl.BlockSpec((None, None, 1, d), lambda i, *_: (mrow(i), 5, 0, 0)),
                pl.BlockSpec((nb, LANES), lambda i, *_: (i, 0)),
                pl.BlockSpec((nb, LANES), lambda i, *_: (i, 0)),
                pl.BlockSpec(memory_space=pl.ANY),
            ],
            out_specs=pl.BlockSpec((nb, d), lambda i, *_: (i, 0)),
            scratch_shapes=[
                pltpu.VMEM((nb, LANES), F32),
                pltpu.VMEM((2, MOE_COMBINE_ROWS, d), BF16),
                pltpu.SemaphoreType.DMA((2,)),
            ],
        ),
        out_shape=jax.ShapeDtypeStruct((rows, d), F32),
        compiler_params=_cparams("parallel"),
        name="moe_combine",
    )(npiece, off, xt, normg, mod, comb, sel, ys)


def _lambda_init(layer):
    return 0.8 - 0.6 * math.exp(-0.3 * layer)


def kernel(x, c, ctx, c_ctx, ada_w, ada_b, norm_g, mix_in_w, mix_out_w, win_sink, diff_qkv_w, diff_out_w,
           diff_lambda, diff_subln_g, ffn_in_w, ffn_out_w, router_w, expert_in_w, expert_out_w):
    n_batch, seq, d = x.shape
    n_ctx = ctx.shape[1]
    depth = ada_w.shape[0]
    n_lat = n_batch * seq
    n_all = n_lat + n_batch * n_ctx
    fdim = mix_in_w.shape[2] - (d // 2 + 2 * LANES)
    q_cols = d // 2
    n_diff_heads = d // LANES

    tm_proj = _pick_tile(512, seq, n_batch * n_ctx)
    tm_out = _pick_tile(512, seq, n_batch * n_ctx)
    tm_ffn = _pick_tile(512, seq, n_batch * n_ctx)
    nb_moe = _pick_tile(512, seq, n_batch * n_ctx)
    tq_diff = _pick_tile(1024, seq)
    tq_win = _pick_tile(512, seq)
    common = dict(seq=seq, n_batch=n_batch)

    xt = jnp.concatenate([x.reshape(n_lat, d), ctx.reshape(n_batch * n_ctx, d)], axis=0)
    n_mod = -(-(n_batch + 1) // 8) * 8
    cv = jnp.zeros((n_mod, d), F32).at[:n_batch].set(c).at[n_batch].set(c_ctx)
    mods = _modvec(cv, ada_w, ada_b).reshape(depth, n_mod, 6, 1, d)
    rope = _rope_tables(seq, tm_proj)
    f_tables = _fourier_tables(seq)
    fc_tables = _dense_fourier_tables(n_ctx)

    n_f = fdim // LANES
    plan_even = ([(0, i * LANES, "plain") for i in range(n_f)]
                 + [(1, i * LANES, "rope_q_log2") for i in range(q_cols // LANES)]
                 + [(1, q_cols, "rope_k"), (1, q_cols + LANES, "plain")])
    plan_odd = ([(0, i * LANES, "rope_q_log2") for i in range(n_diff_heads)]
                + [(0, d + i * LANES, "rope_k") for i in range(n_diff_heads)]
                + [(1, i * LANES, "plain_t") for i in range(n_diff_heads)])

    expert_in_bf16 = expert_in_w.astype(BF16)
    expert_out_bf16 = expert_out_w.astype(BF16)

    for layer in range(depth):
        j = layer // 2
        need_ctx = layer < depth - 1
        rows = n_all if need_ctx else n_lat
        mod = mods[layer]
        ng = norm_g[layer].reshape(4, 1, d)
        if layer % 2 == 0:
            f, z = _proj(xt, ng, mod, 0, 1, mix_in_w[j].astype(BF16), rope, plan_even,
                         [(fdim, F32, False), (q_cols + 2 * LANES, BF16, False)], tm=tm_proj, n_lat=n_lat,
                         **common)
            k_col, v_col = q_cols // LANES, q_cols // LANES + 1
            att = dict(n_batch=n_batch, seq=seq, n_ctx=n_ctx, q_cols=q_cols, k_col=k_col, v_col=v_col)
            mix_f = _fourier(f, f_tables, jnp.zeros((rows, fdim), BF16), n_batch=n_batch, seq=seq)
            mix_a = _win_attention(z, win_sink[j], jnp.zeros((rows, q_cols), BF16), tq=tq_win, **att)
            if need_ctx:
                mix_f = _dense_fourier(f, fc_tables, mix_f, n_batch=n_batch, n_pos=n_ctx,
                                       row_block0=n_lat // n_ctx)
                mix_a = _ctx_gqa_attention(z, win_sink[j], mix_a, **att)
            xt = _outproj(mix_f, 0, mix_a, 0, mix_out_w[j].astype(BF16), xt, ng, mod, 2, tm=tm_out, rows=rows,
                          **common)
            xt = _ffn(xt, ng, mod, ffn_in_w[j].astype(BF16), ffn_out_w[j].astype(BF16),
                      tm=tm_ffn, rows=rows, **common)
        else:
            lam_init = _lambda_init(layer)
            z, vt = _proj(xt, ng, mod, 0, 1, diff_qkv_w[j].astype(BF16), rope, plan_odd,
                          [(2 * d, BF16, False), (d, BF16, True)], tm=tm_proj, n_lat=n_lat, **common)
            subg = diff_subln_g[j].reshape(1, LANES)
            att = dict(n_batch=n_batch, seq=seq, n_ctx=n_ctx, n_heads=n_diff_heads)
            mix = _diff_attention(z, vt, diff_lambda[j], subg, lam_init, jnp.zeros((rows, d), BF16), tq=tq_diff,
                                  **att)
            if need_ctx:
                mix = _diff_attention_ctx(z, vt, diff_lambda[j], subg, lam_init, mix, **att)
            xt = _outproj(mix, 0, mix, 1, diff_out_w[j].astype(BF16), xt, ng, mod, 2, tm=tm_out, rows=rows,
                          **common)
            comb, sel = _router(xt, ng, mod, router_w[j], tm=tm_out, rows=rows, **common)
            xt = _moe(xt, ng, mod, expert_in_bf16, expert_out_bf16, j, comb, sel,
                      nb=nb_moe, tf=1792, rows=rows, **common)
    return xt[:n_lat].reshape(n_batch, seq, d)
```

```python
import functools
import math

import numpy as np
import jax
import jax.numpy as jnp
from jax import lax
from jax.experimental import pallas as pl
from jax.experimental.pallas import tpu as pltpu

EPS = 1e-6
NEG = -1e30
HEAD_DIM = 64
LANES = 128
GRID_W = 64
BLOCK = 128
WINDOW = 128
ROPE_THETA = 10000.0
N_EXPERTS = 8
F32 = jnp.float32
BF16 = jnp.bfloat16
HIGHEST = lax.Precision.HIGHEST
VMEM_LIMIT = 56 * 1024 * 1024


def _cparams(*sem):
    return pltpu.CompilerParams(dimension_semantics=sem, vmem_limit_bytes=VMEM_LIMIT)


def _dot(a, b):
    return jnp.dot(a, b, preferred_element_type=F32)


def _dot_nt(a, b):
    return lax.dot_general(a, b, (((1,), (1,)), ((), ())), preferred_element_type=F32)


def _dot_hi(a, b):
    return jnp.dot(a, b, precision=HIGHEST, preferred_element_type=F32)


def _rms(v):
    return v * lax.rsqrt(jnp.mean(v * v, axis=-1, keepdims=True) + EPS)


def _modulate(x, g, sh, sc):
    return _rms(x) * g * (1.0 + sc) + sh


def _pick_tile(pref, *dims):
    t = pref
    while any(d % t for d in dims):
        t //= 2
    return t


def _modvec_kernel(c_ref, w_ref, b_ref, o_ref):
    cv = c_ref[...]
    s = cv * (1.0 / (1.0 + jnp.exp(-cv)))
    o_ref[...] = _dot(s.astype(BF16), w_ref[...].astype(BF16)) + b_ref[...]


def _modvec(cv, ada_w, ada_b):
    depth, d, n = ada_w.shape
    r = cv.shape[0]
    tn = _pick_tile(1536, n)
    return pl.pallas_call(
        _modvec_kernel,
        grid=(depth, n // tn),
        in_specs=[
            pl.BlockSpec((r, d), lambda l, j: (0, 0)),
            pl.BlockSpec((None, d, tn), lambda l, j: (l, 0, j)),
            pl.BlockSpec((None, 1, tn), lambda l, j: (l, 0, j)),
        ],
        out_specs=pl.BlockSpec((None, r, tn), lambda l, j: (l, 0, j)),
        out_shape=jax.ShapeDtypeStruct((depth, r, n), F32),
        compiler_params=_cparams("parallel", "parallel"),
        name="modvec",
    )(cv, ada_w, ada_b.reshape(depth, 1, n))


def _proj_kernel(x_ref, g_ref, sh_ref, sc_ref, w_ref, cos_ref, sa_ref, sb_ref, *o_refs, plan, group):
    h = _modulate(x_ref[...], g_ref[...], sh_ref[...], sc_ref[...]).astype(BF16)
    n = w_ref.shape[1]
    for g0 in range(0, n, group):
        acc = _dot(h, w_ref[:, g0:g0 + group])
        for c0 in range(0, group, LANES):
            oi, oc, mode = plan[(g0 + c0) // LANES]
            v = acc[:, c0:c0 + LANES]
            if mode.startswith("rope"):
                v = (v * cos_ref[...] + pltpu.roll(v, LANES - 16, 1) * sa_ref[...]
                     + pltpu.roll(v, 16, 1) * sb_ref[...])
                if mode == "rope_q_log2":
                    v = v * (HEAD_DIM ** -0.5 * LOG2E)
            if mode == "plain_t":
                o_refs[oi][oc:oc + LANES, :] = v.T.astype(o_refs[oi].dtype)
            else:
                o_refs[oi][:, oc:oc + LANES] = v.astype(o_refs[oi].dtype)


def _proj(xt, normg, mod, k_sh, k_sc, w, rope, plan, outs, *, tm, n_lat, seq, n_batch):
    t, d = xt.shape
    n = w.shape[1]
    group = _pick_tile(512, n)
    nx = n_lat // tm
    per = seq // tm

    def mrow(i):
        return jnp.minimum(i * tm // seq, n_batch)

    def rrow(i):
        return jnp.where(i < nx, i % per, per)

    in_specs = [
        pl.BlockSpec((tm, d), lambda i: (i, 0)),
        pl.BlockSpec((None, 1, d), lambda i: (0, 0, 0)),
        pl.BlockSpec((None, None, 1, d), lambda i: (mrow(i), k_sh, 0, 0)),
        pl.BlockSpec((None, None, 1, d), lambda i: (mrow(i), k_sc, 0, 0)),
        pl.BlockSpec((d, n), lambda i: (0, 0)),
        pl.BlockSpec((tm, LANES), lambda i: (rrow(i), 0)),
        pl.BlockSpec((tm, LANES), lambda i: (rrow(i), 0)),
        pl.BlockSpec((tm, LANES), lambda i: (rrow(i), 0)),
    ]
    out_specs = [pl.BlockSpec((wd, tm), lambda i: (0, i)) if tr else pl.BlockSpec((tm, wd), lambda i: (i, 0))
                 for wd, _, tr in outs]
    out_shape = [jax.ShapeDtypeStruct((wd, t) if tr else (t, wd), dt) for wd, dt, tr in outs]
    return pl.pallas_call(
        functools.partial(_proj_kernel, plan=plan, group=group),
        grid=(t // tm,),
        in_specs=in_specs,
        out_specs=out_specs,
        out_shape=out_shape,
        compiler_params=_cparams("parallel"),
        name="proj",
    )(xt, normg, mod, mod, w, *rope)


def _rope_tables(seq, tm):
    rows_count = seq // GRID_W
    rows = jnp.repeat(jnp.arange(rows_count), GRID_W).astype(F32)
    cols = jnp.tile(jnp.arange(GRID_W), rows_count).astype(F32)
    axis_dim = HEAD_DIM // 2
    inv = ROPE_THETA ** (-jnp.arange(0, axis_dim, 2, dtype=F32) / axis_dim)
    ar = rows[:, None] * inv
    ac = cols[:, None] * inv
    cr, sr, cc, sc = jnp.cos(ar), jnp.sin(ar), jnp.cos(ac), jnp.sin(ac)
    z = jnp.zeros_like(sr)
    reps = LANES // HEAD_DIM
    cos = jnp.tile(jnp.concatenate([cr, cr, cc, cc], axis=1), (1, reps))
    sa = jnp.tile(jnp.concatenate([-sr, z, -sc, z], axis=1), (1, reps))
    sb = jnp.tile(jnp.concatenate([z, sr, z, sc], axis=1), (1, reps))
    ident = jnp.ones((tm, LANES), F32)
    zero = jnp.zeros((tm, LANES), F32)
    return (jnp.concatenate([cos, ident]), jnp.concatenate([sa, zero]), jnp.concatenate([sb, zero]))


FOURIER_LANES = 2 * LANES


def _split_bf16(t):
    hi = t.astype(BF16)
    return hi, (t - hi.astype(F32)).astype(BF16)


def _dot_split(a, b):
    return _dot(a[0], b[0]) + _dot(a[0], b[1]) + _dot(a[1], b[0])


def _fourier_tables(seq):
    n2 = GRID_W
    n1 = seq // n2
    norm = 1.0 / math.sqrt(seq * LANES)
    a = np.arange(n1)
    k1 = np.arange(n1)
    b = np.arange(n2)
    ang = (b[:, None, None] * k1[None, :, None] + (seq // n1) * k1[None, :, None] * a[None, None, :]) % seq
    th = 2.0 * np.pi * ang / seq
    m1 = np.concatenate([np.cos(th), -np.sin(th)], axis=1).astype(np.float32)
    ph = 2.0 * np.pi * ((b[:, None] * b[None, :]) % n2) / n2
    c2, s2 = np.cos(ph), np.sin(ph)
    g2 = np.block([[c2, s2], [-s2, c2]]).astype(np.float32)
    ch = np.arange(LANES)
    pc = 2.0 * np.pi * ((ch[:, None] * ch[None, :]) % LANES) / LANES
    cc = (np.cos(pc) * norm).astype(np.float32)
    sc = (np.sin(pc) * norm).astype(np.float32)
    cs = np.concatenate([cc, sc], axis=0)
    out = []
    for t in (m1, g2, cs):
        out.extend(_split_bf16(jnp.asarray(t)))
    return tuple(out)


FOURIER_ROW_CHUNK = 512


def _fourier_kernel(*refs, n1, n2, n_grp):
    u_refs = refs[:n_grp]
    m1h_ref, m1l_ref, g2h_ref, g2l_ref, csh_ref, csl_ref, _, o_ref, b_scr, xr_scr, xi_scr = refs[n_grp:]
    seq = n1 * n2
    for b in range(n2):
        xs = jnp.concatenate([u[pl.ds(b, n1, stride=n2), :] for u in u_refs], axis=1)
        z = _dot_split((m1h_ref[b], m1l_ref[b]), _split_bf16(xs))
        for g in range(n_grp):
            b_scr[g, 2 * n1 * b:2 * n1 * (b + 1), :] = z[:, g * LANES:(g + 1) * LANES]
    g2 = (g2h_ref[...], g2l_ref[...])
    for k1 in range(n1):
        bk = jnp.concatenate(
            [jnp.concatenate([b_scr.at[g][pl.ds(k1, n2, stride=2 * n1), :],
                              b_scr.at[g][pl.ds(n1 + k1, n2, stride=2 * n1), :]], axis=0)
             for g in range(n_grp)], axis=1)
        xk = _dot_split(g2, _split_bf16(bk))
        for g in range(n_grp):
            xr_scr[g, n2 * k1:n2 * (k1 + 1), :] = xk[:n2, g * LANES:(g + 1) * LANES]
            xi_scr[g, n2 * k1:n2 * (k1 + 1), :] = xk[n2:, g * LANES:(g + 1) * LANES]
    cs = (csh_ref[...], csl_ref[...])
    rc = min(FOURIER_ROW_CHUNK, seq)
    for g in range(n_grp):
        for r0 in range(0, seq, rc):
            x = jnp.concatenate([xr_scr[g, r0:r0 + rc, :], xi_scr[g, r0:r0 + rc, :]], axis=1)
            xr_scr[g, r0:r0 + rc, :] = _dot_split(_split_bf16(x), cs)
    for g in range(n_grp):
        for k2 in range(n2):
            o_ref[n1 * k2:n1 * (k2 + 1), g * LANES:(g + 1) * LANES] = (
                xr_scr.at[g][pl.ds(k2, n1, stride=n2), :].astype(o_ref.dtype))


def _fourier(f, tables, dst, *, n_batch, seq):
    n2 = GRID_W
    n1 = seq // n2
    n_grp = FOURIER_LANES // LANES
    table_specs = [pl.BlockSpec(t.shape, (lambda b, g, nd=t.ndim: (0,) * nd)) for t in tables]
    u_specs = [pl.BlockSpec((seq, LANES), (lambda b, g, k=k: (b, g * n_grp + k))) for k in range(n_grp)]
    return pl.pallas_call(
        functools.partial(_fourier_kernel, n1=n1, n2=n2, n_grp=n_grp),
        grid=(n_batch, f.shape[1] // FOURIER_LANES),
        in_specs=u_specs + table_specs + [pl.BlockSpec(memory_space=pl.ANY)],
        out_specs=pl.BlockSpec((seq, FOURIER_LANES), lambda b, g: (b, g)),
        out_shape=jax.ShapeDtypeStruct(dst.shape, dst.dtype),
        input_output_aliases={n_grp + len(tables): 0},
        scratch_shapes=[pltpu.VMEM((n_grp, 2 * seq, LANES), F32), pltpu.VMEM((n_grp, seq, LANES), F32),
                        pltpu.VMEM((n_grp, seq, LANES), F32)],
        compiler_params=_cparams("parallel", "parallel"),
        name="fourier",
    )(*([f] * n_grp), *tables, dst)


def _dense_fourier_tables(n):
    norm = 1.0 / math.sqrt(n * LANES)
    p = np.arange(n)
    ph = 2.0 * np.pi * ((p[:, None] * p[None, :]) % n) / n
    ch = np.arange(LANES)
    pc = 2.0 * np.pi * ((ch[:, None] * ch[None, :]) % LANES) / LANES
    return (jnp.asarray(np.cos(ph).astype(np.float32)), jnp.asarray(np.sin(ph).astype(np.float32)),
            jnp.asarray((np.cos(pc) * norm).astype(np.float32)), jnp.asarray((np.sin(pc) * norm).astype(np.float32)))


def _dense_fourier_kernel(u_ref, cl_ref, sl_ref, cc_ref, sc_ref, _, o_ref):
    u = u_ref[...]
    y = _dot_hi(cl_ref[...], _dot_hi(u, cc_ref[...])) - _dot_hi(sl_ref[...], _dot_hi(u, sc_ref[...]))
    o_ref[...] = y.astype(o_ref.dtype)


def _dense_fourier(f, tables, dst, *, n_batch, n_pos, row_block0):
    cl, sl, cc, sc = tables
    groups = f.shape[1] // LANES
    return pl.pallas_call(
        _dense_fourier_kernel,
        grid=(n_batch, groups),
        in_specs=[
            pl.BlockSpec((n_pos, LANES), lambda b, g: (row_block0 + b, g)),
            pl.BlockSpec(cl.shape, lambda b, g: (0, 0)),
            pl.BlockSpec(sl.shape, lambda b, g: (0, 0)),
            pl.BlockSpec(cc.shape, lambda b, g: (0, 0)),
            pl.BlockSpec(sc.shape, lambda b, g: (0, 0)),
            pl.BlockSpec(memory_space=pl.ANY),
        ],
        out_specs=pl.BlockSpec((n_pos, LANES), lambda b, g: (row_block0 + b, g)),
        out_shape=jax.ShapeDtypeStruct(dst.shape, dst.dtype),
        input_output_aliases={5: 0},
        compiler_params=_cparams("parallel", "parallel"),
        name="fourier_ctx",
    )(f, cl, sl, cc, sc, dst)


def _win_kernel(sink_ref, q_ref, *refs, n_qtiles, has_local, n_heads, group_size):
    if has_local:
        kp_ref, kc_ref, kn_ref, vp_ref, vc_ref, vn_ref, kx_ref, vx_ref, _, o_ref = refs
    else:
        kx_ref, vx_ref, _, o_ref = refs
    tq = q_ref.shape[0]
    lane = lax.broadcasted_iota(jnp.int32, (1, LANES), 1)
    half_mask = [lane < HEAD_DIM, lane >= HEAD_DIM]
    pieces = [(kx_ref[...], vx_ref[...], None)]
    if has_local:
        n = pl.program_id(1)
        qi = lax.broadcasted_iota(jnp.int32, (tq, BLOCK), 0)
        kj = lax.broadcasted_iota(jnp.int32, (tq, BLOCK), 1)
        valid_prev = (kj >= qi) & (n >= 1)
        valid_next = (kj <= qi - (tq - WINDOW)) & (n <= n_qtiles - 2)
        di = lax.broadcasted_iota(jnp.int32, (tq, tq), 0) - lax.broadcasted_iota(jnp.int32, (tq, tq), 1)
        valid_mid = (di <= WINDOW) & (di >= -WINDOW)
        pieces += [(kp_ref[...], vp_ref[...], valid_prev), (kc_ref[...], vc_ref[...], valid_mid),
                   (kn_ref[...], vn_ref[...], valid_next)]
    def scores(head):
        pair, half = divmod(head, 2)
        kv = head // group_size
        qp = q_ref[:, pair * LANES:(pair + 1) * LANES].astype(F32)
        src = qp if half == kv else pltpu.roll(qp, HEAD_DIM, 1)
        qe = jnp.where(half_mask[kv], src, 0.0).astype(BF16)
        out = []
        for k, _, valid in pieces:
            s = _dot_nt(qe, k)
            out.append(s if valid is None else jnp.where(valid, s, NEG))
        return out

    def attend(head, sc):
        half = head % 2
        kv = head // group_size
        sink = sink_ref[head] * LOG2E
        m = jnp.zeros((tq, 1), F32) + sink
        for s in sc:
            m = jnp.maximum(m, jnp.max(s, axis=-1, keepdims=True))
        den = jnp.exp2(sink - m)
        pv = jnp.zeros((tq, LANES), F32)
        for s, (_, v, _) in zip(sc, pieces):
            p = jnp.exp2(s - m)
            den = den + jnp.sum(p, axis=-1, keepdims=True)
            pv = pv + _dot(p.astype(BF16), v)
        pv = pv / den
        return pv if half == kv else pltpu.roll(pv, HEAD_DIM, 1)

    sc_next = scores(0)
    out_pair = None
    for head in range(n_heads):
        sc_cur = sc_next
        if head + 1 < n_heads:
            sc_next = scores(head + 1)
        pv = attend(head, sc_cur)
        if head % 2 == 0:
            out_pair = pv
        else:
            pair = head // 2
            o_ref[:, pair * LANES:(pair + 1) * LANES] = jnp.where(half_mask[1], pv, out_pair).astype(o_ref.dtype)


def _win_attention(z, sink, dst, *, n_batch, seq, n_ctx, q_cols, k_col, v_col, tq):
    nbk = seq // BLOCK
    nq = seq // tq
    per = tq // BLOCK
    ctx_blk0 = n_batch * seq // n_ctx
    n_heads = q_cols // HEAD_DIM
    group_size = n_heads // (LANES // HEAD_DIM)

    def edge(col, first):
        return pl.BlockSpec(
            (BLOCK, LANES), lambda b, n: (b * nbk + jnp.clip(n * per + first, 0, nbk - 1), col))

    def mid(col):
        return pl.BlockSpec((tq, LANES), lambda b, n: (b * nq + n, col))

    return pl.pallas_call(
        functools.partial(_win_kernel, n_qtiles=nq, has_local=True, n_heads=n_heads, group_size=group_size),
        grid=(n_batch, nq),
        in_specs=[
            pl.BlockSpec(memory_space=pltpu.SMEM),
            pl.BlockSpec((tq, q_cols), lambda b, n: (b * nq + n, 0)),
            edge(k_col, -1), mid(k_col), edge(k_col, per),
            edge(v_col, -1), mid(v_col), edge(v_col, per),
            pl.BlockSpec((n_ctx, LANES), lambda b, n: (ctx_blk0 + b, k_col)),
            pl.BlockSpec((n_ctx, LANES), lambda b, n: (ctx_blk0 + b, v_col)),
            pl.BlockSpec(memory_space=pl.ANY),
        ],
        out_specs=pl.BlockSpec((tq, q_cols), lambda b, n: (b * nq + n, 0)),
        out_shape=jax.ShapeDtypeStruct(dst.shape, dst.dtype),
        input_output_aliases={10: 0},
        compiler_params=_cparams("parallel", "parallel"),
        name="win_attn",
    )(sink, z, z, z, z, z, z, z, z, z, dst)


def _ctx_gqa_attention(z, sink, dst, *, n_batch, seq, n_ctx, q_cols, k_col, v_col):
    ctx_blk0 = n_batch * seq // n_ctx
    n_heads = q_cols // HEAD_DIM
    group_size = n_heads // (LANES // HEAD_DIM)
    return pl.pallas_call(
        functools.partial(_win_kernel, n_qtiles=0, has_local=False, n_heads=n_heads, group_size=group_size),
        grid=(n_batch,),
        in_specs=[
            pl.BlockSpec(memory_space=pltpu.SMEM),
            pl.BlockSpec((n_ctx, q_cols), lambda b: (ctx_blk0 + b, 0)),
            pl.BlockSpec((n_ctx, LANES), lambda b: (ctx_blk0 + b, k_col)),
            pl.BlockSpec((n_ctx, LANES), lambda b: (ctx_blk0 + b, v_col)),
            pl.BlockSpec(memory_space=pl.ANY),
        ],
        out_specs=pl.BlockSpec((n_ctx, q_cols), lambda b: (ctx_blk0 + b, 0)),
        out_shape=jax.ShapeDtypeStruct(dst.shape, dst.dtype),
        input_output_aliases={4: 0},
        compiler_params=_cparams("parallel"),
        name="ctx_gqa",
    )(sink, z, z, z, dst)


LOG2E = math.log2(math.e)
DIFF_SUB_ROWS = 128
DIFF_KEY_CHUNK = 1024


def _diff_kernel(lam_ref, g_ref, q_ref, *refs, lam_init, has_x):
    if has_x:
        kx_ref, vtx_ref, kc_ref, vtc_ref, _, o_ref, s_scr = refs
        nx = kx_ref.shape[0]
    else:
        kc_ref, vtc_ref, _, o_ref, s_scr = refs
        nx = 0
    nc = kc_ref.shape[0]
    mc = s_scr.shape[2]
    sub = mc // 2
    n_sub = q_ref.shape[0] // sub
    lv = lam_ref[...]
    lam = (jnp.exp(jnp.sum(lv[0:1] * lv[1:2], axis=-1, keepdims=True))
           - jnp.exp(jnp.sum(lv[2:3] * lv[3:4], axis=-1, keepdims=True)) + lam_init)
    lane = lax.broadcasted_iota(jnp.int32, (1, LANES), 1)
    segs = []
    if has_x:
        kw = min(DIFF_KEY_CHUNK, nx)
        segs += [(kx_ref, vtx_ref, r0, kw, r0) for r0 in range(0, nx, kw)]
    segs.append((kc_ref, vtc_ref, 0, nc, nx))

    def scores(u):
        q = q_ref[u * sub:(u + 1) * sub, :]
        zero = jnp.zeros_like(q)
        qs = jnp.concatenate([jnp.where(lane < HEAD_DIM, q, zero), jnp.where(lane >= HEAD_DIM, q, zero)], axis=0)
        m_run = jnp.full((8, mc), -jnp.inf, F32)
        for k_ref, _, r0, w, s0 in segs:
            st = _dot_nt(k_ref[r0:r0 + w, :], qs)
            s_scr[u % 2, s0:s0 + w, :] = st
            m_run = jnp.maximum(m_run, jnp.max(st.reshape(w // 8, 8, mc), axis=0))
        return jnp.broadcast_to(jnp.max(m_run, axis=0, keepdims=True), (8, mc))

    def attend(u, m8):
        l_run = jnp.zeros((8, mc), F32)
        acc = jnp.zeros((LANES, mc), F32)
        for _, vt_ref, r0, w, s0 in segs:
            p = jnp.exp2(s_scr[u % 2, s0:s0 + w, :].reshape(w // 8, 8, mc) - m8[None])
            l_run = l_run + jnp.sum(p, axis=0)
            acc = acc + _dot(vt_ref[:, r0:r0 + w], p.reshape(w, mc).astype(BF16))
        ot = acc / jnp.sum(l_run, axis=0, keepdims=True)
        o = (ot[:, :sub] - lam * ot[:, sub:]).T
        o = _rms(o) * g_ref[...] * (1.0 - lam_init)
        o_ref[u * sub:(u + 1) * sub, :] = o.astype(o_ref.dtype)

    m_next = scores(0)
    for u in range(n_sub):
        m_cur = m_next
        if u + 1 < n_sub:
            m_next = scores(u + 1)
        attend(u, m_cur)


def _diff_attention(z, vt, lam_vec, subln_g, lam_init, dst, *, n_batch, seq, n_ctx, n_heads, tq):
    ctx_blk0 = n_batch * seq // n_ctx
    nq = seq // tq
    return pl.pallas_call(
        functools.partial(_diff_kernel, lam_init=lam_init, has_x=True),
        grid=(n_batch, n_heads, nq),
        in_specs=[
            pl.BlockSpec(lam_vec.shape, lambda b, h, i: (0, 0)),
            pl.BlockSpec((1, LANES), lambda b, h, i: (0, 0)),
            pl.BlockSpec((tq, LANES), lambda b, h, i: (b * nq + i, h)),
            pl.BlockSpec((seq, LANES), lambda b, h, i: (b, n_heads + h)),
            pl.BlockSpec((LANES, seq), lambda b, h, i: (h, b)),
            pl.BlockSpec((n_ctx, LANES), lambda b, h, i: (ctx_blk0 + b, n_heads + h)),
            pl.BlockSpec((LANES, n_ctx), lambda b, h, i: (h, ctx_blk0 + b)),
            pl.BlockSpec(memory_space=pl.ANY),
        ],
        out_specs=pl.BlockSpec((tq, LANES), lambda b, h, i: (b * nq + i, h)),
        out_shape=jax.ShapeDtypeStruct(dst.shape, dst.dtype),
        input_output_aliases={7: 0},
        scratch_shapes=[pltpu.VMEM((2, seq + n_ctx, 2 * min(tq, DIFF_SUB_ROWS)), F32)],
        compiler_params=_cparams("parallel", "parallel", "parallel"),
        name="diff_attn",
    )(lam_vec, subln_g, z, z, vt, z, vt, dst)


def _diff_attention_ctx(z, vt, lam_vec, subln_g, lam_init, dst, *, n_batch, seq, n_ctx, n_heads):
    ctx_blk0 = n_batch * seq // n_ctx
    return pl.pallas_call(
        functools.partial(_diff_kernel, lam_init=lam_init, has_x=False),
        grid=(n_batch, n_heads),
        in_specs=[
            pl.BlockSpec(lam_vec.shape, lambda b, h: (0, 0)),
            pl.BlockSpec((1, LANES), lambda b, h: (0, 0)),
            pl.BlockSpec((n_ctx, LANES), lambda b, h: (ctx_blk0 + b, h)),
            pl.BlockSpec((n_ctx, LANES), lambda b, h: (ctx_blk0 + b, n_heads + h)),
            pl.BlockSpec((LANES, n_ctx), lambda b, h: (h, ctx_blk0 + b)),
            pl.BlockSpec(memory_space=pl.ANY),
        ],
        out_specs=pl.BlockSpec((n_ctx, LANES), lambda b, h: (ctx_blk0 + b, h)),
        out_shape=jax.ShapeDtypeStruct(dst.shape, dst.dtype),
        input_output_aliases={5: 0},
        scratch_shapes=[pltpu.VMEM((2, n_ctx, 2 * min(n_ctx, DIFF_SUB_ROWS)), F32)],
        compiler_params=_cparams("parallel", "parallel"),
        name="diff_attn_ctx",
    )(lam_vec, subln_g, z, z, vt, dst)


def _outproj_kernel(a0_ref, a1_ref, w_ref, x_ref, g_ref, gate_ref, o_ref):
    half = a0_ref.shape[1]
    y = _dot(a0_ref[...], w_ref[:half, :]) + _dot(a1_ref[...], w_ref[half:, :])
    o_ref[...] = x_ref[...] + gate_ref[...] * (_rms(y) * g_ref[...])


def _outproj(a0, a0_col, a1, a1_col, w, xt, normg, mod, k_gate, *, tm, rows, seq, n_batch):
    d = xt.shape[1]
    half = d // 2

    def mrow(i):
        return jnp.minimum(i * tm // seq, n_batch)

    return pl.pallas_call(
        _outproj_kernel,
        grid=(rows // tm,),
        in_specs=[
            pl.BlockSpec((tm, half), lambda i: (i, a0_col)),
            pl.BlockSpec((tm, half), lambda i: (i, a1_col)),
            pl.BlockSpec((d, d), lambda i: (0, 0)),
            pl.BlockSpec((tm, d), lambda i: (i, 0)),
            pl.BlockSpec((None, 1, d), lambda i: (1, 0, 0)),
            pl.BlockSpec((None, None, 1, d), lambda i: (mrow(i), k_gate, 0, 0)),
        ],
        out_specs=pl.BlockSpec((tm, d), lambda i: (i, 0)),
        out_shape=jax.ShapeDtypeStruct((rows, d), F32),
        compiler_params=_cparams("parallel"),
        name="outproj",
    )(a0, a1, w, xt, normg, mod)


def _router_kernel(x_ref, g_ref, sh_ref, sc_ref, rw_ref, o_ref, sel_ref):
    h = _modulate(x_ref[...], g_ref[...], sh_ref[...], sc_ref[...])
    logits = _dot_split(_split_bf16(h), _split_bf16(rw_ref[...]))
    lane = lax.broadcasted_iota(jnp.int32, logits.shape, 1)
    ninf = -jnp.inf
    logits = jnp.where(lane < N_EXPERTS, logits, ninf)
    m1 = jnp.max(logits, axis=-1, keepdims=True)
    i1 = jnp.min(jnp.where(logits == m1, lane, LANES), axis=-1, keepdims=True)
    sel1 = lane == i1
    rest = jnp.where(sel1, ninf, logits)
    m2 = jnp.max(rest, axis=-1, keepdims=True)
    i2 = jnp.min(jnp.where(rest == m2, lane, LANES), axis=-1, keepdims=True)
    sel2 = lane == i2
    e2 = jnp.exp(m2 - m1)
    den = 1.0 + e2
    o_ref[...] = jnp.where(sel1, 1.0 / den, 0.0) + jnp.where(sel2, e2 / den, 0.0)
    sel_ref[...] = jnp.where(sel1 | sel2, 1.0, 0.0)


def _router(xt, normg, mod, router_w, *, tm, rows, seq, n_batch):
    d = xt.shape[1]
    rw = jnp.zeros((d, LANES), F32).at[:, :N_EXPERTS].set(router_w)

    def mrow(i):
        return jnp.minimum(i * tm // seq, n_batch)

    return pl.pallas_call(
        _router_kernel,
        grid=(rows // tm,),
        in_specs=[
            pl.BlockSpec((tm, d), lambda i: (i, 0)),
            pl.BlockSpec((None, 1, d), lambda i: (2, 0, 0)),
            pl.BlockSpec((None, None, 1, d), lambda i: (mrow(i), 3, 0, 0)),
            pl.BlockSpec((None, None, 1, d), lambda i: (mrow(i), 4, 0, 0)),
            pl.BlockSpec((d, LANES), lambda i: (0, 0)),
        ],
        out_specs=[pl.BlockSpec((tm, LANES), lambda i: (i, 0))] * 2,
        out_shape=[jax.ShapeDtypeStruct((rows, LANES), F32)] * 2,
        compiler_params=_cparams("parallel"),
        name="router",
    )(xt, normg, mod, mod, rw)


SWIGLU_CHUNK = 256


def _swiglu(h, wg_ref, wu_ref, wo_ref, u_off):
    width = wo_ref.shape[0]
    acc = None
    for c0 in range(0, width, SWIGLU_CHUNK):
        gp = _dot(h, wg_ref[:, c0:c0 + SWIGLU_CHUNK])
        up = _dot(h, wu_ref[:, u_off + c0:u_off + c0 + SWIGLU_CHUNK])
        a = gp * (1.0 / (1.0 + jnp.exp(-gp))) * up
        part = _dot(a.astype(BF16), wo_ref[c0:c0 + SWIGLU_CHUNK, :])
        acc = part if acc is None else acc + part
    return acc


def _ffn_kernel(x_ref, g2_ref, sh_ref, sc_ref, win_ref, wout_ref, g3_ref, gate_ref, o_ref):
    x = x_ref[...]
    h = _modulate(x, g2_ref[...], sh_ref[...], sc_ref[...]).astype(BF16)
    y = _swiglu(h, win_ref, win_ref, wout_ref, wout_ref.shape[0])
    o_ref[...] = x + gate_ref[...] * (_rms(y) * g3_ref[...])


def _ffn(xt, normg, mod, w_in, w_out, *, tm, rows, seq, n_batch):
    d = xt.shape[1]

    def mrow(i):
        return jnp.minimum(i * tm // seq, n_batch)

    in_specs = [
        pl.BlockSpec((tm, d), lambda i: (i, 0)),
        pl.BlockSpec((None, 1, d), lambda i: (2, 0, 0)),
        pl.BlockSpec((None, None, 1, d), lambda i: (mrow(i), 3, 0, 0)),
        pl.BlockSpec((None, None, 1, d), lambda i: (mrow(i), 4, 0, 0)),
        pl.BlockSpec(w_in.shape, lambda i: (0, 0)),
        pl.BlockSpec(w_out.shape, lambda i: (0, 0)),
        pl.BlockSpec((None, 1, d), lambda i: (3, 0, 0)),
        pl.BlockSpec((None, None, 1, d), lambda i: (mrow(i), 5, 0, 0)),
    ]
    return pl.pallas_call(
        _ffn_kernel,
        grid=(rows // tm,),
        in_specs=in_specs,
        out_specs=pl.BlockSpec((tm, d), lambda i: (i, 0)),
        out_shape=jax.ShapeDtypeStruct((rows, d), F32),
        compiler_params=_cparams("parallel"),
        name="ffn",
    )(xt, normg, mod, mod, w_in, w_out, normg, mod)


MOE_DISPATCH_ROWS = 256
MOE_COMBINE_ROWS = 512
MOE_SEG_ALIGN = 16
MOE_TILE = 1024


def _moe_keys(sel):
    nb = sel.shape[0]
    ti = lax.broadcasted_iota(jnp.int32, (nb, nb), 0)
    tj = lax.broadcasted_iota(jnp.int32, (nb, nb), 1)
    lower = jnp.where(tj < ti, 1.0, 0.0).astype(BF16)
    rank = _dot(lower, sel.astype(BF16))
    return jnp.where(sel > 0.0, rank, -1.0)


def _moe_dispatch_kernel(nchunk_ref, off_ref, x_ref, g2_ref, sh_ref, sc_ref, sel_ref, _, hs_hbm,
                         h_scr, keyt_scr, stage, sems, *, n_exp):
    b = pl.program_id(0)
    rows = MOE_DISPATCH_ROWS
    h_scr[...] = _modulate(x_ref[...], g2_ref[...], sh_ref[...], sc_ref[...]).astype(BF16)
    keyt_scr[...] = _moe_keys(sel_ref[...]).T

    def copy(slot, row0):
        return pltpu.make_async_copy(stage.at[slot], hs_hbm.at[pl.ds(row0, rows), :], sems.at[slot])

    issued = jnp.int32(0)
    for e in range(n_exp):
        key_row = keyt_scr[e:e + 1, :]
        base = off_ref[b * n_exp + e]

        def chunk(c, k):
            slot = k % 2

            @pl.when(k >= 2)
            def _():
                copy(slot, 0).wait()

            r = (c * rows + lax.broadcasted_iota(jnp.int32, (rows, 1), 0)).astype(F32)
            onehot = jnp.where(key_row == r, 1.0, 0.0).astype(BF16)
            stage[slot] = _dot(onehot, h_scr[...]).astype(BF16)
            copy(slot, pl.multiple_of(base + c * rows, MOE_SEG_ALIGN)).start()
            return k + 1

        issued = lax.fori_loop(0, nchunk_ref[b * n_exp + e], chunk, issued)

    @pl.when(issued >= 1)
    def _():
        copy((issued - 1) % 2, 0).wait()

    @pl.when(issued >= 2)
    def _():
        copy(issued % 2, 0).wait()


def _moe_experts_kernel(tile_ref, exp_ref, valid_ref, hs_ref, wg_ref, wu_ref, wo_ref, y_ref, acc, *, n_f):
    t = pl.program_id(0)
    f = pl.program_id(1)

    @pl.when(valid_ref[t] == 1)
    def _():
        part = _swiglu(hs_ref[...], wg_ref, wu_ref, wo_ref, 0)

        @pl.when(f == 0)
        def _():
            acc[...] = part

        @pl.when(f > 0)
        def _():
            acc[...] += part

        @pl.when(f == n_f - 1)
        def _():
            y_ref[...] = acc[...].astype(y_ref.dtype)


def _moe_combine_kernel(npiece_ref, off_ref, x_ref, g3_ref, gate_ref, comb_ref, sel_ref, y_hbm, o_ref,
                        key_scr, stage, sems, *, n_exp):
    b = pl.program_id(0)
    rows = MOE_COMBINE_ROWS
    nb = x_ref.shape[0]
    key_scr[...] = _moe_keys(sel_ref[...])
    o_ref[...] = jnp.zeros_like(o_ref)
    lane = lax.broadcasted_iota(jnp.int32, (nb, LANES), 1)

    def copy(slot, row0):
        return pltpu.make_async_copy(y_hbm.at[pl.ds(row0, rows), :], stage.at[slot], sems.at[slot])

    def row0(e, c):
        return pl.multiple_of(off_ref[b * n_exp + e] + c * rows, MOE_SEG_ALIGN)

    copy(0, row0(0, 0)).start()
    done = jnp.int32(0)
    for e in range(n_exp):
        pick = lane == e
        key_col = jnp.sum(jnp.where(pick, key_scr[...], 0.0), axis=-1, keepdims=True)
        gate_col = jnp.sum(jnp.where(pick, comb_ref[...], 0.0), axis=-1, keepdims=True)
        n_pieces = npiece_ref[b * n_exp + e]

        def piece(c, k, e=e, n_pieces=n_pieces, key_col=key_col, gate_col=gate_col):
            slot = k % 2
            copy(slot, 0).wait()

            @pl.when(c + 1 < n_pieces)
            def _():
                copy(1 - slot, row0(e, c + 1)).start()

            if e + 1 < n_exp:
                @pl.when(c + 1 == n_pieces)
                def _():
                    copy(1 - slot, row0(e + 1, 0)).start()

            r = (c * rows + lax.broadcasted_iota(jnp.int32, (1, rows), 1)).astype(F32)
            onehot = jnp.where(key_col == r, 1.0, 0.0).astype(BF16)
            o_ref[...] += gate_col * _dot(onehot, stage[slot])
            return k + 1

        done = lax.fori_loop(0, n_pieces, piece, done)

    o_ref[...] = x_ref[...] + gate_ref[...] * (_rms(o_ref[...]) * g3_ref[...])


def _moe(xt, normg, mod, w_in, w_out, j, comb, sel, *, nb, tf, rows, seq, n_batch):
    d = xt.shape[1]
    n_exp, fdim = w_out.shape[1], w_out.shape[2]
    n_f = fdim // tf
    n_blk = rows // nb
    i32 = jnp.int32

    counts = jnp.sum(sel.reshape(n_blk, nb, LANES)[:, :, :n_exp], axis=1).astype(i32)
    seg = -(-counts // MOE_SEG_ALIGN) * MOE_SEG_ALIGN
    region = -(-(jnp.sum(seg, axis=0) + MOE_DISPATCH_ROWS) // MOE_TILE) * MOE_TILE
    region_end = jnp.cumsum(region)
    region_start = region_end - region
    off = (region_start[None, :] + jnp.cumsum(seg, axis=0) - seg).reshape(n_blk * n_exp).astype(i32)
    nchunk = (-(-counts // MOE_DISPATCH_ROWS)).reshape(n_blk * n_exp).astype(i32)
    npiece = jnp.maximum(-(-counts // MOE_COMBINE_ROWS), 1).reshape(n_blk * n_exp).astype(i32)
    cap = 2 * rows + n_blk * n_exp * (MOE_SEG_ALIGN - 1) + n_exp * (MOE_DISPATCH_ROWS + MOE_TILE - 1)
    n_tiles = -(-cap // MOE_TILE)
    cap = n_tiles * MOE_TILE + MOE_COMBINE_ROWS
    tiles = jnp.arange(n_tiles, dtype=i32)
    n_used = region_end[-1] // MOE_TILE
    tile_map = jnp.minimum(tiles, n_used - 1).astype(i32)
    tile_exp = jnp.minimum(jnp.searchsorted(region_end, tile_map * MOE_TILE, side="right"), n_exp - 1).astype(i32)
    tile_valid = (tiles < n_used).astype(i32)

    def mrow(i):
        return jnp.minimum(i * nb // seq, n_batch)

    hs = pl.pallas_call(
        functools.partial(_moe_dispatch_kernel, n_exp=n_exp),
        grid_spec=pltpu.PrefetchScalarGridSpec(
            num_scalar_prefetch=2,
            grid=(n_blk,),
            in_specs=[
                pl.BlockSpec((nb, d), lambda i, *_: (i, 0)),
                pl.BlockSpec((None, 1, d), lambda i, *_: (2, 0, 0)),
                pl.BlockSpec((None, None, 1, d), lambda i, *_: (mrow(i), 3, 0, 0)),
                pl.BlockSpec((None, None, 1, d), lambda i, *_: (mrow(i), 4, 0, 0)),
                pl.BlockSpec((nb, LANES), lambda i, *_: (i, 0)),
                pl.BlockSpec(memory_space=pl.ANY),
            ],
            out_specs=pl.BlockSpec(memory_space=pl.ANY),
            scratch_shapes=[
                pltpu.VMEM((nb, d), BF16),
                pltpu.VMEM((LANES, nb), F32),
                pltpu.VMEM((2, MOE_DISPATCH_ROWS, d), BF16),
                pltpu.SemaphoreType.DMA((2,)),
            ],
        ),
        out_shape=jax.ShapeDtypeStruct((cap, d), BF16),
        input_output_aliases={7: 0},
        compiler_params=_cparams("arbitrary"),
        name="moe_dispatch",
    )(nchunk, off, xt, normg, mod, mod, sel, jnp.zeros((cap, d), BF16))

    ys = pl.pallas_call(
        functools.partial(_moe_experts_kernel, n_f=n_f),
        grid_spec=pltpu.PrefetchScalarGridSpec(
            num_scalar_prefetch=3,
            grid=(n_tiles, n_f),
            in_specs=[
                pl.BlockSpec((MOE_TILE, d), lambda t, f, tm, te, tv: (tm[t], 0)),
                pl.BlockSpec((None, None, d, tf), lambda t, f, tm, te, tv: (j, te[t], 0, f)),
                pl.BlockSpec((None, None, d, tf), lambda t, f, tm, te, tv: (j, te[t], 0, n_f + f)),
                pl.BlockSpec((None, None, tf, d), lambda t, f, tm, te, tv: (j, te[t], f, 0)),
            ],
            out_specs=pl.BlockSpec((MOE_TILE, d), lambda t, f, tm, te, tv: (tm[t], 0)),
            scratch_shapes=[pltpu.VMEM((MOE_TILE, d), F32)],
        ),
        out_shape=jax.ShapeDtypeStruct((cap, d), BF16),
        input_output_aliases={3: 0},
        compiler_params=_cparams("arbitrary", "arbitrary"),
        name="moe_experts",
    )(tile_map, tile_exp, tile_valid, hs, w_in, w_in, w_out)

    return pl.pallas_call(
        functools.partial(_moe_combine_kernel, n_exp=n_exp),
        grid_spec=pltpu.PrefetchScalarGridSpec(
            num_scalar_prefetch=2,
            grid=(n_blk,),
            in_specs=[
                pl.BlockSpec((nb, d), lambda i, *_: (i, 0)),
                pl.BlockSpec((None, 1, d), lambda i, *_: (3, 0, 0)),
                pl.BlockSpec((None, None, 1, d), lambda i, *_: (mrow(i), 5, 0, 0)),
                pl.BlockSpec((nb, LANES), lambda i, *_: (i, 0)),
                pl.BlockSpec((nb, LANES), lambda i, *_: (i, 0)),
                pl.BlockSpec(memory_space=pl.ANY),
            ],
            out_specs=pl.BlockSpec((nb, d), lambda i, *_: (i, 0)),
            scratch_shapes=[
                pltpu.VMEM((nb, LANES), F32),
                pltpu.VMEM((2, MOE_COMBINE_ROWS, d), BF16),
                pltpu.SemaphoreType.DMA((2,)),
            ],
        ),
        out_shape=jax.ShapeDtypeStruct((rows, d), F32),
        compiler_params=_cparams("parallel"),
        name="moe_combine",
    )(npiece, off, xt, normg, mod, comb, sel, ys)


def _lambda_init(layer):
    return 0.8 - 0.6 * math.exp(-0.3 * layer)


def kernel(x, c, ctx, c_ctx, ada_w, ada_b, norm_g, mix_in_w, mix_out_w, win_sink, diff_qkv_w, diff_out_w,
           diff_lambda, diff_subln_g, ffn_in_w, ffn_out_w, router_w, expert_in_w, expert_out_w):
    n_batch, seq, d = x.shape
    n_ctx = ctx.shape[1]
    depth = ada_w.shape[0]
    n_lat = n_batch * seq
    n_all = n_lat + n_batch * n_ctx
    fdim = mix_in_w.shape[2] - (d // 2 + 2 * LANES)
    q_cols = d // 2
    n_diff_heads = d // LANES

    tm_proj = _pick_tile(512, seq, n_batch * n_ctx)
    tm_out = _pick_tile(512, seq, n_batch * n_ctx)
    tm_ffn = _pick_tile(512, seq, n_batch * n_ctx)
    nb_moe = _pick_tile(1024, seq, n_batch * n_ctx)
    tq_diff = _pick_tile(2048, seq)
    tq_win = _pick_tile(256, seq)
    common = dict(seq=seq, n_batch=n_batch)

    xt = jnp.concatenate([x.reshape(n_lat, d), ctx.reshape(n_batch * n_ctx, d)], axis=0)
    n_mod = -(-(n_batch + 1) // 8) * 8
    cv = jnp.zeros((n_mod, d), F32).at[:n_batch].set(c).at[n_batch].set(c_ctx)
    mods = _modvec(cv, ada_w, ada_b).reshape(depth, n_mod, 6, 1, d)
    rope = _rope_tables(seq, tm_proj)
    f_tables = _fourier_tables(seq)
    fc_tables = _dense_fourier_tables(n_ctx)

    n_f = fdim // LANES
    plan_even = ([(0, i * LANES, "plain") for i in range(n_f)]
                 + [(1, i * LANES, "rope_q_log2") for i in range(q_cols // LANES)]
                 + [(1, q_cols, "rope_k"), (1, q_cols + LANES, "plain")])
    plan_odd = ([(0, i * LANES, "rope_q_log2") for i in range(n_diff_heads)]
                + [(0, d + i * LANES, "rope_k") for i in range(n_diff_heads)]
                + [(1, i * LANES, "plain_t") for i in range(n_diff_heads)])

    expert_in_bf16 = expert_in_w.astype(BF16)
    expert_out_bf16 = expert_out_w.astype(BF16)

    for layer in range(depth):
        j = layer // 2
        need_ctx = layer < depth - 1
        rows = n_all if need_ctx else n_lat
        mod = mods[layer]
        ng = norm_g[layer].reshape(4, 1, d)
        if layer % 2 == 0:
            f, z = _proj(xt, ng, mod, 0, 1, mix_in_w[j].astype(BF16), rope, plan_even,
                         [(fdim, F32, False), (q_cols + 2 * LANES, BF16, False)], tm=tm_proj, n_lat=n_lat,
                         **common)
            k_col, v_col = q_cols // LANES, q_cols // LANES + 1
            att = dict(n_batch=n_batch, seq=seq, n_ctx=n_ctx, q_cols=q_cols, k_col=k_col, v_col=v_col)
            mix_f = _fourier(f, f_tables, jnp.zeros((rows, fdim), BF16), n_batch=n_batch, seq=seq)
            mix_a = _win_attention(z, win_sink[j], jnp.zeros((rows, q_cols), BF16), tq=tq_win, **att)
            if need_ctx:
                mix_f = _dense_fourier(f, fc_tables, mix_f, n_batch=n_batch, n_pos=n_ctx,
                                       row_block0=n_lat // n_ctx)
                mix_a = _ctx_gqa_attention(z, win_sink[j], mix_a, **att)
            xt = _outproj(mix_f, 0, mix_a, 0, mix_out_w[j].astype(BF16), xt, ng, mod, 2, tm=tm_out, rows=rows,
                          **common)
            xt = _ffn(xt, ng, mod, ffn_in_w[j].astype(BF16), ffn_out_w[j].astype(BF16),
                      tm=tm_ffn, rows=rows, **common)
        else:
            lam_init = _lambda_init(layer)
            z, vt = _proj(xt, ng, mod, 0, 1, diff_qkv_w[j].astype(BF16), rope, plan_odd,
                          [(2 * d, BF16, False), (d, BF16, True)], tm=tm_proj, n_lat=n_lat, **common)
            subg = diff_subln_g[j].reshape(1, LANES)
            att = dict(n_batch=n_batch, seq=seq, n_ctx=n_ctx, n_heads=n_diff_heads)
            mix = _diff_attention(z, vt, diff_lambda[j], subg, lam_init, jnp.zeros((rows, d), BF16), tq=tq_diff,
                                  **att)
            if need_ctx:
                mix = _diff_attention_ctx(z, vt, diff_lambda[j], subg, lam_init, mix, **att)
            xt = _outproj(mix, 0, mix, 1, diff_out_w[j].astype(BF16), xt, ng, mod, 2, tm=tm_out, rows=rows,
                          **common)
            comb, sel = _router(xt, ng, mod, router_w[j], tm=tm_out, rows=rows, **common)
            xt = _moe(xt, ng, mod, expert_in_bf16, expert_out_bf16, j, comb, sel,
                      nb=nb_moe, tf=1792, rows=rows, **common)
    return xt[:n_lat].reshape(n_batch, seq, d)
```

```python
import functools
import math

import numpy as np
import jax
import jax.numpy as jnp
from jax import lax
from jax.experimental import pallas as pl
from jax.experimental.pallas import tpu as pltpu

EPS = 1e-6
NEG = -1e30
HEAD_DIM = 64
LANES = 128
GRID_W = 64
BLOCK = 128
WINDOW = 128
ROPE_THETA = 10000.0
N_EXPERTS = 8
F32 = jnp.float32
BF16 = jnp.bfloat16
HIGHEST = lax.Precision.HIGHEST
VMEM_LIMIT = 56 * 1024 * 1024


def _cparams(*sem):
    return pltpu.CompilerParams(dimension_semantics=sem, vmem_limit_bytes=VMEM_LIMIT)


def _dot(a, b):
    return jnp.dot(a, b, preferred_element_type=F32)


def _dot_nt(a, b):
    return lax.dot_general(a, b, (((1,), (1,)), ((), ())), preferred_element_type=F32)


def _dot_hi(a, b):
    return jnp.dot(a, b, precision=HIGHEST, preferred_element_type=F32)


def _rms(v):
    return v * lax.rsqrt(jnp.mean(v * v, axis=-1, keepdims=True) + EPS)


def _modulate(x, g, sh, sc):
    return _rms(x) * g * (1.0 + sc) + sh


def _pick_tile(pref, *dims):
    t = pref
    while any(d % t for d in dims):
        t //= 2
    return t


def _modvec_kernel(c_ref, w_ref, b_ref, o_ref):
    cv = c_ref[...]
    s = cv * (1.0 / (1.0 + jnp.exp(-cv)))
    o_ref[...] = _dot(s.astype(BF16), w_ref[...].astype(BF16)) + b_ref[...]


def _modvec(cv, ada_w, ada_b):
    depth, d, n = ada_w.shape
    r = cv.shape[0]
    tn = _pick_tile(1536, n)
    return pl.pallas_call(
        _modvec_kernel,
        grid=(depth, n // tn),
        in_specs=[
            pl.BlockSpec((r, d), lambda l, j: (0, 0)),
            pl.BlockSpec((None, d, tn), lambda l, j: (l, 0, j)),
            pl.BlockSpec((None, 1, tn), lambda l, j: (l, 0, j)),
        ],
        out_specs=pl.BlockSpec((None, r, tn), lambda l, j: (l, 0, j)),
        out_shape=jax.ShapeDtypeStruct((depth, r, n), F32),
        compiler_params=_cparams("parallel", "parallel"),
        name="modvec",
    )(cv, ada_w, ada_b.reshape(depth, 1, n))


def _proj_kernel(x_ref, g_ref, sh_ref, sc_ref, w_ref, cos_ref, sa_ref, sb_ref, *o_refs, plan, group):
    h = _modulate(x_ref[...], g_ref[...], sh_ref[...], sc_ref[...]).astype(BF16)
    n = w_ref.shape[1]
    for g0 in range(0, n, group):
        acc = _dot(h, w_ref[:, g0:g0 + group])
        for c0 in range(0, group, LANES):
            oi, oc, mode = plan[(g0 + c0) // LANES]
            v = acc[:, c0:c0 + LANES]
            if mode.startswith("rope"):
                v = (v * cos_ref[...] + pltpu.roll(v, LANES - 16, 1) * sa_ref[...]
                     + pltpu.roll(v, 16, 1) * sb_ref[...])
                if mode == "rope_q_log2":
                    v = v * (HEAD_DIM ** -0.5 * LOG2E)
            if mode == "plain_t":
                o_refs[oi][oc:oc + LANES, :] = v.T.astype(o_refs[oi].dtype)
            else:
                o_refs[oi][:, oc:oc + LANES] = v.astype(o_refs[oi].dtype)


def _proj(xt, normg, mod, k_sh, k_sc, w, rope, plan, outs, *, tm, n_lat, seq, n_batch):
    t, d = xt.shape
    n = w.shape[1]
    group = _pick_tile(512, n)
    nx = n_lat // tm
    per = seq // tm

    def mrow(i):
        return jnp.minimum(i * tm // seq, n_batch)

    def rrow(i):
        return jnp.where(i < nx, i % per, per)

    in_specs = [
        pl.BlockSpec((tm, d), lambda i: (i, 0)),
        pl.BlockSpec((None, 1, d), lambda i: (0, 0, 0)),
        pl.BlockSpec((None, None, 1, d), lambda i: (mrow(i), k_sh, 0, 0)),
        pl.BlockSpec((None, None, 1, d), lambda i: (mrow(i), k_sc, 0, 0)),
        pl.BlockSpec((d, n), lambda i: (0, 0)),
        pl.BlockSpec((tm, LANES), lambda i: (rrow(i), 0)),
        pl.BlockSpec((tm, LANES), lambda i: (rrow(i), 0)),
        pl.BlockSpec((tm, LANES), lambda i: (rrow(i), 0)),
    ]
    out_specs = [pl.BlockSpec((wd, tm), lambda i: (0, i)) if tr else pl.BlockSpec((tm, wd), lambda i: (i, 0))
                 for wd, _, tr in outs]
    out_shape = [jax.ShapeDtypeStruct((wd, t) if tr else (t, wd), dt) for wd, dt, tr in outs]
    return pl.pallas_call(
        functools.partial(_proj_kernel, plan=plan, group=group),
        grid=(t // tm,),
        in_specs=in_specs,
        out_specs=out_specs,
        out_shape=out_shape,
        compiler_params=_cparams("parallel"),
        name="proj",
    )(xt, normg, mod, mod, w, *rope)


def _rope_tables(seq, tm):
    rows_count = seq // GRID_W
    rows = jnp.repeat(jnp.arange(rows_count), GRID_W).astype(F32)
    cols = jnp.tile(jnp.arange(GRID_W), rows_count).astype(F32)
    axis_dim = HEAD_DIM // 2
    inv = ROPE_THETA ** (-jnp.arange(0, axis_dim, 2, dtype=F32) / axis_dim)
    ar = rows[:, None] * inv
    ac = cols[:, None] * inv
    cr, sr, cc, sc = jnp.cos(ar), jnp.sin(ar), jnp.cos(ac), jnp.sin(ac)
    z = jnp.zeros_like(sr)
    reps = LANES // HEAD_DIM
    cos = jnp.tile(jnp.concatenate([cr, cr, cc, cc], axis=1), (1, reps))
    sa = jnp.tile(jnp.concatenate([-sr, z, -sc, z], axis=1), (1, reps))
    sb = jnp.tile(jnp.concatenate([z, sr, z, sc], axis=1), (1, reps))
    ident = jnp.ones((tm, LANES), F32)
    zero = jnp.zeros((tm, LANES), F32)
    return (jnp.concatenate([cos, ident]), jnp.concatenate([sa, zero]), jnp.concatenate([sb, zero]))


FOURIER_LANES = 2 * LANES


def _split_bf16(t):
    hi = t.astype(BF16)
    return hi, (t - hi.astype(F32)).astype(BF16)


def _dot_split(a, b):
    return _dot(a[0], b[0]) + _dot(a[0], b[1]) + _dot(a[1], b[0])


def _fourier_tables(seq):
    n2 = GRID_W
    n1 = seq // n2
    norm = 1.0 / math.sqrt(seq * LANES)
    a = np.arange(n1)
    k1 = np.arange(n1)
    b = np.arange(n2)
    ang = (b[:, None, None] * k1[None, :, None] + (seq // n1) * k1[None, :, None] * a[None, None, :]) % seq
    th = 2.0 * np.pi * ang / seq
    m1 = np.concatenate([np.cos(th), -np.sin(th)], axis=1).astype(np.float32)
    ph = 2.0 * np.pi * ((b[:, None] * b[None, :]) % n2) / n2
    c2, s2 = np.cos(ph), np.sin(ph)
    g2 = np.block([[c2, s2], [-s2, c2]]).astype(np.float32)
    ch = np.arange(LANES)
    pc = 2.0 * np.pi * ((ch[:, None] * ch[None, :]) % LANES) / LANES
    cc = (np.cos(pc) * norm).astype(np.float32)
    sc = (np.sin(pc) * norm).astype(np.float32)
    cs = np.concatenate([cc, sc], axis=0)
    out = []
    for t in (m1, g2, cs):
        out.extend(_split_bf16(jnp.asarray(t)))
    return tuple(out)


FOURIER_ROW_CHUNK = 512


def _fourier_kernel(*refs, n1, n2, n_grp):
    u_refs = refs[:n_grp]
    m1h_ref, m1l_ref, g2h_ref, g2l_ref, csh_ref, csl_ref, _, o_ref, b_scr, xr_scr, xi_scr = refs[n_grp:]
    seq = n1 * n2
    for b in range(n2):
        xs = jnp.concatenate([u[pl.ds(b, n1, stride=n2), :] for u in u_refs], axis=1)
        z = _dot_split((m1h_ref[b], m1l_ref[b]), _split_bf16(xs))
        for g in range(n_grp):
            b_scr[g, 2 * n1 * b:2 * n1 * (b + 1), :] = z[:, g * LANES:(g + 1) * LANES]
    g2 = (g2h_ref[...], g2l_ref[...])
    for k1 in range(n1):
        bk = jnp.concatenate(
            [jnp.concatenate([b_scr.at[g][pl.ds(k1, n2, stride=2 * n1), :],
                              b_scr.at[g][pl.ds(n1 + k1, n2, stride=2 * n1), :]], axis=0)
             for g in range(n_grp)], axis=1)
        xk = _dot_split(g2, _split_bf16(bk))
        for g in range(n_grp):
            xr_scr[g, n2 * k1:n2 * (k1 + 1), :] = xk[:n2, g * LANES:(g + 1) * LANES]
            xi_scr[g, n2 * k1:n2 * (k1 + 1), :] = xk[n2:, g * LANES:(g + 1) * LANES]
    cs = (csh_ref[...], csl_ref[...])
    rc = min(FOURIER_ROW_CHUNK, seq)
    for g in range(n_grp):
        for r0 in range(0, seq, rc):
            x = jnp.concatenate([xr_scr[g, r0:r0 + rc, :], xi_scr[g, r0:r0 + rc, :]], axis=1)
            xr_scr[g, r0:r0 + rc, :] = _dot_split(_split_bf16(x), cs)
    for g in range(n_grp):
        for k2 in range(n2):
            o_ref[n1 * k2:n1 * (k2 + 1), g * LANES:(g + 1) * LANES] = (
                xr_scr.at[g][pl.ds(k2, n1, stride=n2), :].astype(o_ref.dtype))


def _fourier(f, tables, dst, *, n_batch, seq):
    n2 = GRID_W
    n1 = seq // n2
    n_grp = FOURIER_LANES // LANES
    table_specs = [pl.BlockSpec(t.shape, (lambda b, g, nd=t.ndim: (0,) * nd)) for t in tables]
    u_specs = [pl.BlockSpec((seq, LANES), (lambda b, g, k=k: (b, g * n_grp + k))) for k in range(n_grp)]
    return pl.pallas_call(
        functools.partial(_fourier_kernel, n1=n1, n2=n2, n_grp=n_grp),
        grid=(n_batch, f.shape[1] // FOURIER_LANES),
        in_specs=u_specs + table_specs + [pl.BlockSpec(memory_space=pl.ANY)],
        out_specs=pl.BlockSpec((seq, FOURIER_LANES), lambda b, g: (b, g)),
        out_shape=jax.ShapeDtypeStruct(dst.shape, dst.dtype),
        input_output_aliases={n_grp + len(tables): 0},
        scratch_shapes=[pltpu.VMEM((n_grp, 2 * seq, LANES), F32), pltpu.VMEM((n_grp, seq, LANES), F32),
                        pltpu.VMEM((n_grp, seq, LANES), F32)],
        compiler_params=_cparams("parallel", "parallel"),
        name="fourier",
    )(*([f] * n_grp), *tables, dst)


def _dense_fourier_tables(n):
    norm = 1.0 / math.sqrt(n * LANES)
    p = np.arange(n)
    ph = 2.0 * np.pi * ((p[:, None] * p[None, :]) % n) / n
    ch = np.arange(LANES)
    pc = 2.0 * np.pi * ((ch[:, None] * ch[None, :]) % LANES) / LANES
    return (jnp.asarray(np.cos(ph).astype(np.float32)), jnp.asarray(np.sin(ph).astype(np.float32)),
            jnp.asarray((np.cos(pc) * norm).astype(np.float32)), jnp.asarray((np.sin(pc) * norm).astype(np.float32)))


def _dense_fourier_kernel(u_ref, cl_ref, sl_ref, cc_ref, sc_ref, _, o_ref):
    u = u_ref[...]
    y = _dot_hi(cl_ref[...], _dot_hi(u, cc_ref[...])) - _dot_hi(sl_ref[...], _dot_hi(u, sc_ref[...]))
    o_ref[...] = y.astype(o_ref.dtype)


def _dense_fourier(f, tables, dst, *, n_batch, n_pos, row_block0):
    cl, sl, cc, sc = tables
    groups = f.shape[1] // LANES
    return pl.pallas_call(
        _dense_fourier_kernel,
        grid=(n_batch, groups),
        in_specs=[
            pl.BlockSpec((n_pos, LANES), lambda b, g: (row_block0 + b, g)),
            pl.BlockSpec(cl.shape, lambda b, g: (0, 0)),
            pl.BlockSpec(sl.shape, lambda b, g: (0, 0)),
            pl.BlockSpec(cc.shape, lambda b, g: (0, 0)),
            pl.BlockSpec(sc.shape, lambda b, g: (0, 0)),
            pl.BlockSpec(memory_space=pl.ANY),
        ],
        out_specs=pl.BlockSpec((n_pos, LANES), lambda b, g: (row_block0 + b, g)),
        out_shape=jax.ShapeDtypeStruct(dst.shape, dst.dtype),
        input_output_aliases={5: 0},
        compiler_params=_cparams("parallel", "parallel"),
        name="fourier_ctx",
    )(f, cl, sl, cc, sc, dst)


def _win_kernel(sink_ref, q_ref, *refs, n_qtiles, has_local, n_heads, group_size):
    if has_local:
        kp_ref, kc_ref, kn_ref, vp_ref, vc_ref, vn_ref, kx_ref, vx_ref, _, o_ref = refs
    else:
        kx_ref, vx_ref, _, o_ref = refs
    tq = q_ref.shape[0]
    lane = lax.broadcasted_iota(jnp.int32, (1, LANES), 1)
    half_mask = [lane < HEAD_DIM, lane >= HEAD_DIM]
    pieces = [(kx_ref[...], vx_ref[...], None)]
    if has_local:
        n = pl.program_id(1)
        qi = lax.broadcasted_iota(jnp.int32, (tq, BLOCK), 0)
        kj = lax.broadcasted_iota(jnp.int32, (tq, BLOCK), 1)
        valid_prev = (kj >= qi) & (n >= 1)
        valid_next = (kj <= qi - (tq - WINDOW)) & (n <= n_qtiles - 2)
        di = lax.broadcasted_iota(jnp.int32, (tq, tq), 0) - lax.broadcasted_iota(jnp.int32, (tq, tq), 1)
        valid_mid = (di <= WINDOW) & (di >= -WINDOW)
        pieces += [(kp_ref[...], vp_ref[...], valid_prev), (kc_ref[...], vc_ref[...], valid_mid),
                   (kn_ref[...], vn_ref[...], valid_next)]
    def scores(head):
        pair, half = divmod(head, 2)
        kv = head // group_size
        qp = q_ref[:, pair * LANES:(pair + 1) * LANES].astype(F32)
        src = qp if half == kv else pltpu.roll(qp, HEAD_DIM, 1)
        qe = jnp.where(half_mask[kv], src, 0.0).astype(BF16)
        out = []
        for k, _, valid in pieces:
            s = _dot_nt(qe, k)
            out.append(s if valid is None else jnp.where(valid, s, NEG))
        return out

    def attend(head, sc):
        half = head % 2
        kv = head // group_size
        sink = sink_ref[head] * LOG2E
        m = jnp.zeros((tq, 1), F32) + sink
        for s in sc:
            m = jnp.maximum(m, jnp.max(s, axis=-1, keepdims=True))
        den = jnp.exp2(sink - m)
        pv = jnp.zeros((tq, LANES), F32)
        for s, (_, v, _) in zip(sc, pieces):
            p = jnp.exp2(s - m)
            den = den + jnp.sum(p, axis=-1, keepdims=True)
            pv = pv + _dot(p.astype(BF16), v)
        pv = pv / den
        return pv if half == kv else pltpu.roll(pv, HEAD_DIM, 1)

    sc_next = scores(0)
    out_pair = None
    for head in range(n_heads):
        sc_cur = sc_next
        if head + 1 < n_heads:
            sc_next = scores(head + 1)
        pv = attend(head, sc_cur)
        if head % 2 == 0:
            out_pair = pv
        else:
            pair = head // 2
            o_ref[:, pair * LANES:(pair + 1) * LANES] = jnp.where(half_mask[1], pv, out_pair).astype(o_ref.dtype)


def _win_attention(z, sink, dst, *, n_batch, seq, n_ctx, q_cols, k_col, v_col, tq):
    nbk = seq // BLOCK
    nq = seq // tq
    per = tq // BLOCK
    ctx_blk0 = n_batch * seq // n_ctx
    n_heads = q_cols // HEAD_DIM
    group_size = n_heads // (LANES // HEAD_DIM)

    def edge(col, first):
        return pl.BlockSpec(
            (BLOCK, LANES), lambda b, n: (b * nbk + jnp.clip(n * per + first, 0, nbk - 1), col))

    def mid(col):
        return pl.BlockSpec((tq, LANES), lambda b, n: (b * nq + n, col))

    return pl.pallas_call(
        functools.partial(_win_kernel, n_qtiles=nq, has_local=True, n_heads=n_heads, group_size=group_size),
        grid=(n_batch, nq),
        in_specs=[
            pl.BlockSpec(memory_space=pltpu.SMEM),
            pl.BlockSpec((tq, q_cols), lambda b, n: (b * nq + n, 0)),
            edge(k_col, -1), mid(k_col), edge(k_col, per),
            edge(v_col, -1), mid(v_col), edge(v_col, per),
            pl.BlockSpec((n_ctx, LANES), lambda b, n: (ctx_blk0 + b, k_col)),
            pl.BlockSpec((n_ctx, LANES), lambda b, n: (ctx_blk0 + b, v_col)),
            pl.BlockSpec(memory_space=pl.ANY),
        ],
        out_specs=pl.BlockSpec((tq, q_cols), lambda b, n: (b * nq + n, 0)),
        out_shape=jax.ShapeDtypeStruct(dst.shape, dst.dtype),
        input_output_aliases={10: 0},
        compiler_params=_cparams("parallel", "parallel"),
        name="win_attn",
    )(sink, z, z, z, z, z, z, z, z, z, dst)


def _ctx_gqa_attention(z, sink, dst, *, n_batch, seq, n_ctx, q_cols, k_col, v_col):
    ctx_blk0 = n_batch * seq // n_ctx
    n_heads = q_cols // HEAD_DIM
    group_size = n_heads // (LANES // HEAD_DIM)
    return pl.pallas_call(
        functools.partial(_win_kernel, n_qtiles=0, has_local=False, n_heads=n_heads, group_size=group_size),
        grid=(n_batch,),
        in_specs=[
            pl.BlockSpec(memory_space=pltpu.SMEM),
            pl.BlockSpec((n_ctx, q_cols), lambda b: (ctx_blk0 + b, 0)),
            pl.BlockSpec((n_ctx, LANES), lambda b: (ctx_blk0 + b, k_col)),
            pl.BlockSpec((n_ctx, LANES), lambda b: (ctx_blk0 + b, v_col)),
            pl.BlockSpec(memory_space=pl.ANY),
        ],
        out_specs=pl.BlockSpec((n_ctx, q_cols), lambda b: (ctx_blk0 + b, 0)),
        out_shape=jax.ShapeDtypeStruct(dst.shape, dst.dtype),
        input_output_aliases={4: 0},
        compiler_params=_cparams("parallel"),
        name="ctx_gqa",
    )(sink, z, z, z, dst)


LOG2E = math.log2(math.e)
DIFF_SUB_ROWS = 128
DIFF_KEY_CHUNK = 1024


def _diff_kernel(lam_ref, g_ref, q_ref, *refs, lam_init, has_x):
    if has_x:
        kx_ref, vtx_ref, kc_ref, vtc_ref, _, o_ref, s_scr = refs
        nx = kx_ref.shape[0]
    else:
        kc_ref, vtc_ref, _, o_ref, s_scr = refs
        nx = 0
    nc = kc_ref.shape[0]
    mc = s_scr.shape[2]
    sub = mc // 2
    n_sub = q_ref.shape[0] // sub
    lv = lam_ref[...]
    lam = (jnp.exp(jnp.sum(lv[0:1] * lv[1:2], axis=-1, keepdims=True))
           - jnp.exp(jnp.sum(lv[2:3] * lv[3:4], axis=-1, keepdims=True)) + lam_init)
    lane = lax.broadcasted_iota(jnp.int32, (1, LANES), 1)
    segs = []
    if has_x:
        kw = min(DIFF_KEY_CHUNK, nx)
        segs += [(kx_ref, vtx_ref, r0, kw, r0) for r0 in range(0, nx, kw)]
    segs.append((kc_ref, vtc_ref, 0, nc, nx))

    def scores(u):
        q = q_ref[u * sub:(u + 1) * sub, :]
        zero = jnp.zeros_like(q)
        qs = jnp.concatenate([jnp.where(lane < HEAD_DIM, q, zero), jnp.where(lane >= HEAD_DIM, q, zero)], axis=0)
        m_run = jnp.full((8, mc), -jnp.inf, F32)
        for k_ref, _, r0, w, s0 in segs:
            st = _dot_nt(k_ref[r0:r0 + w, :], qs)
            s_scr[u % 2, s0:s0 + w, :] = st
            m_run = jnp.maximum(m_run, jnp.max(st.reshape(w // 8, 8, mc), axis=0))
        return jnp.broadcast_to(jnp.max(m_run, axis=0, keepdims=True), (8, mc))

    def attend(u, m8):
        l_run = jnp.zeros((8, mc), F32)
        acc = jnp.zeros((LANES, mc), F32)
        for _, vt_ref, r0, w, s0 in segs:
            p = jnp.exp2(s_scr[u % 2, s0:s0 + w, :].reshape(w // 8, 8, mc) - m8[None])
            l_run = l_run + jnp.sum(p, axis=0)
            acc = acc + _dot(vt_ref[:, r0:r0 + w], p.reshape(w, mc).astype(BF16))
        ot = acc / jnp.sum(l_run, axis=0, keepdims=True)
        o = (ot[:, :sub] - lam * ot[:, sub:]).T
        o = _rms(o) * g_ref[...] * (1.0 - lam_init)
        o_ref[u * sub:(u + 1) * sub, :] = o.astype(o_ref.dtype)

    m_next = scores(0)
    for u in range(n_sub):
        m_cur = m_next
        if u + 1 < n_sub:
            m_next = scores(u + 1)
        attend(u, m_cur)


def _diff_attention(z, vt, lam_vec, subln_g, lam_init, dst, *, n_batch, seq, n_ctx, n_heads, tq):
    ctx_blk0 = n_batch * seq // n_ctx
    nq = seq // tq
    return pl.pallas_call(
        functools.partial(_diff_kernel, lam_init=lam_init, has_x=True),
        grid=(n_batch, n_heads, nq),
        in_specs=[
            pl.BlockSpec(lam_vec.shape, lambda b, h, i: (0, 0)),
            pl.BlockSpec((1, LANES), lambda b, h, i: (0, 0)),
            pl.BlockSpec((tq, LANES), lambda b, h, i: (b * nq + i, h)),
            pl.BlockSpec((seq, LANES), lambda b, h, i: (b, n_heads + h)),
            pl.BlockSpec((LANES, seq), lambda b, h, i: (h, b)),
            pl.BlockSpec((n_ctx, LANES), lambda b, h, i: (ctx_blk0 + b, n_heads + h)),
            pl.BlockSpec((LANES, n_ctx), lambda b, h, i: (h, ctx_blk0 + b)),
            pl.BlockSpec(memory_space=pl.ANY),
        ],
        out_specs=pl.BlockSpec((tq, LANES), lambda b, h, i: (b * nq + i, h)),
        out_shape=jax.ShapeDtypeStruct(dst.shape, dst.dtype),
        input_output_aliases={7: 0},
        scratch_shapes=[pltpu.VMEM((2, seq + n_ctx, 2 * min(tq, DIFF_SUB_ROWS)), F32)],
        compiler_params=_cparams("parallel", "parallel", "parallel"),
        name="diff_attn",
    )(lam_vec, subln_g, z, z, vt, z, vt, dst)


def _diff_attention_ctx(z, vt, lam_vec, subln_g, lam_init, dst, *, n_batch, seq, n_ctx, n_heads):
    ctx_blk0 = n_batch * seq // n_ctx
    return pl.pallas_call(
        functools.partial(_diff_kernel, lam_init=lam_init, has_x=False),
        grid=(n_batch, n_heads),
        in_specs=[
            pl.BlockSpec(lam_vec.shape, lambda b, h: (0, 0)),
            pl.BlockSpec((1, LANES), lambda b, h: (0, 0)),
            pl.BlockSpec((n_ctx, LANES), lambda b, h: (ctx_blk0 + b, h)),
            pl.BlockSpec((n_ctx, LANES), lambda b, h: (ctx_blk0 + b, n_heads + h)),
            pl.BlockSpec((LANES, n_ctx), lambda b, h: (h, ctx_blk0 + b)),
            pl.BlockSpec(memory_space=pl.ANY),
        ],
        out_specs=pl.BlockSpec((n_ctx, LANES), lambda b, h: (ctx_blk0 + b, h)),
        out_shape=jax.ShapeDtypeStruct(dst.shape, dst.dtype),
        input_output_aliases={5: 0},
        scratch_shapes=[pltpu.VMEM((2, n_ctx, 2 * min(n_ctx, DIFF_SUB_ROWS)), F32)],
        compiler_params=_cparams("parallel", "parallel"),
        name="diff_attn_ctx",
    )(lam_vec, subln_g, z, z, vt, dst)


def _outproj_kernel(a0_ref, a1_ref, w_ref, x_ref, g_ref, gate_ref, o_ref):
    half = a0_ref.shape[1]
    y = _dot(a0_ref[...], w_ref[:half, :]) + _dot(a1_ref[...], w_ref[half:, :])
    o_ref[...] = x_ref[...] + gate_ref[...] * (_rms(y) * g_ref[...])


def _outproj(a0, a0_col, a1, a1_col, w, xt, normg, mod, k_gate, *, tm, rows, seq, n_batch):
    d = xt.shape[1]
    half = d // 2

    def mrow(i):
        return jnp.minimum(i * tm // seq, n_batch)

    return pl.pallas_call(
        _outproj_kernel,
        grid=(rows // tm,),
        in_specs=[
            pl.BlockSpec((tm, half), lambda i: (i, a0_col)),
            pl.BlockSpec((tm, half), lambda i: (i, a1_col)),
            pl.BlockSpec((d, d), lambda i: (0, 0)),
            pl.BlockSpec((tm, d), lambda i: (i, 0)),
            pl.BlockSpec((None, 1, d), lambda i: (1, 0, 0)),
            pl.BlockSpec((None, None, 1, d), lambda i: (mrow(i), k_gate, 0, 0)),
        ],
        out_specs=pl.BlockSpec((tm, d), lambda i: (i, 0)),
        out_shape=jax.ShapeDtypeStruct((rows, d), F32),
        compiler_params=_cparams("parallel"),
        name="outproj",
    )(a0, a1, w, xt, normg, mod)


def _router_kernel(x_ref, g_ref, sh_ref, sc_ref, rw_ref, o_ref, sel_ref):
    h = _modulate(x_ref[...], g_ref[...], sh_ref[...], sc_ref[...])
    logits = _dot_split(_split_bf16(h), _split_bf16(rw_ref[...]))
    lane = lax.broadcasted_iota(jnp.int32, logits.shape, 1)
    ninf = -jnp.inf
    logits = jnp.where(lane < N_EXPERTS, logits, ninf)
    m1 = jnp.max(logits, axis=-1, keepdims=True)
    i1 = jnp.min(jnp.where(logits == m1, lane, LANES), axis=-1, keepdims=True)
    sel1 = lane == i1
    rest = jnp.where(sel1, ninf, logits)
    m2 = jnp.max(rest, axis=-1, keepdims=True)
    i2 = jnp.min(jnp.where(rest == m2, lane, LANES), axis=-1, keepdims=True)
    sel2 = lane == i2
    e2 = jnp.exp(m2 - m1)
    den = 1.0 + e2
    o_ref[...] = jnp.where(sel1, 1.0 / den, 0.0) + jnp.where(sel2, e2 / den, 0.0)
    sel_ref[...] = jnp.where(sel1 | sel2, 1.0, 0.0)


def _router(xt, normg, mod, router_w, *, tm, rows, seq, n_batch):
    d = xt.shape[1]
    rw = jnp.zeros((d, LANES), F32).at[:, :N_EXPERTS].set(router_w)

    def mrow(i):
        return jnp.minimum(i * tm // seq, n_batch)

    return pl.pallas_call(
        _router_kernel,
        grid=(rows // tm,),
        in_specs=[
            pl.BlockSpec((tm, d), lambda i: (i, 0)),
            pl.BlockSpec((None, 1, d), lambda i: (2, 0, 0)),
            pl.BlockSpec((None, None, 1, d), lambda i: (mrow(i), 3, 0, 0)),
            pl.BlockSpec((None, None, 1, d), lambda i: (mrow(i), 4, 0, 0)),
            pl.BlockSpec((d, LANES), lambda i: (0, 0)),
        ],
        out_specs=[pl.BlockSpec((tm, LANES), lambda i: (i, 0))] * 2,
        out_shape=[jax.ShapeDtypeStruct((rows, LANES), F32)] * 2,
        compiler_params=_cparams("parallel"),
        name="router",
    )(xt, normg, mod, mod, rw)


SWIGLU_CHUNK = 256


def _swiglu(h, wg_ref, wu_ref, wo_ref, u_off):
    width = wo_ref.shape[0]
    acc = None
    for c0 in range(0, width, SWIGLU_CHUNK):
        gp = _dot(h, wg_ref[:, c0:c0 + SWIGLU_CHUNK])
        up = _dot(h, wu_ref[:, u_off + c0:u_off + c0 + SWIGLU_CHUNK])
        a = gp * (1.0 / (1.0 + jnp.exp(-gp))) * up
        part = _dot(a.astype(BF16), wo_ref[c0:c0 + SWIGLU_CHUNK, :])
        acc = part if acc is None else acc + part
    return acc


def _ffn_kernel(x_ref, g2_ref, sh_ref, sc_ref, win_ref, wout_ref, g3_ref, gate_ref, o_ref):
    x = x_ref[...]
    h = _modulate(x, g2_ref[...], sh_ref[...], sc_ref[...]).astype(BF16)
    y = _swiglu(h, win_ref, win_ref, wout_ref, wout_ref.shape[0])
    o_ref[...] = x + gate_ref[...] * (_rms(y) * g3_ref[...])


def _ffn(xt, normg, mod, w_in, w_out, *, tm, rows, seq, n_batch):
    d = xt.shape[1]

    def mrow(i):
        return jnp.minimum(i * tm // seq, n_batch)

    in_specs = [
        pl.BlockSpec((tm, d), lambda i: (i, 0)),
        pl.BlockSpec((None, 1, d), lambda i: (2, 0, 0)),
        pl.BlockSpec((None, None, 1, d), lambda i: (mrow(i), 3, 0, 0)),
        pl.BlockSpec((None, None, 1, d), lambda i: (mrow(i), 4, 0, 0)),
        pl.BlockSpec(w_in.shape, lambda i: (0, 0)),
        pl.BlockSpec(w_out.shape, lambda i: (0, 0)),
        pl.BlockSpec((None, 1, d), lambda i: (3, 0, 0)),
        pl.BlockSpec((None, None, 1, d), lambda i: (mrow(i), 5, 0, 0)),
    ]
    return pl.pallas_call(
        _ffn_kernel,
        grid=(rows // tm,),
        in_specs=in_specs,
        out_specs=pl.BlockSpec((tm, d), lambda i: (i, 0)),
        out_shape=jax.ShapeDtypeStruct((rows, d), F32),
        compiler_params=_cparams("parallel"),
        name="ffn",
    )(xt, normg, mod, mod, w_in, w_out, normg, mod)


MOE_DISPATCH_ROWS = 256
MOE_COMBINE_ROWS = 512
MOE_SEG_ALIGN = 16
MOE_TILE = 512


def _moe_keys(sel):
    nb = sel.shape[0]
    ti = lax.broadcasted_iota(jnp.int32, (nb, nb), 0)
    tj = lax.broadcasted_iota(jnp.int32, (nb, nb), 1)
    lower = jnp.where(tj < ti, 1.0, 0.0).astype(BF16)
    rank = _dot(lower, sel.astype(BF16))
    return jnp.where(sel > 0.0, rank, -1.0)


def _moe_dispatch_kernel(nchunk_ref, off_ref, x_ref, g2_ref, sh_ref, sc_ref, sel_ref, _, hs_hbm,
                         h_scr, keyt_scr, stage, sems, *, n_exp):
    b = pl.program_id(0)
    rows = MOE_DISPATCH_ROWS
    h_scr[...] = _modulate(x_ref[...], g2_ref[...], sh_ref[...], sc_ref[...]).astype(BF16)
    keyt_scr[...] = _moe_keys(sel_ref[...]).T

    def copy(slot, row0):
        return pltpu.make_async_copy(stage.at[slot], hs_hbm.at[pl.ds(row0, rows), :], sems.at[slot])

    issued = jnp.int32(0)
    for e in range(n_exp):
        key_row = keyt_scr[e:e + 1, :]
        base = off_ref[b * n_exp + e]

        def chunk(c, k):
            slot = k % 2

            @pl.when(k >= 2)
            def _():
                copy(slot, 0).wait()

            r = (c * rows + lax.broadcasted_iota(jnp.int32, (rows, 1), 0)).astype(F32)
            onehot = jnp.where(key_row == r, 1.0, 0.0).astype(BF16)
            stage[slot] = _dot(onehot, h_scr[...]).astype(BF16)
            copy(slot, pl.multiple_of(base + c * rows, MOE_SEG_ALIGN)).start()
            return k + 1

        issued = lax.fori_loop(0, nchunk_ref[b * n_exp + e], chunk, issued)

    @pl.when(issued >= 1)
    def _():
        copy((issued - 1) % 2, 0).wait()

    @pl.when(issued >= 2)
    def _():
        copy(issued % 2, 0).wait()


def _moe_experts_kernel(tile_ref, exp_ref, valid_ref, hs_ref, wg_ref, wu_ref, wo_ref, y_ref, acc, *, n_f):
    t = pl.program_id(0)
    f = pl.program_id(1)

    @pl.when(valid_ref[t] == 1)
    def _():
        part = _swiglu(hs_ref[...], wg_ref, wu_ref, wo_ref, 0)

        @pl.when(f == 0)
        def _():
            acc[...] = part

        @pl.when(f > 0)
        def _():
            acc[...] += part

        @pl.when(f == n_f - 1)
        def _():
            y_ref[...] = acc[...].astype(y_ref.dtype)


def _moe_combine_kernel(npiece_ref, off_ref, x_ref, g3_ref, gate_ref, comb_ref, sel_ref, y_hbm, o_ref,
                        key_scr, stage, sems, *, n_exp):
    b = pl.program_id(0)
    rows = MOE_COMBINE_ROWS
    nb = x_ref.shape[0]
    key_scr[...] = _moe_keys(sel_ref[...])
    o_ref[...] = jnp.zeros_like(o_ref)
    lane = lax.broadcasted_iota(jnp.int32, (nb, LANES), 1)

    def copy(slot, row0):
        return pltpu.make_async_copy(y_hbm.at[pl.ds(row0, rows), :], stage.at[slot], sems.at[slot])

    def row0(e, c):
        return pl.multiple_of(off_ref[b * n_exp + e] + c * rows, MOE_SEG_ALIGN)

    copy(0, row0(0, 0)).start()
    done = jnp.int32(0)
    for e in range(n_exp):
        pick = lane == e
        key_col = jnp.sum(jnp.where(pick, key_scr[...], 0.0), axis=-1, keepdims=True)
        gate_col = jnp.sum(jnp.where(pick, comb_ref[...], 0.0), axis=-1, keepdims=True)
        n_pieces = npiece_ref[b * n_exp + e]

        def piece(c, k, e=e, n_pieces=n_pieces, key_col=key_col, gate_col=gate_col):
            slot = k % 2
            copy(slot, 0).wait()

            @pl.when(c + 1 < n_pieces)
            def _():
                copy(1 - slot, row0(e, c + 1)).start()

            if e + 1 < n_exp:
                @pl.when(c + 1 == n_pieces)
                def _():
                    copy(1 - slot, row0(e + 1, 0)).start()

            r = (c * rows + lax.broadcasted_iota(jnp.int32, (1, rows), 1)).astype(F32)
            onehot = jnp.where(key_col == r, 1.0, 0.0).astype(BF16)
            o_ref[...] += gate_col * _dot(onehot, stage[slot])
            return k + 1

        done = lax.fori_loop(0, n_pieces, piece, done)

    o_ref[...] = x_ref[...] + gate_ref[...] * (_rms(o_ref[...]) * g3_ref[...])


def _moe(xt, normg, mod, w_in, w_out, j, comb, sel, *, nb, tf, rows, seq, n_batch):
    d = xt.shape[1]
    n_exp, fdim = w_out.shape[1], w_out.shape[2]
    n_f = fdim // tf
    n_blk = rows // nb
    i32 = jnp.int32

    counts = jnp.sum(sel.reshape(n_blk, nb, LANES)[:, :, :n_exp], axis=1).astype(i32)
    seg = -(-counts // MOE_SEG_ALIGN) * MOE_SEG_ALIGN
    region = -(-(jnp.sum(seg, axis=0) + MOE_DISPATCH_ROWS) // MOE_TILE) * MOE_TILE
    region_end = jnp.cumsum(region)
    region_start = region_end - region
    off = (region_start[None, :] + jnp.cumsum(seg, axis=0) - seg).reshape(n_blk * n_exp).astype(i32)
    nchunk = (-(-counts // MOE_DISPATCH_ROWS)).reshape(n_blk * n_exp).astype(i32)
    npiece = jnp.maximum(-(-counts // MOE_COMBINE_ROWS), 1).reshape(n_blk * n_exp).astype(i32)
    cap = 2 * rows + n_blk * n_exp * (MOE_SEG_ALIGN - 1) + n_exp * (MOE_DISPATCH_ROWS + MOE_TILE - 1)
    n_tiles = -(-cap // MOE_TILE)
    cap = n_tiles * MOE_TILE + MOE_COMBINE_ROWS
    tiles = jnp.arange(n_tiles, dtype=i32)
    n_used = region_end[-1] // MOE_TILE
    tile_map = jnp.minimum(tiles, n_used - 1).astype(i32)
    tile_exp = jnp.minimum(jnp.searchsorted(region_end, tile_map * MOE_TILE, side="right"), n_exp - 1).astype(i32)
    tile_valid = (tiles < n_used).astype(i32)

    def mrow(i):
        return jnp.minimum(i * nb // seq, n_batch)

    hs = pl.pallas_call(
        functools.partial(_moe_dispatch_kernel, n_exp=n_exp),
        grid_spec=pltpu.PrefetchScalarGridSpec(
            num_scalar_prefetch=2,
            grid=(n_blk,),
            in_specs=[
                pl.BlockSpec((nb, d), lambda i, *_: (i, 0)),
                pl.BlockSpec((None, 1, d), lambda i, *_: (2, 0, 0)),
                pl.BlockSpec((None, None, 1, d), lambda i, *_: (mrow(i), 3, 0, 0)),
                pl.BlockSpec((None, None, 1, d), lambda i, *_: (mrow(i), 4, 0, 0)),
                pl.BlockSpec((nb, LANES), lambda i, *_: (i, 0)),
                pl.BlockSpec(memory_space=pl.ANY),
            ],
            out_specs=pl.BlockSpec(memory_space=pl.ANY),
            scratch_shapes=[
                pltpu.VMEM((nb, d), BF16),
                pltpu.VMEM((LANES, nb), F32),
                pltpu.VMEM((2, MOE_DISPATCH_ROWS, d), BF16),
                pltpu.SemaphoreType.DMA((2,)),
            ],
        ),
        out_shape=jax.ShapeDtypeStruct((cap, d), BF16),
        input_output_aliases={7: 0},
        compiler_params=_cparams("arbitrary"),
        name="moe_dispatch",
    )(nchunk, off, xt, normg, mod, mod, sel, jnp.zeros((cap, d), BF16))

    ys = pl.pallas_call(
        functools.partial(_moe_experts_kernel, n_f=n_f),
        grid_spec=pltpu.PrefetchScalarGridSpec(
            num_scalar_prefetch=3,
            grid=(n_tiles, n_f),
            in_specs=[
                pl.BlockSpec((MOE_TILE, d), lambda t, f, tm, te, tv: (tm[t], 0)),
                pl.BlockSpec((None, None, d, tf), lambda t, f, tm, te, tv: (j, te[t], 0, f)),
                pl.BlockSpec((None, None, d, tf), lambda t, f, tm, te, tv: (j, te[t], 0, n_f + f)),
                pl.BlockSpec((None, None, tf, d), lambda t, f, tm, te, tv: (j, te[t], f, 0)),
            ],
            out_specs=pl.BlockSpec((MOE_TILE, d), lambda t, f, tm, te, tv: (tm[t], 0)),
            scratch_shapes=[pltpu.VMEM((MOE_TILE, d), F32)],
        ),
        out_shape=jax.ShapeDtypeStruct((cap, d), BF16),
        input_output_aliases={3: 0},
        compiler_params=_cparams("arbitrary", "arbitrary"),
        name="moe_experts",
    )(tile_map, tile_exp, tile_valid, hs, w_in, w_in, w_out)

    return pl.pallas_call(
        functools.partial(_moe_combine_kernel, n_exp=n_exp),
        grid_spec=pltpu.PrefetchScalarGridSpec(
            num_scalar_prefetch=2,
            grid=(n_blk,),
            in_specs=[
                pl.BlockSpec((nb, d), lambda i, *_: (i, 0)),
                pl.BlockSpec((None, 1, d), lambda i, *_: (3, 0, 0)),
                pl.BlockSpec((None, None, 1, d), lambda i, *_: (mrow(i), 5, 0, 0)),
                pl.BlockSpec((nb, LANES), lambda i, *_: (i, 0)),
                pl.BlockSpec((nb, LANES), lambda i, *_: (i, 0)),
                pl.BlockSpec(memory_space=pl.ANY),
            ],
            out_specs=pl.BlockSpec((nb, d), lambda i, *_: (i, 0)),
            scratch_shapes=[
                pltpu.VMEM((nb, LANES), F32),
                pltpu.VMEM((2, MOE_COMBINE_ROWS, d), BF16),
                pltpu.SemaphoreType.DMA((2,)),
            ],
        ),
        out_shape=jax.ShapeDtypeStruct((rows, d), F32),
        compiler_params=_cparams("parallel"),
        name="moe_combine",
    )(npiece, off, xt, normg, mod, comb, sel, ys)


def _lambda_init(layer):
    return 0.8 - 0.6 * math.exp(-0.3 * layer)


def kernel(x, c, ctx, c_ctx, ada_w, ada_b, norm_g, mix_in_w, mix_out_w, win_sink, diff_qkv_w, diff_out_w,
           diff_lambda, diff_subln_g, ffn_in_w, ffn_out_w, router_w, expert_in_w, expert_out_w):
    n_batch, seq, d = x.shape
    n_ctx = ctx.shape[1]
    depth = ada_w.shape[0]
    n_lat = n_batch * seq
    n_all = n_lat + n_batch * n_ctx
    fdim = mix_in_w.shape[2] - (d // 2 + 2 * LANES)
    q_cols = d // 2
    n_diff_heads = d // LANES

    tm_proj = _pick_tile(1024, seq, n_batch * n_ctx)
    tm_out = _pick_tile(1024, seq, n_batch * n_ctx)
    tm_ffn = _pick_tile(512, seq, n_batch * n_ctx)
    nb_moe = _pick_tile(1024, seq, n_batch * n_ctx)
    tq_diff = _pick_tile(2048, seq)
    tq_win = _pick_tile(512, seq)
    common = dict(seq=seq, n_batch=n_batch)

    xt = jnp.concatenate([x.reshape(n_lat, d), ctx.reshape(n_batch * n_ctx, d)], axis=0)
    n_mod = -(-(n_batch + 1) // 8) * 8
    cv = jnp.zeros((n_mod, d), F32).at[:n_batch].set(c).at[n_batch].set(c_ctx)
    mods = _modvec(cv, ada_w, ada_b).reshape(depth, n_mod, 6, 1, d)
    rope = _rope_tables(seq, tm_proj)
    f_tables = _fourier_tables(seq)
    fc_tables = _dense_fourier_tables(n_ctx)

    n_f = fdim // LANES
    plan_even = ([(0, i * LANES, "plain") for i in range(n_f)]
                 + [(1, i * LANES, "rope_q_log2") for i in range(q_cols // LANES)]
                 + [(1, q_cols, "rope_k"), (1, q_cols + LANES, "plain")])
    plan_odd = ([(0, i * LANES, "rope_q_log2") for i in range(n_diff_heads)]
                + [(0, d + i * LANES, "rope_k") for i in range(n_diff_heads)]
                + [(1, i * LANES, "plain_t") for i in range(n_diff_heads)])

    expert_in_bf16 = expert_in_w.astype(BF16)
    expert_out_bf16 = expert_out_w.astype(BF16)

    for layer in range(depth):
        j = layer // 2
        need_ctx = layer < depth - 1
        rows = n_all if need_ctx else n_lat
        mod = mods[layer]
        ng = norm_g[layer].reshape(4, 1, d)
        if layer % 2 == 0:
            f, z = _proj(xt, ng, mod, 0, 1, mix_in_w[j].astype(BF16), rope, plan_even,
                         [(fdim, F32, False), (q_cols + 2 * LANES, BF16, False)], tm=tm_proj, n_lat=n_lat,
                         **common)
            k_col, v_col = q_cols // LANES, q_cols // LANES + 1
            att = dict(n_batch=n_batch, seq=seq, n_ctx=n_ctx, q_cols=q_cols, k_col=k_col, v_col=v_col)
            mix_f = _fourier(f, f_tables, jnp.zeros((rows, fdim), BF16), n_batch=n_batch, seq=seq)
            mix_a = _win_attention(z, win_sink[j], jnp.zeros((rows, q_cols), BF16), tq=tq_win, **att)
            if need_ctx:
                mix_f = _dense_fourier(f, fc_tables, mix_f, n_batch=n_batch, n_pos=n_ctx,
                                       row_block0=n_lat // n_ctx)
                mix_a = _ctx_gqa_attention(z, win_sink[j], mix_a, **att)
            xt = _outproj(mix_f, 0, mix_a, 0, mix_out_w[j].astype(BF16), xt, ng, mod, 2, tm=tm_out, rows=rows,
                          **common)
            xt = _ffn(xt, ng, mod, ffn_in_w[j].astype(BF16), ffn_out_w[j].astype(BF16),
                      tm=tm_ffn, rows=rows, **common)
        else:
            lam_init = _lambda_init(layer)
            z, vt = _proj(xt, ng, mod, 0, 1, diff_qkv_w[j].astype(BF16), rope, plan_odd,
                          [(2 * d, BF16, False), (d, BF16, True)], tm=tm_proj, n_lat=n_lat, **common)
            subg = diff_subln_g[j].reshape(1, LANES)
            att = dict(n_batch=n_batch, seq=seq, n_ctx=n_ctx, n_heads=n_diff_heads)
            mix = _diff_attention(z, vt, diff_lambda[j], subg, lam_init, jnp.zeros((rows, d), BF16), tq=tq_diff,
                                  **att)
            if need_ctx:
                mix = _diff_attention_ctx(z, vt, diff_lambda[j], subg, lam_init, mix, **att)
            xt = _outproj(mix, 0, mix, 1, diff_out_w[j].astype(BF16), xt, ng, mod, 2, tm=tm_out, rows=rows,
                          **common)
            comb, sel = _router(xt, ng, mod, router_w[j], tm=tm_out, rows=rows, **common)
            xt = _moe(xt, ng, mod, expert_in_bf16, expert_out_bf16, j, comb, sel,
                      nb=nb_moe, tf=1792, rows=rows, **common)
    return xt[:n_lat].reshape(n_batch, seq, d)
```

```python
import functools
import math

import numpy as np
import jax
import jax.numpy as jnp
from jax import lax
from jax.experimental import pallas as pl
from jax.experimental.pallas import tpu as pltpu

EPS = 1e-6
NEG = -1e30
HEAD_DIM = 64
LANES = 128
GRID_W = 64
BLOCK = 128
WINDOW = 128
ROPE_THETA = 10000.0
N_EXPERTS = 8
F32 = jnp.float32
BF16 = jnp.bfloat16
HIGHEST = lax.Precision.HIGHEST
VMEM_LIMIT = 56 * 1024 * 1024


def _cparams(*sem):
    return pltpu.CompilerParams(dimension_semantics=sem, vmem_limit_bytes=VMEM_LIMIT)


def _dot(a, b):
    return jnp.dot(a, b, preferred_element_type=F32)


def _dot_nt(a, b):
    return lax.dot_general(a, b, (((1,), (1,)), ((), ())), preferred_element_type=F32)


def _dot_hi(a, b):
    return jnp.dot(a, b, precision=HIGHEST, preferred_element_type=F32)


def _rms(v):
    return v * lax.rsqrt(jnp.mean(v * v, axis=-1, keepdims=True) + EPS)


def _modulate(x, g, sh, sc):
    return _rms(x) * g * (1.0 + sc) + sh


def _pick_tile(pref, *dims):
    t = pref
    while any(d % t for d in dims):
        t //= 2
    return t


def _modvec_kernel(c_ref, w_ref, b_ref, o_ref):
    cv = c_ref[...]
    s = cv * (1.0 / (1.0 + jnp.exp(-cv)))
    o_ref[...] = _dot(s.astype(BF16), w_ref[...].astype(BF16)) + b_ref[...]


def _modvec(cv, ada_w, ada_b):
    depth, d, n = ada_w.shape
    r = cv.shape[0]
    tn = _pick_tile(1536, n)
    return pl.pallas_call(
        _modvec_kernel,
        grid=(depth, n // tn),
        in_specs=[
            pl.BlockSpec((r, d), lambda l, j: (0, 0)),
            pl.BlockSpec((None, d, tn), lambda l, j: (l, 0, j)),
            pl.BlockSpec((None, 1, tn), lambda l, j: (l, 0, j)),
        ],
        out_specs=pl.BlockSpec((None, r, tn), lambda l, j: (l, 0, j)),
        out_shape=jax.ShapeDtypeStruct((depth, r, n), F32),
        compiler_params=_cparams("parallel", "parallel"),
        name="modvec",
    )(cv, ada_w, ada_b.reshape(depth, 1, n))


def _proj_kernel(x_ref, g_ref, sh_ref, sc_ref, w_ref, cos_ref, sa_ref, sb_ref, *o_refs, plan, group):
    h = _modulate(x_ref[...], g_ref[...], sh_ref[...], sc_ref[...]).astype(BF16)
    n = w_ref.shape[1]
    for g0 in range(0, n, group):
        acc = _dot(h, w_ref[:, g0:g0 + group])
        for c0 in range(0, group, LANES):
            oi, oc, mode = plan[(g0 + c0) // LANES]
            v = acc[:, c0:c0 + LANES]
            if mode.startswith("rope"):
                v = (v * cos_ref[...] + pltpu.roll(v, LANES - 16, 1) * sa_ref[...]
                     + pltpu.roll(v, 16, 1) * sb_ref[...])
                if mode == "rope_q_log2":
                    v = v * (HEAD_DIM ** -0.5 * LOG2E)
            if mode == "plain_t":
                o_refs[oi][oc:oc + LANES, :] = v.T.astype(o_refs[oi].dtype)
            else:
                o_refs[oi][:, oc:oc + LANES] = v.astype(o_refs[oi].dtype)


def _proj(xt, normg, mod, k_sh, k_sc, w, rope, plan, outs, *, tm, n_lat, seq, n_batch):
    t, d = xt.shape
    n = w.shape[1]
    group = _pick_tile(512, n)
    nx = n_lat // tm
    per = seq // tm

    def mrow(i):
        return jnp.minimum(i * tm // seq, n_batch)

    def rrow(i):
        return jnp.where(i < nx, i % per, per)

    in_specs = [
        pl.BlockSpec((tm, d), lambda i: (i, 0)),
        pl.BlockSpec((None, 1, d), lambda i: (0, 0, 0)),
        pl.BlockSpec((None, None, 1, d), lambda i: (mrow(i), k_sh, 0, 0)),
        pl.BlockSpec((None, None, 1, d), lambda i: (mrow(i), k_sc, 0, 0)),
        pl.BlockSpec((d, n), lambda i: (0, 0)),
        pl.BlockSpec((tm, LANES), lambda i: (rrow(i), 0)),
        pl.BlockSpec((tm, LANES), lambda i: (rrow(i), 0)),
        pl.BlockSpec((tm, LANES), lambda i: (rrow(i), 0)),
    ]
    out_specs = [pl.BlockSpec((wd, tm), lambda i: (0, i)) if tr else pl.BlockSpec((tm, wd), lambda i: (i, 0))
                 for wd, _, tr in outs]
    out_shape = [jax.ShapeDtypeStruct((wd, t) if tr else (t, wd), dt) for wd, dt, tr in outs]
    return pl.pallas_call(
        functools.partial(_proj_kernel, plan=plan, group=group),
        grid=(t // tm,),
        in_specs=in_specs,
        out_specs=out_specs,
        out_shape=out_shape,
        compiler_params=_cparams("parallel"),
        name="proj",
    )(xt, normg, mod, mod, w, *rope)


def _rope_tables(seq, tm):
    rows_count = seq // GRID_W
    rows = jnp.repeat(jnp.arange(rows_count), GRID_W).astype(F32)
    cols = jnp.tile(jnp.arange(GRID_W), rows_count).astype(F32)
    axis_dim = HEAD_DIM // 2
    inv = ROPE_THETA ** (-jnp.arange(0, axis_dim, 2, dtype=F32) / axis_dim)
    ar = rows[:, None] * inv
    ac = cols[:, None] * inv
    cr, sr, cc, sc = jnp.cos(ar), jnp.sin(ar), jnp.cos(ac), jnp.sin(ac)
    z = jnp.zeros_like(sr)
    reps = LANES // HEAD_DIM
    cos = jnp.tile(jnp.concatenate([cr, cr, cc, cc], axis=1), (1, reps))
    sa = jnp.tile(jnp.concatenate([-sr, z, -sc, z], axis=1), (1, reps))
    sb = jnp.tile(jnp.concatenate([z, sr, z, sc], axis=1), (1, reps))
    ident = jnp.ones((tm, LANES), F32)
    zero = jnp.zeros((tm, LANES), F32)
    return (jnp.concatenate([cos, ident]), jnp.concatenate([sa, zero]), jnp.concatenate([sb, zero]))


FOURIER_LANES = 2 * LANES


def _split_bf16(t):
    hi = t.astype(BF16)
    return hi, (t - hi.astype(F32)).astype(BF16)


def _dot_split(a, b):
    return _dot(a[0], b[0]) + _dot(a[0], b[1]) + _dot(a[1], b[0])


def _fourier_tables(seq):
    n2 = GRID_W
    n1 = seq // n2
    norm = 1.0 / math.sqrt(seq * LANES)
    a = np.arange(n1)
    k1 = np.arange(n1)
    b = np.arange(n2)
    ang = (b[:, None, None] * k1[None, :, None] + (seq // n1) * k1[None, :, None] * a[None, None, :]) % seq
    th = 2.0 * np.pi * ang / seq
    m1 = np.concatenate([np.cos(th), -np.sin(th)], axis=1).astype(np.float32)
    ph = 2.0 * np.pi * ((b[:, None] * b[None, :]) % n2) / n2
    c2, s2 = np.cos(ph), np.sin(ph)
    g2 = np.block([[c2, s2], [-s2, c2]]).astype(np.float32)
    ch = np.arange(LANES)
    pc = 2.0 * np.pi * ((ch[:, None] * ch[None, :]) % LANES) / LANES
    cc = (np.cos(pc) * norm).astype(np.float32)
    sc = (np.sin(pc) * norm).astype(np.float32)
    cs = np.concatenate([cc, sc], axis=0)
    out = []
    for t in (m1, g2, cs):
        out.extend(_split_bf16(jnp.asarray(t)))
    return tuple(out)


FOURIER_ROW_CHUNK = 512


def _fourier_kernel(*refs, n1, n2, n_grp):
    u_refs = refs[:n_grp]
    m1h_ref, m1l_ref, g2h_ref, g2l_ref, csh_ref, csl_ref, _, o_ref, b_scr, xr_scr, xi_scr = refs[n_grp:]
    seq = n1 * n2
    for b in range(n2):
        xs = jnp.concatenate([u[pl.ds(b, n1, stride=n2), :] for u in u_refs], axis=1)
        z = _dot_split((m1h_ref[b], m1l_ref[b]), _split_bf16(xs))
        for g in range(n_grp):
            b_scr[g, 2 * n1 * b:2 * n1 * (b + 1), :] = z[:, g * LANES:(g + 1) * LANES]
    g2 = (g2h_ref[...], g2l_ref[...])
    for k1 in range(n1):
        bk = jnp.concatenate(
            [jnp.concatenate([b_scr.at[g][pl.ds(k1, n2, stride=2 * n1), :],
                              b_scr.at[g][pl.ds(n1 + k1, n2, stride=2 * n1), :]], axis=0)
             for g in range(n_grp)], axis=1)
        xk = _dot_split(g2, _split_bf16(bk))
        for g in range(n_grp):
            xr_scr[g, n2 * k1:n2 * (k1 + 1), :] = xk[:n2, g * LANES:(g + 1) * LANES]
            xi_scr[g, n2 * k1:n2 * (k1 + 1), :] = xk[n2:, g * LANES:(g + 1) * LANES]
    cs = (csh_ref[...], csl_ref[...])
    rc = min(FOURIER_ROW_CHUNK, seq)
    for g in range(n_grp):
        for r0 in range(0, seq, rc):
            x = jnp.concatenate([xr_scr[g, r0:r0 + rc, :], xi_scr[g, r0:r0 + rc, :]], axis=1)
            xr_scr[g, r0:r0 + rc, :] = _dot_split(_split_bf16(x), cs)
    for g in range(n_grp):
        for k2 in range(n2):
            o_ref[n1 * k2:n1 * (k2 + 1), g * LANES:(g + 1) * LANES] = (
                xr_scr.at[g][pl.ds(k2, n1, stride=n2), :].astype(o_ref.dtype))


def _fourier(f, tables, dst, *, n_batch, seq):
    n2 = GRID_W
    n1 = seq // n2
    n_grp = FOURIER_LANES // LANES
    table_specs = [pl.BlockSpec(t.shape, (lambda b, g, nd=t.ndim: (0,) * nd)) for t in tables]
    u_specs = [pl.BlockSpec((seq, LANES), (lambda b, g, k=k: (b, g * n_grp + k))) for k in range(n_grp)]
    return pl.pallas_call(
        functools.partial(_fourier_kernel, n1=n1, n2=n2, n_grp=n_grp),
        grid=(n_batch, f.shape[1] // FOURIER_LANES),
        in_specs=u_specs + table_specs + [pl.BlockSpec(memory_space=pl.ANY)],
        out_specs=pl.BlockSpec((seq, FOURIER_LANES), lambda b, g: (b, g)),
        out_shape=jax.ShapeDtypeStruct(dst.shape, dst.dtype),
        input_output_aliases={n_grp + len(tables): 0},
        scratch_shapes=[pltpu.VMEM((n_grp, 2 * seq, LANES), F32), pltpu.VMEM((n_grp, seq, LANES), F32),
                        pltpu.VMEM((n_grp, seq, LANES), F32)],
        compiler_params=_cparams("parallel", "parallel"),
        name="fourier",
    )(*([f] * n_grp), *tables, dst)


def _dense_fourier_tables(n):
    norm = 1.0 / math.sqrt(n * LANES)
    p = np.arange(n)
    ph = 2.0 * np.pi * ((p[:, None] * p[None, :]) % n) / n
    ch = np.arange(LANES)
    pc = 2.0 * np.pi * ((ch[:, None] * ch[None, :]) % LANES) / LANES
    return (jnp.asarray(np.cos(ph).astype(np.float32)), jnp.asarray(np.sin(ph).astype(np.float32)),
            jnp.asarray((np.cos(pc) * norm).astype(np.float32)), jnp.asarray((np.sin(pc) * norm).astype(np.float32)))


def _dense_fourier_kernel(u_ref, cl_ref, sl_ref, cc_ref, sc_ref, _, o_ref):
    u = u_ref[...]
    y = _dot_hi(cl_ref[...], _dot_hi(u, cc_ref[...])) - _dot_hi(sl_ref[...], _dot_hi(u, sc_ref[...]))
    o_ref[...] = y.astype(o_ref.dtype)


def _dense_fourier(f, tables, dst, *, n_batch, n_pos, row_block0):
    cl, sl, cc, sc = tables
    groups = f.shape[1] // LANES
    return pl.pallas_call(
        _dense_fourier_kernel,
        grid=(n_batch, groups),
        in_specs=[
            pl.BlockSpec((n_pos, LANES), lambda b, g: (row_block0 + b, g)),
            pl.BlockSpec(cl.shape, lambda b, g: (0, 0)),
            pl.BlockSpec(sl.shape, lambda b, g: (0, 0)),
            pl.BlockSpec(cc.shape, lambda b, g: (0, 0)),
            pl.BlockSpec(sc.shape, lambda b, g: (0, 0)),
            pl.BlockSpec(memory_space=pl.ANY),
        ],
        out_specs=pl.BlockSpec((n_pos, LANES), lambda b, g: (row_block0 + b, g)),
        out_shape=jax.ShapeDtypeStruct(dst.shape, dst.dtype),
        input_output_aliases={5: 0},
        compiler_params=_cparams("parallel", "parallel"),
        name="fourier_ctx",
    )(f, cl, sl, cc, sc, dst)


def _win_kernel(sink_ref, q_ref, *refs, n_qtiles, has_local, n_heads, group_size):
    if has_local:
        kp_ref, kc_ref, kn_ref, vp_ref, vc_ref, vn_ref, kx_ref, vx_ref, _, o_ref = refs
    else:
        kx_ref, vx_ref, _, o_ref = refs
    tq = q_ref.shape[0]
    lane = lax.broadcasted_iota(jnp.int32, (1, LANES), 1)
    half_mask = [lane < HEAD_DIM, lane >= HEAD_DIM]
    pieces = [(kx_ref[...], vx_ref[...], None)]
    if has_local:
        n = pl.program_id(1)
        qi = lax.broadcasted_iota(jnp.int32, (tq, BLOCK), 0)
        kj = lax.broadcasted_iota(jnp.int32, (tq, BLOCK), 1)
        valid_prev = (kj >= qi) & (n >= 1)
        valid_next = (kj <= qi - (tq - WINDOW)) & (n <= n_qtiles - 2)
        di = lax.broadcasted_iota(jnp.int32, (tq, tq), 0) - lax.broadcasted_iota(jnp.int32, (tq, tq), 1)
        valid_mid = (di <= WINDOW) & (di >= -WINDOW)
        pieces += [(kp_ref[...], vp_ref[...], valid_prev), (kc_ref[...], vc_ref[...], valid_mid),
                   (kn_ref[...], vn_ref[...], valid_next)]
    def scores(head):
        pair, half = divmod(head, 2)
        kv = head // group_size
        qp = q_ref[:, pair * LANES:(pair + 1) * LANES].astype(F32)
        src = qp if half == kv else pltpu.roll(qp, HEAD_DIM, 1)
        qe = jnp.where(half_mask[kv], src, 0.0).astype(BF16)
        out = []
        for k, _, valid in pieces:
            s = _dot_nt(qe, k)
            out.append(s if valid is None else jnp.where(valid, s, NEG))
        return out

    def attend(head, sc):
        half = head % 2
        kv = head // group_size
        sink = sink_ref[head] * LOG2E
        m = jnp.zeros((tq, 1), F32) + sink
        for s in sc:
            m = jnp.maximum(m, jnp.max(s, axis=-1, keepdims=True))
        den = jnp.exp2(sink - m)
        pv = jnp.zeros((tq, LANES), F32)
        for s, (_, v, _) in zip(sc, pieces):
            p = jnp.exp2(s - m)
            den = den + jnp.sum(p, axis=-1, keepdims=True)
            pv = pv + _dot(p.astype(BF16), v)
        pv = pv / den
        return pv if half == kv else pltpu.roll(pv, HEAD_DIM, 1)

    sc_next = scores(0)
    out_pair = None
    for head in range(n_heads):
        sc_cur = sc_next
        if head + 1 < n_heads:
            sc_next = scores(head + 1)
        pv = attend(head, sc_cur)
        if head % 2 == 0:
            out_pair = pv
        else:
            pair = head // 2
            o_ref[:, pair * LANES:(pair + 1) * LANES] = jnp.where(half_mask[1], pv, out_pair).astype(o_ref.dtype)


def _win_attention(z, sink, dst, *, n_batch, seq, n_ctx, q_cols, k_col, v_col, tq):
    nbk = seq // BLOCK
    nq = seq // tq
    per = tq // BLOCK
    ctx_blk0 = n_batch * seq // n_ctx
    n_heads = q_cols // HEAD_DIM
    group_size = n_heads // (LANES // HEAD_DIM)

    def edge(col, first):
        return pl.BlockSpec(
            (BLOCK, LANES), lambda b, n: (b * nbk + jnp.clip(n * per + first, 0, nbk - 1), col))

    def mid(col):
        return pl.BlockSpec((tq, LANES), lambda b, n: (b * nq + n, col))

    return pl.pallas_call(
        functools.partial(_win_kernel, n_qtiles=nq, has_local=True, n_heads=n_heads, group_size=group_size),
        grid=(n_batch, nq),
        in_specs=[
            pl.BlockSpec(memory_space=pltpu.SMEM),
            pl.BlockSpec((tq, q_cols), lambda b, n: (b * nq + n, 0)),
            edge(k_col, -1), mid(k_col), edge(k_col, per),
            edge(v_col, -1), mid(v_col), edge(v_col, per),
            pl.BlockSpec((n_ctx, LANES), lambda b, n: (ctx_blk0 + b, k_col)),
            pl.BlockSpec((n_ctx, LANES), lambda b, n: (ctx_blk0 + b, v_col)),
            pl.BlockSpec(memory_space=pl.ANY),
        ],
        out_specs=pl.BlockSpec((tq, q_cols), lambda b, n: (b * nq + n, 0)),
        out_shape=jax.ShapeDtypeStruct(dst.shape, dst.dtype),
        input_output_aliases={10: 0},
        compiler_params=_cparams("parallel", "parallel"),
        name="win_attn",
    )(sink, z, z, z, z, z, z, z, z, z, dst)


def _ctx_gqa_attention(z, sink, dst, *, n_batch, seq, n_ctx, q_cols, k_col, v_col):
    ctx_blk0 = n_batch * seq // n_ctx
    n_heads = q_cols // HEAD_DIM
    group_size = n_heads // (LANES // HEAD_DIM)
    return pl.pallas_call(
        functools.partial(_win_kernel, n_qtiles=0, has_local=False, n_heads=n_heads, group_size=group_size),
        grid=(n_batch,),
        in_specs=[
            pl.BlockSpec(memory_space=pltpu.SMEM),
            pl.BlockSpec((n_ctx, q_cols), lambda b: (ctx_blk0 + b, 0)),
            pl.BlockSpec((n_ctx, LANES), lambda b: (ctx_blk0 + b, k_col)),
            pl.BlockSpec((n_ctx, LANES), lambda b: (ctx_blk0 + b, v_col)),
            pl.BlockSpec(memory_space=pl.ANY),
        ],
        out_specs=pl.BlockSpec((n_ctx, q_cols), lambda b: (ctx_blk0 + b, 0)),
        out_shape=jax.ShapeDtypeStruct(dst.shape, dst.dtype),
        input_output_aliases={4: 0},
        compiler_params=_cparams("parallel"),
        name="ctx_gqa",
    )(sink, z, z, z, dst)


LOG2E = math.log2(math.e)
DIFF_SUB_ROWS = 128
DIFF_KEY_CHUNK = 1024


def _diff_kernel(lam_ref, g_ref, q_ref, *refs, lam_init, has_x):
    if has_x:
        kx_ref, vtx_ref, kc_ref, vtc_ref, _, o_ref, s_scr = refs
        nx = kx_ref.shape[0]
    else:
        kc_ref, vtc_ref, _, o_ref, s_scr = refs
        nx = 0
    nc = kc_ref.shape[0]
    mc = s_scr.shape[2]
    sub = mc // 2
    n_sub = q_ref.shape[0] // sub
    lv = lam_ref[...]
    lam = (jnp.exp(jnp.sum(lv[0:1] * lv[1:2], axis=-1, keepdims=True))
           - jnp.exp(jnp.sum(lv[2:3] * lv[3:4], axis=-1, keepdims=True)) + lam_init)
    lane = lax.broadcasted_iota(jnp.int32, (1, LANES), 1)
    segs = []
    if has_x:
        kw = min(DIFF_KEY_CHUNK, nx)
        segs += [(kx_ref, vtx_ref, r0, kw, r0) for r0 in range(0, nx, kw)]
    segs.append((kc_ref, vtc_ref, 0, nc, nx))

    def scores(u):
        q = q_ref[u * sub:(u + 1) * sub, :]
        zero = jnp.zeros_like(q)
        qs = jnp.concatenate([jnp.where(lane < HEAD_DIM, q, zero), jnp.where(lane >= HEAD_DIM, q, zero)], axis=0)
        m_run = jnp.full((8, mc), -jnp.inf, F32)
        for k_ref, _, r0, w, s0 in segs:
            st = _dot_nt(k_ref[r0:r0 + w, :], qs)
            s_scr[u % 2, s0:s0 + w, :] = st
            m_run = jnp.maximum(m_run, jnp.max(st.reshape(w // 8, 8, mc), axis=0))
        return jnp.broadcast_to(jnp.max(m_run, axis=0, keepdims=True), (8, mc))

    def attend(u, m8):
        l_run = jnp.zeros((8, mc), F32)
        acc = jnp.zeros((LANES, mc), F32)
        for _, vt_ref, r0, w, s0 in segs:
            p = jnp.exp2(s_scr[u % 2, s0:s0 + w, :].reshape(w // 8, 8, mc) - m8[None])
            l_run = l_run + jnp.sum(p, axis=0)
            acc = acc + _dot(vt_ref[:, r0:r0 + w], p.reshape(w, mc).astype(BF16))
        ot = acc / jnp.sum(l_run, axis=0, keepdims=True)
        o = (ot[:, :sub] - lam * ot[:, sub:]).T
        o = _rms(o) * g_ref[...] * (1.0 - lam_init)
        o_ref[u * sub:(u + 1) * sub, :] = o.astype(o_ref.dtype)

    m_next = scores(0)
    for u in range(n_sub):
        m_cur = m_next
        if u + 1 < n_sub:
            m_next = scores(u + 1)
        attend(u, m_cur)


def _diff_attention(z, vt, lam_vec, subln_g, lam_init, dst, *, n_batch, seq, n_ctx, n_heads, tq):
    ctx_blk0 = n_batch * seq // n_ctx
    nq = seq // tq
    return pl.pallas_call(
        functools.partial(_diff_kernel, lam_init=lam_init, has_x=True),
        grid=(n_batch, n_heads, nq),
        in_specs=[
            pl.BlockSpec(lam_vec.shape, lambda b, h, i: (0, 0)),
            pl.BlockSpec((1, LANES), lambda b, h, i: (0, 0)),
            pl.BlockSpec((tq, LANES), lambda b, h, i: (b * nq + i, h)),
            pl.BlockSpec((seq, LANES), lambda b, h, i: (b, n_heads + h)),
            pl.BlockSpec((LANES, seq), lambda b, h, i: (h, b)),
            pl.BlockSpec((n_ctx, LANES), lambda b, h, i: (ctx_blk0 + b, n_heads + h)),
            pl.BlockSpec((LANES, n_ctx), lambda b, h, i: (h, ctx_blk0 + b)),
            pl.BlockSpec(memory_space=pl.ANY),
        ],
        out_specs=pl.BlockSpec((tq, LANES), lambda b, h, i: (b * nq + i, h)),
        out_shape=jax.ShapeDtypeStruct(dst.shape, dst.dtype),
        input_output_aliases={7: 0},
        scratch_shapes=[pltpu.VMEM((2, seq + n_ctx, 2 * min(tq, DIFF_SUB_ROWS)), F32)],
        compiler_params=_cparams("parallel", "parallel", "parallel"),
        name="diff_attn",
    )(lam_vec, subln_g, z, z, vt, z, vt, dst)


def _diff_attention_ctx(z, vt, lam_vec, subln_g, lam_init, dst, *, n_batch, seq, n_ctx, n_heads):
    ctx_blk0 = n_batch * seq // n_ctx
    return pl.pallas_call(
        functools.partial(_diff_kernel, lam_init=lam_init, has_x=False),
        grid=(n_batch, n_heads),
        in_specs=[
            pl.BlockSpec(lam_vec.shape, lambda b, h: (0, 0)),
            pl.BlockSpec((1, LANES), lambda b, h: (0, 0)),
            pl.BlockSpec((n_ctx, LANES), lambda b, h: (ctx_blk0 + b, h)),
            pl.BlockSpec((n_ctx, LANES), lambda b, h: (ctx_blk0 + b, n_heads + h)),
            pl.BlockSpec((LANES, n_ctx), lambda b, h: (h, ctx_blk0 + b)),
            pl.BlockSpec(memory_space=pl.ANY),
        ],
        out_specs=pl.BlockSpec((n_ctx, LANES), lambda b, h: (ctx_blk0 + b, h)),
        out_shape=jax.ShapeDtypeStruct(dst.shape, dst.dtype),
        input_output_aliases={5: 0},
        scratch_shapes=[pltpu.VMEM((2, n_ctx, 2 * min(n_ctx, DIFF_SUB_ROWS)), F32)],
        compiler_params=_cparams("parallel", "parallel"),
        name="diff_attn_ctx",
    )(lam_vec, subln_g, z, z, vt, dst)


def _outproj_kernel(a0_ref, a1_ref, w_ref, x_ref, g_ref, gate_ref, o_ref):
    half = a0_ref.shape[1]
    y = _dot(a0_ref[...], w_ref[:half, :]) + _dot(a1_ref[...], w_ref[half:, :])
    o_ref[...] = x_ref[...] + gate_ref[...] * (_rms(y) * g_ref[...])


def _outproj(a0, a0_col, a1, a1_col, w, xt, normg, mod, k_gate, *, tm, rows, seq, n_batch):
    d = xt.shape[1]
    half = d // 2

    def mrow(i):
        return jnp.minimum(i * tm // seq, n_batch)

    return pl.pallas_call(
        _outproj_kernel,
        grid=(rows // tm,),
        in_specs=[
            pl.BlockSpec((tm, half), lambda i: (i, a0_col)),
            pl.BlockSpec((tm, half), lambda i: (i, a1_col)),
            pl.BlockSpec((d, d), lambda i: (0, 0)),
            pl.BlockSpec((tm, d), lambda i: (i, 0)),
            pl.BlockSpec((None, 1, d), lambda i: (1, 0, 0)),
            pl.BlockSpec((None, None, 1, d), lambda i: (mrow(i), k_gate, 0, 0)),
        ],
        out_specs=pl.BlockSpec((tm, d), lambda i: (i, 0)),
        out_shape=jax.ShapeDtypeStruct((rows, d), F32),
        compiler_params=_cparams("parallel"),
        name="outproj",
    )(a0, a1, w, xt, normg, mod)


def _router_kernel(x_ref, g_ref, sh_ref, sc_ref, rw_ref, o_ref, sel_ref):
    h = _modulate(x_ref[...], g_ref[...], sh_ref[...], sc_ref[...])
    logits = _dot_split(_split_bf16(h), _split_bf16(rw_ref[...]))
    lane = lax.broadcasted_iota(jnp.int32, logits.shape, 1)
    ninf = -jnp.inf
    logits = jnp.where(lane < N_EXPERTS, logits, ninf)
    m1 = jnp.max(logits, axis=-1, keepdims=True)
    i1 = jnp.min(jnp.where(logits == m1, lane, LANES), axis=-1, keepdims=True)
    sel1 = lane == i1
    rest = jnp.where(sel1, ninf, logits)
    m2 = jnp.max(rest, axis=-1, keepdims=True)
    i2 = jnp.min(jnp.where(rest == m2, lane, LANES), axis=-1, keepdims=True)
    sel2 = lane == i2
    e2 = jnp.exp(m2 - m1)
    den = 1.0 + e2
    o_ref[...] = jnp.where(sel1, 1.0 / den, 0.0) + jnp.where(sel2, e2 / den, 0.0)
    sel_ref[...] = jnp.where(sel1 | sel2, 1.0, 0.0)


def _router(xt, normg, mod, router_w, *, tm, rows, seq, n_batch):
    d = xt.shape[1]
    rw = jnp.zeros((d, LANES), F32).at[:, :N_EXPERTS].set(router_w)

    def mrow(i):
        return jnp.minimum(i * tm // seq, n_batch)

    return pl.pallas_call(
        _router_kernel,
        grid=(rows // tm,),
        in_specs=[
            pl.BlockSpec((tm, d), lambda i: (i, 0)),
            pl.BlockSpec((None, 1, d), lambda i: (2, 0, 0)),
            pl.BlockSpec((None, None, 1, d), lambda i: (mrow(i), 3, 0, 0)),
            pl.BlockSpec((None, None, 1, d), lambda i: (mrow(i), 4, 0, 0)),
            pl.BlockSpec((d, LANES), lambda i: (0, 0)),
        ],
        out_specs=[pl.BlockSpec((tm, LANES), lambda i: (i, 0))] * 2,
        out_shape=[jax.ShapeDtypeStruct((rows, LANES), F32)] * 2,
        compiler_params=_cparams("parallel"),
        name="router",
    )(xt, normg, mod, mod, rw)


SWIGLU_CHUNK = 256


def _swiglu(h, wg_ref, wu_ref, wo_ref, u_off):
    width = wo_ref.shape[0]
    acc = None
    for c0 in range(0, width, SWIGLU_CHUNK):
        gp = _dot(h, wg_ref[:, c0:c0 + SWIGLU_CHUNK])
        up = _dot(h, wu_ref[:, u_off + c0:u_off + c0 + SWIGLU_CHUNK])
        a = gp * (1.0 / (1.0 + jnp.exp(-gp))) * up
        part = _dot(a.astype(BF16), wo_ref[c0:c0 + SWIGLU_CHUNK, :])
        acc = part if acc is None else acc + part
    return acc


def _ffn_kernel(x_ref, g2_ref, sh_ref, sc_ref, win_ref, wout_ref, g3_ref, gate_ref, o_ref):
    x = x_ref[...]
    h = _modulate(x, g2_ref[...], sh_ref[...], sc_ref[...]).astype(BF16)
    y = _swiglu(h, win_ref, win_ref, wout_ref, wout_ref.shape[0])
    o_ref[...] = x + gate_ref[...] * (_rms(y) * g3_ref[...])


def _ffn(xt, normg, mod, w_in, w_out, *, tm, rows, seq, n_batch):
    d = xt.shape[1]

    def mrow(i):
        return jnp.minimum(i * tm // seq, n_batch)

    in_specs = [
        pl.BlockSpec((tm, d), lambda i: (i, 0)),
        pl.BlockSpec((None, 1, d), lambda i: (2, 0, 0)),
        pl.BlockSpec((None, None, 1, d), lambda i: (mrow(i), 3, 0, 0)),
        pl.BlockSpec((None, None, 1, d), lambda i: (mrow(i), 4, 0, 0)),
        pl.BlockSpec(w_in.shape, lambda i: (0, 0)),
        pl.BlockSpec(w_out.shape, lambda i: (0, 0)),
        pl.BlockSpec((None, 1, d), lambda i: (3, 0, 0)),
        pl.BlockSpec((None, None, 1, d), lambda i: (mrow(i), 5, 0, 0)),
    ]
    return pl.pallas_call(
        _ffn_kernel,
        grid=(rows // tm,),
        in_specs=in_specs,
        out_specs=pl.BlockSpec((tm, d), lambda i: (i, 0)),
        out_shape=jax.ShapeDtypeStruct((rows, d), F32),
        compiler_params=_cparams("parallel"),
        name="ffn",
    )(xt, normg, mod, mod, w_in, w_out, normg, mod)


MOE_DISPATCH_ROWS = 256
MOE_COMBINE_ROWS = 512
MOE_SEG_ALIGN = 16
MOE_TILE = 512


def _moe_keys(sel):
    nb = sel.shape[0]
    ti = lax.broadcasted_iota(jnp.int32, (nb, nb), 0)
    tj = lax.broadcasted_iota(jnp.int32, (nb, nb), 1)
    lower = jnp.where(tj < ti, 1.0, 0.0).astype(BF16)
    rank = _dot(lower, sel.astype(BF16))
    return jnp.where(sel > 0.0, rank, -1.0)


def _moe_dispatch_kernel(nchunk_ref, off_ref, x_ref, g2_ref, sh_ref, sc_ref, sel_ref, _, hs_hbm,
                         h_scr, keyt_scr, stage, sems, *, n_exp):
    b = pl.program_id(0)
    rows = MOE_DISPATCH_ROWS
    h_scr[...] = _modulate(x_ref[...], g2_ref[...], sh_ref[...], sc_ref[...]).astype(BF16)
    keyt_scr[...] = _moe_keys(sel_ref[...]).T

    def copy(slot, row0):
        return pltpu.make_async_copy(stage.at[slot], hs_hbm.at[pl.ds(row0, rows), :], sems.at[slot])

    issued = jnp.int32(0)
    for e in range(n_exp):
        key_row = keyt_scr[e:e + 1, :]
        base = off_ref[b * n_exp + e]

        def chunk(c, k):
            slot = k % 2

            @pl.when(k >= 2)
            def _():
                copy(slot, 0).wait()

            r = (c * rows + lax.broadcasted_iota(jnp.int32, (rows, 1), 0)).astype(F32)
            onehot = jnp.where(key_row == r, 1.0, 0.0).astype(BF16)
            stage[slot] = _dot(onehot, h_scr[...]).astype(BF16)
            copy(slot, pl.multiple_of(base + c * rows, MOE_SEG_ALIGN)).start()
            return k + 1

        issued = lax.fori_loop(0, nchunk_ref[b * n_exp + e], chunk, issued)

    @pl.when(issued >= 1)
    def _():
        copy((issued - 1) % 2, 0).wait()

    @pl.when(issued >= 2)
    def _():
        copy(issued % 2, 0).wait()


def _moe_experts_kernel(tile_ref, exp_ref, valid_ref, hs_ref, wg_ref, wu_ref, wo_ref, y_ref, acc, *, n_f):
    t = pl.program_id(0)
    f = pl.program_id(1)

    @pl.when(valid_ref[t] == 1)
    def _():
        part = _swiglu(hs_ref[...], wg_ref, wu_ref, wo_ref, 0)

        @pl.when(f == 0)
        def _():
            acc[...] = part

        @pl.when(f > 0)
        def _():
            acc[...] += part

        @pl.when(f == n_f - 1)
        def _():
            y_ref[...] = acc[...].astype(y_ref.dtype)


def _moe_combine_kernel(npiece_ref, off_ref, x_ref, g3_ref, gate_ref, comb_ref, sel_ref, y_hbm, o_ref,
                        key_scr, stage, sems, *, n_exp):
    b = pl.program_id(0)
    rows = MOE_COMBINE_ROWS
    nb = x_ref.shape[0]
    key_scr[...] = _moe_keys(sel_ref[...])
    o_ref[...] = jnp.zeros_like(o_ref)
    lane = lax.broadcasted_iota(jnp.int32, (nb, LANES), 1)

    def copy(slot, row0):
        return pltpu.make_async_copy(y_hbm.at[pl.ds(row0, rows), :], stage.at[slot], sems.at[slot])

    def row0(e, c):
        return pl.multiple_of(off_ref[b * n_exp + e] + c * rows, MOE_SEG_ALIGN)

    copy(0, row0(0, 0)).start()
    done = jnp.int32(0)
    for e in range(n_exp):
        pick = lane == e
        key_col = jnp.sum(jnp.where(pick, key_scr[...], 0.0), axis=-1, keepdims=True)
        gate_col = jnp.sum(jnp.where(pick, comb_ref[...], 0.0), axis=-1, keepdims=True)
        n_pieces = npiece_ref[b * n_exp + e]

        def piece(c, k, e=e, n_pieces=n_pieces, key_col=key_col, gate_col=gate_col):
            slot = k % 2
            copy(slot, 0).wait()

            @pl.when(c + 1 < n_pieces)
            def _():
                copy(1 - slot, row0(e, c + 1)).start()

            if e + 1 < n_exp:
                @pl.when(c + 1 == n_pieces)
                def _():
                    copy(1 - slot, row0(e + 1, 0)).start()

            r = (c * rows + lax.broadcasted_iota(jnp.int32, (1, rows), 1)).astype(F32)
            onehot = jnp.where(key_col == r, 1.0, 0.0).astype(BF16)
            o_ref[...] += gate_col * _dot(onehot, stage[slot])
            return k + 1

        done = lax.fori_loop(0, n_pieces, piece, done)

    o_ref[...] = x_ref[...] + gate_ref[...] * (_rms(o_ref[...]) * g3_ref[...])


def _moe(xt, normg, mod, w_in, w_out, j, comb, sel, *, nb, tf, rows, seq, n_batch):
    d = xt.shape[1]
    n_exp, fdim = w_out.shape[1], w_out.shape[2]
    n_f = fdim // tf
    n_blk = rows // nb
    i32 = jnp.int32

    counts = jnp.sum(sel.reshape(n_blk, nb, LANES)[:, :, :n_exp], axis=1).astype(i32)
    seg = -(-counts // MOE_SEG_ALIGN) * MOE_SEG_ALIGN
    region = -(-(jnp.sum(seg, axis=0) + MOE_DISPATCH_ROWS) // MOE_TILE) * MOE_TILE
    region_end = jnp.cumsum(region)
    region_start = region_end - region
    off = (region_start[None, :] + jnp.cumsum(seg, axis=0) - seg).reshape(n_blk * n_exp).astype(i32)
    nchunk = (-(-counts // MOE_DISPATCH_ROWS)).reshape(n_blk * n_exp).astype(i32)
    npiece = jnp.maximum(-(-counts // MOE_COMBINE_ROWS), 1).reshape(n_blk * n_exp).astype(i32)
    cap = 2 * rows + n_blk * n_exp * (MOE_SEG_ALIGN - 1) + n_exp * (MOE_DISPATCH_ROWS + MOE_TILE - 1)
    n_tiles = -(-cap // MOE_TILE)
    cap = n_tiles * MOE_TILE + MOE_COMBINE_ROWS
    tiles = jnp.arange(n_tiles, dtype=i32)
    n_used = region_end[-1] // MOE_TILE
    tile_map = jnp.minimum(tiles, n_used - 1).astype(i32)
    tile_exp = jnp.minimum(jnp.searchsorted(region_end, tile_map * MOE_TILE, side="right"), n_exp - 1).astype(i32)
    tile_valid = (tiles < n_used).astype(i32)

    def mrow(i):
        return jnp.minimum(i * nb // seq, n_batch)

    hs = pl.pallas_call(
        functools.partial(_moe_dispatch_kernel, n_exp=n_exp),
        grid_spec=pltpu.PrefetchScalarGridSpec(
            num_scalar_prefetch=2,
            grid=(n_blk,),
            in_specs=[
                pl.BlockSpec((nb, d), lambda i, *_: (i, 0)),
                pl.BlockSpec((None, 1, d), lambda i, *_: (2, 0, 0)),
                pl.BlockSpec((None, None, 1, d), lambda i, *_: (mrow(i), 3, 0, 0)),
                pl.BlockSpec((None, None, 1, d), lambda i, *_: (mrow(i), 4, 0, 0)),
                pl.BlockSpec((nb, LANES), lambda i, *_: (i, 0)),
                pl.BlockSpec(memory_space=pl.ANY),
            ],
            out_specs=pl.BlockSpec(memory_space=pl.ANY),
            scratch_shapes=[
                pltpu.VMEM((nb, d), BF16),
                pltpu.VMEM((LANES, nb), F32),
                pltpu.VMEM((2, MOE_DISPATCH_ROWS, d), BF16),
                pltpu.SemaphoreType.DMA((2,)),
            ],
        ),
        out_shape=jax.ShapeDtypeStruct((cap, d), BF16),
        input_output_aliases={7: 0},
        compiler_params=_cparams("arbitrary"),
        name="moe_dispatch",
    )(nchunk, off, xt, normg, mod, mod, sel, jnp.zeros((cap, d), BF16))

    ys = pl.pallas_call(
        functools.partial(_moe_experts_kernel, n_f=n_f),
        grid_spec=pltpu.PrefetchScalarGridSpec(
            num_scalar_prefetch=3,
            grid=(n_tiles, n_f),
            in_specs=[
                pl.BlockSpec((MOE_TILE, d), lambda t, f, tm, te, tv: (tm[t], 0)),
                pl.BlockSpec((None, None, d, tf), lambda t, f, tm, te, tv: (j, te[t], 0, f)),
                pl.BlockSpec((None, None, d, tf), lambda t, f, tm, te, tv: (j, te[t], 0, n_f + f)),
                pl.BlockSpec((None, None, tf, d), lambda t, f, tm, te, tv: (j, te[t], f, 0)),
            ],
            out_specs=pl.BlockSpec((MOE_TILE, d), lambda t, f, tm, te, tv: (tm[t], 0)),
            scratch_shapes=[pltpu.VMEM((MOE_TILE, d), F32)],
        ),
        out_shape=jax.ShapeDtypeStruct((cap, d), BF16),
        input_output_aliases={3: 0},
        compiler_params=_cparams("arbitrary", "arbitrary"),
        name="moe_experts",
    )(tile_map, tile_exp, tile_valid, hs, w_in, w_in, w_out)

    return pl.pallas_call(
        functools.partial(_moe_combine_kernel, n_exp=n_exp),
        grid_spec=pltpu.PrefetchScalarGridSpec(
            num_scalar_prefetch=2,
            grid=(n_blk,),
            in_specs=[
                pl.BlockSpec((nb, d), lambda i, *_: (i, 0)),
                pl.BlockSpec((None, 1, d), lambda i, *_: (3, 0, 0)),
                pl.BlockSpec((None, None, 1, d), lambda i, *_: (mrow(i), 5, 0, 0)),
                pl.BlockSpec((nb, LANES), lambda i, *_: (i, 0)),
                pl.BlockSpec((nb, LANES), lambda i, *_: (i, 0)),
                pl.BlockSpec(memory_space=pl.ANY),
            ],
            out_specs=pl.BlockSpec((nb, d), lambda i, *_: (i, 0)),
            scratch_shapes=[
                pltpu.VMEM((nb, LANES), F32),
                pltpu.VMEM((2, MOE_COMBINE_ROWS, d), BF16),
                pltpu.SemaphoreType.DMA((2,)),
            ],
        ),
        out_shape=jax.ShapeDtypeStruct((rows, d), F32),
        compiler_params=_cparams("parallel"),
        name="moe_combine",
    )(npiece, off, xt, normg, mod, comb, sel, ys)


def _lambda_init(layer):
    return 0.8 - 0.6 * math.exp(-0.3 * layer)


def kernel(x, c, ctx, c_ctx, ada_w, ada_b, norm_g, mix_in_w, mix_out_w, win_sink, diff_qkv_w, diff_out_w,
           diff_lambda, diff_subln_g, ffn_in_w, ffn_out_w, router_w, expert_in_w, expert_out_w):
    n_batch, seq, d = x.shape
    n_ctx = ctx.shape[1]
    depth = ada_w.shape[0]
    n_lat = n_batch * seq
    n_all = n_lat + n_batch * n_ctx
    fdim = mix_in_w.shape[2] - (d // 2 + 2 * LANES)
    q_cols = d // 2
    n_diff_heads = d // LANES

    tm_proj = _pick_tile(1024, seq, n_batch * n_ctx)
    tm_out = _pick_tile(2048, seq, n_batch * n_ctx)
    tm_ffn = _pick_tile(512, seq, n_batch * n_ctx)
    nb_moe = _pick_tile(1024, seq, n_batch * n_ctx)
    tq_diff = _pick_tile(4096, seq)
    tq_win = _pick_tile(512, seq)
    common = dict(seq=seq, n_batch=n_batch)

    xt = jnp.concatenate([x.reshape(n_lat, d), ctx.reshape(n_batch * n_ctx, d)], axis=0)
    n_mod = -(-(n_batch + 1) // 8) * 8
    cv = jnp.zeros((n_mod, d), F32).at[:n_batch].set(c).at[n_batch].set(c_ctx)
    mods = _modvec(cv, ada_w, ada_b).reshape(depth, n_mod, 6, 1, d)
    rope = _rope_tables(seq, tm_proj)
    f_tables = _fourier_tables(seq)
    fc_tables = _dense_fourier_tables(n_ctx)

    n_f = fdim // LANES
    plan_even = ([(0, i * LANES, "plain") for i in range(n_f)]
                 + [(1, i * LANES, "rope_q_log2") for i in range(q_cols // LANES)]
                 + [(1, q_cols, "rope_k"), (1, q_cols + LANES, "plain")])
    plan_odd = ([(0, i * LANES, "rope_q_log2") for i in range(n_diff_heads)]
                + [(0, d + i * LANES, "rope_k") for i in range(n_diff_heads)]
                + [(1, i * LANES, "plain_t") for i in range(n_diff_heads)])

    expert_in_bf16 = expert_in_w.astype(BF16)
    expert_out_bf16 = expert_out_w.astype(BF16)

    for layer in range(depth):
        j = layer // 2
        need_ctx = layer < depth - 1
        rows = n_all if need_ctx else n_lat
        mod = mods[layer]
        ng = norm_g[layer].reshape(4, 1, d)
        if layer % 2 == 0:
            f, z = _proj(xt, ng, mod, 0, 1, mix_in_w[j].astype(BF16), rope, plan_even,
                         [(fdim, F32, False), (q_cols + 2 * LANES, BF16, False)], tm=tm_proj, n_lat=n_lat,
                         **common)
            k_col, v_col = q_cols // LANES, q_cols // LANES + 1
            att = dict(n_batch=n_batch, seq=seq, n_ctx=n_ctx, q_cols=q_cols, k_col=k_col, v_col=v_col)
            mix_f = _fourier(f, f_tables, jnp.zeros((rows, fdim), BF16), n_batch=n_batch, seq=seq)
            mix_a = _win_attention(z, win_sink[j], jnp.zeros((rows, q_cols), BF16), tq=tq_win, **att)
            if need_ctx:
                mix_f = _dense_fourier(f, fc_tables, mix_f, n_batch=n_batch, n_pos=n_ctx,
                                       row_block0=n_lat // n_ctx)
                mix_a = _ctx_gqa_attention(z, win_sink[j], mix_a, **att)
            xt = _outproj(mix_f, 0, mix_a, 0, mix_out_w[j].astype(BF16), xt, ng, mod, 2, tm=tm_out, rows=rows,
                          **common)
            xt = _ffn(xt, ng, mod, ffn_in_w[j].astype(BF16), ffn_out_w[j].astype(BF16),
                      tm=tm_ffn, rows=rows, **common)
        else:
            lam_init = _lambda_init(layer)
            z, vt = _proj(xt, ng, mod, 0, 1, diff_qkv_w[j].astype(BF16), rope, plan_odd,
                          [(2 * d, BF16, False), (d, BF16, True)], tm=tm_proj, n_lat=n_lat, **common)
            subg = diff_subln_g[j].reshape(1, LANES)
            att = dict(n_batch=n_batch, seq=seq, n_ctx=n_ctx, n_heads=n_diff_heads)
            mix = _diff_attention(z, vt, diff_lambda[j], subg, lam_init, jnp.zeros((rows, d), BF16), tq=tq_diff,
                                  **att)
            if need_ctx:
                mix = _diff_attention_ctx(z, vt, diff_lambda[j], subg, lam_init, mix, **att)
            xt = _outproj(mix, 0, mix, 1, diff_out_w[j].astype(BF16), xt, ng, mod, 2, tm=tm_out, rows=rows,
                          **common)
            comb, sel = _router(xt, ng, mod, router_w[j], tm=tm_out, rows=rows, **common)
            xt = _moe(xt, ng, mod, expert_in_bf16, expert_out_bf16, j, comb, sel,
                      nb=nb_moe, tf=1792, rows=rows, **common)
    return xt[:n_lat].reshape(n_batch, seq, d)
```

```python
import functools
import math

import numpy as np
import jax
import jax.numpy as jnp
from jax import lax
from jax.experimental import pallas as pl
from jax.experimental.pallas import tpu as pltpu

EPS = 1e-6
NEG = -1e30
HEAD_DIM = 64
LANES = 128
GRID_W = 64
BLOCK = 128
WINDOW = 128
ROPE_THETA = 10000.0
N_EXPERTS = 8
F32 = jnp.float32
BF16 = jnp.bfloat16
HIGHEST = lax.Precision.HIGHEST
VMEM_LIMIT = 56 * 1024 * 1024


def _cparams(*sem):
    return pltpu.CompilerParams(dimension_semantics=sem, vmem_limit_bytes=VMEM_LIMIT)


def _dot(a, b):
    return jnp.dot(a, b, preferred_element_type=F32)


def _dot_nt(a, b):
    return lax.dot_general(a, b, (((1,), (1,)), ((), ())), preferred_element_type=F32)


def _dot_hi(a, b):
    return jnp.dot(a, b, precision=HIGHEST, preferred_element_type=F32)


def _rms(v):
    return v * lax.rsqrt(jnp.mean(v * v, axis=-1, keepdims=True) + EPS)


def _modulate(x, g, sh, sc):
    return _rms(x) * g * (1.0 + sc) + sh


def _pick_tile(pref, *dims):
    t = pref
    while any(d % t for d in dims):
        t //= 2
    return t


def _modvec_kernel(c_ref, w_ref, b_ref, o_ref):
    cv = c_ref[...]
    s = cv * (1.0 / (1.0 + jnp.exp(-cv)))
    o_ref[...] = _dot(s.astype(BF16), w_ref[...].astype(BF16)) + b_ref[...]


def _modvec(cv, ada_w, ada_b):
    depth, d, n = ada_w.shape
    r = cv.shape[0]
    tn = _pick_tile(1536, n)
    return pl.pallas_call(
        _modvec_kernel,
        grid=(depth, n // tn),
        in_specs=[
            pl.BlockSpec((r, d), lambda l, j: (0, 0)),
            pl.BlockSpec((None, d, tn), lambda l, j: (l, 0, j)),
            pl.BlockSpec((None, 1, tn), lambda l, j: (l, 0, j)),
        ],
        out_specs=pl.BlockSpec((None, r, tn), lambda l, j: (l, 0, j)),
        out_shape=jax.ShapeDtypeStruct((depth, r, n), F32),
        compiler_params=_cparams("parallel", "parallel"),
        name="modvec",
    )(cv, ada_w, ada_b.reshape(depth, 1, n))


def _proj_kernel(x_ref, g_ref, sh_ref, sc_ref, w_ref, cos_ref, sa_ref, sb_ref, *o_refs, plan, group):
    h = _modulate(x_ref[...], g_ref[...], sh_ref[...], sc_ref[...]).astype(BF16)
    n = w_ref.shape[1]
    for g0 in range(0, n, group):
        acc = _dot(h, w_ref[:, g0:g0 + group])
        for c0 in range(0, group, LANES):
            oi, oc, mode = plan[(g0 + c0) // LANES]
            v = acc[:, c0:c0 + LANES]
            if mode.startswith("rope"):
                v = (v * cos_ref[...] + pltpu.roll(v, LANES - 16, 1) * sa_ref[...]
                     + pltpu.roll(v, 16, 1) * sb_ref[...])
                if mode == "rope_q_log2":
                    v = v * (HEAD_DIM ** -0.5 * LOG2E)
            if mode == "plain_t":
                o_refs[oi][oc:oc + LANES, :] = v.T.astype(o_refs[oi].dtype)
            else:
                o_refs[oi][:, oc:oc + LANES] = v.astype(o_refs[oi].dtype)


def _proj(xt, normg, mod, k_sh, k_sc, w, rope, plan, outs, *, tm, n_lat, seq, n_batch):
    t, d = xt.shape
    n = w.shape[1]
    group = _pick_tile(512, n)
    nx = n_lat // tm
    per = seq // tm

    def mrow(i):
        return jnp.minimum(i * tm // seq, n_batch)

    def rrow(i):
        return jnp.where(i < nx, i % per, per)

    in_specs = [
        pl.BlockSpec((tm, d), lambda i: (i, 0)),
        pl.BlockSpec((None, 1, d), lambda i: (0, 0, 0)),
        pl.BlockSpec((None, None, 1, d), lambda i: (mrow(i), k_sh, 0, 0)),
        pl.BlockSpec((None, None, 1, d), lambda i: (mrow(i), k_sc, 0, 0)),
        pl.BlockSpec((d, n), lambda i: (0, 0)),
        pl.BlockSpec((tm, LANES), lambda i: (rrow(i), 0)),
        pl.BlockSpec((tm, LANES), lambda i: (rrow(i), 0)),
        pl.BlockSpec((tm, LANES), lambda i: (rrow(i), 0)),
    ]
    out_specs = [pl.BlockSpec((wd, tm), lambda i: (0, i)) if tr else pl.BlockSpec((tm, wd), lambda i: (i, 0))
                 for wd, _, tr in outs]
    out_shape = [jax.ShapeDtypeStruct((wd, t) if tr else (t, wd), dt) for wd, dt, tr in outs]
    return pl.pallas_call(
        functools.partial(_proj_kernel, plan=plan, group=group),
        grid=(t // tm,),
        in_specs=in_specs,
        out_specs=out_specs,
        out_shape=out_shape,
        compiler_params=_cparams("parallel"),
        name="proj",
    )(xt, normg, mod, mod, w, *rope)


def _rope_tables(seq, tm):
    rows_count = seq // GRID_W
    rows = jnp.repeat(jnp.arange(rows_count), GRID_W).astype(F32)
    cols = jnp.tile(jnp.arange(GRID_W), rows_count).astype(F32)
    axis_dim = HEAD_DIM // 2
    inv = ROPE_THETA ** (-jnp.arange(0, axis_dim, 2, dtype=F32) / axis_dim)
    ar = rows[:, None] * inv
    ac = cols[:, None] * inv
    cr, sr, cc, sc = jnp.cos(ar), jnp.sin(ar), jnp.cos(ac), jnp.sin(ac)
    z = jnp.zeros_like(sr)
    reps = LANES // HEAD_DIM
    cos = jnp.tile(jnp.concatenate([cr, cr, cc, cc], axis=1), (1, reps))
    sa = jnp.tile(jnp.concatenate([-sr, z, -sc, z], axis=1), (1, reps))
    sb = jnp.tile(jnp.concatenate([z, sr, z, sc], axis=1), (1, reps))
    ident = jnp.ones((tm, LANES), F32)
    zero = jnp.zeros((tm, LANES), F32)
    return (jnp.concatenate([cos, ident]), jnp.concatenate([sa, zero]), jnp.concatenate([sb, zero]))


FOURIER_LANES = 2 * LANES


def _split_bf16(t):
    hi = t.astype(BF16)
    return hi, (t - hi.astype(F32)).astype(BF16)


def _dot_split(a, b):
    return _dot(a[0], b[0]) + _dot(a[0], b[1]) + _dot(a[1], b[0])


def _fourier_tables(seq):
    n2 = GRID_W
    n1 = seq // n2
    norm = 1.0 / math.sqrt(seq * LANES)
    a = np.arange(n1)
    k1 = np.arange(n1)
    b = np.arange(n2)
    ang = (b[:, None, None] * k1[None, :, None] + (seq // n1) * k1[None, :, None] * a[None, None, :]) % seq
    th = 2.0 * np.pi * ang / seq
    m1 = np.concatenate([np.cos(th), -np.sin(th)], axis=1).astype(np.float32)
    ph = 2.0 * np.pi * ((b[:, None] * b[None, :]) % n2) / n2
    c2, s2 = np.cos(ph), np.sin(ph)
    g2 = np.block([[c2, s2], [-s2, c2]]).astype(np.float32)
    ch = np.arange(LANES)
    pc = 2.0 * np.pi * ((ch[:, None] * ch[None, :]) % LANES) / LANES
    cc = (np.cos(pc) * norm).astype(np.float32)
    sc = (np.sin(pc) * norm).astype(np.float32)
    cs = np.concatenate([cc, sc], axis=0)
    out = []
    for t in (m1, g2, cs):
        out.extend(_split_bf16(jnp.asarray(t)))
    return tuple(out)


FOURIER_ROW_CHUNK = 512


def _fourier_kernel(*refs, n1, n2, n_grp):
    u_refs = refs[:n_grp]
    m1h_ref, m1l_ref, g2h_ref, g2l_ref, csh_ref, csl_ref, _, o_ref, b_scr, xr_scr, xi_scr = refs[n_grp:]
    seq = n1 * n2
    for b in range(n2):
        xs = jnp.concatenate([u[pl.ds(b, n1, stride=n2), :] for u in u_refs], axis=1)
        z = _dot_split((m1h_ref[b], m1l_ref[b]), _split_bf16(xs))
        for g in range(n_grp):
            b_scr[g, 2 * n1 * b:2 * n1 * (b + 1), :] = z[:, g * LANES:(g + 1) * LANES]
    g2 = (g2h_ref[...], g2l_ref[...])
    for k1 in range(n1):
        bk = jnp.concatenate(
            [jnp.concatenate([b_scr.at[g][pl.ds(k1, n2, stride=2 * n1), :],
                              b_scr.at[g][pl.ds(n1 + k1, n2, stride=2 * n1), :]], axis=0)
             for g in range(n_grp)], axis=1)
        xk = _dot_split(g2, _split_bf16(bk))
        for g in range(n_grp):
            xr_scr[g, n2 * k1:n2 * (k1 + 1), :] = xk[:n2, g * LANES:(g + 1) * LANES]
            xi_scr[g, n2 * k1:n2 * (k1 + 1), :] = xk[n2:, g * LANES:(g + 1) * LANES]
    cs = (csh_ref[...], csl_ref[...])
    rc = min(FOURIER_ROW_CHUNK, seq)
    for g in range(n_grp):
        for r0 in range(0, seq, rc):
            x = jnp.concatenate([xr_scr[g, r0:r0 + rc, :], xi_scr[g, r0:r0 + rc, :]], axis=1)
            xr_scr[g, r0:r0 + rc, :] = _dot_split(_split_bf16(x), cs)
    for g in range(n_grp):
        for k2 in range(n2):
            o_ref[n1 * k2:n1 * (k2 + 1), g * LANES:(g + 1) * LANES] = (
                xr_scr.at[g][pl.ds(k2, n1, stride=n2), :].astype(o_ref.dtype))


def _fourier(f, tables, dst, *, n_batch, seq):
    n2 = GRID_W
    n1 = seq // n2
    n_grp = FOURIER_LANES // LANES
    table_specs = [pl.BlockSpec(t.shape, (lambda b, g, nd=t.ndim: (0,) * nd)) for t in tables]
    u_specs = [pl.BlockSpec((seq, LANES), (lambda b, g, k=k: (b, g * n_grp + k))) for k in range(n_grp)]
    return pl.pallas_call(
        functools.partial(_fourier_kernel, n1=n1, n2=n2, n_grp=n_grp),
        grid=(n_batch, f.shape[1] // FOURIER_LANES),
        in_specs=u_specs + table_specs + [pl.BlockSpec(memory_space=pl.ANY)],
        out_specs=pl.BlockSpec((seq, FOURIER_LANES), lambda b, g: (b, g)),
        out_shape=jax.ShapeDtypeStruct(dst.shape, dst.dtype),
        input_output_aliases={n_grp + len(tables): 0},
        scratch_shapes=[pltpu.VMEM((n_grp, 2 * seq, LANES), F32), pltpu.VMEM((n_grp, seq, LANES), F32),
                        pltpu.VMEM((n_grp, seq, LANES), F32)],
        compiler_params=_cparams("parallel", "parallel"),
        name="fourier",
    )(*([f] * n_grp), *tables, dst)


def _dense_fourier_tables(n):
    norm = 1.0 / math.sqrt(n * LANES)
    p = np.arange(n)
    ph = 2.0 * np.pi * ((p[:, None] * p[None, :]) % n) / n
    ch = np.arange(LANES)
    pc = 2.0 * np.pi * ((ch[:, None] * ch[None, :]) % LANES) / LANES
    return (jnp.asarray(np.cos(ph).astype(np.float32)), jnp.asarray(np.sin(ph).astype(np.float32)),
            jnp.asarray((np.cos(pc) * norm).astype(np.float32)), jnp.asarray((np.sin(pc) * norm).astype(np.float32)))


def _dense_fourier_kernel(u_ref, cl_ref, sl_ref, cc_ref, sc_ref, _, o_ref):
    u = u_ref[...]
    y = _dot_hi(cl_ref[...], _dot_hi(u, cc_ref[...])) - _dot_hi(sl_ref[...], _dot_hi(u, sc_ref[...]))
    o_ref[...] = y.astype(o_ref.dtype)


def _dense_fourier(f, tables, dst, *, n_batch, n_pos, row_block0):
    cl, sl, cc, sc = tables
    groups = f.shape[1] // LANES
    return pl.pallas_call(
        _dense_fourier_kernel,
        grid=(n_batch, groups),
        in_specs=[
            pl.BlockSpec((n_pos, LANES), lambda b, g: (row_block0 + b, g)),
            pl.BlockSpec(cl.shape, lambda b, g: (0, 0)),
            pl.BlockSpec(sl.shape, lambda b, g: (0, 0)),
            pl.BlockSpec(cc.shape, lambda b, g: (0, 0)),
            pl.BlockSpec(sc.shape, lambda b, g: (0, 0)),
            pl.BlockSpec(memory_space=pl.ANY),
        ],
        out_specs=pl.BlockSpec((n_pos, LANES), lambda b, g: (row_block0 + b, g)),
        out_shape=jax.ShapeDtypeStruct(dst.shape, dst.dtype),
        input_output_aliases={5: 0},
        compiler_params=_cparams("parallel", "parallel"),
        name="fourier_ctx",
    )(f, cl, sl, cc, sc, dst)


def _win_kernel(sink_ref, q_ref, *refs, n_qtiles, has_local, n_heads, group_size):
    if has_local:
        kp_ref, kc_ref, kn_ref, vp_ref, vc_ref, vn_ref, kx_ref, vx_ref, _, o_ref = refs
    else:
        kx_ref, vx_ref, _, o_ref = refs
    tq = q_ref.shape[0]
    lane = lax.broadcasted_iota(jnp.int32, (1, LANES), 1)
    half_mask = [lane < HEAD_DIM, lane >= HEAD_DIM]
    pieces = [(kx_ref[...], vx_ref[...], None)]
    if has_local:
        n = pl.program_id(1)
        qi = lax.broadcasted_iota(jnp.int32, (tq, BLOCK), 0)
        kj = lax.broadcasted_iota(jnp.int32, (tq, BLOCK), 1)
        valid_prev = (kj >= qi) & (n >= 1)
        valid_next = (kj <= qi - (tq - WINDOW)) & (n <= n_qtiles - 2)
        di = lax.broadcasted_iota(jnp.int32, (tq, tq), 0) - lax.broadcasted_iota(jnp.int32, (tq, tq), 1)
        valid_mid = (di <= WINDOW) & (di >= -WINDOW)
        pieces += [(kp_ref[...], vp_ref[...], valid_prev), (kc_ref[...], vc_ref[...], valid_mid),
                   (kn_ref[...], vn_ref[...], valid_next)]
    def scores(head):
        pair, half = divmod(head, 2)
        kv = head // group_size
        qp = q_ref[:, pair * LANES:(pair + 1) * LANES].astype(F32)
        src = qp if half == kv else pltpu.roll(qp, HEAD_DIM, 1)
        qe = jnp.where(half_mask[kv], src, 0.0).astype(BF16)
        out = []
        for k, _, valid in pieces:
            s = _dot_nt(qe, k)
            out.append(s if valid is None else jnp.where(valid, s, NEG))
        return out

    def attend(head, sc):
        half = head % 2
        kv = head // group_size
        sink = sink_ref[head] * LOG2E
        m = jnp.zeros((tq, 1), F32) + sink
        for s in sc:
            m = jnp.maximum(m, jnp.max(s, axis=-1, keepdims=True))
        den = jnp.exp2(sink - m)
        pv = jnp.zeros((tq, LANES), F32)
        for s, (_, v, _) in zip(sc, pieces):
            p = jnp.exp2(s - m)
            den = den + jnp.sum(p, axis=-1, keepdims=True)
            pv = pv + _dot(p.astype(BF16), v)
        pv = pv / den
        return pv if half == kv else pltpu.roll(pv, HEAD_DIM, 1)

    sc_next = scores(0)
    out_pair = None
    for head in range(n_heads):
        sc_cur = sc_next
        if head + 1 < n_heads:
            sc_next = scores(head + 1)
        pv = attend(head, sc_cur)
        if head % 2 == 0:
            out_pair = pv
        else:
            pair = head // 2
            o_ref[:, pair * LANES:(pair + 1) * LANES] = jnp.where(half_mask[1], pv, out_pair).astype(o_ref.dtype)


def _win_attention(z, sink, dst, *, n_batch, seq, n_ctx, q_cols, k_col, v_col, tq):
    nbk = seq // BLOCK
    nq = seq // tq
    per = tq // BLOCK
    ctx_blk0 = n_batch * seq // n_ctx
    n_heads = q_cols // HEAD_DIM
    group_size = n_heads // (LANES // HEAD_DIM)

    def edge(col, first):
        return pl.BlockSpec(
            (BLOCK, LANES), lambda b, n: (b * nbk + jnp.clip(n * per + first, 0, nbk - 1), col))

    def mid(col):
        return pl.BlockSpec((tq, LANES), lambda b, n: (b * nq + n, col))

    return pl.pallas_call(
        functools.partial(_win_kernel, n_qtiles=nq, has_local=True, n_heads=n_heads, group_size=group_size),
        grid=(n_batch, nq),
        in_specs=[
            pl.BlockSpec(memory_space=pltpu.SMEM),
            pl.BlockSpec((tq, q_cols), lambda b, n: (b * nq + n, 0)),
            edge(k_col, -1), mid(k_col), edge(k_col, per),
            edge(v_col, -1), mid(v_col), edge(v_col, per),
            pl.BlockSpec((n_ctx, LANES), lambda b, n: (ctx_blk0 + b, k_col)),
            pl.BlockSpec((n_ctx, LANES), lambda b, n: (ctx_blk0 + b, v_col)),
            pl.BlockSpec(memory_space=pl.ANY),
        ],
        out_specs=pl.BlockSpec((tq, q_cols), lambda b, n: (b * nq + n, 0)),
        out_shape=jax.ShapeDtypeStruct(dst.shape, dst.dtype),
        input_output_aliases={10: 0},
        compiler_params=_cparams("parallel", "parallel"),
        name="win_attn",
    )(sink, z, z, z, z, z, z, z, z, z, dst)


def _ctx_gqa_attention(z, sink, dst, *, n_batch, seq, n_ctx, q_cols, k_col, v_col):
    ctx_blk0 = n_batch * seq // n_ctx
    n_heads = q_cols // HEAD_DIM
    group_size = n_heads // (LANES // HEAD_DIM)
    return pl.pallas_call(
        functools.partial(_win_kernel, n_qtiles=0, has_local=False, n_heads=n_heads, group_size=group_size),
        grid=(n_batch,),
        in_specs=[
            pl.BlockSpec(memory_space=pltpu.SMEM),
            pl.BlockSpec((n_ctx, q_cols), lambda b: (ctx_blk0 + b, 0)),
            pl.BlockSpec((n_ctx, LANES), lambda b: (ctx_blk0 + b, k_col)),
            pl.BlockSpec((n_ctx, LANES), lambda b: (ctx_blk0 + b, v_col)),
            pl.BlockSpec(memory_space=pl.ANY),
        ],
        out_specs=pl.BlockSpec((n_ctx, q_cols), lambda b: (ctx_blk0 + b, 0)),
        out_shape=jax.ShapeDtypeStruct(dst.shape, dst.dtype),
        input_output_aliases={4: 0},
        compiler_params=_cparams("parallel"),
        name="ctx_gqa",
    )(sink, z, z, z, dst)


LOG2E = math.log2(math.e)
DIFF_SUB_ROWS = 128
DIFF_KEY_CHUNK = 1024


def _diff_kernel(lam_ref, g_ref, q_ref, *refs, lam_init, has_x):
    if has_x:
        kx_ref, vtx_ref, kc_ref, vtc_ref, _, o_ref, s_scr = refs
        nx = kx_ref.shape[0]
    else:
        kc_ref, vtc_ref, _, o_ref, s_scr = refs
        nx = 0
    nc = kc_ref.shape[0]
    mc = s_scr.shape[2]
    sub = mc // 2
    n_sub = q_ref.shape[0] // sub
    lv = lam_ref[...]
    lam = (jnp.exp(jnp.sum(lv[0:1] * lv[1:2], axis=-1, keepdims=True))
           - jnp.exp(jnp.sum(lv[2:3] * lv[3:4], axis=-1, keepdims=True)) + lam_init)
    lane = lax.broadcasted_iota(jnp.int32, (1, LANES), 1)
    segs = []
    if has_x:
        kw = min(DIFF_KEY_CHUNK, nx)
        segs += [(kx_ref, vtx_ref, r0, kw, r0) for r0 in range(0, nx, kw)]
    segs.append((kc_ref, vtc_ref, 0, nc, nx))

    def scores(u):
        q = q_ref[u * sub:(u + 1) * sub, :]
        zero = jnp.zeros_like(q)
        qs = jnp.concatenate([jnp.where(lane < HEAD_DIM, q, zero), jnp.where(lane >= HEAD_DIM, q, zero)], axis=0)
        m_run = jnp.full((8, mc), -jnp.inf, F32)
        for k_ref, _, r0, w, s0 in segs:
            st = _dot_nt(k_ref[r0:r0 + w, :], qs)
            s_scr[u % 2, s0:s0 + w, :] = st
            m_run = jnp.maximum(m_run, jnp.max(st.reshape(w // 8, 8, mc), axis=0))
        return jnp.broadcast_to(jnp.max(m_run, axis=0, keepdims=True), (8, mc))

    def attend(u, m8):
        l_run = jnp.zeros((8, mc), F32)
        acc = jnp.zeros((LANES, mc), F32)
        for _, vt_ref, r0, w, s0 in segs:
            p = jnp.exp2(s_scr[u % 2, s0:s0 + w, :].reshape(w // 8, 8, mc) - m8[None])
            l_run = l_run + jnp.sum(p, axis=0)
            acc = acc + _dot(vt_ref[:, r0:r0 + w], p.reshape(w, mc).astype(BF16))
        ot = acc / jnp.sum(l_run, axis=0, keepdims=True)
        o = (ot[:, :sub] - lam * ot[:, sub:]).T
        o = _rms(o) * g_ref[...] * (1.0 - lam_init)
        o_ref[u * sub:(u + 1) * sub, :] = o.astype(o_ref.dtype)

    m_next = scores(0)
    for u in range(n_sub):
        m_cur = m_next
        if u + 1 < n_sub:
            m_next = scores(u + 1)
        attend(u, m_cur)


def _diff_attention(z, vt, lam_vec, subln_g, lam_init, dst, *, n_batch, seq, n_ctx, n_heads, tq):
    ctx_blk0 = n_batch * seq // n_ctx
    nq = seq // tq
    return pl.pallas_call(
        functools.partial(_diff_kernel, lam_init=lam_init, has_x=True),
        grid=(n_batch, n_heads, nq),
        in_specs=[
            pl.BlockSpec(lam_vec.shape, lambda b, h, i: (0, 0)),
            pl.BlockSpec((1, LANES), lambda b, h, i: (0, 0)),
            pl.BlockSpec((tq, LANES), lambda b, h, i: (b * nq + i, h)),
            pl.BlockSpec((seq, LANES), lambda b, h, i: (b, n_heads + h)),
            pl.BlockSpec((LANES, seq), lambda b, h, i: (h, b)),
            pl.BlockSpec((n_ctx, LANES), lambda b, h, i: (ctx_blk0 + b, n_heads + h)),
            pl.BlockSpec((LANES, n_ctx), lambda b, h, i: (h, ctx_blk0 + b)),
            pl.BlockSpec(memory_space=pl.ANY),
        ],
        out_specs=pl.BlockSpec((tq, LANES), lambda b, h, i: (b * nq + i, h)),
        out_shape=jax.ShapeDtypeStruct(dst.shape, dst.dtype),
        input_output_aliases={7: 0},
        scratch_shapes=[pltpu.VMEM((2, seq + n_ctx, 2 * min(tq, DIFF_SUB_ROWS)), F32)],
        compiler_params=_cparams("parallel", "parallel", "parallel"),
        name="diff_attn",
    )(lam_vec, subln_g, z, z, vt, z, vt, dst)


def _diff_attention_ctx(z, vt, lam_vec, subln_g, lam_init, dst, *, n_batch, seq, n_ctx, n_heads):
    ctx_blk0 = n_batch * seq // n_ctx
    return pl.pallas_call(
        functools.partial(_diff_kernel, lam_init=lam_init, has_x=False),
        grid=(n_batch, n_heads),
        in_specs=[
            pl.BlockSpec(lam_vec.shape, lambda b, h: (0, 0)),
            pl.BlockSpec((1, LANES), lambda b, h: (0, 0)),
            pl.BlockSpec((n_ctx, LANES), lambda b, h: (ctx_blk0 + b, h)),
            pl.BlockSpec((n_ctx, LANES), lambda b, h: (ctx_blk0 + b, n_heads + h)),
            pl.BlockSpec((LANES, n_ctx), lambda b, h: (h, ctx_blk0 + b)),
            pl.BlockSpec(memory_space=pl.ANY),
        ],
        out_specs=pl.BlockSpec((n_ctx, LANES), lambda b, h: (ctx_blk0 + b, h)),
        out_shape=jax.ShapeDtypeStruct(dst.shape, dst.dtype),
        input_output_aliases={5: 0},
        scratch_shapes=[pltpu.VMEM((2, n_ctx, 2 * min(n_ctx, DIFF_SUB_ROWS)), F32)],
        compiler_params=_cparams("parallel", "parallel"),
        name="diff_attn_ctx",
    )(lam_vec, subln_g, z, z, vt, dst)


def _outproj_kernel(a0_ref, a1_ref, w_ref, x_ref, g_ref, gate_ref, o_ref):
    half = a0_ref.shape[1]
    y = _dot(a0_ref[...], w_ref[:half, :]) + _dot(a1_ref[...], w_ref[half:, :])
    o_ref[...] = x_ref[...] + gate_ref[...] * (_rms(y) * g_ref[...])


def _outproj(a0, a0_col, a1, a1_col, w, xt, normg, mod, k_gate, *, tm, rows, seq, n_batch):
    d = xt.shape[1]
    half = d // 2

    def mrow(i):
        return jnp.minimum(i * tm // seq, n_batch)

    return pl.pallas_call(
        _outproj_kernel,
        grid=(rows // tm,),
        in_specs=[
            pl.BlockSpec((tm, half), lambda i: (i, a0_col)),
            pl.BlockSpec((tm, half), lambda i: (i, a1_col)),
            pl.BlockSpec((d, d), lambda i: (0, 0)),
            pl.BlockSpec((tm, d), lambda i: (i, 0)),
            pl.BlockSpec((None, 1, d), lambda i: (1, 0, 0)),
            pl.BlockSpec((None, None, 1, d), lambda i: (mrow(i), k_gate, 0, 0)),
        ],
        out_specs=pl.BlockSpec((tm, d), lambda i: (i, 0)),
        out_shape=jax.ShapeDtypeStruct((rows, d), F32),
        compiler_params=_cparams("parallel"),
        name="outproj",
    )(a0, a1, w, xt, normg, mod)


def _router_kernel(x_ref, g_ref, sh_ref, sc_ref, rw_ref, o_ref, sel_ref):
    h = _modulate(x_ref[...], g_ref[...], sh_ref[...], sc_ref[...])
    logits = _dot_split(_split_bf16(h), _split_bf16(rw_ref[...]))
    lane = lax.broadcasted_iota(jnp.int32, logits.shape, 1)
    ninf = -jnp.inf
    logits = jnp.where(lane < N_EXPERTS, logits, ninf)
    m1 = jnp.max(logits, axis=-1, keepdims=True)
    i1 = jnp.min(jnp.where(logits == m1, lane, LANES), axis=-1, keepdims=True)
    sel1 = lane == i1
    rest = jnp.where(sel1, ninf, logits)
    m2 = jnp.max(rest, axis=-1, keepdims=True)
    i2 = jnp.min(jnp.where(rest == m2, lane, LANES), axis=-1, keepdims=True)
    sel2 = lane == i2
    e2 = jnp.exp(m2 - m1)
    den = 1.0 + e2
    o_ref[...] = jnp.where(sel1, 1.0 / den, 0.0) + jnp.where(sel2, e2 / den, 0.0)
    sel_ref[...] = jnp.where(sel1 | sel2, 1.0, 0.0)


def _router(xt, normg, mod, router_w, *, tm, rows, seq, n_batch):
    d = xt.shape[1]
    rw = jnp.zeros((d, LANES), F32).at[:, :N_EXPERTS].set(router_w)

    def mrow(i):
        return jnp.minimum(i * tm // seq, n_batch)

    return pl.pallas_call(
        _router_kernel,
        grid=(rows // tm,),
        in_specs=[
            pl.BlockSpec((tm, d), lambda i: (i, 0)),
            pl.BlockSpec((None, 1, d), lambda i: (2, 0, 0)),
            pl.BlockSpec((None, None, 1, d), lambda i: (mrow(i), 3, 0, 0)),
            pl.BlockSpec((None, None, 1, d), lambda i: (mrow(i), 4, 0, 0)),
            pl.BlockSpec((d, LANES), lambda i: (0, 0)),
        ],
        out_specs=[pl.BlockSpec((tm, LANES), lambda i: (i, 0))] * 2,
        out_shape=[jax.ShapeDtypeStruct((rows, LANES), F32)] * 2,
        compiler_params=_cparams("parallel"),
        name="router",
    )(xt, normg, mod, mod, rw)


SWIGLU_CHUNK = 256


def _swiglu(h, wg_ref, wu_ref, wo_ref, u_off):
    width = wo_ref.shape[0]
    acc = None
    for c0 in range(0, width, SWIGLU_CHUNK):
        gp = _dot(h, wg_ref[:, c0:c0 + SWIGLU_CHUNK])
        up = _dot(h, wu_ref[:, u_off + c0:u_off + c0 + SWIGLU_CHUNK])
        a = gp * (1.0 / (1.0 + jnp.exp(-gp))) * up
        part = _dot(a.astype(BF16), wo_ref[c0:c0 + SWIGLU_CHUNK, :])
        acc = part if acc is None else acc + part
    return acc


def _ffn_kernel(x_ref, g2_ref, sh_ref, sc_ref, win_ref, wout_ref, g3_ref, gate_ref, o_ref):
    x = x_ref[...]
    h = _modulate(x, g2_ref[...], sh_ref[...], sc_ref[...]).astype(BF16)
    y = _swiglu(h, win_ref, win_ref, wout_ref, wout_ref.shape[0])
    o_ref[...] = x + gate_ref[...] * (_rms(y) * g3_ref[...])


def _ffn(xt, normg, mod, w_in, w_out, *, tm, rows, seq, n_batch):
    d = xt.shape[1]

    def mrow(i):
        return jnp.minimum(i * tm // seq, n_batch)

    in_specs = [
        pl.BlockSpec((tm, d), lambda i: (i, 0)),
        pl.BlockSpec((None, 1, d), lambda i: (2, 0, 0)),
        pl.BlockSpec((None, None, 1, d), lambda i: (mrow(i), 3, 0, 0)),
        pl.BlockSpec((None, None, 1, d), lambda i: (mrow(i), 4, 0, 0)),
        pl.BlockSpec(w_in.shape, lambda i: (0, 0)),
        pl.BlockSpec(w_out.shape, lambda i: (0, 0)),
        pl.BlockSpec((None, 1, d), lambda i: (3, 0, 0)),
        pl.BlockSpec((None, None, 1, d), lambda i: (mrow(i), 5, 0, 0)),
    ]
    return pl.pallas_call(
        _ffn_kernel,
        grid=(rows // tm,),
        in_specs=in_specs,
        out_specs=pl.BlockSpec((tm, d), lambda i: (i, 0)),
        out_shape=jax.ShapeDtypeStruct((rows, d), F32),
        compiler_params=_cparams("parallel"),
        name="ffn",
    )(xt, normg, mod, mod, w_in, w_out, normg, mod)


MOE_DISPATCH_ROWS = 256
MOE_COMBINE_ROWS = 512
MOE_SEG_ALIGN = 16
MOE_TILE = 512


def _moe_dispatch_kernel(nchunk_ref, off_ref, x_ref, g2_ref, sh_ref, sc_ref, sel_ref, tri_ref, _, hs_hbm, key_ref,
                         h_scr, keyt_scr, stage, sems, *, n_exp):
    b = pl.program_id(0)
    rows = MOE_DISPATCH_ROWS
    h_scr[...] = _modulate(x_ref[...], g2_ref[...], sh_ref[...], sc_ref[...]).astype(BF16)
    sel = sel_ref[...]
    key = jnp.where(sel > 0.0, _dot(tri_ref[...], sel.astype(BF16)), -1.0)
    key_ref[...] = key
    keyt_scr[...] = key.T

    def copy(slot, row0):
        return pltpu.make_async_copy(stage.at[slot], hs_hbm.at[pl.ds(row0, rows), :], sems.at[slot])

    issued = jnp.int32(0)
    for e in range(n_exp):
        key_row = keyt_scr[e:e + 1, :]
        base = off_ref[b * n_exp + e]

        def chunk(c, k):
            slot = k % 2

            @pl.when(k >= 2)
            def _():
                copy(slot, 0).wait()

            r = (c * rows + lax.broadcasted_iota(jnp.int32, (rows, 1), 0)).astype(F32)
            onehot = jnp.where(key_row == r, 1.0, 0.0).astype(BF16)
            stage[slot] = _dot(onehot, h_scr[...]).astype(BF16)
            copy(slot, pl.multiple_of(base + c * rows, MOE_SEG_ALIGN)).start()
            return k + 1

        issued = lax.fori_loop(0, nchunk_ref[b * n_exp + e], chunk, issued)

    @pl.when(issued >= 1)
    def _():
        copy((issued - 1) % 2, 0).wait()

    @pl.when(issued >= 2)
    def _():
        copy(issued % 2, 0).wait()


def _moe_experts_kernel(tile_ref, exp_ref, valid_ref, hs_ref, wg_ref, wu_ref, wo_ref, y_ref, acc, *, n_f):
    t = pl.program_id(0)
    f = pl.program_id(1)

    @pl.when(valid_ref[t] == 1)
    def _():
        part = _swiglu(hs_ref[...], wg_ref, wu_ref, wo_ref, 0)

        @pl.when(f == 0)
        def _():
            acc[...] = part

        @pl.when(f > 0)
        def _():
            acc[...] += part

        @pl.when(f == n_f - 1)
        def _():
            y_ref[...] = acc[...].astype(y_ref.dtype)


def _moe_combine_kernel(npiece_ref, off_ref, x_ref, g3_ref, gate_ref, comb_ref, key_ref, y_hbm, o_ref,
                        stage, sems, *, n_exp):
    b = pl.program_id(0)
    rows = MOE_COMBINE_ROWS
    nb = x_ref.shape[0]
    o_ref[...] = jnp.zeros_like(o_ref)
    lane = lax.broadcasted_iota(jnp.int32, (nb, LANES), 1)

    def copy(slot, row0):
        return pltpu.make_async_copy(y_hbm.at[pl.ds(row0, rows), :], stage.at[slot], sems.at[slot])

    def row0(e, c):
        return pl.multiple_of(off_ref[b * n_exp + e] + c * rows, MOE_SEG_ALIGN)

    copy(0, row0(0, 0)).start()
    done = jnp.int32(0)
    for e in range(n_exp):
        pick = lane == e
        key_col = jnp.sum(jnp.where(pick, key_ref[...], 0.0), axis=-1, keepdims=True)
        gate_col = jnp.sum(jnp.where(pick, comb_ref[...], 0.0), axis=-1, keepdims=True)
        n_pieces = npiece_ref[b * n_exp + e]

        def piece(c, k, e=e, n_pieces=n_pieces, key_col=key_col, gate_col=gate_col):
            slot = k % 2
            copy(slot, 0).wait()

            @pl.when(c + 1 < n_pieces)
            def _():
                copy(1 - slot, row0(e, c + 1)).start()

            if e + 1 < n_exp:
                @pl.when(c + 1 == n_pieces)
                def _():
                    copy(1 - slot, row0(e + 1, 0)).start()

            r = (c * rows + lax.broadcasted_iota(jnp.int32, (1, rows), 1)).astype(F32)
            onehot = jnp.where(key_col == r, 1.0, 0.0).astype(BF16)
            o_ref[...] += gate_col * _dot(onehot, stage[slot])
            return k + 1

        done = lax.fori_loop(0, n_pieces, piece, done)

    o_ref[...] = x_ref[...] + gate_ref[...] * (_rms(o_ref[...]) * g3_ref[...])


def _moe(xt, normg, mod, w_in, w_out, j, comb, sel, *, nb, tf, rows, seq, n_batch):
    d = xt.shape[1]
    n_exp, fdim = w_out.shape[1], w_out.shape[2]
    n_f = fdim // tf
    n_blk = rows // nb
    i32 = jnp.int32

    counts = jnp.sum(sel.reshape(n_blk, nb, LANES)[:, :, :n_exp], axis=1).astype(i32)
    seg = -(-counts // MOE_SEG_ALIGN) * MOE_SEG_ALIGN
    region = -(-(jnp.sum(seg, axis=0) + MOE_DISPATCH_ROWS) // MOE_TILE) * MOE_TILE
    region_end = jnp.cumsum(region)
    region_start = region_end - region
    off = (region_start[None, :] + jnp.cumsum(seg, axis=0) - seg).reshape(n_blk * n_exp).astype(i32)
    nchunk = (-(-counts // MOE_DISPATCH_ROWS)).reshape(n_blk * n_exp).astype(i32)
    npiece = jnp.maximum(-(-counts // MOE_COMBINE_ROWS), 1).reshape(n_blk * n_exp).astype(i32)
    cap = 2 * rows + n_blk * n_exp * (MOE_SEG_ALIGN - 1) + n_exp * (MOE_DISPATCH_ROWS + MOE_TILE - 1)
    n_tiles = -(-cap // MOE_TILE)
    cap = n_tiles * MOE_TILE + MOE_COMBINE_ROWS
    tiles = jnp.arange(n_tiles, dtype=i32)
    n_used = region_end[-1] // MOE_TILE
    tile_map = jnp.minimum(tiles, n_used - 1).astype(i32)
    tile_exp = jnp.minimum(jnp.searchsorted(region_end, tile_map * MOE_TILE, side="right"), n_exp - 1).astype(i32)
    tile_valid = (tiles < n_used).astype(i32)

    def mrow(i):
        return jnp.minimum(i * nb // seq, n_batch)

    tri = (jnp.arange(nb)[None, :] < jnp.arange(nb)[:, None]).astype(BF16)
    hs, key = pl.pallas_call(
        functools.partial(_moe_dispatch_kernel, n_exp=n_exp),
        grid_spec=pltpu.PrefetchScalarGridSpec(
            num_scalar_prefetch=2,
            grid=(n_blk,),
            in_specs=[
                pl.BlockSpec((nb, d), lambda i, *_: (i, 0)),
                pl.BlockSpec((None, 1, d), lambda i, *_: (2, 0, 0)),
                pl.BlockSpec((None, None, 1, d), lambda i, *_: (mrow(i), 3, 0, 0)),
                pl.BlockSpec((None, None, 1, d), lambda i, *_: (mrow(i), 4, 0, 0)),
                pl.BlockSpec((nb, LANES), lambda i, *_: (i, 0)),
                pl.BlockSpec((nb, nb), lambda i, *_: (0, 0)),
                pl.BlockSpec(memory_space=pl.ANY),
            ],
            out_specs=[pl.BlockSpec(memory_space=pl.ANY), pl.BlockSpec((nb, LANES), lambda i, *_: (i, 0))],
            scratch_shapes=[
                pltpu.VMEM((nb, d), BF16),
                pltpu.VMEM((LANES, nb), F32),
                pltpu.VMEM((2, MOE_DISPATCH_ROWS, d), BF16),
                pltpu.SemaphoreType.DMA((2,)),
            ],
        ),
        out_shape=[jax.ShapeDtypeStruct((cap, d), BF16), jax.ShapeDtypeStruct((rows, LANES), F32)],
        input_output_aliases={8: 0},
        compiler_params=_cparams("arbitrary"),
        name="moe_dispatch",
    )(nchunk, off, xt, normg, mod, mod, sel, tri, jnp.zeros((cap, d), BF16))

    ys = pl.pallas_call(
        functools.partial(_moe_experts_kernel, n_f=n_f),
        grid_spec=pltpu.PrefetchScalarGridSpec(
            num_scalar_prefetch=3,
            grid=(n_tiles, n_f),
            in_specs=[
                pl.BlockSpec((MOE_TILE, d), lambda t, f, tm, te, tv: (tm[t], 0)),
                pl.BlockSpec((None, None, d, tf), lambda t, f, tm, te, tv: (j, te[t], 0, f)),
                pl.BlockSpec((None, None, d, tf), lambda t, f, tm, te, tv: (j, te[t], 0, n_f + f)),
                pl.BlockSpec((None, None, tf, d), lambda t, f, tm, te, tv: (j, te[t], f, 0)),
            ],
            out_specs=pl.BlockSpec((MOE_TILE, d), lambda t, f, tm, te, tv: (tm[t], 0)),
            scratch_shapes=[pltpu.VMEM((MOE_TILE, d), F32)],
        ),
        out_shape=jax.ShapeDtypeStruct((cap, d), BF16),
        input_output_aliases={3: 0},
        compiler_params=_cparams("arbitrary", "arbitrary"),
        name="moe_experts",
    )(tile_map, tile_exp, tile_valid, hs, w_in, w_in, w_out)

    return pl.pallas_call(
        functools.partial(_moe_combine_kernel, n_exp=n_exp),
        grid_spec=pltpu.PrefetchScalarGridSpec(
            num_scalar_prefetch=2,
            grid=(n_blk,),
            in_specs=[
                pl.BlockSpec((nb, d), lambda i, *_: (i, 0)),
                pl.BlockSpec((None, 1, d), lambda i, *_: (3, 0, 0)),
                pl.BlockSpec((None, None, 1, d), lambda i, *_: (mrow(i), 5, 0, 0)),
                pl.BlockSpec((nb, LANES), lambda i, *_: (i, 0)),
                pl.BlockSpec((nb, LANES), lambda i, *_: (i, 0)),
                pl.BlockSpec(memory_space=pl.ANY),
            ],
            out_specs=pl.BlockSpec((nb, d), lambda i, *_: (i, 0)),
            scratch_shapes=[
                pltpu.VMEM((2, MOE_COMBINE_ROWS, d), BF16),
                pltpu.SemaphoreType.DMA((2,)),
            ],
        ),
        out_shape=jax.ShapeDtypeStruct((rows, d), F32),
        compiler_params=_cparams("parallel"),
        name="moe_combine",
    )(npiece, off, xt, normg, mod, comb, key, ys)


def _lambda_init(layer):
    return 0.8 - 0.6 * math.exp(-0.3 * layer)


def kernel(x, c, ctx, c_ctx, ada_w, ada_b, norm_g, mix_in_w, mix_out_w, win_sink, diff_qkv_w, diff_out_w,
           diff_lambda, diff_subln_g, ffn_in_w, ffn_out_w, router_w, expert_in_w, expert_out_w):
    n_batch, seq, d = x.shape
    n_ctx = ctx.shape[1]
    depth = ada_w.shape[0]
    n_lat = n_batch * seq
    n_all = n_lat + n_batch * n_ctx
    fdim = mix_in_w.shape[2] - (d // 2 + 2 * LANES)
    q_cols = d // 2
    n_diff_heads = d // LANES

    tm_proj = _pick_tile(1024, seq, n_batch * n_ctx)
    tm_out = _pick_tile(2048, seq, n_batch * n_ctx)
    tm_ffn = _pick_tile(512, seq, n_batch * n_ctx)
    nb_moe = _pick_tile(1024, seq, n_batch * n_ctx)
    tq_diff = _pick_tile(4096, seq)
    tq_win = _pick_tile(512, seq)
    common = dict(seq=seq, n_batch=n_batch)

    xt = jnp.concatenate([x.reshape(n_lat, d), ctx.reshape(n_batch * n_ctx, d)], axis=0)
    n_mod = -(-(n_batch + 1) // 8) * 8
    cv = jnp.zeros((n_mod, d), F32).at[:n_batch].set(c).at[n_batch].set(c_ctx)
    mods = _modvec(cv, ada_w, ada_b).reshape(depth, n_mod, 6, 1, d)
    rope = _rope_tables(seq, tm_proj)
    f_tables = _fourier_tables(seq)
    fc_tables = _dense_fourier_tables(n_ctx)

    n_f = fdim // LANES
    plan_even = ([(0, i * LANES, "plain") for i in range(n_f)]
                 + [(1, i * LANES, "rope_q_log2") for i in range(q_cols // LANES)]
                 + [(1, q_cols, "rope_k"), (1, q_cols + LANES, "plain")])
    plan_odd = ([(0, i * LANES, "rope_q_log2") for i in range(n_diff_heads)]
                + [(0, d + i * LANES, "rope_k") for i in range(n_diff_heads)]
                + [(1, i * LANES, "plain_t") for i in range(n_diff_heads)])

    expert_in_bf16 = expert_in_w.astype(BF16)
    expert_out_bf16 = expert_out_w.astype(BF16)

    for layer in range(depth):
        j = layer // 2
        need_ctx = layer < depth - 1
        rows = n_all if need_ctx else n_lat
        mod = mods[layer]
        ng = norm_g[layer].reshape(4, 1, d)
        if layer % 2 == 0:
            f, z = _proj(xt, ng, mod, 0, 1, mix_in_w[j].astype(BF16), rope, plan_even,
                         [(fdim, F32, False), (q_cols + 2 * LANES, BF16, False)], tm=tm_proj, n_lat=n_lat,
                         **common)
            k_col, v_col = q_cols // LANES, q_cols // LANES + 1
            att = dict(n_batch=n_batch, seq=seq, n_ctx=n_ctx, q_cols=q_cols, k_col=k_col, v_col=v_col)
            mix_f = _fourier(f, f_tables, jnp.zeros((rows, fdim), BF16), n_batch=n_batch, seq=seq)
            mix_a = _win_attention(z, win_sink[j], jnp.zeros((rows, q_cols), BF16), tq=tq_win, **att)
            if need_ctx:
                mix_f = _dense_fourier(f, fc_tables, mix_f, n_batch=n_batch, n_pos=n_ctx,
                                       row_block0=n_lat // n_ctx)
                mix_a = _ctx_gqa_attention(z, win_sink[j], mix_a, **att)
            xt = _outproj(mix_f, 0, mix_a, 0, mix_out_w[j].astype(BF16), xt, ng, mod, 2, tm=tm_out, rows=rows,
                          **common)
            xt = _ffn(xt, ng, mod, ffn_in_w[j].astype(BF16), ffn_out_w[j].astype(BF16),
                      tm=tm_ffn, rows=rows, **common)
        else:
            lam_init = _lambda_init(layer)
            z, vt = _proj(xt, ng, mod, 0, 1, diff_qkv_w[j].astype(BF16), rope, plan_odd,
                          [(2 * d, BF16, False), (d, BF16, True)], tm=tm_proj, n_lat=n_lat, **common)
            subg = diff_subln_g[j].reshape(1, LANES)
            att = dict(n_batch=n_batch, seq=seq, n_ctx=n_ctx, n_heads=n_diff_heads)
            mix = _diff_attention(z, vt, diff_lambda[j], subg, lam_init, jnp.zeros((rows, d), BF16), tq=tq_diff,
                                  **att)
            if need_ctx:
                mix = _diff_attention_ctx(z, vt, diff_lambda[j], subg, lam_init, mix, **att)
            xt = _outproj(mix, 0, mix, 1, diff_out_w[j].astype(BF16), xt, ng, mod, 2, tm=tm_out, rows=rows,
                          **common)
            comb, sel = _router(xt, ng, mod, router_w[j], tm=tm_out, rows=rows, **common)
            xt = _moe(xt, ng, mod, expert_in_bf16, expert_out_bf16, j, comb, sel,
                      nb=nb_moe, tf=3584, rows=rows, **common)
    return xt[:n_lat].reshape(n_batch, seq, d)
```

```python
import functools
import math

import numpy as np
import jax
import jax.numpy as jnp
from jax import lax
from jax.experimental import pallas as pl
from jax.experimental.pallas import tpu as pltpu

EPS = 1e-6
NEG = -1e30
HEAD_DIM = 64
LANES = 128
GRID_W = 64
BLOCK = 128
WINDOW = 128
ROPE_THETA = 10000.0
N_EXPERTS = 8
F32 = jnp.float32
BF16 = jnp.bfloat16
HIGHEST = lax.Precision.HIGHEST
VMEM_LIMIT = 56 * 1024 * 1024


def _cparams(*sem):
    return pltpu.CompilerParams(dimension_semantics=sem, vmem_limit_bytes=VMEM_LIMIT)


def _dot(a, b):
    return jnp.dot(a, b, preferred_element_type=F32)


def _dot_nt(a, b):
    return lax.dot_general(a, b, (((1,), (1,)), ((), ())), preferred_element_type=F32)


def _dot_hi(a, b):
    return jnp.dot(a, b, precision=HIGHEST, preferred_element_type=F32)


def _rms(v):
    return v * lax.rsqrt(jnp.mean(v * v, axis=-1, keepdims=True) + EPS)


def _modulate(x, g, sh, sc):
    return _rms(x) * g * (1.0 + sc) + sh


def _pick_tile(pref, *dims):
    t = pref
    while any(d % t for d in dims):
        t //= 2
    return t


def _modvec_kernel(c_ref, w_ref, b_ref, o_ref):
    cv = c_ref[...]
    s = cv * (1.0 / (1.0 + jnp.exp(-cv)))
    o_ref[...] = _dot(s.astype(BF16), w_ref[...].astype(BF16)) + b_ref[...]


def _modvec(cv, ada_w, ada_b):
    depth, d, n = ada_w.shape
    r = cv.shape[0]
    tn = _pick_tile(1536, n)
    return pl.pallas_call(
        _modvec_kernel,
        grid=(depth, n // tn),
        in_specs=[
            pl.BlockSpec((r, d), lambda l, j: (0, 0)),
            pl.BlockSpec((None, d, tn), lambda l, j: (l, 0, j)),
            pl.BlockSpec((None, 1, tn), lambda l, j: (l, 0, j)),
        ],
        out_specs=pl.BlockSpec((None, r, tn), lambda l, j: (l, 0, j)),
        out_shape=jax.ShapeDtypeStruct((depth, r, n), F32),
        compiler_params=_cparams("parallel", "parallel"),
        name="modvec",
    )(cv, ada_w, ada_b.reshape(depth, 1, n))


def _proj_kernel(x_ref, g_ref, sh_ref, sc_ref, w_ref, cos_ref, sa_ref, sb_ref, *o_refs, plan, group):
    h = _modulate(x_ref[...], g_ref[...], sh_ref[...], sc_ref[...]).astype(BF16)
    n = w_ref.shape[1]
    for g0 in range(0, n, group):
        acc = _dot(h, w_ref[:, g0:g0 + group])
        for c0 in range(0, group, LANES):
            oi, oc, mode = plan[(g0 + c0) // LANES]
            v = acc[:, c0:c0 + LANES]
            if mode.startswith("rope"):
                v = (v * cos_ref[...] + pltpu.roll(v, LANES - 16, 1) * sa_ref[...]
                     + pltpu.roll(v, 16, 1) * sb_ref[...])
                if mode == "rope_q_log2":
                    v = v * (HEAD_DIM ** -0.5 * LOG2E)
            if mode == "plain_t":
                o_refs[oi][oc:oc + LANES, :] = v.T.astype(o_refs[oi].dtype)
            else:
                o_refs[oi][:, oc:oc + LANES] = v.astype(o_refs[oi].dtype)


def _proj(xt, normg, mod, k_sh, k_sc, w, rope, plan, outs, *, tm, n_lat, seq, n_batch):
    t, d = xt.shape
    n = w.shape[1]
    group = _pick_tile(512, n)
    nx = n_lat // tm
    per = seq // tm

    def mrow(i):
        return jnp.minimum(i * tm // seq, n_batch)

    def rrow(i):
        return jnp.where(i < nx, i % per, per)

    in_specs = [
        pl.BlockSpec((tm, d), lambda i: (i, 0)),
        pl.BlockSpec((None, 1, d), lambda i: (0, 0, 0)),
        pl.BlockSpec((None, None, 1, d), lambda i: (mrow(i), k_sh, 0, 0)),
        pl.BlockSpec((None, None, 1, d), lambda i: (mrow(i), k_sc, 0, 0)),
        pl.BlockSpec((d, n), lambda i: (0, 0)),
        pl.BlockSpec((tm, LANES), lambda i: (rrow(i), 0)),
        pl.BlockSpec((tm, LANES), lambda i: (rrow(i), 0)),
        pl.BlockSpec((tm, LANES), lambda i: (rrow(i), 0)),
    ]
    out_specs = [pl.BlockSpec((wd, tm), lambda i: (0, i)) if tr else pl.BlockSpec((tm, wd), lambda i: (i, 0))
                 for wd, _, tr in outs]
    out_shape = [jax.ShapeDtypeStruct((wd, t) if tr else (t, wd), dt) for wd, dt, tr in outs]
    return pl.pallas_call(
        functools.partial(_proj_kernel, plan=plan, group=group),
        grid=(t // tm,),
        in_specs=in_specs,
        out_specs=out_specs,
        out_shape=out_shape,
        compiler_params=_cparams("parallel"),
        name="proj",
    )(xt, normg, mod, mod, w, *rope)


def _rope_tables(seq, tm):
    rows_count = seq // GRID_W
    rows = jnp.repeat(jnp.arange(rows_count), GRID_W).astype(F32)
    cols = jnp.tile(jnp.arange(GRID_W), rows_count).astype(F32)
    axis_dim = HEAD_DIM // 2
    inv = ROPE_THETA ** (-jnp.arange(0, axis_dim, 2, dtype=F32) / axis_dim)
    ar = rows[:, None] * inv
    ac = cols[:, None] * inv
    cr, sr, cc, sc = jnp.cos(ar), jnp.sin(ar), jnp.cos(ac), jnp.sin(ac)
    z = jnp.zeros_like(sr)
    reps = LANES // HEAD_DIM
    cos = jnp.tile(jnp.concatenate([cr, cr, cc, cc], axis=1), (1, reps))
    sa = jnp.tile(jnp.concatenate([-sr, z, -sc, z], axis=1), (1, reps))
    sb = jnp.tile(jnp.concatenate([z, sr, z, sc], axis=1), (1, reps))
    ident = jnp.ones((tm, LANES), F32)
    zero = jnp.zeros((tm, LANES), F32)
    return (jnp.concatenate([cos, ident]), jnp.concatenate([sa, zero]), jnp.concatenate([sb, zero]))


FOURIER_LANES = 2 * LANES


def _split_bf16(t):
    hi = t.astype(BF16)
    return hi, (t - hi.astype(F32)).astype(BF16)


def _dot_split(a, b):
    return _dot(a[0], b[0]) + _dot(a[0], b[1]) + _dot(a[1], b[0])


def _fourier_tables(seq):
    n2 = GRID_W
    n1 = seq // n2
    norm = 1.0 / math.sqrt(seq * LANES)
    a = np.arange(n1)
    k1 = np.arange(n1)
    b = np.arange(n2)
    ang = (b[:, None, None] * k1[None, :, None] + (seq // n1) * k1[None, :, None] * a[None, None, :]) % seq
    th = 2.0 * np.pi * ang / seq
    m1 = np.concatenate([np.cos(th), -np.sin(th)], axis=1).astype(np.float32)
    ph = 2.0 * np.pi * ((b[:, None] * b[None, :]) % n2) / n2
    c2, s2 = np.cos(ph), np.sin(ph)
    g2 = np.block([[c2, s2], [-s2, c2]]).astype(np.float32)
    ch = np.arange(LANES)
    pc = 2.0 * np.pi * ((ch[:, None] * ch[None, :]) % LANES) / LANES
    cc = (np.cos(pc) * norm).astype(np.float32)
    sc = (np.sin(pc) * norm).astype(np.float32)
    cs = np.concatenate([cc, sc], axis=0)
    out = []
    for t in (m1, g2, cs):
        out.extend(_split_bf16(jnp.asarray(t)))
    return tuple(out)


FOURIER_ROW_CHUNK = 512


def _fourier_kernel(*refs, n1, n2, n_grp):
    u_refs = refs[:n_grp]
    m1h_ref, m1l_ref, g2h_ref, g2l_ref, csh_ref, csl_ref, _, o_ref, b_scr, xr_scr, xi_scr = refs[n_grp:]
    seq = n1 * n2
    for b in range(n2):
        xs = jnp.concatenate([u[pl.ds(b, n1, stride=n2), :] for u in u_refs], axis=1)
        z = _dot_split((m1h_ref[b], m1l_ref[b]), _split_bf16(xs))
        for g in range(n_grp):
            b_scr[g, 2 * n1 * b:2 * n1 * (b + 1), :] = z[:, g * LANES:(g + 1) * LANES]
    g2 = (g2h_ref[...], g2l_ref[...])
    for k1 in range(n1):
        bk = jnp.concatenate(
            [jnp.concatenate([b_scr.at[g][pl.ds(k1, n2, stride=2 * n1), :],
                              b_scr.at[g][pl.ds(n1 + k1, n2, stride=2 * n1), :]], axis=0)
             for g in range(n_grp)], axis=1)
        xk = _dot_split(g2, _split_bf16(bk))
        for g in range(n_grp):
            xr_scr[g, n2 * k1:n2 * (k1 + 1), :] = xk[:n2, g * LANES:(g + 1) * LANES]
            xi_scr[g, n2 * k1:n2 * (k1 + 1), :] = xk[n2:, g * LANES:(g + 1) * LANES]
    cs = (csh_ref[...], csl_ref[...])
    rc = min(FOURIER_ROW_CHUNK, seq)
    for g in range(n_grp):
        for r0 in range(0, seq, rc):
            x = jnp.concatenate([xr_scr[g, r0:r0 + rc, :], xi_scr[g, r0:r0 + rc, :]], axis=1)
            xr_scr[g, r0:r0 + rc, :] = _dot_split(_split_bf16(x), cs)
    for g in range(n_grp):
        for k2 in range(n2):
            o_ref[n1 * k2:n1 * (k2 + 1), g * LANES:(g + 1) * LANES] = (
                xr_scr.at[g][pl.ds(k2, n1, stride=n2), :].astype(o_ref.dtype))


def _fourier(f, tables, dst, *, n_batch, seq):
    n2 = GRID_W
    n1 = seq // n2
    n_grp = FOURIER_LANES // LANES
    table_specs = [pl.BlockSpec(t.shape, (lambda b, g, nd=t.ndim: (0,) * nd)) for t in tables]
    u_specs = [pl.BlockSpec((seq, LANES), (lambda b, g, k=k: (b, g * n_grp + k))) for k in range(n_grp)]
    return pl.pallas_call(
        functools.partial(_fourier_kernel, n1=n1, n2=n2, n_grp=n_grp),
        grid=(n_batch, f.shape[1] // FOURIER_LANES),
        in_specs=u_specs + table_specs + [pl.BlockSpec(memory_space=pl.ANY)],
        out_specs=pl.BlockSpec((seq, FOURIER_LANES), lambda b, g: (b, g)),
        out_shape=jax.ShapeDtypeStruct(dst.shape, dst.dtype),
        input_output_aliases={n_grp + len(tables): 0},
        scratch_shapes=[pltpu.VMEM((n_grp, 2 * seq, LANES), F32), pltpu.VMEM((n_grp, seq, LANES), F32),
                        pltpu.VMEM((n_grp, seq, LANES), F32)],
        compiler_params=_cparams("parallel", "parallel"),
        name="fourier",
    )(*([f] * n_grp), *tables, dst)


def _dense_fourier_tables(n):
    norm = 1.0 / math.sqrt(n * LANES)
    p = np.arange(n)
    ph = 2.0 * np.pi * ((p[:, None] * p[None, :]) % n) / n
    ch = np.arange(LANES)
    pc = 2.0 * np.pi * ((ch[:, None] * ch[None, :]) % LANES) / LANES
    return (jnp.asarray(np.cos(ph).astype(np.float32)), jnp.asarray(np.sin(ph).astype(np.float32)),
            jnp.asarray((np.cos(pc) * norm).astype(np.float32)), jnp.asarray((np.sin(pc) * norm).astype(np.float32)))


def _dense_fourier_kernel(u_ref, cl_ref, sl_ref, cc_ref, sc_ref, _, o_ref):
    u = u_ref[...]
    y = _dot_hi(cl_ref[...], _dot_hi(u, cc_ref[...])) - _dot_hi(sl_ref[...], _dot_hi(u, sc_ref[...]))
    o_ref[...] = y.astype(o_ref.dtype)


def _dense_fourier(f, tables, dst, *, n_batch, n_pos, row_block0):
    cl, sl, cc, sc = tables
    groups = f.shape[1] // LANES
    return pl.pallas_call(
        _dense_fourier_kernel,
        grid=(n_batch, groups),
        in_specs=[
            pl.BlockSpec((n_pos, LANES), lambda b, g: (row_block0 + b, g)),
            pl.BlockSpec(cl.shape, lambda b, g: (0, 0)),
            pl.BlockSpec(sl.shape, lambda b, g: (0, 0)),
            pl.BlockSpec(cc.shape, lambda b, g: (0, 0)),
            pl.BlockSpec(sc.shape, lambda b, g: (0, 0)),
            pl.BlockSpec(memory_space=pl.ANY),
        ],
        out_specs=pl.BlockSpec((n_pos, LANES), lambda b, g: (row_block0 + b, g)),
        out_shape=jax.ShapeDtypeStruct(dst.shape, dst.dtype),
        input_output_aliases={5: 0},
        compiler_params=_cparams("parallel", "parallel"),
        name="fourier_ctx",
    )(f, cl, sl, cc, sc, dst)


def _win_kernel(sink_ref, q_ref, *refs, n_qtiles, has_local, n_heads, group_size):
    if has_local:
        kp_ref, kc_ref, kn_ref, vp_ref, vc_ref, vn_ref, kx_ref, vx_ref, _, o_ref = refs
    else:
        kx_ref, vx_ref, _, o_ref = refs
    tq = q_ref.shape[0]
    lane = lax.broadcasted_iota(jnp.int32, (1, LANES), 1)
    half_mask = [lane < HEAD_DIM, lane >= HEAD_DIM]
    pieces = [(kx_ref[...], vx_ref[...], None)]
    if has_local:
        n = pl.program_id(1)
        qi = lax.broadcasted_iota(jnp.int32, (tq, BLOCK), 0)
        kj = lax.broadcasted_iota(jnp.int32, (tq, BLOCK), 1)
        valid_prev = (kj >= qi) & (n >= 1)
        valid_next = (kj <= qi - (tq - WINDOW)) & (n <= n_qtiles - 2)
        di = lax.broadcasted_iota(jnp.int32, (tq, tq), 0) - lax.broadcasted_iota(jnp.int32, (tq, tq), 1)
        valid_mid = (di <= WINDOW) & (di >= -WINDOW)
        pieces += [(kp_ref[...], vp_ref[...], valid_prev), (kc_ref[...], vc_ref[...], valid_mid),
                   (kn_ref[...], vn_ref[...], valid_next)]
    def scores(head):
        pair, half = divmod(head, 2)
        kv = head // group_size
        qp = q_ref[:, pair * LANES:(pair + 1) * LANES].astype(F32)
        src = qp if half == kv else pltpu.roll(qp, HEAD_DIM, 1)
        qe = jnp.where(half_mask[kv], src, 0.0).astype(BF16)
        out = []
        for k, _, valid in pieces:
            s = _dot_nt(qe, k)
            out.append(s if valid is None else jnp.where(valid, s, NEG))
        return out

    def attend(head, sc):
        half = head % 2
        kv = head // group_size
        sink = sink_ref[head] * LOG2E
        m = jnp.zeros((tq, 1), F32) + sink
        for s in sc:
            m = jnp.maximum(m, jnp.max(s, axis=-1, keepdims=True))
        den = jnp.exp2(sink - m)
        pv = jnp.zeros((tq, LANES), F32)
        for s, (_, v, _) in zip(sc, pieces):
            p = jnp.exp2(s - m)
            den = den + jnp.sum(p, axis=-1, keepdims=True)
            pv = pv + _dot(p.astype(BF16), v)
        pv = pv / den
        return pv if half == kv else pltpu.roll(pv, HEAD_DIM, 1)

    sc_next = scores(0)
    out_pair = None
    for head in range(n_heads):
        sc_cur = sc_next
        if head + 1 < n_heads:
            sc_next = scores(head + 1)
        pv = attend(head, sc_cur)
        if head % 2 == 0:
            out_pair = pv
        else:
            pair = head // 2
            o_ref[:, pair * LANES:(pair + 1) * LANES] = jnp.where(half_mask[1], pv, out_pair).astype(o_ref.dtype)


def _win_attention(z, sink, dst, *, n_batch, seq, n_ctx, q_cols, k_col, v_col, tq):
    nbk = seq // BLOCK
    nq = seq // tq
    per = tq // BLOCK
    ctx_blk0 = n_batch * seq // n_ctx
    n_heads = q_cols // HEAD_DIM
    group_size = n_heads // (LANES // HEAD_DIM)

    def edge(col, first):
        return pl.BlockSpec(
            (BLOCK, LANES), lambda b, n: (b * nbk + jnp.clip(n * per + first, 0, nbk - 1), col))

    def mid(col):
        return pl.BlockSpec((tq, LANES), lambda b, n: (b * nq + n, col))

    return pl.pallas_call(
        functools.partial(_win_kernel, n_qtiles=nq, has_local=True, n_heads=n_heads, group_size=group_size),
        grid=(n_batch, nq),
        in_specs=[
            pl.BlockSpec(memory_space=pltpu.SMEM),
            pl.BlockSpec((tq, q_cols), lambda b, n: (b * nq + n, 0)),
            edge(k_col, -1), mid(k_col), edge(k_col, per),
            edge(v_col, -1), mid(v_col), edge(v_col, per),
            pl.BlockSpec((n_ctx, LANES), lambda b, n: (ctx_blk0 + b, k_col)),
            pl.BlockSpec((n_ctx, LANES), lambda b, n: (ctx_blk0 + b, v_col)),
            pl.BlockSpec(memory_space=pl.ANY),
        ],
        out_specs=pl.BlockSpec((tq, q_cols), lambda b, n: (b * nq + n, 0)),
        out_shape=jax.ShapeDtypeStruct(dst.shape, dst.dtype),
        input_output_aliases={10: 0},
        compiler_params=_cparams("parallel", "parallel"),
        name="win_attn",
    )(sink, z, z, z, z, z, z, z, z, z, dst)


def _ctx_gqa_attention(z, sink, dst, *, n_batch, seq, n_ctx, q_cols, k_col, v_col):
    ctx_blk0 = n_batch * seq // n_ctx
    n_heads = q_cols // HEAD_DIM
    group_size = n_heads // (LANES // HEAD_DIM)
    return pl.pallas_call(
        functools.partial(_win_kernel, n_qtiles=0, has_local=False, n_heads=n_heads, group_size=group_size),
        grid=(n_batch,),
        in_specs=[
            pl.BlockSpec(memory_space=pltpu.SMEM),
            pl.BlockSpec((n_ctx, q_cols), lambda b: (ctx_blk0 + b, 0)),
            pl.BlockSpec((n_ctx, LANES), lambda b: (ctx_blk0 + b, k_col)),
            pl.BlockSpec((n_ctx, LANES), lambda b: (ctx_blk0 + b, v_col)),
            pl.BlockSpec(memory_space=pl.ANY),
        ],
        out_specs=pl.BlockSpec((n_ctx, q_cols), lambda b: (ctx_blk0 + b, 0)),
        out_shape=jax.ShapeDtypeStruct(dst.shape, dst.dtype),
        input_output_aliases={4: 0},
        compiler_params=_cparams("parallel"),
        name="ctx_gqa",
    )(sink, z, z, z, dst)


LOG2E = math.log2(math.e)
DIFF_SUB_ROWS = 128
DIFF_KEY_CHUNK = 1024


def _diff_kernel(lam_ref, g_ref, q_ref, *refs, lam_init, has_x):
    if has_x:
        kx_ref, vtx_ref, kc_ref, vtc_ref, _, o_ref, s_scr = refs
        nx = kx_ref.shape[0]
    else:
        kc_ref, vtc_ref, _, o_ref, s_scr = refs
        nx = 0
    nc = kc_ref.shape[0]
    mc = s_scr.shape[2]
    sub = mc // 2
    n_sub = q_ref.shape[0] // sub
    lv = lam_ref[...]
    lam = (jnp.exp(jnp.sum(lv[0:1] * lv[1:2], axis=-1, keepdims=True))
           - jnp.exp(jnp.sum(lv[2:3] * lv[3:4], axis=-1, keepdims=True)) + lam_init)
    lane = lax.broadcasted_iota(jnp.int32, (1, LANES), 1)
    segs = []
    if has_x:
        kw = min(DIFF_KEY_CHUNK, nx)
        segs += [(kx_ref, vtx_ref, r0, kw, r0) for r0 in range(0, nx, kw)]
    segs.append((kc_ref, vtc_ref, 0, nc, nx))

    def scores(u):
        q = q_ref[u * sub:(u + 1) * sub, :]
        zero = jnp.zeros_like(q)
        qs = jnp.concatenate([jnp.where(lane < HEAD_DIM, q, zero), jnp.where(lane >= HEAD_DIM, q, zero)], axis=0)
        m_run = jnp.full((8, mc), -jnp.inf, F32)
        for k_ref, _, r0, w, s0 in segs:
            st = _dot_nt(k_ref[r0:r0 + w, :], qs)
            s_scr[u % 2, s0:s0 + w, :] = st
            m_run = jnp.maximum(m_run, jnp.max(st.reshape(w // 8, 8, mc), axis=0))
        return jnp.broadcast_to(jnp.max(m_run, axis=0, keepdims=True), (8, mc))

    def attend(u, m8):
        l_run = jnp.zeros((8, mc), F32)
        acc = jnp.zeros((LANES, mc), F32)
        for _, vt_ref, r0, w, s0 in segs:
            p = jnp.exp2(s_scr[u % 2, s0:s0 + w, :].reshape(w // 8, 8, mc) - m8[None])
            l_run = l_run + jnp.sum(p, axis=0)
            acc = acc + _dot(vt_ref[:, r0:r0 + w], p.reshape(w, mc).astype(BF16))
        ot = acc / jnp.sum(l_run, axis=0, keepdims=True)
        o = (ot[:, :sub] - lam * ot[:, sub:]).T
        o = _rms(o) * g_ref[...] * (1.0 - lam_init)
        o_ref[u * sub:(u + 1) * sub, :] = o.astype(o_ref.dtype)

    m_next = scores(0)
    for u in range(n_sub):
        m_cur = m_next
        if u + 1 < n_sub:
            m_next = scores(u + 1)
        attend(u, m_cur)


def _diff_attention(z, vt, lam_vec, subln_g, lam_init, dst, *, n_batch, seq, n_ctx, n_heads, tq):
    ctx_blk0 = n_batch * seq // n_ctx
    nq = seq // tq
    return pl.pallas_call(
        functools.partial(_diff_kernel, lam_init=lam_init, has_x=True),
        grid=(n_batch, n_heads, nq),
        in_specs=[
            pl.BlockSpec(lam_vec.shape, lambda b, h, i: (0, 0)),
            pl.BlockSpec((1, LANES), lambda b, h, i: (0, 0)),
            pl.BlockSpec((tq, LANES), lambda b, h, i: (b * nq + i, h)),
            pl.BlockSpec((seq, LANES), lambda b, h, i: (b, n_heads + h)),
            pl.BlockSpec((LANES, seq), lambda b, h, i: (h, b)),
            pl.BlockSpec((n_ctx, LANES), lambda b, h, i: (ctx_blk0 + b, n_heads + h)),
            pl.BlockSpec((LANES, n_ctx), lambda b, h, i: (h, ctx_blk0 + b)),
            pl.BlockSpec(memory_space=pl.ANY),
        ],
        out_specs=pl.BlockSpec((tq, LANES), lambda b, h, i: (b * nq + i, h)),
        out_shape=jax.ShapeDtypeStruct(dst.shape, dst.dtype),
        input_output_aliases={7: 0},
        scratch_shapes=[pltpu.VMEM((2, seq + n_ctx, 2 * min(tq, DIFF_SUB_ROWS)), F32)],
        compiler_params=_cparams("parallel", "parallel", "parallel"),
        name="diff_attn",
    )(lam_vec, subln_g, z, z, vt, z, vt, dst)


def _diff_attention_ctx(z, vt, lam_vec, subln_g, lam_init, dst, *, n_batch, seq, n_ctx, n_heads):
    ctx_blk0 = n_batch * seq // n_ctx
    return pl.pallas_call(
        functools.partial(_diff_kernel, lam_init=lam_init, has_x=False),
        grid=(n_batch, n_heads),
        in_specs=[
            pl.BlockSpec(lam_vec.shape, lambda b, h: (0, 0)),
            pl.BlockSpec((1, LANES), lambda b, h: (0, 0)),
            pl.BlockSpec((n_ctx, LANES), lambda b, h: (ctx_blk0 + b, h)),
            pl.BlockSpec((n_ctx, LANES), lambda b, h: (ctx_blk0 + b, n_heads + h)),
            pl.BlockSpec((LANES, n_ctx), lambda b, h: (h, ctx_blk0 + b)),
            pl.BlockSpec(memory_space=pl.ANY),
        ],
        out_specs=pl.BlockSpec((n_ctx, LANES), lambda b, h: (ctx_blk0 + b, h)),
        out_shape=jax.ShapeDtypeStruct(dst.shape, dst.dtype),
        input_output_aliases={5: 0},
        scratch_shapes=[pltpu.VMEM((2, n_ctx, 2 * min(n_ctx, DIFF_SUB_ROWS)), F32)],
        compiler_params=_cparams("parallel", "parallel"),
        name="diff_attn_ctx",
    )(lam_vec, subln_g, z, z, vt, dst)


def _outproj_kernel(a0_ref, a1_ref, w_ref, x_ref, g_ref, gate_ref, o_ref):
    half = a0_ref.shape[1]
    y = _dot(a0_ref[...], w_ref[:half, :]) + _dot(a1_ref[...], w_ref[half:, :])
    o_ref[...] = x_ref[...] + gate_ref[...] * (_rms(y) * g_ref[...])


def _outproj(a0, a0_col, a1, a1_col, w, xt, normg, mod, k_gate, *, tm, rows, seq, n_batch):
    d = xt.shape[1]
    half = d // 2

    def mrow(i):
        return jnp.minimum(i * tm // seq, n_batch)

    return pl.pallas_call(
        _outproj_kernel,
        grid=(rows // tm,),
        in_specs=[
            pl.BlockSpec((tm, half), lambda i: (i, a0_col)),
            pl.BlockSpec((tm, half), lambda i: (i, a1_col)),
            pl.BlockSpec((d, d), lambda i: (0, 0)),
            pl.BlockSpec((tm, d), lambda i: (i, 0)),
            pl.BlockSpec((None, 1, d), lambda i: (1, 0, 0)),
            pl.BlockSpec((None, None, 1, d), lambda i: (mrow(i), k_gate, 0, 0)),
        ],
        out_specs=pl.BlockSpec((tm, d), lambda i: (i, 0)),
        out_shape=jax.ShapeDtypeStruct((rows, d), F32),
        compiler_params=_cparams("parallel"),
        name="outproj",
    )(a0, a1, w, xt, normg, mod)


def _router_kernel(x_ref, g_ref, sh_ref, sc_ref, rw_ref, o_ref, sel_ref):
    h = _modulate(x_ref[...], g_ref[...], sh_ref[...], sc_ref[...])
    logits = _dot_split(_split_bf16(h), _split_bf16(rw_ref[...]))
    lane = lax.broadcasted_iota(jnp.int32, logits.shape, 1)
    ninf = -jnp.inf
    logits = jnp.where(lane < N_EXPERTS, logits, ninf)
    m1 = jnp.max(logits, axis=-1, keepdims=True)
    i1 = jnp.min(jnp.where(logits == m1, lane, LANES), axis=-1, keepdims=True)
    sel1 = lane == i1
    rest = jnp.where(sel1, ninf, logits)
    m2 = jnp.max(rest, axis=-1, keepdims=True)
    i2 = jnp.min(jnp.where(rest == m2, lane, LANES), axis=-1, keepdims=True)
    sel2 = lane == i2
    e2 = jnp.exp(m2 - m1)
    den = 1.0 + e2
    o_ref[...] = jnp.where(sel1, 1.0 / den, 0.0) + jnp.where(sel2, e2 / den, 0.0)
    sel_ref[...] = jnp.where(sel1 | sel2, 1.0, 0.0)


def _router(xt, normg, mod, router_w, *, tm, rows, seq, n_batch):
    d = xt.shape[1]
    rw = jnp.zeros((d, LANES), F32).at[:, :N_EXPERTS].set(router_w)

    def mrow(i):
        return jnp.minimum(i * tm // seq, n_batch)

    return pl.pallas_call(
        _router_kernel,
        grid=(rows // tm,),
        in_specs=[
            pl.BlockSpec((tm, d), lambda i: (i, 0)),
            pl.BlockSpec((None, 1, d), lambda i: (2, 0, 0)),
            pl.BlockSpec((None, None, 1, d), lambda i: (mrow(i), 3, 0, 0)),
            pl.BlockSpec((None, None, 1, d), lambda i: (mrow(i), 4, 0, 0)),
            pl.BlockSpec((d, LANES), lambda i: (0, 0)),
        ],
        out_specs=[pl.BlockSpec((tm, LANES), lambda i: (i, 0))] * 2,
        out_shape=[jax.ShapeDtypeStruct((rows, LANES), F32)] * 2,
        compiler_params=_cparams("parallel"),
        name="router",
    )(xt, normg, mod, mod, rw)


SWIGLU_CHUNK = 256


def _swiglu(h, wg_ref, wu_ref, wo_ref, u_off):
    width = wo_ref.shape[0]
    acc = None
    for c0 in range(0, width, SWIGLU_CHUNK):
        gp = _dot(h, wg_ref[:, c0:c0 + SWIGLU_CHUNK])
        up = _dot(h, wu_ref[:, u_off + c0:u_off + c0 + SWIGLU_CHUNK])
        a = gp * (1.0 / (1.0 + jnp.exp(-gp))) * up
        part = _dot(a.astype(BF16), wo_ref[c0:c0 + SWIGLU_CHUNK, :])
        acc = part if acc is None else acc + part
    return acc


def _ffn_kernel(x_ref, g2_ref, sh_ref, sc_ref, win_ref, wout_ref, g3_ref, gate_ref, o_ref):
    x = x_ref[...]
    h = _modulate(x, g2_ref[...], sh_ref[...], sc_ref[...]).astype(BF16)
    y = _swiglu(h, win_ref, win_ref, wout_ref, wout_ref.shape[0])
    o_ref[...] = x + gate_ref[...] * (_rms(y) * g3_ref[...])


def _ffn(xt, normg, mod, w_in, w_out, *, tm, rows, seq, n_batch):
    d = xt.shape[1]

    def mrow(i):
        return jnp.minimum(i * tm // seq, n_batch)

    in_specs = [
        pl.BlockSpec((tm, d), lambda i: (i, 0)),
        pl.BlockSpec((None, 1, d), lambda i: (2, 0, 0)),
        pl.BlockSpec((None, None, 1, d), lambda i: (mrow(i), 3, 0, 0)),
        pl.BlockSpec((None, None, 1, d), lambda i: (mrow(i), 4, 0, 0)),
        pl.BlockSpec(w_in.shape, lambda i: (0, 0), pipeline_mode=pl.Buffered(1)),
        pl.BlockSpec(w_out.shape, lambda i: (0, 0), pipeline_mode=pl.Buffered(1)),
        pl.BlockSpec((None, 1, d), lambda i: (3, 0, 0)),
        pl.BlockSpec((None, None, 1, d), lambda i: (mrow(i), 5, 0, 0)),
    ]
    return pl.pallas_call(
        _ffn_kernel,
        grid=(rows // tm,),
        in_specs=in_specs,
        out_specs=pl.BlockSpec((tm, d), lambda i: (i, 0)),
        out_shape=jax.ShapeDtypeStruct((rows, d), F32),
        compiler_params=_cparams("parallel"),
        name="ffn",
    )(xt, normg, mod, mod, w_in, w_out, normg, mod)


MOE_DISPATCH_ROWS = 256
MOE_COMBINE_ROWS = 512
MOE_SEG_ALIGN = 16
MOE_TILE = 512


def _moe_keys(sel):
    nb = sel.shape[0]
    ti = lax.broadcasted_iota(jnp.int32, (nb, nb), 0)
    tj = lax.broadcasted_iota(jnp.int32, (nb, nb), 1)
    lower = jnp.where(tj < ti, 1.0, 0.0).astype(BF16)
    rank = _dot(lower, sel.astype(BF16))
    return jnp.where(sel > 0.0, rank, -1.0)


def _moe_dispatch_kernel(nchunk_ref, off_ref, x_ref, g2_ref, sh_ref, sc_ref, sel_ref, _, hs_hbm,
                         h_scr, keyt_scr, stage, sems, *, n_exp):
    b = pl.program_id(0)
    rows = MOE_DISPATCH_ROWS
    h_scr[...] = _modulate(x_ref[...], g2_ref[...], sh_ref[...], sc_ref[...]).astype(BF16)
    keyt_scr[...] = _moe_keys(sel_ref[...]).T

    def copy(slot, row0):
        return pltpu.make_async_copy(stage.at[slot], hs_hbm.at[pl.ds(row0, rows), :], sems.at[slot])

    issued = jnp.int32(0)
    for e in range(n_exp):
        key_row = keyt_scr[e:e + 1, :]
        base = off_ref[b * n_exp + e]

        def chunk(c, k):
            slot = k % 2

            @pl.when(k >= 2)
            def _():
                copy(slot, 0).wait()

            r = (c * rows + lax.broadcasted_iota(jnp.int32, (rows, 1), 0)).astype(F32)
            onehot = jnp.where(key_row == r, 1.0, 0.0).astype(BF16)
            stage[slot] = _dot(onehot, h_scr[...]).astype(BF16)
            copy(slot, pl.multiple_of(base + c * rows, MOE_SEG_ALIGN)).start()
            return k + 1

        issued = lax.fori_loop(0, nchunk_ref[b * n_exp + e], chunk, issued)

    @pl.when(issued >= 1)
    def _():
        copy((issued - 1) % 2, 0).wait()

    @pl.when(issued >= 2)
    def _():
        copy(issued % 2, 0).wait()


def _moe_experts_kernel(tile_ref, exp_ref, valid_ref, hs_ref, wg_ref, wu_ref, wo_ref, y_ref, acc, *, n_f):
    t = pl.program_id(0)
    f = pl.program_id(1)

    @pl.when(valid_ref[t] == 1)
    def _():
        part = _swiglu(hs_ref[...], wg_ref, wu_ref, wo_ref, 0)

        @pl.when(f == 0)
        def _():
            acc[...] = part

        @pl.when(f > 0)
        def _():
            acc[...] += part

        @pl.when(f == n_f - 1)
        def _():
            y_ref[...] = acc[...].astype(y_ref.dtype)


def _moe_combine_kernel(npiece_ref, off_ref, x_ref, g3_ref, gate_ref, comb_ref, sel_ref, y_hbm, o_ref,
                        key_scr, stage, sems, *, n_exp):
    b = pl.program_id(0)
    rows = MOE_COMBINE_ROWS
    nb = x_ref.shape[0]
    key_scr[...] = _moe_keys(sel_ref[...])
    o_ref[...] = jnp.zeros_like(o_ref)
    lane = lax.broadcasted_iota(jnp.int32, (nb, LANES), 1)

    def copy(slot, row0):
        return pltpu.make_async_copy(y_hbm.at[pl.ds(row0, rows), :], stage.at[slot], sems.at[slot])

    def row0(e, c):
        return pl.multiple_of(off_ref[b * n_exp + e] + c * rows, MOE_SEG_ALIGN)

    copy(0, row0(0, 0)).start()
    done = jnp.int32(0)
    for e in range(n_exp):
        pick = lane == e
        key_col = jnp.sum(jnp.where(pick, key_scr[...], 0.0), axis=-1, keepdims=True)
        gate_col = jnp.sum(jnp.where(pick, comb_ref[...], 0.0), axis=-1, keepdims=True)
        n_pieces = npiece_ref[b * n_exp + e]

        def piece(c, k, e=e, n_pieces=n_pieces, key_col=key_col, gate_col=gate_col):
            slot = k % 2
            copy(slot, 0).wait()

            @pl.when(c + 1 < n_pieces)
            def _():
                copy(1 - slot, row0(e, c + 1)).start()

            if e + 1 < n_exp:
                @pl.when(c + 1 == n_pieces)
                def _():
                    copy(1 - slot, row0(e + 1, 0)).start()

            r = (c * rows + lax.broadcasted_iota(jnp.int32, (1, rows), 1)).astype(F32)
            onehot = jnp.where(key_col == r, 1.0, 0.0).astype(BF16)
            o_ref[...] += gate_col * _dot(onehot, stage[slot])
            return k + 1

        done = lax.fori_loop(0, n_pieces, piece, done)

    o_ref[...] = x_ref[...] + gate_ref[...] * (_rms(o_ref[...]) * g3_ref[...])


def _moe(xt, normg, mod, w_in, w_out, j, comb, sel, *, nb, tf, rows, seq, n_batch):
    d = xt.shape[1]
    n_exp, fdim = w_out.shape[1], w_out.shape[2]
    n_f = fdim // tf
    n_blk = rows // nb
    i32 = jnp.int32

    counts = jnp.sum(sel.reshape(n_blk, nb, LANES)[:, :, :n_exp], axis=1).astype(i32)
    seg = -(-counts // MOE_SEG_ALIGN) * MOE_SEG_ALIGN
    region = -(-(jnp.sum(seg, axis=0) + MOE_DISPATCH_ROWS) // MOE_TILE) * MOE_TILE
    region_end = jnp.cumsum(region)
    region_start = region_end - region
    off = (region_start[None, :] + jnp.cumsum(seg, axis=0) - seg).reshape(n_blk * n_exp).astype(i32)
    nchunk = (-(-counts // MOE_DISPATCH_ROWS)).reshape(n_blk * n_exp).astype(i32)
    npiece = jnp.maximum(-(-counts // MOE_COMBINE_ROWS), 1).reshape(n_blk * n_exp).astype(i32)
    cap = 2 * rows + n_blk * n_exp * (MOE_SEG_ALIGN - 1) + n_exp * (MOE_DISPATCH_ROWS + MOE_TILE - 1)
    n_tiles = -(-cap // MOE_TILE)
    cap = n_tiles * MOE_TILE + MOE_COMBINE_ROWS
    tiles = jnp.arange(n_tiles, dtype=i32)
    n_used = region_end[-1] // MOE_TILE
    tile_map = jnp.minimum(tiles, n_used - 1).astype(i32)
    tile_exp = jnp.minimum(jnp.searchsorted(region_end, tile_map * MOE_TILE, side="right"), n_exp - 1).astype(i32)
    tile_valid = (tiles < n_used).astype(i32)

    def mrow(i):
        return jnp.minimum(i * nb // seq, n_batch)

    hs = pl.pallas_call(
        functools.partial(_moe_dispatch_kernel, n_exp=n_exp),
        grid_spec=pltpu.PrefetchScalarGridSpec(
            num_scalar_prefetch=2,
            grid=(n_blk,),
            in_specs=[
                pl.BlockSpec((nb, d), lambda i, *_: (i, 0)),
                pl.BlockSpec((None, 1, d), lambda i, *_: (2, 0, 0)),
                pl.BlockSpec((None, None, 1, d), lambda i, *_: (mrow(i), 3, 0, 0)),
                pl.BlockSpec((None, None, 1, d), lambda i, *_: (mrow(i), 4, 0, 0)),
                pl.BlockSpec((nb, LANES), lambda i, *_: (i, 0)),
                pl.BlockSpec(memory_space=pl.ANY),
            ],
            out_specs=pl.BlockSpec(memory_space=pl.ANY),
            scratch_shapes=[
                pltpu.VMEM((nb, d), BF16),
                pltpu.VMEM((LANES, nb), F32),
                pltpu.VMEM((2, MOE_DISPATCH_ROWS, d), BF16),
                pltpu.SemaphoreType.DMA((2,)),
            ],
        ),
        out_shape=jax.ShapeDtypeStruct((cap, d), BF16),
        input_output_aliases={7: 0},
        compiler_params=_cparams("arbitrary"),
        name="moe_dispatch",
    )(nchunk, off, xt, normg, mod, mod, sel, jnp.zeros((cap, d), BF16))

    ys = pl.pallas_call(
        functools.partial(_moe_experts_kernel, n_f=n_f),
        grid_spec=pltpu.PrefetchScalarGridSpec(
            num_scalar_prefetch=3,
            grid=(n_tiles, n_f),
            in_specs=[
                pl.BlockSpec((MOE_TILE, d), lambda t, f, tm, te, tv: (tm[t], 0)),
                pl.BlockSpec((None, None, d, tf), lambda t, f, tm, te, tv: (j, te[t], 0, f)),
                pl.BlockSpec((None, None, d, tf), lambda t, f, tm, te, tv: (j, te[t], 0, n_f + f)),
                pl.BlockSpec((None, None, tf, d), lambda t, f, tm, te, tv: (j, te[t], f, 0)),
            ],
            out_specs=pl.BlockSpec((MOE_TILE, d), lambda t, f, tm, te, tv: (tm[t], 0)),
            scratch_shapes=[pltpu.VMEM((MOE_TILE, d), F32)],
        ),
        out_shape=jax.ShapeDtypeStruct((cap, d), BF16),
        input_output_aliases={3: 0},
        compiler_params=_cparams("arbitrary", "arbitrary"),
        name="moe_experts",
    )(tile_map, tile_exp, tile_valid, hs, w_in, w_in, w_out)

    return pl.pallas_call(
        functools.partial(_moe_combine_kernel, n_exp=n_exp),
        grid_spec=pltpu.PrefetchScalarGridSpec(
            num_scalar_prefetch=2,
            grid=(n_blk,),
            in_specs=[
                pl.BlockSpec((nb, d), lambda i, *_: (i, 0)),
                pl.BlockSpec((None, 1, d), lambda i, *_: (3, 0, 0)),
                pl.BlockSpec((None, None, 1, d), lambda i, *_: (mrow(i), 5, 0, 0)),
                pl.BlockSpec((nb, LANES), lambda i, *_: (i, 0)),
                pl.BlockSpec((nb, LANES), lambda i, *_: (i, 0)),
                pl.BlockSpec(memory_space=pl.ANY),
            ],
            out_specs=pl.BlockSpec((nb, d), lambda i, *_: (i, 0)),
            scratch_shapes=[
                pltpu.VMEM((nb, LANES), F32),
                pltpu.VMEM((2, MOE_COMBINE_ROWS, d), BF16),
                pltpu.SemaphoreType.DMA((2,)),
            ],
        ),
        out_shape=jax.ShapeDtypeStruct((rows, d), F32),
        compiler_params=_cparams("parallel"),
        name="moe_combine",
    )(npiece, off, xt, normg, mod, comb, sel, ys)


def _lambda_init(layer):
    return 0.8 - 0.6 * math.exp(-0.3 * layer)


def kernel(x, c, ctx, c_ctx, ada_w, ada_b, norm_g, mix_in_w, mix_out_w, win_sink, diff_qkv_w, diff_out_w,
           diff_lambda, diff_subln_g, ffn_in_w, ffn_out_w, router_w, expert_in_w, expert_out_w):
    n_batch, seq, d = x.shape
    n_ctx = ctx.shape[1]
    depth = ada_w.shape[0]
    n_lat = n_batch * seq
    n_all = n_lat + n_batch * n_ctx
    fdim = mix_in_w.shape[2] - (d // 2 + 2 * LANES)
    q_cols = d // 2
    n_diff_heads = d // LANES

    tm_proj = _pick_tile(1024, seq, n_batch * n_ctx)
    tm_out = _pick_tile(2048, seq, n_batch * n_ctx)
    tm_ffn = _pick_tile(1024, seq, n_batch * n_ctx)
    nb_moe = _pick_tile(1024, seq, n_batch * n_ctx)
    tq_diff = _pick_tile(4096, seq)
    tq_win = _pick_tile(512, seq)
    common = dict(seq=seq, n_batch=n_batch)

    xt = jnp.concatenate([x.reshape(n_lat, d), ctx.reshape(n_batch * n_ctx, d)], axis=0)
    n_mod = -(-(n_batch + 1) // 8) * 8
    cv = jnp.zeros((n_mod, d), F32).at[:n_batch].set(c).at[n_batch].set(c_ctx)
    mods = _modvec(cv, ada_w, ada_b).reshape(depth, n_mod, 6, 1, d)
    rope = _rope_tables(seq, tm_proj)
    f_tables = _fourier_tables(seq)
    fc_tables = _dense_fourier_tables(n_ctx)

    n_f = fdim // LANES
    plan_even = ([(0, i * LANES, "plain") for i in range(n_f)]
                 + [(1, i * LANES, "rope_q_log2") for i in range(q_cols // LANES)]
                 + [(1, q_cols, "rope_k"), (1, q_cols + LANES, "plain")])
    plan_odd = ([(0, i * LANES, "rope_q_log2") for i in range(n_diff_heads)]
                + [(0, d + i * LANES, "rope_k") for i in range(n_diff_heads)]
                + [(1, i * LANES, "plain_t") for i in range(n_diff_heads)])

    expert_in_bf16 = expert_in_w.astype(BF16)
    expert_out_bf16 = expert_out_w.astype(BF16)

    for layer in range(depth):
        j = layer // 2
        need_ctx = layer < depth - 1
        rows = n_all if need_ctx else n_lat
        mod = mods[layer]
        ng = norm_g[layer].reshape(4, 1, d)
        if layer % 2 == 0:
            f, z = _proj(xt, ng, mod, 0, 1, mix_in_w[j].astype(BF16), rope, plan_even,
                         [(fdim, F32, False), (q_cols + 2 * LANES, BF16, False)], tm=tm_proj, n_lat=n_lat,
                         **common)
            k_col, v_col = q_cols // LANES, q_cols // LANES + 1
            att = dict(n_batch=n_batch, seq=seq, n_ctx=n_ctx, q_cols=q_cols, k_col=k_col, v_col=v_col)
            mix_f = _fourier(f, f_tables, jnp.zeros((rows, fdim), BF16), n_batch=n_batch, seq=seq)
            mix_a = _win_attention(z, win_sink[j], jnp.zeros((rows, q_cols), BF16), tq=tq_win, **att)
            if need_ctx:
                mix_f = _dense_fourier(f, fc_tables, mix_f, n_batch=n_batch, n_pos=n_ctx,
                                       row_block0=n_lat // n_ctx)
                mix_a = _ctx_gqa_attention(z, win_sink[j], mix_a, **att)
            xt = _outproj(mix_f, 0, mix_a, 0, mix_out_w[j].astype(BF16), xt, ng, mod, 2, tm=tm_out, rows=rows,
                          **common)
            xt = _ffn(xt, ng, mod, ffn_in_w[j].astype(BF16), ffn_out_w[j].astype(BF16),
                      tm=tm_ffn, rows=rows, **common)
        else:
            lam_init = _lambda_init(layer)
            z, vt = _proj(xt, ng, mod, 0, 1, diff_qkv_w[j].astype(BF16), rope, plan_odd,
                          [(2 * d, BF16, False), (d, BF16, True)], tm=tm_proj, n_lat=n_lat, **common)
            subg = diff_subln_g[j].reshape(1, LANES)
            att = dict(n_batch=n_batch, seq=seq, n_ctx=n_ctx, n_heads=n_diff_heads)
            mix = _diff_attention(z, vt, diff_lambda[j], subg, lam_init, jnp.zeros((rows, d), BF16), tq=tq_diff,
                                  **att)
            if need_ctx:
                mix = _diff_attention_ctx(z, vt, diff_lambda[j], subg, lam_init, mix, **att)
            xt = _outproj(mix, 0, mix, 1, diff_out_w[j].astype(BF16), xt, ng, mod, 2, tm=tm_out, rows=rows,
                          **common)
            comb, sel = _router(xt, ng, mod, router_w[j], tm=tm_out, rows=rows, **common)
            xt = _moe(xt, ng, mod, expert_in_bf16, expert_out_bf16, j, comb, sel,
                      nb=nb_moe, tf=3584, rows=rows, **common)
    return xt[:n_lat].reshape(n_batch, seq, d)
```
